```python
import jax, jax.numpy as jnp
from jax import lax
import numpy as np

D_MODEL = 1024
BATCH = 16
SEQ = 256
DEPTH = 1
DEC_BATCH = 8
DEC_SEQ = 1024
PAST_LEN = 256

GRID_W = 64
N_MOD = 6
RET_HEADS = 8
RET_DK = 64
RET_DV = 64
RET_WIDTH = RET_HEADS * RET_DV
RET_QK_WIDTH = RET_HEADS * RET_DK
RET_CHUNK = 128
RWKV_HEADS = 8
RWKV_HD = 64
RWKV_WIDTH = RWKV_HEADS * RWKV_HD
LORA_W = 64
LORA_A = 64
LORA_G = 128
CONV_W = 3
PEER_HEADS = 8
N_KEYS = 128
N_EXPERTS = N_KEYS * N_KEYS
PEER_DQ = 256
PEER_TOPK = 16
PEER_BLOCK = 128
ROPE_BASE = 10000.0
NORM_EPS = 1e-6
GN_EPS = 64e-5
IN_SIZES = (RET_QK_WIDTH, RET_QK_WIDTH, RET_WIDTH, RET_WIDTH,
            RWKV_WIDTH, RWKV_WIDTH, RWKV_WIDTH, LORA_W, LORA_A, LORA_G,
            D_MODEL, D_MODEL)
IN_COLS = sum(IN_SIZES)
IN_SPLITS = tuple(sum(IN_SIZES[:i + 1]) for i in range(len(IN_SIZES) - 1))

kernel_name = 'hybrid_retention_rwkv7_peer_diffusion_step'


def rms_norm(x, g):
    xf = x.astype(jnp.float32)
    y = xf * lax.rsqrt(jnp.mean(xf * xf, axis=-1, keepdims=True) + NORM_EPS)
    return (y * g.astype(jnp.float32)).astype(x.dtype)


def head_norm(y, w, b):
    H, hd = y.shape[-2:]
    mu = jnp.mean(y, axis=-1, keepdims=True)
    var = jnp.mean(jnp.square(y - mu), axis=-1, keepdims=True)
    return ((y - mu) * lax.rsqrt(var + GN_EPS) * w.astype(jnp.float32).reshape(H, hd)
            + b.astype(jnp.float32).reshape(H, hd))


def rope_1d(x, pos):
    q = x.shape[-1] // 2
    inv = ROPE_BASE ** (-jnp.arange(q, dtype=jnp.float32) / q)
    ang = pos.astype(jnp.float32)[:, None] * inv[None, :]
    cos = jnp.cos(ang)[None, :, None, :]
    sin = jnp.sin(ang)[None, :, None, :]
    x1, x2 = x[..., :q], x[..., q:]
    return jnp.concatenate([x1 * cos - x2 * sin, x1 * sin + x2 * cos], axis=-1)


def rope_2d(x, row, col):
    half = x.shape[-1] // 2
    return jnp.concatenate([rope_1d(x[..., :half], row), rope_1d(x[..., half:], col)], axis=-1)


def retention_scan(q, k, v, log_gamma, r0):
    B, S, H, dk = q.shape
    dv = v.shape[-1]
    n = S // RET_CHUNK
    C = RET_CHUNK
    to_chunks = lambda t: t.reshape(B, n, C, H, t.shape[-1]).transpose(1, 0, 3, 2, 4)
    qc, kc, vc = to_chunks(q), to_chunks(k), to_chunks(v)
    idx = jnp.arange(C, dtype=jnp.float32)
    diff = idx[:, None] - idx[None, :]
    lg = log_gamma[:, None, None]
    decay_in = jnp.where(diff >= 0, jnp.exp(jnp.maximum(diff, 0.0) * lg), 0.0)
    cross_decay = jnp.exp((idx + 1.0)[None, :] * log_gamma[:, None])
    state_decay = jnp.exp((C - 1.0 - idx)[None, :] * log_gamma[:, None])
    chunk_decay = jnp.exp(C * log_gamma)

    def step(R, xs):
        qi, ki, vi = xs
        scores = jnp.einsum('bhid,bhjd->bhij', qi, ki) * decay_in
        o = (jnp.einsum('bhij,bhjv->bhiv', scores, vi)
             + jnp.einsum('bhid,bhdv->bhiv', qi, R) * cross_decay[None, :, :, None])
        R = (R * chunk_decay[None, :, None, None]
             + jnp.einsum('bhjd,bhjv->bhdv', ki * state_decay[None, :, :, None], vi))
        return R, o

    R, o = lax.scan(step, r0, (qc, kc, vc))
    o = o.transpose(1, 0, 3, 2, 4).reshape(B, S, H, dv)
    return o, R


def retention_bidir(q, k, v, log_gamma, s0):
    o_f, r_f = retention_scan(q, k, v, log_gamma[0], s0[:, 0])
    o_b, r_b = retention_scan(jnp.flip(q, 1), jnp.flip(k, 1), jnp.flip(v, 1), log_gamma[1], s0[:, 1])
    return o_f + jnp.flip(o_b, 1), jnp.stack([r_f, r_b], axis=1)


def rwkv7_scan(r, w, k, v, kk, a, s0):
    xs = tuple(jnp.moveaxis(t, 1, 0) for t in (r, w, k, v, kk, a))

    def step(S, xt):
        r_t, w_t, k_t, v_t, kk_t, a_t = xt
        sk = jnp.einsum('bhvk,bhk->bhv', S, kk_t)
        S = (S * w_t[:, :, None, :] - sk[..., None] * (kk_t * a_t)[:, :, None, :]
             + v_t[..., None] * k_t[:, :, None, :])
        y = jnp.einsum('bhvk,bhk->bhv', S, r_t)
        return S, y

    S, y = lax.scan(step, s0, xs)
    return jnp.moveaxis(y, 0, 1), S


def rwkv7_mixer(r, k, v, dw, da, dg, s0, lp):
    f32 = jnp.float32
    B, S, _ = r.shape
    heads = lambda t: t.reshape(B, S, RWKV_HEADS, RWKV_HD)
    r, k, v, dw, da, dg = (t.astype(f32) for t in (r, k, v, dw, da, dg))
    kk = heads(k * lp['rwkv_k_k'].astype(f32))
    kk = kk * lax.rsqrt(jnp.sum(kk * kk, axis=-1, keepdims=True) + 1e-12)
    ys, finals = [], []
    for d in range(2):
        w_log = -jax.nn.softplus(-(lp['rwkv_w0'][d] + jnp.tanh(dw) @ lp['rwkv_w2'][d])) - 0.5
        decay = jnp.exp(-jnp.exp(w_log))
        a = jax.nn.sigmoid(lp['rwkv_a0'][d] + da @ lp['rwkv_a2'][d])
        kd = k * (1.0 + (a - 1.0) * lp['rwkv_k_a'].astype(f32))
        seqs = (heads(r), heads(decay), heads(kd), heads(v), kk, heads(a))
        if d == 1:
            seqs = tuple(jnp.flip(t, 1) for t in seqs)
        y, s_fin = rwkv7_scan(*seqs, s0[:, d].astype(f32))
        if d == 1:
            y = jnp.flip(y, 1)
        ys.append(y)
        finals.append(s_fin)
    y = head_norm(ys[0] + ys[1], lp['rwkv_gn_w'], lp['rwkv_gn_b'])
    bonus = jnp.sum(heads(r) * heads(k) * lp['rwkv_r_k'].astype(f32), axis=-1, keepdims=True)
    y = y + bonus * heads(v)
    g = jax.nn.sigmoid(dg) @ lp['rwkv_g2'].astype(f32)
    return y.reshape(B, S, RWKV_WIDTH) * g, jnp.stack(finals, axis=1)


def centred_dwconv(x, w):
    return lax.conv_general_dilated(x, w[:, None, :].astype(x.dtype), window_strides=(1,),
                                    padding='SAME', dimension_numbers=('NWC', 'WIO', 'NWC'),
                                    feature_group_count=x.shape[-1])


def token_mixer(h, rot, s_ret0, s_rwkv0, lp):
    f32 = jnp.float32
    B, S, _ = h.shape
    parts = jnp.split(h @ lp['w_in'], IN_SPLITS, axis=-1)
    rq, rk, rv, rg, wr, wk, wv, dw, da, dg, gate_a, gate_b = parts
    q = rq.astype(f32).reshape(B, S, RET_HEADS, RET_DK)
    k = rk.astype(f32).reshape(B, S, RET_HEADS, RET_DK)
    v = rv.astype(f32).reshape(B, S, RET_HEADS, RET_DV)
    if rot is not None:
        q = rope_2d(q, rot[0], rot[1])
        k = rope_2d(k, rot[0], rot[1])
    log_gamma = jax.nn.log_sigmoid(lp['ret_decay'].astype(f32))
    o, ret_fin = retention_bidir(q, k * RET_DK ** -0.5, v, log_gamma, s_ret0.astype(f32))
    y_a = head_norm(o, lp['ret_gn_w'], lp['ret_gn_b']).reshape(B, S, RET_WIDTH) * jax.nn.silu(rg.astype(f32))
    rkv = centred_dwconv(jnp.concatenate([wr, wk, wv], axis=-1), lp['rwkv_conv'])
    wr, wk, wv = jnp.split(rkv, 3, axis=-1)
    y_b, rwkv_fin = rwkv7_mixer(wr, wk, wv, dw, da, dg, s_rwkv0, lp)
    br_a = y_a.astype(h.dtype) @ lp['w_br_a']
    br_b = y_b.astype(h.dtype) @ lp['w_br_b']
    merged = jax.nn.sigmoid(gate_a) * br_a + jax.nn.sigmoid(gate_b) * br_b
    return merged @ lp['w_out'], ret_fin, rwkv_fin


def peer_ffn(h, wq, sub_keys, u_tab, v_tab):
    B, S, D = h.shape
    xb = h.reshape((B * S) // PEER_BLOCK, PEER_BLOCK, D)

    def block(xt):
        q = (xt @ wq).reshape(PEER_BLOCK, PEER_HEADS, 2, PEER_DQ // 2)
        s = jnp.einsum('thcd,hckd->thck', q, sub_keys).astype(jnp.float32)
        s1, i1 = lax.top_k(s[:, :, 0], PEER_TOPK)
        s2, i2 = lax.top_k(s[:, :, 1], PEER_TOPK)
        cand = (s1[..., :, None] + s2[..., None, :]).reshape(PEER_BLOCK, PEER_HEADS, PEER_TOPK * PEER_TOPK)
        cidx = (i1[..., :, None] * N_KEYS + i2[..., None, :]).reshape(PEER_BLOCK, PEER_HEADS, PEER_TOPK * PEER_TOPK)
        top, pos = lax.top_k(cand, PEER_TOPK)
        idx = jnp.take_along_axis(cidx, pos, axis=-1)
        gate = jax.nn.softmax(top, axis=-1)
        act = jax.nn.gelu(jnp.einsum('td,thkd->thk', xt, u_tab[idx]).astype(jnp.float32))
        return jnp.einsum('thk,thkd->td', (gate * act).astype(xt.dtype), v_tab[idx])

    return lax.map(block, xb).reshape(B, S, D)


def trunk_layer(x, mod, rot, s_ret0, s_rwkv0, lp):
    sh1, sc1, g1, sh2, sc2, g2 = jnp.split(mod, N_MOD, axis=-1)
    h = rms_norm(x, lp['norm1_g']) * (1.0 + sc1) + sh1
    m, ret_fin, rwkv_fin = token_mixer(h, rot, s_ret0, s_rwkv0, lp)
    x = x + g1 * m
    h = rms_norm(x, lp['norm2_g']) * (1.0 + sc2) + sh2
    x = x + g2 * peer_ffn(h, lp['peer_wq'], lp['peer_keys'], lp['peer_u'], lp['peer_v'])
    return x, ret_fin, rwkv_fin


def setup_inputs(seed: int = 0) -> dict:
    key = jax.random.key(seed)
    ks = jax.random.split(key, 40)
    f32 = jnp.float32

    def nrm(i, shape, scale):
        return scale * jax.random.normal(ks[i], shape, f32)

    base_decay = jnp.log(2.0 ** (5.0 + jnp.arange(RET_HEADS, dtype=f32)) - 1.0)
    conv_base = jnp.array([0.25, 1.0, 0.25], f32)[None, :, None]
    return {
        'x_prompt': nrm(0, (BATCH, SEQ, D_MODEL), 1.0),
        'x_sample': nrm(1, (DEC_BATCH, DEC_SEQ, D_MODEL), 1.0),
        'state_ret': nrm(2, (DEC_BATCH, DEPTH, 2, RET_HEADS, RET_DK, RET_DV), 1.0),
        'state_rwkv': nrm(3, (DEC_BATCH, DEPTH, 2, RWKV_HEADS, RWKV_HD, RWKV_HD), 1.0),
        'c': nrm(4, (DEC_BATCH, D_MODEL), 1.0),
        'c_ctx': nrm(5, (D_MODEL,), 1.0),
        'ada_w': nrm(6, (DEPTH, D_MODEL, N_MOD * D_MODEL), 0.5 * D_MODEL ** -0.5),
        'ada_b': nrm(7, (DEPTH, N_MOD * D_MODEL), 0.02),
        'norm1_g': 1.0 + nrm(8, (DEPTH, D_MODEL), 0.02),
        'w_in': nrm(9, (DEPTH, D_MODEL, IN_COLS), D_MODEL ** -0.5),
        'ret_decay': base_decay[None, None, :] + nrm(10, (DEPTH, 2, RET_HEADS), 0.1),
        'ret_gn_w': 1.0 + nrm(11, (DEPTH, RET_WIDTH), 0.02),
        'ret_gn_b': nrm(12, (DEPTH, RET_WIDTH), 0.02),
        'rwkv_conv': conv_base + nrm(13, (DEPTH, CONV_W, 3 * RWKV_WIDTH), 0.05),
        'rwkv_w0': jax.random.uniform(ks[14], (DEPTH, 2, RWKV_WIDTH), f32, -5.0, 1.0),
        'rwkv_w2': nrm(15, (DEPTH, 2, LORA_W, RWKV_WIDTH), 0.5 * LORA_W ** -0.5),
        'rwkv_a0': nrm(16, (DEPTH, 2, RWKV_WIDTH), 0.5),
        'rwkv_a2': nrm(17, (DEPTH, 2, LORA_A, RWKV_WIDTH), 0.5 * LORA_A ** -0.5),
        'rwkv_g2': nrm(18, (DEPTH, LORA_G, RWKV_WIDTH), LORA_G ** -0.5),
        'rwkv_k_k': 0.85 + nrm(19, (DEPTH, RWKV_WIDTH), 0.05),
        'rwkv_k_a': 1.0 + nrm(20, (DEPTH, RWKV_WIDTH), 0.05),
        'rwkv_r_k': nrm(21, (DEPTH, RWKV_HEADS, RWKV_HD), 0.1),
        'rwkv_gn_w': 1.0 + nrm(22, (DEPTH, RWKV_WIDTH), 0.02),
        'rwkv_gn_b': nrm(23, (DEPTH, RWKV_WIDTH), 0.02),
        'w_br_a': nrm(24, (DEPTH, RET_WIDTH, D_MODEL), RET_WIDTH ** -0.5),
        'w_br_b': nrm(25, (DEPTH, RWKV_WIDTH, D_MODEL), RWKV_WIDTH ** -0.5),
        'w_out': nrm(26, (DEPTH, D_MODEL, D_MODEL), D_MODEL ** -0.5),
        'norm2_g': 1.0 + nrm(27, (DEPTH, D_MODEL), 0.02),
        'peer_wq': nrm(28, (DEPTH, D_MODEL, PEER_HEADS * PEER_DQ), D_MODEL ** -0.5),
        'peer_keys': nrm(29, (DEPTH, PEER_HEADS, 2, N_KEYS, PEER_DQ // 2), (PEER_DQ // 2) ** -0.5),
        'peer_u': nrm(30, (DEPTH, N_EXPERTS, D_MODEL), D_MODEL ** -0.5),
        'peer_v': nrm(31, (DEPTH, N_EXPERTS, D_MODEL), 1.0),
        'final_norm_g': 1.0 + nrm(32, (D_MODEL,), 0.02),
    }


def reference(x_prompt, x_sample, state_ret, state_rwkv, c, c_ctx, ada_w, ada_b, norm1_g, w_in,
              ret_decay, ret_gn_w, ret_gn_b, rwkv_conv, rwkv_w0, rwkv_w2, rwkv_a0, rwkv_a2,
              rwkv_g2, rwkv_k_k, rwkv_k_a, rwkv_r_k, rwkv_gn_w, rwkv_gn_b, w_br_a, w_br_b,
              w_out, norm2_g, peer_wq, peer_keys, peer_u, peer_v, final_norm_g):
    n_lat = x_sample.shape[1]
    rows = n_lat // GRID_W
    grid_row, grid_col = jnp.meshgrid(jnp.arange(rows), jnp.arange(GRID_W), indexing='ij')
    rot = (grid_row.reshape(-1), grid_col.reshape(-1))

    b_ctx = x_prompt.shape[0]
    zero_ret = jnp.zeros((b_ctx, 2, RET_HEADS, RET_DK, RET_DV), jnp.float32)
    zero_rwkv = jnp.zeros((b_ctx, 2, RWKV_HEADS, RWKV_HD, RWKV_HD), jnp.float32)

    xp, xs = x_prompt, x_sample
    new_ret, new_rwkv = [], []
    for l in range(DEPTH):
        lp = {
            'norm1_g': norm1_g[l], 'w_in': w_in[l], 'ret_decay': ret_decay[l],
            'ret_gn_w': ret_gn_w[l], 'ret_gn_b': ret_gn_b[l], 'rwkv_conv': rwkv_conv[l],
            'rwkv_w0': rwkv_w0[l], 'rwkv_w2': rwkv_w2[l], 'rwkv_a0': rwkv_a0[l],
            'rwkv_a2': rwkv_a2[l], 'rwkv_g2': rwkv_g2[l], 'rwkv_k_k': rwkv_k_k[l],
            'rwkv_k_a': rwkv_k_a[l], 'rwkv_r_k': rwkv_r_k[l], 'rwkv_gn_w': rwkv_gn_w[l],
            'rwkv_gn_b': rwkv_gn_b[l], 'w_br_a': w_br_a[l], 'w_br_b': w_br_b[l],
            'w_out': w_out[l], 'norm2_g': norm2_g[l], 'peer_wq': peer_wq[l],
            'peer_keys': peer_keys[l], 'peer_u': peer_u[l], 'peer_v': peer_v[l],
        }
        mod_ctx = (jax.nn.silu(c_ctx) @ ada_w[l] + ada_b[l])[None, None, :]
        mod_lat = (jax.nn.silu(c) @ ada_w[l] + ada_b[l])[:, None, :]
        xp, ret_fin, rwkv_fin = trunk_layer(xp, mod_ctx, None, zero_ret, zero_rwkv, lp)
        new_ret.append(ret_fin.astype(x_prompt.dtype))
        new_rwkv.append(rwkv_fin.astype(x_prompt.dtype))
        xs, _, _ = trunk_layer(xs, mod_lat, rot, state_ret[:, l], state_rwkv[:, l], lp)

    y_prompt = rms_norm(xp, final_norm_g)
    y_sample = rms_norm(xs, final_norm_g)
    new_state_ret = jnp.stack(new_ret, axis=1)
    new_state_rwkv = jnp.stack(new_rwkv, axis=1)
    return (y_prompt, y_sample, new_state_ret, new_state_rwkv)
```

```python
import functools

import jax
import jax.numpy as jnp
from jax import lax
from jax.experimental import pallas as pl
from jax.experimental.pallas import tpu as pltpu

F32 = jnp.float32
BF16 = jnp.bfloat16

D_MODEL = 1024
GRID_W = 64
N_MOD = 6
HEADS = 8
HEAD_DIM = 64
BRANCH_W = HEADS * HEAD_DIM
RET_CHUNK = 128
RWKV_CHUNK = 64
LORA_W = 64
LORA_A = 64
LORA_G = 128
PEER_HEADS = 8
N_KEYS = 128
N_EXPERTS = N_KEYS * N_KEYS
PEER_DQ = 256
PEER_TOPK = 16
ROPE_BASE = 10000.0
NORM_EPS = 1e-6
GN_EPS = 64e-5
COLS_MIX = 7 * BRANCH_W
COLS_LORA = LORA_W + LORA_A + LORA_G
COLS_GATE = 2 * D_MODEL
IN_COLS = COLS_MIX + COLS_LORA + COLS_GATE

V7X_VMEM_LIMIT_BYTES = 56 * 1024 * 1024
LANES = 128
NEG_INF = float("-inf")


def _cparams(*sem):
    return pltpu.CompilerParams(dimension_semantics=sem, vmem_limit_bytes=V7X_VMEM_LIMIT_BYTES)


def _mm(a, b):
    return jnp.dot(a.astype(BF16), b.astype(BF16), preferred_element_type=F32)


def _mm_nt(a, b):
    return lax.dot_general(a.astype(BF16), b.astype(BF16), (((1,), (1,)), ((), ())),
                           preferred_element_type=F32)


def _mm_tn(a, b):
    return lax.dot_general(a.astype(BF16), b.astype(BF16), (((0,), (0,)), ((), ())),
                           preferred_element_type=F32)


def _sigmoid(x):
    return 1.0 / (1.0 + jnp.exp(-x))


def _rms(x, g):
    return x * lax.rsqrt(jnp.mean(x * x, axis=-1, keepdims=True) + NORM_EPS) * g


def _head_sum(x):
    t, w = x.shape
    lo = lax.broadcasted_iota(jnp.int32, (t, LANES), 1) < HEAD_DIM
    outs = []
    for j in range(w // LANES):
        xt = x[:, j * LANES:(j + 1) * LANES]
        s_lo = jnp.sum(jnp.where(lo, xt, 0.0), axis=-1, keepdims=True)
        s_hi = jnp.sum(jnp.where(lo, 0.0, xt), axis=-1, keepdims=True)
        outs.append(jnp.where(lo, s_lo, s_hi))
    return outs[0] if len(outs) == 1 else jnp.concatenate(outs, axis=-1)


def _head_norm(y, w, b):
    mu = _head_sum(y) * (1.0 / HEAD_DIM)
    d = y - mu
    var = _head_sum(d * d) * (1.0 / HEAD_DIM)
    return d * lax.rsqrt(var + GN_EPS) * w + b


def _mod_kernel(c_ref, w_ref, b_ref, o_ref):
    c = c_ref[...]
    o_ref[...] = _mm(c * _sigmoid(c), w_ref[...]) + b_ref[...]


def _adaln(cc, ada_w, ada_b):
    rows = cc.shape[0]
    n = ada_w.shape[1]
    tn = n // 4
    return pl.pallas_call(
        _mod_kernel,
        grid=(n // tn,),
        in_specs=[pl.BlockSpec((rows, D_MODEL), lambda j: (0, 0)),
                  pl.BlockSpec((D_MODEL, tn), lambda j: (0, j)),
                  pl.BlockSpec((1, tn), lambda j: (0, j))],
        out_specs=pl.BlockSpec((rows, tn), lambda j: (0, j)),
        out_shape=jax.ShapeDtypeStruct((rows, n), F32),
        compiler_params=_cparams("parallel"),
        name="adaln_mod",
    )(cc, ada_w, ada_b)


def _in_kernel(x_ref, mod_ref, g_ref, w_ref, mix_ref, lora_ref, gate_ref):
    m = mod_ref[0]
    h = (_rms(x_ref[...], g_ref[...]) * (1.0 + m[1:2]) + m[0:1]).astype(BF16)
    mix_ref[...] = jnp.dot(h, w_ref[:, 0:COLS_MIX], preferred_element_type=F32)
    lora_ref[...] = jnp.dot(h, w_ref[:, COLS_MIX:COLS_MIX + COLS_LORA], preferred_element_type=F32)
    gate_ref[...] = jnp.dot(h, w_ref[:, COLS_MIX + COLS_LORA:IN_COLS], preferred_element_type=F32)


def _in_proj(x2, mod, mod_base, mod_step, seq, norm_g, w_in_bf):
    n = x2.shape[0]
    tm = 256
    mod_idx = lambda i: (mod_base + mod_step * ((i * tm) // seq), 0, 0)
    return pl.pallas_call(
        _in_kernel,
        grid=(n // tm,),
        in_specs=[pl.BlockSpec((tm, D_MODEL), lambda i: (i, 0)),
                  pl.BlockSpec((1, N_MOD, D_MODEL), mod_idx),
                  pl.BlockSpec((1, D_MODEL), lambda i: (0, 0)),
                  pl.BlockSpec((D_MODEL, IN_COLS), lambda i: (0, 0))],
        out_specs=[pl.BlockSpec((tm, COLS_MIX), lambda i: (i, 0)),
                   pl.BlockSpec((tm, COLS_LORA), lambda i: (i, 0)),
                   pl.BlockSpec((tm, COLS_GATE), lambda i: (i, 0))],
        out_shape=[jax.ShapeDtypeStruct((n, COLS_MIX), F32),
                   jax.ShapeDtypeStruct((n, COLS_LORA), F32),
                   jax.ShapeDtypeStruct((n, COLS_GATE), F32)],
        compiler_params=_cparams("parallel"),
        name="in_proj",
    )(x2, mod, norm_g, w_in_bf)


def _rope(x, cos, sin):
    lane = lax.broadcasted_iota(jnp.int32, x.shape, 1)
    first = (lane % 32) < 16
    partner = jnp.where(first, pltpu.roll(x, LANES - 16, 1), pltpu.roll(x, 16, 1))
    return x * cos + partner * sin


def _ret_kernel(*refs, seq, rope, has_init, want_fin):
    it = iter(refs)
    q_ref, k_ref, v_ref, g_ref, rd_ref, gw_ref, gb_ref = (next(it) for _ in range(7))
    cos_ref = next(it) if rope else None
    sin_ref = next(it) if rope else None
    s0_ref = next(it) if has_init else None
    y_ref = next(it)
    fin_ref = next(it) if want_fin else None
    qs_ref, ks_ref, o_ref = next(it), next(it), next(it)

    C = RET_CHUNK
    n = seq // C
    rd = rd_ref[...]
    lg = jnp.minimum(rd, 0.0) - jnp.log(1.0 + jnp.exp(-jnp.abs(rd)))
    ii = lax.broadcasted_iota(jnp.int32, (C, C), 0)
    jj = lax.broadcasted_iota(jnp.int32, (C, C), 1)
    diff = (ii - jj).astype(F32)
    col = lax.broadcasted_iota(jnp.int32, (C, 1), 0).astype(F32)

    def rows(c):
        return pl.ds(pl.multiple_of(c * C, C), C)

    def prep(c, carry):
        r = rows(c)
        q = q_ref[0, r, :]
        k = k_ref[0, r, :] * (HEAD_DIM ** -0.5)
        if rope:
            q = _rope(q, cos_ref[r, :], sin_ref[r, :])
            k = _rope(k, cos_ref[r, :], sin_ref[r, :])
        qs_ref[r, :] = q
        ks_ref[r, :] = k
        return carry

    lax.fori_loop(0, n, prep, 0)

    for h in range(2):
        lanes = slice(h * HEAD_DIM, (h + 1) * HEAD_DIM)
        lgf = lg[0:1, h * HEAD_DIM:h * HEAD_DIM + 1]
        lgb = lg[1:2, h * HEAD_DIM:h * HEAD_DIM + 1]
        decay = (jnp.where(diff >= 0, jnp.exp(jnp.maximum(diff, 0.0) * lgf), 0.0)
                 + jnp.where(diff <= 0, jnp.exp(jnp.maximum(-diff, 0.0) * lgb), 0.0))
        cross_f = jnp.exp((col + 1.0) * lgf)
        state_f = jnp.exp((C - 1.0 - col) * lgf)
        chunk_f = jnp.exp(C * lgf)
        cross_b = jnp.exp((C - col) * lgb)
        state_b = jnp.exp(col * lgb)
        chunk_b = jnp.exp(C * lgb)

        def fwd(c, rf):
            r = rows(c)
            q, k, v = qs_ref[r, lanes], ks_ref[r, lanes], v_ref[0, r, lanes]
            sc = _mm_nt(q, k) * decay
            o_ref[r, lanes] = _mm(sc, v) + _mm(q, rf) * cross_f
            return rf * chunk_f + _mm_tn(k * state_f, v)

        rf0 = s0_ref[0, 0, h] if has_init else jnp.zeros((HEAD_DIM, HEAD_DIM), F32)
        rf = lax.fori_loop(0, n, fwd, rf0)

        def bwd(i, rb):
            r = rows(n - 1 - i)
            q, k, v = qs_ref[r, lanes], ks_ref[r, lanes], v_ref[0, r, lanes]
            o_ref[r, lanes] = o_ref[r, lanes] + _mm(q, rb) * cross_b
            return rb * chunk_b + _mm_tn(k * state_b, v)

        rb0 = s0_ref[0, 1, h] if has_init else jnp.zeros((HEAD_DIM, HEAD_DIM), F32)
        rb = lax.fori_loop(0, n, bwd, rb0)
        if want_fin:
            fin_ref[0, 0, h] = rf
            fin_ref[0, 1, h] = rb

    def post(c, carry):
        r = rows(c)
        g = g_ref[0, r, :]
        y_ref[0, r, :] = _head_norm(o_ref[r, :], gw_ref[...], gb_ref[...]) * (g * _sigmoid(g))
        return carry

    lax.fori_loop(0, n, post, 0)


def _retention(mix3, rd_l, gn_w, gn_b, rope_tabs, s0, want_fin):
    b, seq, _ = mix3.shape
    npair = HEADS // 2
    rope = rope_tabs is not None
    has_init = s0 is not None
    col_spec = lambda off: pl.BlockSpec((1, seq, LANES), lambda i, p: (i, 0, off + p))
    in_specs = [col_spec(0), col_spec(npair), col_spec(2 * npair), col_spec(3 * npair),
                pl.BlockSpec((2, LANES), lambda i, p: (0, p)),
                pl.BlockSpec((1, LANES), lambda i, p: (0, p)),
                pl.BlockSpec((1, LANES), lambda i, p: (0, p))]
    args = [mix3, mix3, mix3, mix3, rd_l, gn_w, gn_b]
    if rope:
        in_specs += [pl.BlockSpec((seq, LANES), lambda i, p: (0, 0))] * 2
        args += list(rope_tabs)
    if has_init:
        in_specs.append(pl.BlockSpec((1, 2, 2, HEAD_DIM, HEAD_DIM), lambda i, p: (i, 0, p, 0, 0)))
        args.append(s0)
    out_specs = [pl.BlockSpec((1, seq, LANES), lambda i, p: (i, 0, p))]
    out_shape = [jax.ShapeDtypeStruct((b, seq, BRANCH_W), F32)]
    if want_fin:
        out_specs.append(pl.BlockSpec((1, 2, 2, HEAD_DIM, HEAD_DIM), lambda i, p: (i, 0, p, 0, 0)))
        out_shape.append(jax.ShapeDtypeStruct((b, 2, HEADS, HEAD_DIM, HEAD_DIM), F32))
    res = pl.pallas_call(
        functools.partial(_ret_kernel, seq=seq, rope=rope, has_init=has_init, want_fin=want_fin),
        grid=(b, npair),
        in_specs=in_specs,
        out_specs=out_specs,
        out_shape=out_shape,
        scratch_shapes=[pltpu.VMEM((seq, LANES), F32)] * 3,
        compiler_params=_cparams("parallel", "parallel"),
        name="retention",
    )(*args)
    return (res[0], res[1]) if want_fin else (res[0], None)


def _softplus(z):
    return jnp.maximum(z, 0.0) + jnp.log(1.0 + jnp.exp(-jnp.abs(z)))


def _rwkv_kernel(*refs, seq, has_init, want_fin):
    it = iter(refs)
    (r_ref, k_ref, v_ref, lora_ref, cw_ref, w0_ref, w2_ref, a0_ref, a2_ref, g2_ref,
     kk_w_ref, ka_ref, rk_ref, gw_ref, gb_ref) = (next(it) for _ in range(15))
    s0_ref = next(it) if has_init else None
    y_ref = next(it)
    fin_ref = next(it) if want_fin else None
    r_s, v_s, kk_s, g_s, bv_s, lw_s, b_s, kd_s, st_s = (next(it) for _ in range(9))

    C = RWKV_CHUNK
    n = seq // C
    W = BRANCH_W
    row_id = lax.broadcasted_iota(jnp.int32, (C, W), 0)

    def rows(c):
        return pl.ds(pl.multiple_of(c * C, C), C)

    def conv(ref, c, w):
        x = ref[0, rows(c), :]
        prev8 = ref[0, pl.ds(pl.multiple_of(jnp.maximum(c * C - 8, 0), 8), 8), :]
        next8 = ref[0, pl.ds(pl.multiple_of(jnp.minimum(c * C + C, seq - 8), 8), 8), :]
        prev_row = jnp.where(c > 0, prev8[7:8, :], 0.0)
        next_row = jnp.where(c < n - 1, next8[0:1, :], 0.0)
        xm = jnp.where(row_id == 0, prev_row, pltpu.roll(x, 1, 0))
        xp = jnp.where(row_id == C - 1, next_row, pltpu.roll(x, C - 1, 0))
        return w[0:1] * xm + w[1:2] * x + w[2:3] * xp

    def prep(c, carry):
        rws = rows(c)
        cw = cw_ref[...]
        r = conv(r_ref, c, cw[:, 0:W])
        k = conv(k_ref, c, cw[:, W:2 * W])
        v = conv(v_ref, c, cw[:, 2 * W:3 * W])
        lo = lora_ref[0, rws, :]
        dw = lo[:, 0:LORA_W]
        da = lo[:, LORA_W:LORA_W + LORA_A]
        dg = lo[:, LORA_W + LORA_A:]
        kk = k * kk_w_ref[...]
        kk = kk * lax.rsqrt(_head_sum(kk * kk) + 1e-12)
        r_s[rws, :] = r
        v_s[rws, :] = v
        kk_s[rws, :] = kk
        g_s[rws, :] = _mm(_sigmoid(dg), g2_ref[...])
        bv_s[rws, :] = _head_sum(r * k * rk_ref[...]) * v
        tdw = jnp.tanh(dw)
        for d in range(2):
            w_log = -_softplus(-(w0_ref[d:d + 1, :] + _mm(tdw, w2_ref[d]))) - 0.5
            a = _sigmoid(a0_ref[d:d + 1, :] + _mm(da, a2_ref[d]))
            lw_s[d, rws, :] = -jnp.exp(w_log)
            b_s[d, rws, :] = kk * a
            kd_s[d, rws, :] = k * (1.0 + (a - 1.0) * ka_ref[...])
        return carry

    lax.fori_loop(0, n, prep, 0)

    ii = lax.broadcasted_iota(jnp.int32, (C, C), 0)
    jj = lax.broadcasted_iota(jnp.int32, (C, C), 1)
    eye = (ii == jj).astype(F32)

    for d in range(2):
        rev = d == 1
        incl = (ii <= jj) if rev else (ii >= jj)
        strict = (ii < jj) if rev else (ii > jj)
        tri = incl.astype(F32)
        for h in range(HEADS):
            st_s[h] = s0_ref[0, d, h] if has_init else jnp.zeros((HEAD_DIM, HEAD_DIM), F32)

        def chunk(i, carry, rev=rev, incl=incl, strict=strict, tri=tri, d=d):
            c = (n - 1 - i) if rev else i
            rws = rows(c)
            rc, vc, kkc = r_s[rws, :], v_s[rws, :], kk_s[rws, :]
            lwc, bc, kc = lw_s[d, rws, :], b_s[d, rws, :], kd_s[d, rws, :]
            cum = jnp.dot(tri, lwc, precision=lax.Precision.HIGHEST, preferred_element_type=F32)
            tot = cum[0:1, :] if rev else cum[C - 1:C, :]
            pinv = jnp.exp(-cum)
            pend = jnp.exp(tot - cum)
            at = -kkc * jnp.exp(cum - lwc)
            bt = bc * pinv
            kt = kc * pinv
            rt = rc * jnp.exp(cum)
            bh = bc * pend
            kh = kc * pend
            etot = jnp.exp(tot)
            ys = []
            for h in range(HEADS):
                ln = slice(h * HEAD_DIM, (h + 1) * HEAD_DIM)
                wk = jnp.concatenate([bt[:, ln], kt[:, ln]], axis=0)
                ma = _mm_nt(at[:, ln], wk)
                mr = _mm_nt(rt[:, ln], wk)
                a_ab = jnp.where(strict, ma[:, 0:C], 0.0)
                a_ak = jnp.where(strict, ma[:, C:2 * C], 0.0)
                a_rb = jnp.where(incl, mr[:, 0:C], 0.0)
                a_rk = jnp.where(incl, mr[:, C:2 * C], 0.0)
                vh = vc[:, ln]
                x = jnp.concatenate([at[:, ln], _mm(a_ak, vh)], axis=1)
                p = a_ab
                for step in range(6):
                    x = x + _mm(p, x)
                    if step < 5:
                        p = _mm(p, p)
                wt = x[:, 0:HEAD_DIM]
                ut = x[:, HEAD_DIM:2 * HEAD_DIM]
                s = st_s[h]
                gh = _mm_tn(x, bh[:, ln])
                s_new = (s * etot[:, ln] + _mm(s, gh[0:HEAD_DIM, :])
                         + gh[HEAD_DIM:2 * HEAD_DIM, :] + _mm_tn(vh, kh[:, ln]))
                qt = rt[:, ln] + _mm(a_rb, wt)
                ys.append(_mm_nt(qt, s) + _mm(a_rb, ut) + _mm(a_rk, vh))
                st_s[h] = s_new
            y = jnp.concatenate(ys, axis=1)
            if d == 0:
                y_ref[0, rws, :] = y
            else:
                y_ref[0, rws, :] = y_ref[0, rws, :] + y
            return carry

        lax.fori_loop(0, n, chunk, 0)
        if want_fin:
            for h in range(HEADS):
                fin_ref[0, d, h] = st_s[h]

    def post(c, carry):
        rws = rows(c)
        y = _head_norm(y_ref[0, rws, :], gw_ref[...], gb_ref[...])
        y_ref[0, rws, :] = (y + bv_s[rws, :]) * g_s[rws, :]
        return carry

    lax.fori_loop(0, n, post, 0)


def _rwkv(mix3, lora3, lw, s0, want_fin):
    b, seq, _ = mix3.shape
    has_init = s0 is not None
    W = BRANCH_W
    col_spec = lambda j: pl.BlockSpec((1, seq, W), lambda i: (i, 0, j))
    full = lambda a: pl.BlockSpec(a.shape, lambda i: (0,) * a.ndim)
    weights = [lw['conv'], lw['w0'], lw['w2'], lw['a0'], lw['a2'], lw['g2'],
               lw['k_k'], lw['k_a'], lw['r_k'], lw['gn_w'], lw['gn_b']]
    in_specs = [col_spec(4), col_spec(5), col_spec(6),
                pl.BlockSpec((1, seq, COLS_LORA), lambda i: (i, 0, 0))] + [full(a) for a in weights]
    args = [mix3, mix3, mix3, lora3] + weights
    if has_init:
        in_specs.append(pl.BlockSpec((1, 2, HEADS, HEAD_DIM, HEAD_DIM), lambda i: (i, 0, 0, 0, 0)))
        args.append(s0)
    out_specs = [pl.BlockSpec((1, seq, W), lambda i: (i, 0, 0))]
    out_shape = [jax.ShapeDtypeStruct((b, seq, W), F32)]
    if want_fin:
        out_specs.append(pl.BlockSpec((1, 2, HEADS, HEAD_DIM, HEAD_DIM), lambda i: (i, 0, 0, 0, 0)))
        out_shape.append(jax.ShapeDtypeStruct((b, 2, HEADS, HEAD_DIM, HEAD_DIM), F32))
    sw = pltpu.VMEM((seq, W), F32)
    sw2 = pltpu.VMEM((2, seq, W), F32)
    res = pl.pallas_call(
        functools.partial(_rwkv_kernel, seq=seq, has_init=has_init, want_fin=want_fin),
        grid=(b,),
        in_specs=in_specs,
        out_specs=out_specs,
        out_shape=out_shape,
        scratch_shapes=[sw, sw, sw, sw, sw, sw2, sw2, sw2, pltpu.VMEM((HEADS, HEAD_DIM, HEAD_DIM), F32)],
        compiler_params=_cparams("parallel"),
        name="rwkv7",
    )(*args)
    return (res[0], res[1]) if want_fin else (res[0], None)


def _merge_kernel(ya_ref, yb_ref, gate_ref, x_ref, mod_ref, wa_ref, wb_ref, wo_ref, n2_ref, x1_ref, h2_ref):
    m = mod_ref[0]
    br_a = jnp.dot(ya_ref[...].astype(BF16), wa_ref[...], preferred_element_type=F32)
    br_b = jnp.dot(yb_ref[...].astype(BF16), wb_ref[...], preferred_element_type=F32)
    gate = gate_ref[...]
    merged = _sigmoid(gate[:, 0:D_MODEL]) * br_a + _sigmoid(gate[:, D_MODEL:]) * br_b
    mix = jnp.dot(merged.astype(BF16), wo_ref[...], preferred_element_type=F32)
    x1 = x_ref[...] + m[2:3] * mix
    x1_ref[...] = x1
    h2_ref[...] = (_rms(x1, n2_ref[...]) * (1.0 + m[4:5]) + m[3:4]).astype(BF16)


def _merge(ya2, yb2, gate2, x2, mod, mod_base, mod_step, seq, wa, wb, wo, norm2_g):
    n = x2.shape[0]
    tm = 256
    mod_idx = lambda i: (mod_base + mod_step * ((i * tm) // seq), 0, 0)
    row = lambda w: pl.BlockSpec((tm, w), lambda i: (i, 0))
    full = lambda a: pl.BlockSpec(a.shape, lambda i: (0,) * a.ndim)
    return pl.pallas_call(
        _merge_kernel,
        grid=(n // tm,),
        in_specs=[row(BRANCH_W), row(BRANCH_W), row(COLS_GATE), row(D_MODEL),
                  pl.BlockSpec((1, N_MOD, D_MODEL), mod_idx),
                  full(wa), full(wb), full(wo), full(norm2_g)],
        out_specs=[row(D_MODEL), row(D_MODEL)],
        out_shape=[jax.ShapeDtypeStruct((n, D_MODEL), F32), jax.ShapeDtypeStruct((n, D_MODEL), BF16)],
        compiler_params=_cparams("parallel"),
        name="merge_out",
    )(ya2, yb2, gate2, x2, mod, wa, wb, wo, norm2_g)


def _top_values(s, k):
    r, t = s.shape
    rid = lax.broadcasted_iota(jnp.int32, (r, t), 0)
    out = []
    cur = s
    for _ in range(k):
        m = jnp.max(cur, axis=0, keepdims=True)
        first = jnp.min(jnp.where(cur == m, rid, r), axis=0, keepdims=True)
        cur = jnp.where(rid == first, NEG_INF, cur)
        out.append(m)
    return out


def _peer_score_kernel(h_ref, wq_ref, keys_ref, s_ref, tau_ref, c_ref, cand_ref):
    qt = _mm_nt(wq_ref[...], h_ref[...])
    half = PEER_DQ // 2
    taus, cs = [], []
    for h in range(PEER_HEADS):
        tops = []
        for c in range(2):
            hc = 2 * h + c
            s = _mm(keys_ref[hc], qt[hc * half:(hc + 1) * half, :])
            s_ref[hc] = s
            tops.append(_top_values(s, PEER_TOPK))
        pairs = [(i, j) for i in range(PEER_TOPK) for j in range(PEER_TOPK) if (i + 1) * (j + 1) <= PEER_TOPK]
        cand_ref[...] = jnp.full(cand_ref.shape, NEG_INF, F32)
        for r, (i, j) in enumerate(pairs):
            cand_ref[r:r + 1, :] = tops[0][i] + tops[1][j]
        cand = cand_ref[...]
        best = _top_values(cand, PEER_TOPK)
        mx, tau = best[0], best[-1]
        z = jnp.sum(jnp.where(cand >= tau, jnp.exp(cand - mx), 0.0), axis=0, keepdims=True)
        taus.append(tau)
        cs.append(-(mx + jnp.log(z)))
    tau_ref[...] = jnp.concatenate(taus, axis=0)
    c_ref[...] = jnp.concatenate(cs, axis=0)


def _peer_scores(h2, wq_t, keys):
    n = h2.shape[0]
    tb = 256
    full = lambda a: pl.BlockSpec(a.shape, lambda i: (0,) * a.ndim)
    return pl.pallas_call(
        _peer_score_kernel,
        grid=(n // tb,),
        in_specs=[pl.BlockSpec((tb, D_MODEL), lambda i: (i, 0)), full(wq_t), full(keys)],
        out_specs=[pl.BlockSpec((2 * PEER_HEADS, N_KEYS, tb), lambda i: (0, 0, i)),
                   pl.BlockSpec((PEER_HEADS, tb), lambda i: (0, i)),
                   pl.BlockSpec((PEER_HEADS, tb), lambda i: (0, i))],
        out_shape=[jax.ShapeDtypeStruct((2 * PEER_HEADS, N_KEYS, n), F32),
                   jax.ShapeDtypeStruct((PEER_HEADS, n), F32),
                   jax.ShapeDtypeStruct((PEER_HEADS, n), F32)],
        scratch_shapes=[pltpu.VMEM((56, tb), F32)],
        compiler_params=_cparams("parallel"),
        name="peer_scores",
    )(h2, wq_t, keys)


def _gelu_tanh(x):
    return 0.5 * x * (1.0 + jnp.tanh(0.7978845608028654 * (x + 0.044715 * x * x * x)))


def _peer_expert_kernel(h_ref, u_ref, vt_ref, s_ref, tau_ref, c_ref, x1_ref, mod_ref, fg_ref, y_ref, acc_ref,
                        *, rows_per_step):
    e = pl.program_id(1)

    @pl.when(e == 0)
    def _():
        acc_ref[...] = jnp.zeros_like(acc_ref)

    act = _gelu_tanh(_mm_nt(u_ref[...], h_ref[...]))
    tau = tau_ref[...]
    cc = c_ref[...]
    blocks = []
    for al in range(rows_per_step):
        a = e * rows_per_step + al
        wsum = None
        for h in range(PEER_HEADS):
            v = s_ref[2 * h, pl.ds(a, 1), :] + s_ref[2 * h + 1]
            w = jnp.where(v >= tau[h:h + 1, :], jnp.exp(v + cc[h:h + 1, :]), 0.0)
            wsum = w if wsum is None else wsum + w
        blocks.append(wsum)
    gates = jnp.concatenate(blocks, axis=0)
    acc_ref[...] += jnp.dot(vt_ref[...], (gates * act).astype(BF16), preferred_element_type=F32)

    @pl.when(e == pl.num_programs(1) - 1)
    def _():
        m = mod_ref[0]
        x2 = x1_ref[...] + m[5:6] * acc_ref[...].T
        y_ref[...] = _rms(x2, fg_ref[...])


def _peer_experts(h2, u_bf, vt_bf, s_all, tau, cc, x1, mod, mod_base, mod_step, seq, final_g):
    n = h2.shape[0]
    tb = min(512, seq)
    rows_per_step = 8
    ec = rows_per_step * N_KEYS
    mod_idx = lambda i, e: (mod_base + mod_step * ((i * tb) // seq), 0, 0)
    return pl.pallas_call(
        functools.partial(_peer_expert_kernel, rows_per_step=rows_per_step),
        grid=(n // tb, N_EXPERTS // ec),
        in_specs=[pl.BlockSpec((tb, D_MODEL), lambda i, e: (i, 0)),
                  pl.BlockSpec((ec, D_MODEL), lambda i, e: (e, 0)),
                  pl.BlockSpec((D_MODEL, ec), lambda i, e: (0, e)),
                  pl.BlockSpec((2 * PEER_HEADS, N_KEYS, tb), lambda i, e: (0, 0, i)),
                  pl.BlockSpec((PEER_HEADS, tb), lambda i, e: (0, i)),
                  pl.BlockSpec((PEER_HEADS, tb), lambda i, e: (0, i)),
                  pl.BlockSpec((tb, D_MODEL), lambda i, e: (i, 0)),
                  pl.BlockSpec((1, N_MOD, D_MODEL), mod_idx),
                  pl.BlockSpec((1, D_MODEL), lambda i, e: (0, 0))],
        out_specs=pl.BlockSpec((tb, D_MODEL), lambda i, e: (i, 0)),
        out_shape=jax.ShapeDtypeStruct((n, D_MODEL), F32),
        scratch_shapes=[pltpu.VMEM((D_MODEL, tb), F32)],
        compiler_params=_cparams("parallel", "arbitrary"),
        name="peer_experts",
    )(h2, u_bf, vt_bf, s_all, tau, cc, x1, mod, final_g)


def _rope_tables(seq):
    pos = jnp.arange(seq, dtype=jnp.int32)
    lane = jnp.arange(LANES, dtype=jnp.int32) % HEAD_DIM
    use_col = (lane // 32) == 1
    p = jnp.where(use_col[None, :], (pos % GRID_W)[:, None], (pos // GRID_W)[:, None]).astype(F32)
    inv = ROPE_BASE ** (-(lane % 16).astype(F32) / 16.0)
    ang = p * inv[None, :]
    first = (lane % 32) < 16
    return jnp.cos(ang), jnp.where(first[None, :], -jnp.sin(ang), jnp.sin(ang))


def _trunk_path(x, mod, mod_base, mod_step, rope_tabs, s_ret0, s_rwkv0, want_fin, lw, final_g):
    b, seq, _ = x.shape
    x2 = x.reshape(b * seq, D_MODEL)
    mix, lora, gate = _in_proj(x2, mod, mod_base, mod_step, seq, lw['norm1_g'], lw['w_in'])
    mix3 = mix.reshape(b, seq, COLS_MIX)
    lora3 = lora.reshape(b, seq, COLS_LORA)
    ya, ret_fin = _retention(mix3, lw['ret_decay'], lw['ret_gn_w'], lw['ret_gn_b'], rope_tabs, s_ret0, want_fin)
    yb, rwkv_fin = _rwkv(mix3, lora3, lw['rwkv'], s_rwkv0, want_fin)
    x1, h2 = _merge(ya.reshape(b * seq, BRANCH_W), yb.reshape(b * seq, BRANCH_W), gate, x2, mod, mod_base,
                    mod_step, seq, lw['w_br_a'], lw['w_br_b'], lw['w_out'], lw['norm2_g'])
    s_all, tau, cc = _peer_scores(h2, lw['peer_wq_t'], lw['peer_keys'])
    y = _peer_experts(h2, lw['peer_u'], lw['peer_vt'], s_all, tau, cc, x1, mod, mod_base, mod_step, seq, final_g)
    return y.reshape(b, seq, D_MODEL), x1.reshape(b, seq, D_MODEL), ret_fin, rwkv_fin


def kernel(x_prompt, x_sample, state_ret, state_rwkv, c, c_ctx, ada_w, ada_b, norm1_g, w_in, ret_decay, ret_gn_w, ret_gn_b, rwkv_conv, rwkv_w0, rwkv_w2, rwkv_a0, rwkv_a2, rwkv_g2, rwkv_k_k, rwkv_k_a, rwkv_r_k, rwkv_gn_w, rwkv_gn_b, w_br_a, w_br_b, w_out, norm2_g, peer_wq, peer_keys, peer_u, peer_v, final_norm_g):
    assert w_in.shape[0] == 1, "the final norm is fused into the layer's last kernel: single trunk layer only"
    row = lambda a: a.reshape(1, -1)
    cc = jnp.concatenate([c_ctx[None, :], c], axis=0)
    cc = jnp.pad(cc, ((0, (-cc.shape[0]) % 8), (0, 0)))
    rope_tabs = _rope_tables(x_sample.shape[1])
    final_g = row(final_norm_g)

    if True:
        l = 0
        lw = {
            'norm1_g': row(norm1_g[l]), 'w_in': w_in[l].astype(BF16),
            'ret_decay': jnp.repeat(ret_decay[l], HEAD_DIM, axis=1),
            'ret_gn_w': row(ret_gn_w[l]), 'ret_gn_b': row(ret_gn_b[l]),
            'rwkv': {'conv': rwkv_conv[l], 'w0': rwkv_w0[l], 'w2': rwkv_w2[l].astype(BF16), 'a0': rwkv_a0[l],
                     'a2': rwkv_a2[l].astype(BF16), 'g2': rwkv_g2[l].astype(BF16), 'k_k': row(rwkv_k_k[l]),
                     'k_a': row(rwkv_k_a[l]), 'r_k': row(rwkv_r_k[l]), 'gn_w': row(rwkv_gn_w[l]),
                     'gn_b': row(rwkv_gn_b[l])},
            'w_br_a': w_br_a[l].astype(BF16), 'w_br_b': w_br_b[l].astype(BF16), 'w_out': w_out[l].astype(BF16),
            'norm2_g': row(norm2_g[l]),
            'peer_wq_t': peer_wq[l].T.astype(BF16),
            'peer_keys': peer_keys[l].reshape(2 * PEER_HEADS, N_KEYS, PEER_DQ // 2).astype(BF16),
            'peer_u': peer_u[l].astype(BF16), 'peer_vt': peer_v[l].T.astype(BF16),
        }
        mod = _adaln(cc, ada_w[l], row(ada_b[l])).reshape(cc.shape[0], N_MOD, D_MODEL)
        yp, _, ret_fin, rwkv_fin = _trunk_path(x_prompt, mod, 0, 0, None, None, None, True, lw, final_g)
        ys, _, _, _ = _trunk_path(x_sample, mod, 1, 1, rope_tabs, state_ret[:, l], state_rwkv[:, l], False, lw,
                                  final_g)
    return (yp, ys, ret_fin[:, None], rwkv_fin[:, None])
```

```python
import functools

import jax
import jax.numpy as jnp
from jax import lax
from jax.experimental import pallas as pl
from jax.experimental.pallas import tpu as pltpu

F32 = jnp.float32
BF16 = jnp.bfloat16

D_MODEL = 1024
GRID_W = 64
N_MOD = 6
HEADS = 8
HEAD_DIM = 64
BRANCH_W = HEADS * HEAD_DIM
RET_CHUNK = 128
RWKV_CHUNK = 64
LORA_W = 64
LORA_A = 64
LORA_G = 128
PEER_HEADS = 8
N_KEYS = 128
N_EXPERTS = N_KEYS * N_KEYS
PEER_DQ = 256
PEER_TOPK = 16
ROPE_BASE = 10000.0
NORM_EPS = 1e-6
GN_EPS = 64e-5
COLS_MIX = 7 * BRANCH_W
COLS_LORA = LORA_W + LORA_A + LORA_G
COLS_GATE = 2 * D_MODEL
IN_COLS = COLS_MIX + COLS_LORA + COLS_GATE

V7X_VMEM_LIMIT_BYTES = 56 * 1024 * 1024
LANES = 128
NEG_INF = float("-inf")


def _cparams(*sem):
    return pltpu.CompilerParams(dimension_semantics=sem, vmem_limit_bytes=V7X_VMEM_LIMIT_BYTES)


def _mm(a, b):
    return jnp.dot(a.astype(BF16), b.astype(BF16), preferred_element_type=F32)


def _mm_nt(a, b):
    return lax.dot_general(a.astype(BF16), b.astype(BF16), (((1,), (1,)), ((), ())),
                           preferred_element_type=F32)


def _mm_tn(a, b):
    return lax.dot_general(a.astype(BF16), b.astype(BF16), (((0,), (0,)), ((), ())),
                           preferred_element_type=F32)


def _sigmoid(x):
    return 1.0 / (1.0 + jnp.exp(-x))


def _rms(x, g):
    return x * lax.rsqrt(jnp.mean(x * x, axis=-1, keepdims=True) + NORM_EPS) * g


def _head_sum(x):
    t, w = x.shape
    lo = lax.broadcasted_iota(jnp.int32, (t, LANES), 1) < HEAD_DIM
    outs = []
    for j in range(w // LANES):
        xt = x[:, j * LANES:(j + 1) * LANES]
        s_lo = jnp.sum(jnp.where(lo, xt, 0.0), axis=-1, keepdims=True)
        s_hi = jnp.sum(jnp.where(lo, 0.0, xt), axis=-1, keepdims=True)
        outs.append(jnp.where(lo, s_lo, s_hi))
    return outs[0] if len(outs) == 1 else jnp.concatenate(outs, axis=-1)


def _head_norm(y, w, b):
    mu = _head_sum(y) * (1.0 / HEAD_DIM)
    d = y - mu
    var = _head_sum(d * d) * (1.0 / HEAD_DIM)
    return d * lax.rsqrt(var + GN_EPS) * w + b


def _mod_kernel(c_ref, w_ref, b_ref, o_ref):
    c = c_ref[...]
    o_ref[...] = _mm(c * _sigmoid(c), w_ref[...]) + b_ref[...]


def _adaln(cc, ada_w, ada_b):
    rows = cc.shape[0]
    n = ada_w.shape[1]
    tn = n // 4
    return pl.pallas_call(
        _mod_kernel,
        grid=(n // tn,),
        in_specs=[pl.BlockSpec((rows, D_MODEL), lambda j: (0, 0)),
                  pl.BlockSpec((D_MODEL, tn), lambda j: (0, j)),
                  pl.BlockSpec((1, tn), lambda j: (0, j))],
        out_specs=pl.BlockSpec((rows, tn), lambda j: (0, j)),
        out_shape=jax.ShapeDtypeStruct((rows, n), F32),
        compiler_params=_cparams("parallel"),
        name="adaln_mod",
    )(cc, ada_w, ada_b)


def _in_kernel(x_ref, mod_ref, g_ref, w_ref, mix_ref, lora_ref, gate_ref):
    m = mod_ref[0]
    h = (_rms(x_ref[...], g_ref[...]) * (1.0 + m[1:2]) + m[0:1]).astype(BF16)
    mix_ref[...] = jnp.dot(h, w_ref[:, 0:COLS_MIX], preferred_element_type=F32)
    lora_ref[...] = jnp.dot(h, w_ref[:, COLS_MIX:COLS_MIX + COLS_LORA], preferred_element_type=F32)
    gate_ref[...] = jnp.dot(h, w_ref[:, COLS_MIX + COLS_LORA:IN_COLS], preferred_element_type=F32)


def _in_proj(x2, mod, mod_base, mod_step, seq, norm_g, w_in_bf):
    n = x2.shape[0]
    tm = 256
    mod_idx = lambda i: (mod_base + mod_step * ((i * tm) // seq), 0, 0)
    return pl.pallas_call(
        _in_kernel,
        grid=(n // tm,),
        in_specs=[pl.BlockSpec((tm, D_MODEL), lambda i: (i, 0)),
                  pl.BlockSpec((1, N_MOD, D_MODEL), mod_idx),
                  pl.BlockSpec((1, D_MODEL), lambda i: (0, 0)),
                  pl.BlockSpec((D_MODEL, IN_COLS), lambda i: (0, 0))],
        out_specs=[pl.BlockSpec((tm, COLS_MIX), lambda i: (i, 0)),
                   pl.BlockSpec((tm, COLS_LORA), lambda i: (i, 0)),
                   pl.BlockSpec((tm, COLS_GATE), lambda i: (i, 0))],
        out_shape=[jax.ShapeDtypeStruct((n, COLS_MIX), F32),
                   jax.ShapeDtypeStruct((n, COLS_LORA), F32),
                   jax.ShapeDtypeStruct((n, COLS_GATE), F32)],
        compiler_params=_cparams("parallel"),
        name="in_proj",
    )(x2, mod, norm_g, w_in_bf)


def _rope(x, cos, sin):
    lane = lax.broadcasted_iota(jnp.int32, x.shape, 1)
    first = (lane % 32) < 16
    partner = jnp.where(first, pltpu.roll(x, LANES - 16, 1), pltpu.roll(x, 16, 1))
    return x * cos + partner * sin


def _ret_kernel(*refs, seq, rope, has_init, want_fin):
    it = iter(refs)
    q_ref, k_ref, v_ref, g_ref, rd_ref, gw_ref, gb_ref = (next(it) for _ in range(7))
    cos_ref = next(it) if rope else None
    sin_ref = next(it) if rope else None
    s0_ref = next(it) if has_init else None
    y_ref = next(it)
    fin_ref = next(it) if want_fin else None
    qs_ref, ks_ref, o_ref = next(it), next(it), next(it)

    C = RET_CHUNK
    n = seq // C
    rd = rd_ref[...]
    lg = jnp.minimum(rd, 0.0) - jnp.log(1.0 + jnp.exp(-jnp.abs(rd)))
    ii = lax.broadcasted_iota(jnp.int32, (C, C), 0)
    jj = lax.broadcasted_iota(jnp.int32, (C, C), 1)
    diff = (ii - jj).astype(F32)
    col = lax.broadcasted_iota(jnp.int32, (C, 1), 0).astype(F32)

    def rows(c):
        return pl.ds(pl.multiple_of(c * C, C), C)

    def prep(c, carry):
        r = rows(c)
        q = q_ref[0, r, :]
        k = k_ref[0, r, :] * (HEAD_DIM ** -0.5)
        if rope:
            q = _rope(q, cos_ref[r, :], sin_ref[r, :])
            k = _rope(k, cos_ref[r, :], sin_ref[r, :])
        qs_ref[r, :] = q
        ks_ref[r, :] = k
        return carry

    lax.fori_loop(0, n, prep, 0)

    for h in range(2):
        lanes = slice(h * HEAD_DIM, (h + 1) * HEAD_DIM)
        lgf = lg[0:1, h * HEAD_DIM:h * HEAD_DIM + 1]
        lgb = lg[1:2, h * HEAD_DIM:h * HEAD_DIM + 1]
        decay = (jnp.where(diff >= 0, jnp.exp(jnp.maximum(diff, 0.0) * lgf), 0.0)
                 + jnp.where(diff <= 0, jnp.exp(jnp.maximum(-diff, 0.0) * lgb), 0.0))
        cross_f = jnp.exp((col + 1.0) * lgf)
        state_f = jnp.exp((C - 1.0 - col) * lgf)
        chunk_f = jnp.exp(C * lgf)
        cross_b = jnp.exp((C - col) * lgb)
        state_b = jnp.exp(col * lgb)
        chunk_b = jnp.exp(C * lgb)

        def fwd(c, rf):
            r = rows(c)
            q, k, v = qs_ref[r, lanes], ks_ref[r, lanes], v_ref[0, r, lanes]
            sc = _mm_nt(q, k) * decay
            o_ref[r, lanes] = _mm(sc, v) + _mm(q, rf) * cross_f
            return rf * chunk_f + _mm_tn(k * state_f, v)

        rf0 = s0_ref[0, 0, h] if has_init else jnp.zeros((HEAD_DIM, HEAD_DIM), F32)
        rf = lax.fori_loop(0, n, fwd, rf0)

        def bwd(i, rb):
            r = rows(n - 1 - i)
            q, k, v = qs_ref[r, lanes], ks_ref[r, lanes], v_ref[0, r, lanes]
            o_ref[r, lanes] = o_ref[r, lanes] + _mm(q, rb) * cross_b
            return rb * chunk_b + _mm_tn(k * state_b, v)

        rb0 = s0_ref[0, 1, h] if has_init else jnp.zeros((HEAD_DIM, HEAD_DIM), F32)
        rb = lax.fori_loop(0, n, bwd, rb0)
        if want_fin:
            fin_ref[0, 0, h] = rf
            fin_ref[0, 1, h] = rb

    def post(c, carry):
        r = rows(c)
        g = g_ref[0, r, :]
        y_ref[0, r, :] = _head_norm(o_ref[r, :], gw_ref[...], gb_ref[...]) * (g * _sigmoid(g))
        return carry

    lax.fori_loop(0, n, post, 0)


def _retention(mix3, rd_l, gn_w, gn_b, rope_tabs, s0, want_fin):
    b, seq, _ = mix3.shape
    npair = HEADS // 2
    rope = rope_tabs is not None
    has_init = s0 is not None
    col_spec = lambda off: pl.BlockSpec((1, seq, LANES), lambda i, p: (i, 0, off + p))
    in_specs = [col_spec(0), col_spec(npair), col_spec(2 * npair), col_spec(3 * npair),
                pl.BlockSpec((2, LANES), lambda i, p: (0, p)),
                pl.BlockSpec((1, LANES), lambda i, p: (0, p)),
                pl.BlockSpec((1, LANES), lambda i, p: (0, p))]
    args = [mix3, mix3, mix3, mix3, rd_l, gn_w, gn_b]
    if rope:
        in_specs += [pl.BlockSpec((seq, LANES), lambda i, p: (0, 0))] * 2
        args += list(rope_tabs)
    if has_init:
        in_specs.append(pl.BlockSpec((1, 2, 2, HEAD_DIM, HEAD_DIM), lambda i, p: (i, 0, p, 0, 0)))
        args.append(s0)
    out_specs = [pl.BlockSpec((1, seq, LANES), lambda i, p: (i, 0, p))]
    out_shape = [jax.ShapeDtypeStruct((b, seq, BRANCH_W), F32)]
    if want_fin:
        out_specs.append(pl.BlockSpec((1, 2, 2, HEAD_DIM, HEAD_DIM), lambda i, p: (i, 0, p, 0, 0)))
        out_shape.append(jax.ShapeDtypeStruct((b, 2, HEADS, HEAD_DIM, HEAD_DIM), F32))
    res = pl.pallas_call(
        functools.partial(_ret_kernel, seq=seq, rope=rope, has_init=has_init, want_fin=want_fin),
        grid=(b, npair),
        in_specs=in_specs,
        out_specs=out_specs,
        out_shape=out_shape,
        scratch_shapes=[pltpu.VMEM((seq, LANES), F32)] * 3,
        compiler_params=_cparams("parallel", "parallel"),
        name="retention",
    )(*args)
    return (res[0], res[1]) if want_fin else (res[0], None)


def _softplus(z):
    return jnp.maximum(z, 0.0) + jnp.log(1.0 + jnp.exp(-jnp.abs(z)))


def _rwkv_kernel(*refs, seq, has_init, want_fin):
    it = iter(refs)
    (r_ref, k_ref, v_ref, lora_ref, cw_ref, w0_ref, w2_ref, a0_ref, a2_ref, g2_ref,
     kk_w_ref, ka_ref, rk_ref, gw_ref, gb_ref) = (next(it) for _ in range(15))
    s0_ref = next(it) if has_init else None
    y_ref = next(it)
    fin_ref = next(it) if want_fin else None
    r_s, v_s, kk_s, g_s, bv_s, lw_s, b_s, kd_s, st_s = (next(it) for _ in range(9))

    C = RWKV_CHUNK
    n = seq // C
    W = BRANCH_W
    row_id = lax.broadcasted_iota(jnp.int32, (C, W), 0)

    def rows(c):
        return pl.ds(pl.multiple_of(c * C, C), C)

    def conv(ref, c, w):
        x = ref[0, rows(c), :]
        prev8 = ref[0, pl.ds(pl.multiple_of(jnp.maximum(c * C - 8, 0), 8), 8), :]
        next8 = ref[0, pl.ds(pl.multiple_of(jnp.minimum(c * C + C, seq - 8), 8), 8), :]
        prev_row = jnp.where(c > 0, prev8[7:8, :], 0.0)
        next_row = jnp.where(c < n - 1, next8[0:1, :], 0.0)
        xm = jnp.where(row_id == 0, prev_row, pltpu.roll(x, 1, 0))
        xp = jnp.where(row_id == C - 1, next_row, pltpu.roll(x, C - 1, 0))
        return w[0:1] * xm + w[1:2] * x + w[2:3] * xp

    def prep(c, carry):
        rws = rows(c)
        cw = cw_ref[...]
        r = conv(r_ref, c, cw[:, 0:W])
        k = conv(k_ref, c, cw[:, W:2 * W])
        v = conv(v_ref, c, cw[:, 2 * W:3 * W])
        lo = lora_ref[0, rws, :]
        dw = lo[:, 0:LORA_W]
        da = lo[:, LORA_W:LORA_W + LORA_A]
        dg = lo[:, LORA_W + LORA_A:]
        kk = k * kk_w_ref[...]
        kk = kk * lax.rsqrt(_head_sum(kk * kk) + 1e-12)
        y_ref[0, rws, :] = jnp.zeros((C, W), F32)
        r_s[rws, :] = r
        v_s[rws, :] = v
        kk_s[rws, :] = kk
        g_s[rws, :] = _mm(_sigmoid(dg), g2_ref[...])
        bv_s[rws, :] = _head_sum(r * k * rk_ref[...]) * v
        tdw = jnp.tanh(dw)
        for d in range(2):
            w_log = -_softplus(-(w0_ref[d:d + 1, :] + _mm(tdw, w2_ref[d]))) - 0.5
            a = _sigmoid(a0_ref[d:d + 1, :] + _mm(da, a2_ref[d]))
            lw_s[d, rws, :] = -jnp.exp(w_log)
            b_s[d, rws, :] = kk * a
            kd_s[d, rws, :] = k * (1.0 + (a - 1.0) * ka_ref[...])
        return carry

    lax.fori_loop(0, n, prep, 0)

    ii = lax.broadcasted_iota(jnp.int32, (C, C), 0)
    jj = lax.broadcasted_iota(jnp.int32, (C, C), 1)
    incl = [ii >= jj, ii <= jj]
    strict = [ii > jj, ii < jj]

    for d in range(2):
        for h in range(HEADS):
            st_s[d, h] = s0_ref[0, d, h] if has_init else jnp.zeros((HEAD_DIM, HEAD_DIM), F32)

    def chunk(i, carry):
        rws, at, bt, kt, rt, bh, kh, etot, vc = ([None] * 2 for _ in range(9))
        for d in range(2):
            rws[d] = rows(i if d == 0 else n - 1 - i)
            rc, vc[d], kkc = r_s[rws[d], :], v_s[rws[d], :], kk_s[rws[d], :]
            lwc, bc, kc = lw_s[d, rws[d], :], b_s[d, rws[d], :], kd_s[d, rws[d], :]
            cum = jnp.dot(incl[d].astype(F32), lwc, precision=lax.Precision.HIGHEST,
                          preferred_element_type=F32)
            tot = cum[C - 1:C, :] if d == 0 else cum[0:1, :]
            pinv = jnp.exp(-cum)
            pend = jnp.exp(tot - cum)
            at[d] = -kkc * jnp.exp(cum - lwc)
            bt[d] = bc * pinv
            kt[d] = kc * pinv
            rt[d] = rc * jnp.exp(cum)
            bh[d] = bc * pend
            kh[d] = kc * pend
            etot[d] = jnp.exp(tot)
        chains = [(d, h) for d in range(2) for h in range(HEADS)]
        ln = lambda h: slice(h * HEAD_DIM, (h + 1) * HEAD_DIM)
        wk = [jnp.concatenate([bt[d][:, ln(h)], kt[d][:, ln(h)]], axis=0) for d, h in chains]
        ma = [_mm_nt(at[d][:, ln(h)], wk[j]) for j, (d, h) in enumerate(chains)]
        mr = [_mm_nt(rt[d][:, ln(h)], wk[j]) for j, (d, h) in enumerate(chains)]
        a_ak = [jnp.where(strict[d], ma[j][:, C:2 * C], 0.0) for j, (d, h) in enumerate(chains)]
        p = [jnp.where(strict[d], ma[j][:, 0:C], 0.0) for j, (d, h) in enumerate(chains)]
        a_rb = [jnp.where(incl[d], mr[j][:, 0:C], 0.0) for j, (d, h) in enumerate(chains)]
        a_rk = [jnp.where(incl[d], mr[j][:, C:2 * C], 0.0) for j, (d, h) in enumerate(chains)]
        vh = [vc[d][:, ln(h)] for d, h in chains]
        x = [jnp.concatenate([at[d][:, ln(h)], _mm(a_ak[j], vh[j])], axis=1)
             for j, (d, h) in enumerate(chains)]
        for step in range(6):
            x = [x[j] + _mm(p[j], x[j]) for j in range(len(chains))]
            if step < 5:
                p = [_mm(p[j], p[j]) for j in range(len(chains))]
        gh = [_mm_tn(x[j], bh[d][:, ln(h)]) for j, (d, h) in enumerate(chains)]
        vk = [_mm_tn(vh[j], kh[d][:, ln(h)]) for j, (d, h) in enumerate(chains)]
        qt = [rt[d][:, ln(h)] + _mm(a_rb[j], x[j][:, 0:HEAD_DIM]) for j, (d, h) in enumerate(chains)]
        y0 = [_mm(a_rb[j], x[j][:, HEAD_DIM:2 * HEAD_DIM]) + _mm(a_rk[j], vh[j]) for j in range(len(chains))]
        ys = []
        for j, (d, h) in enumerate(chains):
            s = st_s[d, h]
            ys.append(_mm_nt(qt[j], s) + y0[j])
            st_s[d, h] = (s * etot[d][:, ln(h)] + _mm(s, gh[j][0:HEAD_DIM, :])
                          + gh[j][HEAD_DIM:2 * HEAD_DIM, :] + vk[j])
        for d in range(2):
            y = jnp.concatenate(ys[d * HEADS:(d + 1) * HEADS], axis=1)
            y_ref[0, rws[d], :] = y_ref[0, rws[d], :] + y
        return carry

    lax.fori_loop(0, n, chunk, 0)
    if want_fin:
        for d in range(2):
            for h in range(HEADS):
                fin_ref[0, d, h] = st_s[d, h]

    def post(c, carry):
        rws = rows(c)
        y = _head_norm(y_ref[0, rws, :], gw_ref[...], gb_ref[...])
        y_ref[0, rws, :] = (y + bv_s[rws, :]) * g_s[rws, :]
        return carry

    lax.fori_loop(0, n, post, 0)


def _rwkv(mix3, lora3, lw, s0, want_fin):
    b, seq, _ = mix3.shape
    has_init = s0 is not None
    W = BRANCH_W
    col_spec = lambda j: pl.BlockSpec((1, seq, W), lambda i: (i, 0, j))
    full = lambda a: pl.BlockSpec(a.shape, lambda i: (0,) * a.ndim)
    weights = [lw['conv'], lw['w0'], lw['w2'], lw['a0'], lw['a2'], lw['g2'],
               lw['k_k'], lw['k_a'], lw['r_k'], lw['gn_w'], lw['gn_b']]
    in_specs = [col_spec(4), col_spec(5), col_spec(6),
                pl.BlockSpec((1, seq, COLS_LORA), lambda i: (i, 0, 0))] + [full(a) for a in weights]
    args = [mix3, mix3, mix3, lora3] + weights
    if has_init:
        in_specs.append(pl.BlockSpec((1, 2, HEADS, HEAD_DIM, HEAD_DIM), lambda i: (i, 0, 0, 0, 0)))
        args.append(s0)
    out_specs = [pl.BlockSpec((1, seq, W), lambda i: (i, 0, 0))]
    out_shape = [jax.ShapeDtypeStruct((b, seq, W), F32)]
    if want_fin:
        out_specs.append(pl.BlockSpec((1, 2, HEADS, HEAD_DIM, HEAD_DIM), lambda i: (i, 0, 0, 0, 0)))
        out_shape.append(jax.ShapeDtypeStruct((b, 2, HEADS, HEAD_DIM, HEAD_DIM), F32))
    sw = pltpu.VMEM((seq, W), F32)
    sw2 = pltpu.VMEM((2, seq, W), F32)
    res = pl.pallas_call(
        functools.partial(_rwkv_kernel, seq=seq, has_init=has_init, want_fin=want_fin),
        grid=(b,),
        in_specs=in_specs,
        out_specs=out_specs,
        out_shape=out_shape,
        scratch_shapes=[sw, sw, sw, sw, sw, sw2, sw2, sw2, pltpu.VMEM((2, HEADS, HEAD_DIM, HEAD_DIM), F32)],
        compiler_params=_cparams("parallel"),
        name="rwkv7",
    )(*args)
    return (res[0], res[1]) if want_fin else (res[0], None)


def _merge_kernel(ya_ref, yb_ref, gate_ref, x_ref, mod_ref, wa_ref, wb_ref, wo_ref, n2_ref, x1_ref, h2_ref):
    m = mod_ref[0]
    br_a = jnp.dot(ya_ref[...].astype(BF16), wa_ref[...], preferred_element_type=F32)
    br_b = jnp.dot(yb_ref[...].astype(BF16), wb_ref[...], preferred_element_type=F32)
    gate = gate_ref[...]
    merged = _sigmoid(gate[:, 0:D_MODEL]) * br_a + _sigmoid(gate[:, D_MODEL:]) * br_b
    mix = jnp.dot(merged.astype(BF16), wo_ref[...], preferred_element_type=F32)
    x1 = x_ref[...] + m[2:3] * mix
    x1_ref[...] = x1
    h2_ref[...] = (_rms(x1, n2_ref[...]) * (1.0 + m[4:5]) + m[3:4]).astype(BF16)


def _merge(ya2, yb2, gate2, x2, mod, mod_base, mod_step, seq, wa, wb, wo, norm2_g):
    n = x2.shape[0]
    tm = 256
    mod_idx = lambda i: (mod_base + mod_step * ((i * tm) // seq), 0, 0)
    row = lambda w: pl.BlockSpec((tm, w), lambda i: (i, 0))
    full = lambda a: pl.BlockSpec(a.shape, lambda i: (0,) * a.ndim)
    return pl.pallas_call(
        _merge_kernel,
        grid=(n // tm,),
        in_specs=[row(BRANCH_W), row(BRANCH_W), row(COLS_GATE), row(D_MODEL),
                  pl.BlockSpec((1, N_MOD, D_MODEL), mod_idx),
                  full(wa), full(wb), full(wo), full(norm2_g)],
        out_specs=[row(D_MODEL), row(D_MODEL)],
        out_shape=[jax.ShapeDtypeStruct((n, D_MODEL), F32), jax.ShapeDtypeStruct((n, D_MODEL), BF16)],
        compiler_params=_cparams("parallel"),
        name="merge_out",
    )(ya2, yb2, gate2, x2, mod, wa, wb, wo, norm2_g)


def _top_values(s, k):
    r, t = s.shape
    rid = lax.broadcasted_iota(jnp.int32, (r, t), 0)
    out = []
    cur = s
    for _ in range(k):
        m = jnp.max(cur, axis=0, keepdims=True)
        first = jnp.min(jnp.where(cur == m, rid, r), axis=0, keepdims=True)
        cur = jnp.where(rid == first, NEG_INF, cur)
        out.append(m)
    return out


def _peer_score_kernel(h_ref, wq_ref, keys_ref, s_ref, tau_ref, c_ref, cand_ref):
    qt = _mm_nt(wq_ref[...], h_ref[...])
    half = PEER_DQ // 2
    taus, cs = [], []
    for h in range(PEER_HEADS):
        tops = []
        for c in range(2):
            hc = 2 * h + c
            s = _mm(keys_ref[hc], qt[hc * half:(hc + 1) * half, :])
            s_ref[hc] = s
            tops.append(_top_values(s, PEER_TOPK))
        pairs = [(i, j) for i in range(PEER_TOPK) for j in range(PEER_TOPK) if (i + 1) * (j + 1) <= PEER_TOPK]
        cand_ref[...] = jnp.full(cand_ref.shape, NEG_INF, F32)
        for r, (i, j) in enumerate(pairs):
            cand_ref[r:r + 1, :] = tops[0][i] + tops[1][j]
        cand = cand_ref[...]
        best = _top_values(cand, PEER_TOPK)
        mx, tau = best[0], best[-1]
        z = jnp.sum(jnp.where(cand >= tau, jnp.exp(cand - mx), 0.0), axis=0, keepdims=True)
        taus.append(tau)
        cs.append(-(mx + jnp.log(z)))
    tau_ref[...] = jnp.concatenate(taus, axis=0)
    c_ref[...] = jnp.concatenate(cs, axis=0)


def _peer_scores(h2, wq_t, keys):
    n = h2.shape[0]
    tb = 256
    full = lambda a: pl.BlockSpec(a.shape, lambda i: (0,) * a.ndim)
    return pl.pallas_call(
        _peer_score_kernel,
        grid=(n // tb,),
        in_specs=[pl.BlockSpec((tb, D_MODEL), lambda i: (i, 0)), full(wq_t), full(keys)],
        out_specs=[pl.BlockSpec((2 * PEER_HEADS, N_KEYS, tb), lambda i: (0, 0, i)),
                   pl.BlockSpec((PEER_HEADS, tb), lambda i: (0, i)),
                   pl.BlockSpec((PEER_HEADS, tb), lambda i: (0, i))],
        out_shape=[jax.ShapeDtypeStruct((2 * PEER_HEADS, N_KEYS, n), F32),
                   jax.ShapeDtypeStruct((PEER_HEADS, n), F32),
                   jax.ShapeDtypeStruct((PEER_HEADS, n), F32)],
        scratch_shapes=[pltpu.VMEM((56, tb), F32)],
        compiler_params=_cparams("parallel"),
        name="peer_scores",
    )(h2, wq_t, keys)


def _gelu_tanh(x):
    return 0.5 * x * (1.0 + jnp.tanh(0.7978845608028654 * (x + 0.044715 * x * x * x)))


def _peer_expert_kernel(h_ref, u_ref, vt_ref, s_ref, tau_ref, c_ref, x1_ref, mod_ref, fg_ref, y_ref, acc_ref,
                        *, rows_per_step):
    e = pl.program_id(1)

    @pl.when(e == 0)
    def _():
        acc_ref[...] = jnp.zeros_like(acc_ref)

    act = _gelu_tanh(_mm_nt(u_ref[...], h_ref[...]))
    tau = tau_ref[...]
    cc = c_ref[...]
    blocks = []
    for al in range(rows_per_step):
        a = e * rows_per_step + al
        wsum = None
        for h in range(PEER_HEADS):
            v = s_ref[2 * h, pl.ds(a, 1), :] + s_ref[2 * h + 1]
            w = jnp.where(v >= tau[h:h + 1, :], jnp.exp(v + cc[h:h + 1, :]), 0.0)
            wsum = w if wsum is None else wsum + w
        blocks.append(wsum)
    gates = jnp.concatenate(blocks, axis=0)
    acc_ref[...] += jnp.dot(vt_ref[...], (gates * act).astype(BF16), preferred_element_type=F32)

    @pl.when(e == pl.num_programs(1) - 1)
    def _():
        m = mod_ref[0]
        x2 = x1_ref[...] + m[5:6] * acc_ref[...].T
        y_ref[...] = _rms(x2, fg_ref[...])


def _peer_experts(h2, u_bf, vt_bf, s_all, tau, cc, x1, mod, mod_base, mod_step, seq, final_g):
    n = h2.shape[0]
    tb = min(512, seq)
    rows_per_step = 8
    ec = rows_per_step * N_KEYS
    mod_idx = lambda i, e: (mod_base + mod_step * ((i * tb) // seq), 0, 0)
    return pl.pallas_call(
        functools.partial(_peer_expert_kernel, rows_per_step=rows_per_step),
        grid=(n // tb, N_EXPERTS // ec),
        in_specs=[pl.BlockSpec((tb, D_MODEL), lambda i, e: (i, 0)),
                  pl.BlockSpec((ec, D_MODEL), lambda i, e: (e, 0)),
                  pl.BlockSpec((D_MODEL, ec), lambda i, e: (0, e)),
                  pl.BlockSpec((2 * PEER_HEADS, N_KEYS, tb), lambda i, e: (0, 0, i)),
                  pl.BlockSpec((PEER_HEADS, tb), lambda i, e: (0, i)),
                  pl.BlockSpec((PEER_HEADS, tb), lambda i, e: (0, i)),
                  pl.BlockSpec((tb, D_MODEL), lambda i, e: (i, 0)),
                  pl.BlockSpec((1, N_MOD, D_MODEL), mod_idx),
                  pl.BlockSpec((1, D_MODEL), lambda i, e: (0, 0))],
        out_specs=pl.BlockSpec((tb, D_MODEL), lambda i, e: (i, 0)),
        out_shape=jax.ShapeDtypeStruct((n, D_MODEL), F32),
        scratch_shapes=[pltpu.VMEM((D_MODEL, tb), F32)],
        compiler_params=_cparams("parallel", "arbitrary"),
        name="peer_experts",
    )(h2, u_bf, vt_bf, s_all, tau, cc, x1, mod, final_g)


def _rope_tables(seq):
    pos = jnp.arange(seq, dtype=jnp.int32)
    lane = jnp.arange(LANES, dtype=jnp.int32) % HEAD_DIM
    use_col = (lane // 32) == 1
    p = jnp.where(use_col[None, :], (pos % GRID_W)[:, None], (pos // GRID_W)[:, None]).astype(F32)
    inv = ROPE_BASE ** (-(lane % 16).astype(F32) / 16.0)
    ang = p * inv[None, :]
    first = (lane % 32) < 16
    return jnp.cos(ang), jnp.where(first[None, :], -jnp.sin(ang), jnp.sin(ang))


def _trunk_path(x, mod, mod_base, mod_step, rope_tabs, s_ret0, s_rwkv0, want_fin, lw, final_g):
    b, seq, _ = x.shape
    x2 = x.reshape(b * seq, D_MODEL)
    mix, lora, gate = _in_proj(x2, mod, mod_base, mod_step, seq, lw['norm1_g'], lw['w_in'])
    mix3 = mix.reshape(b, seq, COLS_MIX)
    lora3 = lora.reshape(b, seq, COLS_LORA)
    ya, ret_fin = _retention(mix3, lw['ret_decay'], lw['ret_gn_w'], lw['ret_gn_b'], rope_tabs, s_ret0, want_fin)
    yb, rwkv_fin = _rwkv(mix3, lora3, lw['rwkv'], s_rwkv0, want_fin)
    x1, h2 = _merge(ya.reshape(b * seq, BRANCH_W), yb.reshape(b * seq, BRANCH_W), gate, x2, mod, mod_base,
                    mod_step, seq, lw['w_br_a'], lw['w_br_b'], lw['w_out'], lw['norm2_g'])
    s_all, tau, cc = _peer_scores(h2, lw['peer_wq_t'], lw['peer_keys'])
    y = _peer_experts(h2, lw['peer_u'], lw['peer_vt'], s_all, tau, cc, x1, mod, mod_base, mod_step, seq, final_g)
    return y.reshape(b, seq, D_MODEL), x1.reshape(b, seq, D_MODEL), ret_fin, rwkv_fin


def kernel(x_prompt, x_sample, state_ret, state_rwkv, c, c_ctx, ada_w, ada_b, norm1_g, w_in, ret_decay, ret_gn_w, ret_gn_b, rwkv_conv, rwkv_w0, rwkv_w2, rwkv_a0, rwkv_a2, rwkv_g2, rwkv_k_k, rwkv_k_a, rwkv_r_k, rwkv_gn_w, rwkv_gn_b, w_br_a, w_br_b, w_out, norm2_g, peer_wq, peer_keys, peer_u, peer_v, final_norm_g):
    assert w_in.shape[0] == 1, "the final norm is fused into the layer's last kernel: single trunk layer only"
    row = lambda a: a.reshape(1, -1)
    cc = jnp.concatenate([c_ctx[None, :], c], axis=0)
    cc = jnp.pad(cc, ((0, (-cc.shape[0]) % 8), (0, 0)))
    rope_tabs = _rope_tables(x_sample.shape[1])
    final_g = row(final_norm_g)

    if True:
        l = 0
        lw = {
            'norm1_g': row(norm1_g[l]), 'w_in': w_in[l].astype(BF16),
            'ret_decay': jnp.repeat(ret_decay[l], HEAD_DIM, axis=1),
            'ret_gn_w': row(ret_gn_w[l]), 'ret_gn_b': row(ret_gn_b[l]),
            'rwkv': {'conv': rwkv_conv[l], 'w0': rwkv_w0[l], 'w2': rwkv_w2[l].astype(BF16), 'a0': rwkv_a0[l],
                     'a2': rwkv_a2[l].astype(BF16), 'g2': rwkv_g2[l].astype(BF16), 'k_k': row(rwkv_k_k[l]),
                     'k_a': row(rwkv_k_a[l]), 'r_k': row(rwkv_r_k[l]), 'gn_w': row(rwkv_gn_w[l]),
                     'gn_b': row(rwkv_gn_b[l])},
            'w_br_a': w_br_a[l].astype(BF16), 'w_br_b': w_br_b[l].astype(BF16), 'w_out': w_out[l].astype(BF16),
            'norm2_g': row(norm2_g[l]),
            'peer_wq_t': peer_wq[l].T.astype(BF16),
            'peer_keys': peer_keys[l].reshape(2 * PEER_HEADS, N_KEYS, PEER_DQ // 2).astype(BF16),
            'peer_u': peer_u[l].astype(BF16), 'peer_vt': peer_v[l].T.astype(BF16),
        }
        mod = _adaln(cc, ada_w[l], row(ada_b[l])).reshape(cc.shape[0], N_MOD, D_MODEL)
        yp, _, ret_fin, rwkv_fin = _trunk_path(x_prompt, mod, 0, 0, None, None, None, True, lw, final_g)
        ys, _, _, _ = _trunk_path(x_sample, mod, 1, 1, rope_tabs, state_ret[:, l], state_rwkv[:, l], False, lw,
                                  final_g)
    return (yp, ys, ret_fin[:, None], rwkv_fin[:, None])
```

```python
import functools

import jax
import jax.numpy as jnp
from jax import lax
from jax.experimental import pallas as pl
from jax.experimental.pallas import tpu as pltpu

F32 = jnp.float32
BF16 = jnp.bfloat16

D_MODEL = 1024
GRID_W = 64
N_MOD = 6
HEADS = 8
HEAD_DIM = 64
BRANCH_W = HEADS * HEAD_DIM
RET_CHUNK = 128
RWKV_CHUNK = 64
LORA_W = 64
LORA_A = 64
LORA_G = 128
PEER_HEADS = 8
N_KEYS = 128
N_EXPERTS = N_KEYS * N_KEYS
PEER_DQ = 256
PEER_TOPK = 16
ROPE_BASE = 10000.0
NORM_EPS = 1e-6
GN_EPS = 64e-5
COLS_MIX = 7 * BRANCH_W
COLS_LORA = LORA_W + LORA_A + LORA_G
COLS_GATE = 2 * D_MODEL
IN_COLS = COLS_MIX + COLS_LORA + COLS_GATE

V7X_VMEM_LIMIT_BYTES = 56 * 1024 * 1024
LANES = 128
NEG_INF = float("-inf")


def _cparams(*sem):
    return pltpu.CompilerParams(dimension_semantics=sem, vmem_limit_bytes=V7X_VMEM_LIMIT_BYTES)


def _mm(a, b):
    return jnp.dot(a.astype(BF16), b.astype(BF16), preferred_element_type=F32)


def _mm_nt(a, b):
    return lax.dot_general(a.astype(BF16), b.astype(BF16), (((1,), (1,)), ((), ())),
                           preferred_element_type=F32)


def _mm_tn(a, b):
    return lax.dot_general(a.astype(BF16), b.astype(BF16), (((0,), (0,)), ((), ())),
                           preferred_element_type=F32)


def _sigmoid(x):
    return 1.0 / (1.0 + jnp.exp(-x))


def _rms(x, g):
    return x * lax.rsqrt(jnp.mean(x * x, axis=-1, keepdims=True) + NORM_EPS) * g


def _head_sum(x):
    t, w = x.shape
    lo = lax.broadcasted_iota(jnp.int32, (t, LANES), 1) < HEAD_DIM
    outs = []
    for j in range(w // LANES):
        xt = x[:, j * LANES:(j + 1) * LANES]
        s_lo = jnp.sum(jnp.where(lo, xt, 0.0), axis=-1, keepdims=True)
        s_hi = jnp.sum(jnp.where(lo, 0.0, xt), axis=-1, keepdims=True)
        outs.append(jnp.where(lo, s_lo, s_hi))
    return outs[0] if len(outs) == 1 else jnp.concatenate(outs, axis=-1)


def _head_norm(y, w, b):
    mu = _head_sum(y) * (1.0 / HEAD_DIM)
    d = y - mu
    var = _head_sum(d * d) * (1.0 / HEAD_DIM)
    return d * lax.rsqrt(var + GN_EPS) * w + b


def _mod_kernel(c_ref, w_ref, b_ref, o_ref):
    c = c_ref[...]
    o_ref[...] = _mm(c * _sigmoid(c), w_ref[...]) + b_ref[...]


def _adaln(cc, ada_w, ada_b):
    rows = cc.shape[0]
    n = ada_w.shape[1]
    tn = n // 4
    return pl.pallas_call(
        _mod_kernel,
        grid=(n // tn,),
        in_specs=[pl.BlockSpec((rows, D_MODEL), lambda j: (0, 0)),
                  pl.BlockSpec((D_MODEL, tn), lambda j: (0, j)),
                  pl.BlockSpec((1, tn), lambda j: (0, j))],
        out_specs=pl.BlockSpec((rows, tn), lambda j: (0, j)),
        out_shape=jax.ShapeDtypeStruct((rows, n), F32),
        compiler_params=_cparams("parallel"),
        name="adaln_mod",
    )(cc, ada_w, ada_b)


def _in_kernel(x_ref, mod_ref, g_ref, w_ref, mix_ref, lora_ref, gate_ref):
    m = mod_ref[0]
    h = (_rms(x_ref[...], g_ref[...]) * (1.0 + m[1:2]) + m[0:1]).astype(BF16)
    mix_ref[...] = jnp.dot(h, w_ref[:, 0:COLS_MIX], preferred_element_type=F32)
    lora_ref[...] = jnp.dot(h, w_ref[:, COLS_MIX:COLS_MIX + COLS_LORA], preferred_element_type=F32)
    gate_ref[...] = jnp.dot(h, w_ref[:, COLS_MIX + COLS_LORA:IN_COLS], preferred_element_type=F32)


def _in_proj(x2, mod, mod_base, mod_step, seq, norm_g, w_in_bf):
    n = x2.shape[0]
    tm = 256
    mod_idx = lambda i: (mod_base + mod_step * ((i * tm) // seq), 0, 0)
    return pl.pallas_call(
        _in_kernel,
        grid=(n // tm,),
        in_specs=[pl.BlockSpec((tm, D_MODEL), lambda i: (i, 0)),
                  pl.BlockSpec((1, N_MOD, D_MODEL), mod_idx),
                  pl.BlockSpec((1, D_MODEL), lambda i: (0, 0)),
                  pl.BlockSpec((D_MODEL, IN_COLS), lambda i: (0, 0))],
        out_specs=[pl.BlockSpec((tm, COLS_MIX), lambda i: (i, 0)),
                   pl.BlockSpec((tm, COLS_LORA), lambda i: (i, 0)),
                   pl.BlockSpec((tm, COLS_GATE), lambda i: (i, 0))],
        out_shape=[jax.ShapeDtypeStruct((n, COLS_MIX), F32),
                   jax.ShapeDtypeStruct((n, COLS_LORA), F32),
                   jax.ShapeDtypeStruct((n, COLS_GATE), F32)],
        compiler_params=_cparams("parallel"),
        name="in_proj",
    )(x2, mod, norm_g, w_in_bf)


def _rope(x, cos, sin):
    lane = lax.broadcasted_iota(jnp.int32, x.shape, 1)
    first = (lane % 32) < 16
    partner = jnp.where(first, pltpu.roll(x, LANES - 16, 1), pltpu.roll(x, 16, 1))
    return x * cos + partner * sin


def _ret_kernel(*refs, seq, rope, has_init, want_fin):
    it = iter(refs)
    q_ref, k_ref, v_ref, g_ref, rd_ref, gw_ref, gb_ref = (next(it) for _ in range(7))
    cos_ref = next(it) if rope else None
    sin_ref = next(it) if rope else None
    s0_ref = next(it) if has_init else None
    y_ref = next(it)
    fin_ref = next(it) if want_fin else None
    qs_ref, ks_ref, o_ref = next(it), next(it), next(it)

    C = RET_CHUNK
    n = seq // C
    rd = rd_ref[...]
    lg = jnp.minimum(rd, 0.0) - jnp.log(1.0 + jnp.exp(-jnp.abs(rd)))
    ii = lax.broadcasted_iota(jnp.int32, (C, C), 0)
    jj = lax.broadcasted_iota(jnp.int32, (C, C), 1)
    diff = (ii - jj).astype(F32)
    col = lax.broadcasted_iota(jnp.int32, (C, 1), 0).astype(F32)

    def rows(c):
        return pl.ds(pl.multiple_of(c * C, C), C)

    def prep(c, carry):
        r = rows(c)
        q = q_ref[0, r, :]
        k = k_ref[0, r, :] * (HEAD_DIM ** -0.5)
        if rope:
            q = _rope(q, cos_ref[r, :], sin_ref[r, :])
            k = _rope(k, cos_ref[r, :], sin_ref[r, :])
        qs_ref[r, :] = q
        ks_ref[r, :] = k
        return carry

    lax.fori_loop(0, n, prep, 0)

    for h in range(2):
        lanes = slice(h * HEAD_DIM, (h + 1) * HEAD_DIM)
        lgf = lg[0:1, h * HEAD_DIM:h * HEAD_DIM + 1]
        lgb = lg[1:2, h * HEAD_DIM:h * HEAD_DIM + 1]
        decay = (jnp.where(diff >= 0, jnp.exp(jnp.maximum(diff, 0.0) * lgf), 0.0)
                 + jnp.where(diff <= 0, jnp.exp(jnp.maximum(-diff, 0.0) * lgb), 0.0))
        cross_f = jnp.exp((col + 1.0) * lgf)
        state_f = jnp.exp((C - 1.0 - col) * lgf)
        chunk_f = jnp.exp(C * lgf)
        cross_b = jnp.exp((C - col) * lgb)
        state_b = jnp.exp(col * lgb)
        chunk_b = jnp.exp(C * lgb)

        def fwd(c, rf):
            r = rows(c)
            q, k, v = qs_ref[r, lanes], ks_ref[r, lanes], v_ref[0, r, lanes]
            sc = _mm_nt(q, k) * decay
            o_ref[r, lanes] = _mm(sc, v) + _mm(q, rf) * cross_f
            return rf * chunk_f + _mm_tn(k * state_f, v)

        rf0 = s0_ref[0, 0, h] if has_init else jnp.zeros((HEAD_DIM, HEAD_DIM), F32)
        rf = lax.fori_loop(0, n, fwd, rf0)

        def bwd(i, rb):
            r = rows(n - 1 - i)
            q, k, v = qs_ref[r, lanes], ks_ref[r, lanes], v_ref[0, r, lanes]
            o_ref[r, lanes] = o_ref[r, lanes] + _mm(q, rb) * cross_b
            return rb * chunk_b + _mm_tn(k * state_b, v)

        rb0 = s0_ref[0, 1, h] if has_init else jnp.zeros((HEAD_DIM, HEAD_DIM), F32)
        rb = lax.fori_loop(0, n, bwd, rb0)
        if want_fin:
            fin_ref[0, 0, h] = rf
            fin_ref[0, 1, h] = rb

    def post(c, carry):
        r = rows(c)
        g = g_ref[0, r, :]
        y_ref[0, r, :] = _head_norm(o_ref[r, :], gw_ref[...], gb_ref[...]) * (g * _sigmoid(g))
        return carry

    lax.fori_loop(0, n, post, 0)


def _retention(mix3, rd_l, gn_w, gn_b, rope_tabs, s0, want_fin):
    b, seq, _ = mix3.shape
    npair = HEADS // 2
    rope = rope_tabs is not None
    has_init = s0 is not None
    col_spec = lambda off: pl.BlockSpec((1, seq, LANES), lambda i, p: (i, 0, off + p))
    in_specs = [col_spec(0), col_spec(npair), col_spec(2 * npair), col_spec(3 * npair),
                pl.BlockSpec((2, LANES), lambda i, p: (0, p)),
                pl.BlockSpec((1, LANES), lambda i, p: (0, p)),
                pl.BlockSpec((1, LANES), lambda i, p: (0, p))]
    args = [mix3, mix3, mix3, mix3, rd_l, gn_w, gn_b]
    if rope:
        in_specs += [pl.BlockSpec((seq, LANES), lambda i, p: (0, 0))] * 2
        args += list(rope_tabs)
    if has_init:
        in_specs.append(pl.BlockSpec((1, 2, 2, HEAD_DIM, HEAD_DIM), lambda i, p: (i, 0, p, 0, 0)))
        args.append(s0)
    out_specs = [pl.BlockSpec((1, seq, LANES), lambda i, p: (i, 0, p))]
    out_shape = [jax.ShapeDtypeStruct((b, seq, BRANCH_W), F32)]
    if want_fin:
        out_specs.append(pl.BlockSpec((1, 2, 2, HEAD_DIM, HEAD_DIM), lambda i, p: (i, 0, p, 0, 0)))
        out_shape.append(jax.ShapeDtypeStruct((b, 2, HEADS, HEAD_DIM, HEAD_DIM), F32))
    res = pl.pallas_call(
        functools.partial(_ret_kernel, seq=seq, rope=rope, has_init=has_init, want_fin=want_fin),
        grid=(b, npair),
        in_specs=in_specs,
        out_specs=out_specs,
        out_shape=out_shape,
        scratch_shapes=[pltpu.VMEM((seq, LANES), F32)] * 3,
        compiler_params=_cparams("parallel", "parallel"),
        name="retention",
    )(*args)
    return (res[0], res[1]) if want_fin else (res[0], None)


def _softplus(z):
    return jnp.maximum(z, 0.0) + jnp.log(1.0 + jnp.exp(-jnp.abs(z)))


def _rwkv_kernel(*refs, seq, has_init, want_fin):
    it = iter(refs)
    (r_ref, k_ref, v_ref, lora_ref, cw_ref, w0_ref, w2_ref, a0_ref, a2_ref, g2_ref,
     kk_w_ref, ka_ref, rk_ref, gw_ref, gb_ref) = (next(it) for _ in range(15))
    s0_ref = next(it) if has_init else None
    y_ref = next(it)
    fin_ref = next(it) if want_fin else None
    r_s, v_s, kk_s, g_s, bv_s, lw_s, b_s, kd_s, st_s = (next(it) for _ in range(9))

    C = RWKV_CHUNK
    n = seq // C
    W = BRANCH_W
    row_id = lax.broadcasted_iota(jnp.int32, (C, W), 0)

    def rows(c):
        return pl.ds(pl.multiple_of(c * C, C), C)

    def conv(ref, c, w):
        x = ref[0, rows(c), :]
        prev8 = ref[0, pl.ds(pl.multiple_of(jnp.maximum(c * C - 8, 0), 8), 8), :]
        next8 = ref[0, pl.ds(pl.multiple_of(jnp.minimum(c * C + C, seq - 8), 8), 8), :]
        prev_row = jnp.where(c > 0, prev8[7:8, :], 0.0)
        next_row = jnp.where(c < n - 1, next8[0:1, :], 0.0)
        xm = jnp.where(row_id == 0, prev_row, pltpu.roll(x, 1, 0))
        xp = jnp.where(row_id == C - 1, next_row, pltpu.roll(x, C - 1, 0))
        return w[0:1] * xm + w[1:2] * x + w[2:3] * xp

    def prep(c, carry):
        rws = rows(c)
        cw = cw_ref[...]
        r = conv(r_ref, c, cw[:, 0:W])
        k = conv(k_ref, c, cw[:, W:2 * W])
        v = conv(v_ref, c, cw[:, 2 * W:3 * W])
        lo = lora_ref[0, rws, :]
        dw = lo[:, 0:LORA_W]
        da = lo[:, LORA_W:LORA_W + LORA_A]
        dg = lo[:, LORA_W + LORA_A:]
        kk = k * kk_w_ref[...]
        kk = kk * lax.rsqrt(_head_sum(kk * kk) + 1e-12)
        y_ref[0, rws, :] = jnp.zeros((C, W), F32)
        r_s[rws, :] = r
        v_s[rws, :] = v
        kk_s[rws, :] = kk
        g_s[rws, :] = _mm(_sigmoid(dg), g2_ref[...])
        bv_s[rws, :] = _head_sum(r * k * rk_ref[...]) * v
        tdw = jnp.tanh(dw)
        for d in range(2):
            w_log = -_softplus(-(w0_ref[d:d + 1, :] + _mm(tdw, w2_ref[d]))) - 0.5
            a = _sigmoid(a0_ref[d:d + 1, :] + _mm(da, a2_ref[d]))
            lw_s[d, rws, :] = -jnp.exp(w_log)
            b_s[d, rws, :] = kk * a
            kd_s[d, rws, :] = k * (1.0 + (a - 1.0) * ka_ref[...])
        return carry

    lax.fori_loop(0, n, prep, 0)

    ii = lax.broadcasted_iota(jnp.int32, (C, C), 0)
    jj = lax.broadcasted_iota(jnp.int32, (C, C), 1)
    incl = [ii >= jj, ii <= jj]
    strict = [ii > jj, ii < jj]

    for d in range(2):
        for h in range(HEADS):
            st_s[d, h] = s0_ref[0, d, h] if has_init else jnp.zeros((HEAD_DIM, HEAD_DIM), F32)

    def chunk(i, carry):
        rws, at, bt, kt, rt, bh, kh, etot, vc = ([None] * 2 for _ in range(9))
        for d in range(2):
            rws[d] = rows(i if d == 0 else n - 1 - i)
            rc, vc[d], kkc = r_s[rws[d], :], v_s[rws[d], :], kk_s[rws[d], :]
            lwc, bc, kc = lw_s[d, rws[d], :], b_s[d, rws[d], :], kd_s[d, rws[d], :]
            cum = jnp.dot(incl[d].astype(F32), lwc, precision=lax.Precision.HIGHEST,
                          preferred_element_type=F32)
            tot = cum[C - 1:C, :] if d == 0 else cum[0:1, :]
            pinv = jnp.exp(-cum)
            pend = jnp.exp(tot - cum)
            at[d] = -kkc * jnp.exp(cum - lwc)
            bt[d] = bc * pinv
            kt[d] = kc * pinv
            rt[d] = rc * jnp.exp(cum)
            bh[d] = bc * pend
            kh[d] = kc * pend
            etot[d] = jnp.exp(tot)
        chains = [(d, h) for d in range(2) for h in range(HEADS)]
        ln = lambda h: slice(h * HEAD_DIM, (h + 1) * HEAD_DIM)
        wk = [jnp.concatenate([bt[d][:, ln(h)], kt[d][:, ln(h)]], axis=0) for d, h in chains]
        ma = [_mm_nt(at[d][:, ln(h)], wk[j]) for j, (d, h) in enumerate(chains)]
        mr = [_mm_nt(rt[d][:, ln(h)], wk[j]) for j, (d, h) in enumerate(chains)]
        a_ak = [jnp.where(strict[d], ma[j][:, C:2 * C], 0.0) for j, (d, h) in enumerate(chains)]
        p = [jnp.where(strict[d], ma[j][:, 0:C], 0.0) for j, (d, h) in enumerate(chains)]
        a_rb = [jnp.where(incl[d], mr[j][:, 0:C], 0.0) for j, (d, h) in enumerate(chains)]
        a_rk = [jnp.where(incl[d], mr[j][:, C:2 * C], 0.0) for j, (d, h) in enumerate(chains)]
        vh = [vc[d][:, ln(h)] for d, h in chains]
        x = [jnp.concatenate([at[d][:, ln(h)], _mm(a_ak[j], vh[j])], axis=1)
             for j, (d, h) in enumerate(chains)]
        for step in range(6):
            x = [x[j] + _mm(p[j], x[j]) for j in range(len(chains))]
            if step < 5:
                p = [_mm(p[j], p[j]) for j in range(len(chains))]
        gh = [_mm_tn(x[j], bh[d][:, ln(h)]) for j, (d, h) in enumerate(chains)]
        vk = [_mm_tn(vh[j], kh[d][:, ln(h)]) for j, (d, h) in enumerate(chains)]
        qt = [rt[d][:, ln(h)] + _mm(a_rb[j], x[j][:, 0:HEAD_DIM]) for j, (d, h) in enumerate(chains)]
        y0 = [_mm(a_rb[j], x[j][:, HEAD_DIM:2 * HEAD_DIM]) + _mm(a_rk[j], vh[j]) for j in range(len(chains))]
        ys = []
        for j, (d, h) in enumerate(chains):
            s = st_s[d, h]
            ys.append(_mm_nt(qt[j], s) + y0[j])
            st_s[d, h] = (s * etot[d][:, ln(h)] + _mm(s, gh[j][0:HEAD_DIM, :])
                          + gh[j][HEAD_DIM:2 * HEAD_DIM, :] + vk[j])
        for d in range(2):
            y = jnp.concatenate(ys[d * HEADS:(d + 1) * HEADS], axis=1)
            y_ref[0, rws[d], :] = y_ref[0, rws[d], :] + y
        return carry

    lax.fori_loop(0, n, chunk, 0)
    if want_fin:
        for d in range(2):
            for h in range(HEADS):
                fin_ref[0, d, h] = st_s[d, h]

    def post(c, carry):
        rws = rows(c)
        y = _head_norm(y_ref[0, rws, :], gw_ref[...], gb_ref[...])
        y_ref[0, rws, :] = (y + bv_s[rws, :]) * g_s[rws, :]
        return carry

    lax.fori_loop(0, n, post, 0)


def _rwkv(mix3, lora3, lw, s0, want_fin):
    b, seq, _ = mix3.shape
    has_init = s0 is not None
    W = BRANCH_W
    col_spec = lambda j: pl.BlockSpec((1, seq, W), lambda i: (i, 0, j))
    full = lambda a: pl.BlockSpec(a.shape, lambda i: (0,) * a.ndim)
    weights = [lw['conv'], lw['w0'], lw['w2'], lw['a0'], lw['a2'], lw['g2'],
               lw['k_k'], lw['k_a'], lw['r_k'], lw['gn_w'], lw['gn_b']]
    in_specs = [col_spec(4), col_spec(5), col_spec(6),
                pl.BlockSpec((1, seq, COLS_LORA), lambda i: (i, 0, 0))] + [full(a) for a in weights]
    args = [mix3, mix3, mix3, lora3] + weights
    if has_init:
        in_specs.append(pl.BlockSpec((1, 2, HEADS, HEAD_DIM, HEAD_DIM), lambda i: (i, 0, 0, 0, 0)))
        args.append(s0)
    out_specs = [pl.BlockSpec((1, seq, W), lambda i: (i, 0, 0))]
    out_shape = [jax.ShapeDtypeStruct((b, seq, W), F32)]
    if want_fin:
        out_specs.append(pl.BlockSpec((1, 2, HEADS, HEAD_DIM, HEAD_DIM), lambda i: (i, 0, 0, 0, 0)))
        out_shape.append(jax.ShapeDtypeStruct((b, 2, HEADS, HEAD_DIM, HEAD_DIM), F32))
    sw = pltpu.VMEM((seq, W), F32)
    sw2 = pltpu.VMEM((2, seq, W), F32)
    res = pl.pallas_call(
        functools.partial(_rwkv_kernel, seq=seq, has_init=has_init, want_fin=want_fin),
        grid=(b,),
        in_specs=in_specs,
        out_specs=out_specs,
        out_shape=out_shape,
        scratch_shapes=[sw, sw, sw, sw, sw, sw2, sw2, sw2, pltpu.VMEM((2, HEADS, HEAD_DIM, HEAD_DIM), F32)],
        compiler_params=_cparams("parallel"),
        name="rwkv7",
    )(*args)
    return (res[0], res[1]) if want_fin else (res[0], None)


def _merge_kernel(ya_ref, yb_ref, gate_ref, x_ref, mod_ref, wa_ref, wb_ref, wo_ref, n2_ref, x1_ref, h2_ref):
    m = mod_ref[0]
    br_a = jnp.dot(ya_ref[...].astype(BF16), wa_ref[...], preferred_element_type=F32)
    br_b = jnp.dot(yb_ref[...].astype(BF16), wb_ref[...], preferred_element_type=F32)
    gate = gate_ref[...]
    merged = _sigmoid(gate[:, 0:D_MODEL]) * br_a + _sigmoid(gate[:, D_MODEL:]) * br_b
    mix = jnp.dot(merged.astype(BF16), wo_ref[...], preferred_element_type=F32)
    x1 = x_ref[...] + m[2:3] * mix
    x1_ref[...] = x1
    h2_ref[...] = (_rms(x1, n2_ref[...]) * (1.0 + m[4:5]) + m[3:4]).astype(BF16)


def _merge(ya2, yb2, gate2, x2, mod, mod_base, mod_step, seq, wa, wb, wo, norm2_g):
    n = x2.shape[0]
    tm = 256
    mod_idx = lambda i: (mod_base + mod_step * ((i * tm) // seq), 0, 0)
    row = lambda w: pl.BlockSpec((tm, w), lambda i: (i, 0))
    full = lambda a: pl.BlockSpec(a.shape, lambda i: (0,) * a.ndim)
    return pl.pallas_call(
        _merge_kernel,
        grid=(n // tm,),
        in_specs=[row(BRANCH_W), row(BRANCH_W), row(COLS_GATE), row(D_MODEL),
                  pl.BlockSpec((1, N_MOD, D_MODEL), mod_idx),
                  full(wa), full(wb), full(wo), full(norm2_g)],
        out_specs=[row(D_MODEL), row(D_MODEL)],
        out_shape=[jax.ShapeDtypeStruct((n, D_MODEL), F32), jax.ShapeDtypeStruct((n, D_MODEL), BF16)],
        compiler_params=_cparams("parallel"),
        name="merge_out",
    )(ya2, yb2, gate2, x2, mod, wa, wb, wo, norm2_g)


def _top_values(s, k):
    r, t = s.shape
    rid = lax.broadcasted_iota(jnp.int32, (r, t), 0)
    out = []
    cur = s
    for _ in range(k):
        m = jnp.max(cur, axis=0, keepdims=True)
        first = jnp.min(jnp.where(cur == m, rid, r), axis=0, keepdims=True)
        cur = jnp.where(rid == first, NEG_INF, cur)
        out.append(m)
    return out


def _peer_score_kernel(h_ref, wq_ref, keys_ref, thr_ref, g1_ref, s2_ref, e2_ref, cand_ref):
    qt = _mm_nt(wq_ref[...], h_ref[...])
    half = PEER_DQ // 2
    k1 = PEER_TOPK + 1
    pairs = [(i, j) for i in range(k1) for j in range(k1) if (i + 1) * (j + 1) <= k1]
    assert len(pairs) <= cand_ref.shape[0]
    for h in range(PEER_HEADS):
        s = [_mm(keys_ref[2 * h + c], qt[(2 * h + c) * half:(2 * h + c + 1) * half, :]) for c in range(2)]
        tops = [_top_values(s[c], k1) for c in range(2)]
        cand_ref[...] = jnp.full(cand_ref.shape, NEG_INF, F32)
        for r, (i, j) in enumerate(pairs):
            cand_ref[r:r + 1, :] = tops[0][i] + tops[1][j]
        best = _top_values(cand_ref[...], k1)
        mx = best[0]
        z = jnp.exp(best[0] - mx)
        for r in range(1, PEER_TOPK):
            z = z + jnp.exp(best[r] - mx)
        theta = 0.5 * (best[PEER_TOPK - 1] + best[PEER_TOPK])
        thr_ref[h] = theta - s[0]
        g1_ref[h] = jnp.exp(s[0] - tops[0][0]) / z
        s2_ref[h] = s[1]
        e2_ref[h] = jnp.exp(s[1] - tops[1][0])


def _peer_scores(h2, wq_t, keys):
    n = h2.shape[0]
    tb = 256
    full = lambda a: pl.BlockSpec(a.shape, lambda i: (0,) * a.ndim)
    return pl.pallas_call(
        _peer_score_kernel,
        grid=(n // tb,),
        in_specs=[pl.BlockSpec((tb, D_MODEL), lambda i: (i, 0)), full(wq_t), full(keys)],
        out_specs=[pl.BlockSpec((PEER_HEADS, N_KEYS, tb), lambda i: (0, 0, i))] * 4,
        out_shape=[jax.ShapeDtypeStruct((PEER_HEADS, N_KEYS, n), F32)] * 4,
        scratch_shapes=[pltpu.VMEM((56, tb), F32)],
        compiler_params=_cparams("parallel"),
        name="peer_scores",
    )(h2, wq_t, keys)


def _gelu_tanh(x):
    return 0.5 * x * (1.0 + jnp.tanh(0.7978845608028654 * (x + 0.044715 * x * x * x)))


def _peer_expert_kernel(h_ref, u_ref, vt_ref, thr_ref, g1_ref, s2_ref, e2_ref, x1_ref, mod_ref, fg_ref, y_ref,
                        acc_ref, st_ref, w_ref, *, rows_per_step):
    e = pl.program_id(1)
    tb = h_ref.shape[0]

    @pl.when(e == 0)
    def _():
        acc_ref[...] = jnp.zeros_like(acc_ref)

    for al in range(rows_per_step):
        a = e * rows_per_step + al
        r0 = al * N_KEYS
        thr = [thr_ref[h, pl.ds(a, 1), :] for h in range(PEER_HEADS)]
        g1 = [g1_ref[h, pl.ds(a, 1), :] for h in range(PEER_HEADS)]
        sub = 32
        for lt in range(tb // LANES):
            ls = slice(lt * LANES, (lt + 1) * LANES)
            for sb in range(N_KEYS // sub):
                bs = slice(sb * sub, (sb + 1) * sub)
                wsum = None
                for h in range(PEER_HEADS):
                    w = jnp.where(s2_ref[h, bs, ls] >= thr[h][:, ls], e2_ref[h, bs, ls] * g1[h][:, ls], 0.0)
                    wsum = w if wsum is None else wsum + w
                st_ref[r0 + sb * sub:r0 + (sb + 1) * sub, ls] = wsum

    st = _mm_nt(u_ref[...], h_ref[...])
    partial = None
    rows_per_dot = 2
    for al in range(rows_per_step):
        r0 = al * N_KEYS
        for lt in range(tb // LANES):
            ls = slice(lt * LANES, (lt + 1) * LANES)
            act = _gelu_tanh(st[r0:r0 + N_KEYS, ls])
            w_ref[r0:r0 + N_KEYS, ls] = (st_ref[r0:r0 + N_KEYS, ls] * act).astype(BF16)
        if (al + 1) % rows_per_dot == 0:
            k0 = (al + 1 - rows_per_dot) * N_KEYS
            k1 = (al + 1) * N_KEYS
            d = jnp.dot(vt_ref[:, k0:k1], w_ref[k0:k1, :], preferred_element_type=F32)
            partial = d if partial is None else partial + d
    acc_ref[...] += partial

    @pl.when(e == pl.num_programs(1) - 1)
    def _():
        m = mod_ref[0]
        x2 = x1_ref[...] + m[5:6] * acc_ref[...].T
        y_ref[...] = _rms(x2, fg_ref[...])


def _peer_experts(h2, u_bf, vt_bf, stats, x1, mod, mod_base, mod_step, seq, final_g):
    n = h2.shape[0]
    tb = 512 if (mod_step == 0 or seq % 512 == 0) else seq
    rows_per_step = 8
    ec = rows_per_step * N_KEYS
    mod_idx = lambda i, e: (mod_base + mod_step * ((i * tb) // seq), 0, 0)
    stat_spec = pl.BlockSpec((PEER_HEADS, N_KEYS, tb), lambda i, e: (0, 0, i))
    return pl.pallas_call(
        functools.partial(_peer_expert_kernel, rows_per_step=rows_per_step),
        grid=(n // tb, N_EXPERTS // ec),
        in_specs=[pl.BlockSpec((tb, D_MODEL), lambda i, e: (i, 0)),
                  pl.BlockSpec((ec, D_MODEL), lambda i, e: (e, 0)),
                  pl.BlockSpec((D_MODEL, ec), lambda i, e: (0, e)),
                  stat_spec, stat_spec, stat_spec, stat_spec,
                  pl.BlockSpec((tb, D_MODEL), lambda i, e: (i, 0)),
                  pl.BlockSpec((1, N_MOD, D_MODEL), mod_idx),
                  pl.BlockSpec((1, D_MODEL), lambda i, e: (0, 0))],
        out_specs=pl.BlockSpec((tb, D_MODEL), lambda i, e: (i, 0)),
        out_shape=jax.ShapeDtypeStruct((n, D_MODEL), F32),
        scratch_shapes=[pltpu.VMEM((D_MODEL, tb), F32), pltpu.VMEM((ec, tb), F32), pltpu.VMEM((ec, tb), BF16)],
        compiler_params=_cparams("parallel", "arbitrary"),
        name="peer_experts",
    )(h2, u_bf, vt_bf, *stats, x1, mod, final_g)


def _rope_tables(seq):
    pos = jnp.arange(seq, dtype=jnp.int32)
    lane = jnp.arange(LANES, dtype=jnp.int32) % HEAD_DIM
    use_col = (lane // 32) == 1
    p = jnp.where(use_col[None, :], (pos % GRID_W)[:, None], (pos // GRID_W)[:, None]).astype(F32)
    inv = ROPE_BASE ** (-(lane % 16).astype(F32) / 16.0)
    ang = p * inv[None, :]
    first = (lane % 32) < 16
    return jnp.cos(ang), jnp.where(first[None, :], -jnp.sin(ang), jnp.sin(ang))


def _trunk_path(x, mod, mod_base, mod_step, rope_tabs, s_ret0, s_rwkv0, want_fin, lw, final_g):
    b, seq, _ = x.shape
    x2 = x.reshape(b * seq, D_MODEL)
    mix, lora, gate = _in_proj(x2, mod, mod_base, mod_step, seq, lw['norm1_g'], lw['w_in'])
    mix3 = mix.reshape(b, seq, COLS_MIX)
    lora3 = lora.reshape(b, seq, COLS_LORA)
    ya, ret_fin = _retention(mix3, lw['ret_decay'], lw['ret_gn_w'], lw['ret_gn_b'], rope_tabs, s_ret0, want_fin)
    yb, rwkv_fin = _rwkv(mix3, lora3, lw['rwkv'], s_rwkv0, want_fin)
    x1, h2 = _merge(ya.reshape(b * seq, BRANCH_W), yb.reshape(b * seq, BRANCH_W), gate, x2, mod, mod_base,
                    mod_step, seq, lw['w_br_a'], lw['w_br_b'], lw['w_out'], lw['norm2_g'])
    stats = _peer_scores(h2, lw['peer_wq_t'], lw['peer_keys'])
    y = _peer_experts(h2, lw['peer_u'], lw['peer_vt'], stats, x1, mod, mod_base, mod_step, seq, final_g)
    return y.reshape(b, seq, D_MODEL), x1.reshape(b, seq, D_MODEL), ret_fin, rwkv_fin


def kernel(x_prompt, x_sample, state_ret, state_rwkv, c, c_ctx, ada_w, ada_b, norm1_g, w_in, ret_decay, ret_gn_w, ret_gn_b, rwkv_conv, rwkv_w0, rwkv_w2, rwkv_a0, rwkv_a2, rwkv_g2, rwkv_k_k, rwkv_k_a, rwkv_r_k, rwkv_gn_w, rwkv_gn_b, w_br_a, w_br_b, w_out, norm2_g, peer_wq, peer_keys, peer_u, peer_v, final_norm_g):
    assert w_in.shape[0] == 1, "the final norm is fused into the layer's last kernel: single trunk layer only"
    row = lambda a: a.reshape(1, -1)
    cc = jnp.concatenate([c_ctx[None, :], c], axis=0)
    cc = jnp.pad(cc, ((0, (-cc.shape[0]) % 8), (0, 0)))
    rope_tabs = _rope_tables(x_sample.shape[1])
    final_g = row(final_norm_g)

    if True:
        l = 0
        lw = {
            'norm1_g': row(norm1_g[l]), 'w_in': w_in[l].astype(BF16),
            'ret_decay': jnp.repeat(ret_decay[l], HEAD_DIM, axis=1),
            'ret_gn_w': row(ret_gn_w[l]), 'ret_gn_b': row(ret_gn_b[l]),
            'rwkv': {'conv': rwkv_conv[l], 'w0': rwkv_w0[l], 'w2': rwkv_w2[l].astype(BF16), 'a0': rwkv_a0[l],
                     'a2': rwkv_a2[l].astype(BF16), 'g2': rwkv_g2[l].astype(BF16), 'k_k': row(rwkv_k_k[l]),
                     'k_a': row(rwkv_k_a[l]), 'r_k': row(rwkv_r_k[l]), 'gn_w': row(rwkv_gn_w[l]),
                     'gn_b': row(rwkv_gn_b[l])},
            'w_br_a': w_br_a[l].astype(BF16), 'w_br_b': w_br_b[l].astype(BF16), 'w_out': w_out[l].astype(BF16),
            'norm2_g': row(norm2_g[l]),
            'peer_wq_t': peer_wq[l].T.astype(BF16),
            'peer_keys': peer_keys[l].reshape(2 * PEER_HEADS, N_KEYS, PEER_DQ // 2).astype(BF16),
            'peer_u': peer_u[l].astype(BF16), 'peer_vt': peer_v[l].T.astype(BF16),
        }
        mod = _adaln(cc, ada_w[l], row(ada_b[l])).reshape(cc.shape[0], N_MOD, D_MODEL)
        yp, _, ret_fin, rwkv_fin = _trunk_path(x_prompt, mod, 0, 0, None, None, None, True, lw, final_g)
        ys, _, _, _ = _trunk_path(x_sample, mod, 1, 1, rope_tabs, state_ret[:, l], state_rwkv[:, l], False, lw,
                                  final_g)
    return (yp, ys, ret_fin[:, None], rwkv_fin[:, None])
```

```python
import functools

import jax
import jax.numpy as jnp
from jax import lax
from jax.experimental import pallas as pl
from jax.experimental.pallas import tpu as pltpu

F32 = jnp.float32
BF16 = jnp.bfloat16

D_MODEL = 1024
GRID_W = 64
N_MOD = 6
HEADS = 8
HEAD_DIM = 64
BRANCH_W = HEADS * HEAD_DIM
RET_CHUNK = 128
RWKV_CHUNK = 64
LORA_W = 64
LORA_A = 64
LORA_G = 128
PEER_HEADS = 8
N_KEYS = 128
N_EXPERTS = N_KEYS * N_KEYS
PEER_DQ = 256
PEER_TOPK = 16
ROPE_BASE = 10000.0
NORM_EPS = 1e-6
GN_EPS = 64e-5
COLS_MIX = 7 * BRANCH_W
COLS_LORA = LORA_W + LORA_A + LORA_G
COLS_GATE = 2 * D_MODEL
IN_COLS = COLS_MIX + COLS_LORA + COLS_GATE

V7X_VMEM_LIMIT_BYTES = 56 * 1024 * 1024
LANES = 128
NEG_INF = float("-inf")


def _cparams(*sem):
    return pltpu.CompilerParams(dimension_semantics=sem, vmem_limit_bytes=V7X_VMEM_LIMIT_BYTES)


def _mm(a, b):
    return jnp.dot(a.astype(BF16), b.astype(BF16), preferred_element_type=F32)


def _mm_nt(a, b):
    return lax.dot_general(a.astype(BF16), b.astype(BF16), (((1,), (1,)), ((), ())),
                           preferred_element_type=F32)


def _mm_tn(a, b):
    return lax.dot_general(a.astype(BF16), b.astype(BF16), (((0,), (0,)), ((), ())),
                           preferred_element_type=F32)


def _sigmoid(x):
    return 1.0 / (1.0 + jnp.exp(-x))


def _rms(x, g):
    return x * lax.rsqrt(jnp.mean(x * x, axis=-1, keepdims=True) + NORM_EPS) * g


def _head_sum(x):
    t, w = x.shape
    lo = lax.broadcasted_iota(jnp.int32, (t, LANES), 1) < HEAD_DIM
    outs = []
    for j in range(w // LANES):
        xt = x[:, j * LANES:(j + 1) * LANES]
        s_lo = jnp.sum(jnp.where(lo, xt, 0.0), axis=-1, keepdims=True)
        s_hi = jnp.sum(jnp.where(lo, 0.0, xt), axis=-1, keepdims=True)
        outs.append(jnp.where(lo, s_lo, s_hi))
    return outs[0] if len(outs) == 1 else jnp.concatenate(outs, axis=-1)


def _head_norm(y, w, b):
    mu = _head_sum(y) * (1.0 / HEAD_DIM)
    d = y - mu
    var = _head_sum(d * d) * (1.0 / HEAD_DIM)
    return d * lax.rsqrt(var + GN_EPS) * w + b


def _mod_kernel(c_ref, w_ref, b_ref, o_ref):
    c = c_ref[...]
    o_ref[...] = _mm(c * _sigmoid(c), w_ref[...]) + b_ref[...]


def _adaln(cc, ada_w, ada_b):
    rows = cc.shape[0]
    n = ada_w.shape[1]
    tn = n // 4
    return pl.pallas_call(
        _mod_kernel,
        grid=(n // tn,),
        in_specs=[pl.BlockSpec((rows, D_MODEL), lambda j: (0, 0)),
                  pl.BlockSpec((D_MODEL, tn), lambda j: (0, j)),
                  pl.BlockSpec((1, tn), lambda j: (0, j))],
        out_specs=pl.BlockSpec((rows, tn), lambda j: (0, j)),
        out_shape=jax.ShapeDtypeStruct((rows, n), F32),
        compiler_params=_cparams("parallel"),
        name="adaln_mod",
    )(cc, ada_w, ada_b)


def _in_kernel(x_ref, mod_ref, g_ref, w_ref, mix_ref, lora_ref, gate_ref):
    m = mod_ref[0]
    h = (_rms(x_ref[...], g_ref[...]) * (1.0 + m[1:2]) + m[0:1]).astype(BF16)
    mix_ref[...] = jnp.dot(h, w_ref[:, 0:COLS_MIX], preferred_element_type=F32)
    lora_ref[...] = jnp.dot(h, w_ref[:, COLS_MIX:COLS_MIX + COLS_LORA], preferred_element_type=F32)
    gate_ref[...] = jnp.dot(h, w_ref[:, COLS_MIX + COLS_LORA:IN_COLS], preferred_element_type=F32)


def _in_proj(x2, mod, mod_base, mod_step, seq, norm_g, w_in_bf):
    n = x2.shape[0]
    tm = 256
    mod_idx = lambda i: (mod_base + mod_step * ((i * tm) // seq), 0, 0)
    return pl.pallas_call(
        _in_kernel,
        grid=(n // tm,),
        in_specs=[pl.BlockSpec((tm, D_MODEL), lambda i: (i, 0)),
                  pl.BlockSpec((1, N_MOD, D_MODEL), mod_idx),
                  pl.BlockSpec((1, D_MODEL), lambda i: (0, 0)),
                  pl.BlockSpec((D_MODEL, IN_COLS), lambda i: (0, 0))],
        out_specs=[pl.BlockSpec((tm, COLS_MIX), lambda i: (i, 0)),
                   pl.BlockSpec((tm, COLS_LORA), lambda i: (i, 0)),
                   pl.BlockSpec((tm, COLS_GATE), lambda i: (i, 0))],
        out_shape=[jax.ShapeDtypeStruct((n, COLS_MIX), F32),
                   jax.ShapeDtypeStruct((n, COLS_LORA), F32),
                   jax.ShapeDtypeStruct((n, COLS_GATE), F32)],
        compiler_params=_cparams("parallel"),
        name="in_proj",
    )(x2, mod, norm_g, w_in_bf)


def _rope(x, cos, sin):
    lane = lax.broadcasted_iota(jnp.int32, x.shape, 1)
    first = (lane % 32) < 16
    partner = jnp.where(first, pltpu.roll(x, LANES - 16, 1), pltpu.roll(x, 16, 1))
    return x * cos + partner * sin


def _ret_kernel(*refs, seq, rope, has_init, want_fin):
    it = iter(refs)
    q_ref, k_ref, v_ref, g_ref, rd_ref, gw_ref, gb_ref = (next(it) for _ in range(7))
    cos_ref = next(it) if rope else None
    sin_ref = next(it) if rope else None
    s0_ref = next(it) if has_init else None
    y_ref = next(it)
    fin_ref = next(it) if want_fin else None
    qs_ref, ks_ref, vec_ref, dec_ref, r_ref = (next(it) for _ in range(5))

    C = RET_CHUNK
    W = BRANCH_W
    n = seq // C
    rd = rd_ref[...]
    lg = jnp.minimum(rd, 0.0) - jnp.log(1.0 + jnp.exp(-jnp.abs(rd)))
    lgf, lgb = lg[0:1, :], lg[1:2, :]
    ii = lax.broadcasted_iota(jnp.int32, (C, C), 0)
    jj = lax.broadcasted_iota(jnp.int32, (C, C), 1)
    diff = (ii - jj).astype(F32)
    col = lax.broadcasted_iota(jnp.int32, (C, W), 0).astype(F32)
    ln = lambda h: slice(h * HEAD_DIM, (h + 1) * HEAD_DIM)

    vec_ref[0] = jnp.exp((col + 1.0) * lgf)
    vec_ref[1] = jnp.exp((C - 1.0 - col) * lgf)
    vec_ref[2] = jnp.exp((C - col) * lgb)
    vec_ref[3] = jnp.exp(col * lgb)
    chunk_f = jnp.exp(C * lgf)
    chunk_b = jnp.exp(C * lgb)
    for h in range(HEADS):
        gf = lgf[:, h * HEAD_DIM:h * HEAD_DIM + 1]
        gb = lgb[:, h * HEAD_DIM:h * HEAD_DIM + 1]
        dec_ref[h] = (jnp.where(diff >= 0, jnp.exp(jnp.maximum(diff, 0.0) * gf), 0.0)
                      + jnp.where(diff <= 0, jnp.exp(jnp.maximum(-diff, 0.0) * gb), 0.0))
        for d in range(2):
            r_ref[d, h] = s0_ref[0, d, h] if has_init else jnp.zeros((HEAD_DIM, HEAD_DIM), F32)

    def rows(c):
        return pl.ds(pl.multiple_of(c * C, C), C)

    def prep(c, carry):
        r = rows(c)
        q = q_ref[0, r, :]
        k = k_ref[0, r, :] * (HEAD_DIM ** -0.5)
        if rope:
            cos, sin = cos_ref[r, :], sin_ref[r, :]
            tiles = lambda x: [x[:, j * LANES:(j + 1) * LANES] for j in range(W // LANES)]
            q = jnp.concatenate([_rope(t, cos, sin) for t in tiles(q)], axis=1)
            k = jnp.concatenate([_rope(t, cos, sin) for t in tiles(k)], axis=1)
        qs_ref[r, :] = q
        ks_ref[r, :] = k
        y_ref[0, r, :] = jnp.zeros((C, W), F32)
        return carry

    lax.fori_loop(0, n, prep, 0)

    def step(i, carry):
        rf, rb = rows(i), rows(n - 1 - i)
        qf, kf, vf = qs_ref[rf, :], ks_ref[rf, :], v_ref[0, rf, :]
        qb, kb, vb = qs_ref[rb, :], ks_ref[rb, :], v_ref[0, rb, :]
        qfx, kfx = qf * vec_ref[0], kf * vec_ref[1]
        qbx, kbx = qb * vec_ref[2], kb * vec_ref[3]
        sf = [r_ref[0, h] for h in range(HEADS)]
        sb = [r_ref[1, h] for h in range(HEADS)]
        of, ob, nf, nb = [], [], [], []
        for h in range(HEADS):
            sc = _mm_nt(qf[:, ln(h)], kf[:, ln(h)]) * dec_ref[h]
            of.append(_mm(sc, vf[:, ln(h)]) + _mm(qfx[:, ln(h)], sf[h]))
            nf.append(sf[h] * chunk_f[:, ln(h)] + _mm_tn(kfx[:, ln(h)], vf[:, ln(h)]))
            ob.append(_mm(qbx[:, ln(h)], sb[h]))
            nb.append(sb[h] * chunk_b[:, ln(h)] + _mm_tn(kbx[:, ln(h)], vb[:, ln(h)]))
        yf, yb = y_ref[0, rf, :], y_ref[0, rb, :]
        y_ref[0, rf, :] = yf + jnp.concatenate(of, axis=1)
        y_ref[0, rb, :] = yb + jnp.concatenate(ob, axis=1)
        for h in range(HEADS):
            r_ref[0, h] = nf[h]
            r_ref[1, h] = nb[h]
        return carry

    lax.fori_loop(0, n, step, 0)
    if want_fin:
        for d in range(2):
            for h in range(HEADS):
                fin_ref[0, d, h] = r_ref[d, h]

    def post(c, carry):
        r = rows(c)
        g = g_ref[0, r, :]
        y_ref[0, r, :] = _head_norm(y_ref[0, r, :], gw_ref[...], gb_ref[...]) * (g * _sigmoid(g))
        return carry

    lax.fori_loop(0, n, post, 0)


def _retention(mix3, rd_l, gn_w, gn_b, rope_tabs, s0, want_fin):
    b, seq, _ = mix3.shape
    W = BRANCH_W
    rope = rope_tabs is not None
    has_init = s0 is not None
    col_spec = lambda j: pl.BlockSpec((1, seq, W), lambda i: (i, 0, j))
    full = lambda a: pl.BlockSpec(a.shape, lambda i: (0,) * a.ndim)
    in_specs = [col_spec(0), col_spec(1), col_spec(2), col_spec(3), full(rd_l), full(gn_w), full(gn_b)]
    args = [mix3, mix3, mix3, mix3, rd_l, gn_w, gn_b]
    if rope:
        in_specs += [full(rope_tabs[0]), full(rope_tabs[1])]
        args += list(rope_tabs)
    state_spec = pl.BlockSpec((1, 2, HEADS, HEAD_DIM, HEAD_DIM), lambda i: (i, 0, 0, 0, 0))
    if has_init:
        in_specs.append(state_spec)
        args.append(s0)
    out_specs = [pl.BlockSpec((1, seq, W), lambda i: (i, 0, 0))]
    out_shape = [jax.ShapeDtypeStruct((b, seq, W), F32)]
    if want_fin:
        out_specs.append(state_spec)
        out_shape.append(jax.ShapeDtypeStruct((b, 2, HEADS, HEAD_DIM, HEAD_DIM), F32))
    res = pl.pallas_call(
        functools.partial(_ret_kernel, seq=seq, rope=rope, has_init=has_init, want_fin=want_fin),
        grid=(b,),
        in_specs=in_specs,
        out_specs=out_specs,
        out_shape=out_shape,
        scratch_shapes=[pltpu.VMEM((seq, W), F32), pltpu.VMEM((seq, W), F32),
                        pltpu.VMEM((4, RET_CHUNK, W), F32), pltpu.VMEM((HEADS, RET_CHUNK, RET_CHUNK), F32),
                        pltpu.VMEM((2, HEADS, HEAD_DIM, HEAD_DIM), F32)],
        compiler_params=_cparams("parallel"),
        name="retention",
    )(*args)
    return (res[0], res[1]) if want_fin else (res[0], None)


def _softplus(z):
    return jnp.maximum(z, 0.0) + jnp.log(1.0 + jnp.exp(-jnp.abs(z)))


def _rwkv_kernel(*refs, seq, has_init, want_fin):
    it = iter(refs)
    (r_ref, k_ref, v_ref, lora_ref, cw_ref, w0_ref, w2_ref, a0_ref, a2_ref, g2_ref,
     kk_w_ref, ka_ref, rk_ref, gw_ref, gb_ref) = (next(it) for _ in range(15))
    s0_ref = next(it) if has_init else None
    y_ref = next(it)
    fin_ref = next(it) if want_fin else None
    r_s, v_s, kk_s, g_s, bv_s, lw_s, b_s, kd_s, st_s = (next(it) for _ in range(9))

    C = RWKV_CHUNK
    n = seq // C
    W = BRANCH_W
    row_id = lax.broadcasted_iota(jnp.int32, (C, W), 0)

    def rows(c):
        return pl.ds(pl.multiple_of(c * C, C), C)

    def conv(ref, c, w):
        x = ref[0, rows(c), :]
        prev8 = ref[0, pl.ds(pl.multiple_of(jnp.maximum(c * C - 8, 0), 8), 8), :]
        next8 = ref[0, pl.ds(pl.multiple_of(jnp.minimum(c * C + C, seq - 8), 8), 8), :]
        prev_row = jnp.where(c > 0, prev8[7:8, :], 0.0)
        next_row = jnp.where(c < n - 1, next8[0:1, :], 0.0)
        xm = jnp.where(row_id == 0, prev_row, pltpu.roll(x, 1, 0))
        xp = jnp.where(row_id == C - 1, next_row, pltpu.roll(x, C - 1, 0))
        return w[0:1] * xm + w[1:2] * x + w[2:3] * xp

    def prep(c, carry):
        rws = rows(c)
        cw = cw_ref[...]
        r = conv(r_ref, c, cw[:, 0:W])
        k = conv(k_ref, c, cw[:, W:2 * W])
        v = conv(v_ref, c, cw[:, 2 * W:3 * W])
        lo = lora_ref[0, rws, :]
        dw = lo[:, 0:LORA_W]
        da = lo[:, LORA_W:LORA_W + LORA_A]
        dg = lo[:, LORA_W + LORA_A:]
        kk = k * kk_w_ref[...]
        kk = kk * lax.rsqrt(_head_sum(kk * kk) + 1e-12)
        y_ref[0, rws, :] = jnp.zeros((C, W), F32)
        r_s[rws, :] = r
        v_s[rws, :] = v
        kk_s[rws, :] = kk
        g_s[rws, :] = _mm(_sigmoid(dg), g2_ref[...])
        bv_s[rws, :] = _head_sum(r * k * rk_ref[...]) * v
        tdw = jnp.tanh(dw)
        for d in range(2):
            w_log = -_softplus(-(w0_ref[d:d + 1, :] + _mm(tdw, w2_ref[d]))) - 0.5
            a = _sigmoid(a0_ref[d:d + 1, :] + _mm(da, a2_ref[d]))
            lw_s[d, rws, :] = -jnp.exp(w_log)
            b_s[d, rws, :] = kk * a
            kd_s[d, rws, :] = k * (1.0 + (a - 1.0) * ka_ref[...])
        return carry

    lax.fori_loop(0, n, prep, 0)

    ii = lax.broadcasted_iota(jnp.int32, (C, C), 0)
    jj = lax.broadcasted_iota(jnp.int32, (C, C), 1)
    incl = [ii >= jj, ii <= jj]
    strict = [ii > jj, ii < jj]

    for d in range(2):
        for h in range(HEADS):
            st_s[d, h] = s0_ref[0, d, h] if has_init else jnp.zeros((HEAD_DIM, HEAD_DIM), F32)

    def chunk(i, carry):
        rws, at, bt, kt, rt, bh, kh, etot, vc = ([None] * 2 for _ in range(9))
        for d in range(2):
            rws[d] = rows(i if d == 0 else n - 1 - i)
            rc, vc[d], kkc = r_s[rws[d], :], v_s[rws[d], :], kk_s[rws[d], :]
            lwc, bc, kc = lw_s[d, rws[d], :], b_s[d, rws[d], :], kd_s[d, rws[d], :]
            cum = jnp.dot(incl[d].astype(F32), lwc, precision=lax.Precision.HIGHEST,
                          preferred_element_type=F32)
            tot = cum[C - 1:C, :] if d == 0 else cum[0:1, :]
            pinv = jnp.exp(-cum)
            pend = jnp.exp(tot - cum)
            at[d] = -kkc * jnp.exp(cum - lwc)
            bt[d] = bc * pinv
            kt[d] = kc * pinv
            rt[d] = rc * jnp.exp(cum)
            bh[d] = bc * pend
            kh[d] = kc * pend
            etot[d] = jnp.exp(tot)
        chains = [(d, h) for d in range(2) for h in range(HEADS)]
        ln = lambda h: slice(h * HEAD_DIM, (h + 1) * HEAD_DIM)
        wk = [jnp.concatenate([bt[d][:, ln(h)], kt[d][:, ln(h)]], axis=0) for d, h in chains]
        ma = [_mm_nt(at[d][:, ln(h)], wk[j]) for j, (d, h) in enumerate(chains)]
        mr = [_mm_nt(rt[d][:, ln(h)], wk[j]) for j, (d, h) in enumerate(chains)]
        a_ak = [jnp.where(strict[d], ma[j][:, C:2 * C], 0.0) for j, (d, h) in enumerate(chains)]
        p = [jnp.where(strict[d], ma[j][:, 0:C], 0.0) for j, (d, h) in enumerate(chains)]
        a_rb = [jnp.where(incl[d], mr[j][:, 0:C], 0.0) for j, (d, h) in enumerate(chains)]
        a_rk = [jnp.where(incl[d], mr[j][:, C:2 * C], 0.0) for j, (d, h) in enumerate(chains)]
        vh = [vc[d][:, ln(h)] for d, h in chains]
        x = [jnp.concatenate([at[d][:, ln(h)], _mm(a_ak[j], vh[j])], axis=1)
             for j, (d, h) in enumerate(chains)]
        for step in range(6):
            x = [x[j] + _mm(p[j], x[j]) for j in range(len(chains))]
            if step < 5:
                p = [_mm(p[j], p[j]) for j in range(len(chains))]
        gh = [_mm_tn(x[j], bh[d][:, ln(h)]) for j, (d, h) in enumerate(chains)]
        vk = [_mm_tn(vh[j], kh[d][:, ln(h)]) for j, (d, h) in enumerate(chains)]
        qt = [rt[d][:, ln(h)] + _mm(a_rb[j], x[j][:, 0:HEAD_DIM]) for j, (d, h) in enumerate(chains)]
        y0 = [_mm(a_rb[j], x[j][:, HEAD_DIM:2 * HEAD_DIM]) + _mm(a_rk[j], vh[j]) for j in range(len(chains))]
        ys = []
        for j, (d, h) in enumerate(chains):
            s = st_s[d, h]
            ys.append(_mm_nt(qt[j], s) + y0[j])
            st_s[d, h] = (s * etot[d][:, ln(h)] + _mm(s, gh[j][0:HEAD_DIM, :])
                          + gh[j][HEAD_DIM:2 * HEAD_DIM, :] + vk[j])
        for d in range(2):
            y = jnp.concatenate(ys[d * HEADS:(d + 1) * HEADS], axis=1)
            y_ref[0, rws[d], :] = y_ref[0, rws[d], :] + y
        return carry

    lax.fori_loop(0, n, chunk, 0)
    if want_fin:
        for d in range(2):
            for h in range(HEADS):
                fin_ref[0, d, h] = st_s[d, h]

    def post(c, carry):
        rws = rows(c)
        y = _head_norm(y_ref[0, rws, :], gw_ref[...], gb_ref[...])
        y_ref[0, rws, :] = (y + bv_s[rws, :]) * g_s[rws, :]
        return carry

    lax.fori_loop(0, n, post, 0)


def _rwkv(mix3, lora3, lw, s0, want_fin):
    b, seq, _ = mix3.shape
    has_init = s0 is not None
    W = BRANCH_W
    col_spec = lambda j: pl.BlockSpec((1, seq, W), lambda i: (i, 0, j))
    full = lambda a: pl.BlockSpec(a.shape, lambda i: (0,) * a.ndim)
    weights = [lw['conv'], lw['w0'], lw['w2'], lw['a0'], lw['a2'], lw['g2'],
               lw['k_k'], lw['k_a'], lw['r_k'], lw['gn_w'], lw['gn_b']]
    in_specs = [col_spec(4), col_spec(5), col_spec(6),
                pl.BlockSpec((1, seq, COLS_LORA), lambda i: (i, 0, 0))] + [full(a) for a in weights]
    args = [mix3, mix3, mix3, lora3] + weights
    if has_init:
        in_specs.append(pl.BlockSpec((1, 2, HEADS, HEAD_DIM, HEAD_DIM), lambda i: (i, 0, 0, 0, 0)))
        args.append(s0)
    out_specs = [pl.BlockSpec((1, seq, W), lambda i: (i, 0, 0))]
    out_shape = [jax.ShapeDtypeStruct((b, seq, W), F32)]
    if want_fin:
        out_specs.append(pl.BlockSpec((1, 2, HEADS, HEAD_DIM, HEAD_DIM), lambda i: (i, 0, 0, 0, 0)))
        out_shape.append(jax.ShapeDtypeStruct((b, 2, HEADS, HEAD_DIM, HEAD_DIM), F32))
    sw = pltpu.VMEM((seq, W), F32)
    sw2 = pltpu.VMEM((2, seq, W), F32)
    res = pl.pallas_call(
        functools.partial(_rwkv_kernel, seq=seq, has_init=has_init, want_fin=want_fin),
        grid=(b,),
        in_specs=in_specs,
        out_specs=out_specs,
        out_shape=out_shape,
        scratch_shapes=[sw, sw, sw, sw, sw, sw2, sw2, sw2, pltpu.VMEM((2, HEADS, HEAD_DIM, HEAD_DIM), F32)],
        compiler_params=_cparams("parallel"),
        name="rwkv7",
    )(*args)
    return (res[0], res[1]) if want_fin else (res[0], None)


def _merge_kernel(ya_ref, yb_ref, gate_ref, x_ref, mod_ref, wa_ref, wb_ref, wo_ref, n2_ref, x1_ref, h2_ref):
    m = mod_ref[0]
    br_a = jnp.dot(ya_ref[...].astype(BF16), wa_ref[...], preferred_element_type=F32)
    br_b = jnp.dot(yb_ref[...].astype(BF16), wb_ref[...], preferred_element_type=F32)
    gate = gate_ref[...]
    merged = _sigmoid(gate[:, 0:D_MODEL]) * br_a + _sigmoid(gate[:, D_MODEL:]) * br_b
    mix = jnp.dot(merged.astype(BF16), wo_ref[...], preferred_element_type=F32)
    x1 = x_ref[...] + m[2:3] * mix
    x1_ref[...] = x1
    h2_ref[...] = (_rms(x1, n2_ref[...]) * (1.0 + m[4:5]) + m[3:4]).astype(BF16)


def _merge(ya2, yb2, gate2, x2, mod, mod_base, mod_step, seq, wa, wb, wo, norm2_g):
    n = x2.shape[0]
    tm = 256
    mod_idx = lambda i: (mod_base + mod_step * ((i * tm) // seq), 0, 0)
    row = lambda w: pl.BlockSpec((tm, w), lambda i: (i, 0))
    full = lambda a: pl.BlockSpec(a.shape, lambda i: (0,) * a.ndim)
    return pl.pallas_call(
        _merge_kernel,
        grid=(n // tm,),
        in_specs=[row(BRANCH_W), row(BRANCH_W), row(COLS_GATE), row(D_MODEL),
                  pl.BlockSpec((1, N_MOD, D_MODEL), mod_idx),
                  full(wa), full(wb), full(wo), full(norm2_g)],
        out_specs=[row(D_MODEL), row(D_MODEL)],
        out_shape=[jax.ShapeDtypeStruct((n, D_MODEL), F32), jax.ShapeDtypeStruct((n, D_MODEL), BF16)],
        compiler_params=_cparams("parallel"),
        name="merge_out",
    )(ya2, yb2, gate2, x2, mod, wa, wb, wo, norm2_g)


def _top_values(s, k):
    r, t = s.shape
    rid = lax.broadcasted_iota(jnp.int32, (r, t), 0)
    out = []
    cur = s
    for _ in range(k):
        m = jnp.max(cur, axis=0, keepdims=True)
        first = jnp.min(jnp.where(cur == m, rid, r), axis=0, keepdims=True)
        cur = jnp.where(rid == first, NEG_INF, cur)
        out.append(m)
    return out


def _bitonic_pairs(n):
    pairs = []
    k = 2
    while k <= n:
        j = k // 2
        while j >= 1:
            for i in range(n):
                l = i ^ j
                if l > i:
                    pairs.append((i, l) if (i & k) == 0 else (l, i))
            j //= 2
        k *= 2
    return pairs


def _top_values_of_keys(s, k):
    r, t = s.shape
    nl = r // 8
    lv = [s[8 * j:8 * j + 8, :] for j in range(nl)]
    for a, b in _bitonic_pairs(nl):
        lv[a], lv[b] = jnp.maximum(lv[a], lv[b]), jnp.minimum(lv[a], lv[b])
    sid = lax.broadcasted_iota(jnp.int32, (8, t), 0)
    out = []
    for it in range(k):
        m = jnp.max(lv[0], axis=0, keepdims=True)
        out.append(m)
        first = jnp.min(jnp.where(lv[0] == m, sid, 8), axis=0, keepdims=True)
        pop = sid == first
        for j in range(min(nl, k - 1 - it)):
            lv[j] = jnp.where(pop, lv[j + 1] if j + 1 < nl else NEG_INF, lv[j])
    return out


def _peer_score_kernel(h_ref, wq_ref, keys_ref, thr_ref, g1_ref, s2_ref, e2_ref, cand_ref):
    qt = _mm_nt(wq_ref[...], h_ref[...])
    half = PEER_DQ // 2
    k1 = PEER_TOPK + 1
    pairs = [(i, j) for i in range(k1) for j in range(k1) if (i + 1) * (j + 1) <= k1]
    assert len(pairs) <= cand_ref.shape[0]
    for h in range(PEER_HEADS):
        s = [_mm(keys_ref[2 * h + c], qt[(2 * h + c) * half:(2 * h + c + 1) * half, :]) for c in range(2)]
        tops = [_top_values_of_keys(s[c], k1) for c in range(2)]
        cand_ref[...] = jnp.full(cand_ref.shape, NEG_INF, F32)
        for r, (i, j) in enumerate(pairs):
            cand_ref[r:r + 1, :] = tops[0][i] + tops[1][j]
        best = _top_values(cand_ref[...], k1)
        mx = best[0]
        z = jnp.exp(best[0] - mx)
        for r in range(1, PEER_TOPK):
            z = z + jnp.exp(best[r] - mx)
        theta = 0.5 * (best[PEER_TOPK - 1] + best[PEER_TOPK])
        thr_ref[h] = theta - s[0]
        g1_ref[h] = jnp.exp(s[0] - tops[0][0]) / z
        s2_ref[h] = s[1]
        e2_ref[h] = jnp.exp(s[1] - tops[1][0])


def _peer_scores(h2, wq_t, keys):
    n = h2.shape[0]
    tb = 256
    full = lambda a: pl.BlockSpec(a.shape, lambda i: (0,) * a.ndim)
    return pl.pallas_call(
        _peer_score_kernel,
        grid=(n // tb,),
        in_specs=[pl.BlockSpec((tb, D_MODEL), lambda i: (i, 0)), full(wq_t), full(keys)],
        out_specs=[pl.BlockSpec((PEER_HEADS, N_KEYS, tb), lambda i: (0, 0, i))] * 4,
        out_shape=[jax.ShapeDtypeStruct((PEER_HEADS, N_KEYS, n), F32)] * 4,
        scratch_shapes=[pltpu.VMEM((56, tb), F32)],
        compiler_params=_cparams("parallel"),
        name="peer_scores",
    )(h2, wq_t, keys)


def _gelu_tanh(x):
    return 0.5 * x * (1.0 + jnp.tanh(0.7978845608028654 * (x + 0.044715 * x * x * x)))


def _peer_expert_kernel(h_ref, u_ref, vt_ref, thr_ref, g1_ref, s2_ref, e2_ref, x1_ref, mod_ref, fg_ref, y_ref,
                        acc_ref, st_ref, w_ref, *, rows_per_step):
    e = pl.program_id(1)
    tb = h_ref.shape[0]

    @pl.when(e == 0)
    def _():
        acc_ref[...] = jnp.zeros_like(acc_ref)

    for al in range(rows_per_step):
        a = e * rows_per_step + al
        r0 = al * N_KEYS
        thr = [thr_ref[h, pl.ds(a, 1), :] for h in range(PEER_HEADS)]
        g1 = [g1_ref[h, pl.ds(a, 1), :] for h in range(PEER_HEADS)]
        sub = 32
        for lt in range(tb // LANES):
            ls = slice(lt * LANES, (lt + 1) * LANES)
            for sb in range(N_KEYS // sub):
                bs = slice(sb * sub, (sb + 1) * sub)
                wsum = None
                for h in range(PEER_HEADS):
                    w = jnp.where(s2_ref[h, bs, ls] >= thr[h][:, ls], e2_ref[h, bs, ls] * g1[h][:, ls], 0.0)
                    wsum = w if wsum is None else wsum + w
                st_ref[r0 + sb * sub:r0 + (sb + 1) * sub, ls] = wsum

    st = _mm_nt(u_ref[...], h_ref[...])
    partial = None
    rows_per_dot = 2
    for al in range(rows_per_step):
        r0 = al * N_KEYS
        for lt in range(tb // LANES):
            ls = slice(lt * LANES, (lt + 1) * LANES)
            act = _gelu_tanh(st[r0:r0 + N_KEYS, ls])
            w_ref[r0:r0 + N_KEYS, ls] = (st_ref[r0:r0 + N_KEYS, ls] * act).astype(BF16)
        if (al + 1) % rows_per_dot == 0:
            k0 = (al + 1 - rows_per_dot) * N_KEYS
            k1 = (al + 1) * N_KEYS
            d = jnp.dot(vt_ref[:, k0:k1], w_ref[k0:k1, :], preferred_element_type=F32)
            partial = d if partial is None else partial + d
    acc_ref[...] += partial

    @pl.when(e == pl.num_programs(1) - 1)
    def _():
        m = mod_ref[0]
        x2 = x1_ref[...] + m[5:6] * acc_ref[...].T
        y_ref[...] = _rms(x2, fg_ref[...])


def _peer_experts(h2, u_bf, vt_bf, stats, x1, mod, mod_base, mod_step, seq, final_g):
    n = h2.shape[0]
    tb = 512 if (mod_step == 0 or seq % 512 == 0) else seq
    rows_per_step = 8
    ec = rows_per_step * N_KEYS
    mod_idx = lambda i, e: (mod_base + mod_step * ((i * tb) // seq), 0, 0)
    stat_spec = pl.BlockSpec((PEER_HEADS, N_KEYS, tb), lambda i, e: (0, 0, i))
    return pl.pallas_call(
        functools.partial(_peer_expert_kernel, rows_per_step=rows_per_step),
        grid=(n // tb, N_EXPERTS // ec),
        in_specs=[pl.BlockSpec((tb, D_MODEL), lambda i, e: (i, 0)),
                  pl.BlockSpec((ec, D_MODEL), lambda i, e: (e, 0)),
                  pl.BlockSpec((D_MODEL, ec), lambda i, e: (0, e)),
                  stat_spec, stat_spec, stat_spec, stat_spec,
                  pl.BlockSpec((tb, D_MODEL), lambda i, e: (i, 0)),
                  pl.BlockSpec((1, N_MOD, D_MODEL), mod_idx),
                  pl.BlockSpec((1, D_MODEL), lambda i, e: (0, 0))],
        out_specs=pl.BlockSpec((tb, D_MODEL), lambda i, e: (i, 0)),
        out_shape=jax.ShapeDtypeStruct((n, D_MODEL), F32),
        scratch_shapes=[pltpu.VMEM((D_MODEL, tb), F32), pltpu.VMEM((ec, tb), F32), pltpu.VMEM((ec, tb), BF16)],
        compiler_params=_cparams("parallel", "arbitrary"),
        name="peer_experts",
    )(h2, u_bf, vt_bf, *stats, x1, mod, final_g)


def _rope_tables(seq):
    pos = jnp.arange(seq, dtype=jnp.int32)
    lane = jnp.arange(LANES, dtype=jnp.int32) % HEAD_DIM
    use_col = (lane // 32) == 1
    p = jnp.where(use_col[None, :], (pos % GRID_W)[:, None], (pos // GRID_W)[:, None]).astype(F32)
    inv = ROPE_BASE ** (-(lane % 16).astype(F32) / 16.0)
    ang = p * inv[None, :]
    first = (lane % 32) < 16
    return jnp.cos(ang), jnp.where(first[None, :], -jnp.sin(ang), jnp.sin(ang))


def _trunk_path(x, mod, mod_base, mod_step, rope_tabs, s_ret0, s_rwkv0, want_fin, lw, final_g):
    b, seq, _ = x.shape
    x2 = x.reshape(b * seq, D_MODEL)
    mix, lora, gate = _in_proj(x2, mod, mod_base, mod_step, seq, lw['norm1_g'], lw['w_in'])
    mix3 = mix.reshape(b, seq, COLS_MIX)
    lora3 = lora.reshape(b, seq, COLS_LORA)
    ya, ret_fin = _retention(mix3, lw['ret_decay'], lw['ret_gn_w'], lw['ret_gn_b'], rope_tabs, s_ret0, want_fin)
    yb, rwkv_fin = _rwkv(mix3, lora3, lw['rwkv'], s_rwkv0, want_fin)
    x1, h2 = _merge(ya.reshape(b * seq, BRANCH_W), yb.reshape(b * seq, BRANCH_W), gate, x2, mod, mod_base,
                    mod_step, seq, lw['w_br_a'], lw['w_br_b'], lw['w_out'], lw['norm2_g'])
    stats = _peer_scores(h2, lw['peer_wq_t'], lw['peer_keys'])
    y = _peer_experts(h2, lw['peer_u'], lw['peer_vt'], stats, x1, mod, mod_base, mod_step, seq, final_g)
    return y.reshape(b, seq, D_MODEL), x1.reshape(b, seq, D_MODEL), ret_fin, rwkv_fin


def kernel(x_prompt, x_sample, state_ret, state_rwkv, c, c_ctx, ada_w, ada_b, norm1_g, w_in, ret_decay, ret_gn_w, ret_gn_b, rwkv_conv, rwkv_w0, rwkv_w2, rwkv_a0, rwkv_a2, rwkv_g2, rwkv_k_k, rwkv_k_a, rwkv_r_k, rwkv_gn_w, rwkv_gn_b, w_br_a, w_br_b, w_out, norm2_g, peer_wq, peer_keys, peer_u, peer_v, final_norm_g):
    assert w_in.shape[0] == 1, "the final norm is fused into the layer's last kernel: single trunk layer only"
    row = lambda a: a.reshape(1, -1)
    cc = jnp.concatenate([c_ctx[None, :], c], axis=0)
    cc = jnp.pad(cc, ((0, (-cc.shape[0]) % 8), (0, 0)))
    rope_tabs = _rope_tables(x_sample.shape[1])
    final_g = row(final_norm_g)

    if True:
        l = 0
        lw = {
            'norm1_g': row(norm1_g[l]), 'w_in': w_in[l].astype(BF16),
            'ret_decay': jnp.repeat(ret_decay[l], HEAD_DIM, axis=1),
            'ret_gn_w': row(ret_gn_w[l]), 'ret_gn_b': row(ret_gn_b[l]),
            'rwkv': {'conv': rwkv_conv[l], 'w0': rwkv_w0[l], 'w2': rwkv_w2[l].astype(BF16), 'a0': rwkv_a0[l],
                     'a2': rwkv_a2[l].astype(BF16), 'g2': rwkv_g2[l].astype(BF16), 'k_k': row(rwkv_k_k[l]),
                     'k_a': row(rwkv_k_a[l]), 'r_k': row(rwkv_r_k[l]), 'gn_w': row(rwkv_gn_w[l]),
                     'gn_b': row(rwkv_gn_b[l])},
            'w_br_a': w_br_a[l].astype(BF16), 'w_br_b': w_br_b[l].astype(BF16), 'w_out': w_out[l].astype(BF16),
            'norm2_g': row(norm2_g[l]),
            'peer_wq_t': peer_wq[l].T.astype(BF16),
            'peer_keys': peer_keys[l].reshape(2 * PEER_HEADS, N_KEYS, PEER_DQ // 2).astype(BF16),
            'peer_u': peer_u[l].astype(BF16), 'peer_vt': peer_v[l].T.astype(BF16),
        }
        mod = _adaln(cc, ada_w[l], row(ada_b[l])).reshape(cc.shape[0], N_MOD, D_MODEL)
        yp, _, ret_fin, rwkv_fin = _trunk_path(x_prompt, mod, 0, 0, None, None, None, True, lw, final_g)
        ys, _, _, _ = _trunk_path(x_sample, mod, 1, 1, rope_tabs, state_ret[:, l], state_rwkv[:, l], False, lw,
                                  final_g)
    return (yp, ys, ret_fin[:, None], rwkv_fin[:, None])
```

```python
import functools

import jax
import jax.numpy as jnp
from jax import lax
from jax.experimental import pallas as pl
from jax.experimental.pallas import tpu as pltpu

F32 = jnp.float32
BF16 = jnp.bfloat16

D_MODEL = 1024
GRID_W = 64
N_MOD = 6
HEADS = 8
HEAD_DIM = 64
BRANCH_W = HEADS * HEAD_DIM
RET_CHUNK = 128
RWKV_CHUNK = 64
LORA_W = 64
LORA_A = 64
LORA_G = 128
PEER_HEADS = 8
N_KEYS = 128
N_EXPERTS = N_KEYS * N_KEYS
PEER_DQ = 256
PEER_TOPK = 16
ROPE_BASE = 10000.0
NORM_EPS = 1e-6
GN_EPS = 64e-5
COLS_MIX = 7 * BRANCH_W
COLS_LORA = LORA_W + LORA_A + LORA_G
COLS_GATE = 2 * D_MODEL
IN_COLS = COLS_MIX + COLS_LORA + COLS_GATE

V7X_VMEM_LIMIT_BYTES = 56 * 1024 * 1024
LANES = 128
NEG_INF = float("-inf")


def _cparams(*sem):
    return pltpu.CompilerParams(dimension_semantics=sem, vmem_limit_bytes=V7X_VMEM_LIMIT_BYTES)


def _mm(a, b):
    return jnp.dot(a.astype(BF16), b.astype(BF16), preferred_element_type=F32)


def _mm_nt(a, b):
    return lax.dot_general(a.astype(BF16), b.astype(BF16), (((1,), (1,)), ((), ())),
                           preferred_element_type=F32)


def _mm_tn(a, b):
    return lax.dot_general(a.astype(BF16), b.astype(BF16), (((0,), (0,)), ((), ())),
                           preferred_element_type=F32)


def _mm_tn_3pass(a, b):
    a_hi, b_hi = a.astype(BF16), b.astype(BF16)
    a_lo = (a - a_hi.astype(F32)).astype(BF16)
    b_lo = (b - b_hi.astype(F32)).astype(BF16)
    dims = (((0,), (0,)), ((), ()))
    dot = lambda x, y: lax.dot_general(x, y, dims, preferred_element_type=F32)
    return dot(a_hi, b_hi) + (dot(a_hi, b_lo) + dot(a_lo, b_hi))


def _sigmoid(x):
    return 1.0 / (1.0 + jnp.exp(-x))


def _rms(x, g):
    return x * lax.rsqrt(jnp.mean(x * x, axis=-1, keepdims=True) + NORM_EPS) * g


def _head_sum(x):
    t, w = x.shape
    lo = lax.broadcasted_iota(jnp.int32, (t, LANES), 1) < HEAD_DIM
    outs = []
    for j in range(w // LANES):
        xt = x[:, j * LANES:(j + 1) * LANES]
        s_lo = jnp.sum(jnp.where(lo, xt, 0.0), axis=-1, keepdims=True)
        s_hi = jnp.sum(jnp.where(lo, 0.0, xt), axis=-1, keepdims=True)
        outs.append(jnp.where(lo, s_lo, s_hi))
    return outs[0] if len(outs) == 1 else jnp.concatenate(outs, axis=-1)


def _head_norm(y, w, b):
    mu = _head_sum(y) * (1.0 / HEAD_DIM)
    d = y - mu
    var = _head_sum(d * d) * (1.0 / HEAD_DIM)
    return d * lax.rsqrt(var + GN_EPS) * w + b


def _mod_kernel(c_ref, w_ref, b_ref, o_ref):
    c = c_ref[...]
    o_ref[...] = _mm(c * _sigmoid(c), w_ref[...]) + b_ref[...]


def _adaln(cc, ada_w, ada_b):
    rows = cc.shape[0]
    n = ada_w.shape[1]
    tn = n // 4
    return pl.pallas_call(
        _mod_kernel,
        grid=(n // tn,),
        in_specs=[pl.BlockSpec((rows, D_MODEL), lambda j: (0, 0)),
                  pl.BlockSpec((D_MODEL, tn), lambda j: (0, j)),
                  pl.BlockSpec((1, tn), lambda j: (0, j))],
        out_specs=pl.BlockSpec((rows, tn), lambda j: (0, j)),
        out_shape=jax.ShapeDtypeStruct((rows, n), F32),
        compiler_params=_cparams("parallel"),
        name="adaln_mod",
    )(cc, ada_w, ada_b)


STATE_COL0 = 5 * BRANCH_W


def _in_kernel(x_ref, mod_ref, g_ref, w_ref, wlo_ref, mix_ref, lora_ref, gate_ref):
    m = mod_ref[0]
    hf = _rms(x_ref[...], g_ref[...]) * (1.0 + m[1:2]) + m[0:1]
    h = hf.astype(BF16)
    mix_ref[...] = jnp.dot(h, w_ref[:, 0:COLS_MIX], preferred_element_type=F32)
    lora_ref[...] = jnp.dot(h, w_ref[:, COLS_MIX:COLS_MIX + COLS_LORA], preferred_element_type=F32)
    gate_ref[...] = jnp.dot(h, w_ref[:, COLS_MIX + COLS_LORA:IN_COLS], preferred_element_type=F32)
    h_lo = (hf - h.astype(F32)).astype(BF16)
    corr = (jnp.dot(h_lo, w_ref[:, STATE_COL0:COLS_MIX + COLS_LORA], preferred_element_type=F32)
            + jnp.dot(h, wlo_ref[...], preferred_element_type=F32))
    mix_ref[:, STATE_COL0:COLS_MIX] += corr[:, 0:COLS_MIX - STATE_COL0]
    lora_ref[...] += corr[:, COLS_MIX - STATE_COL0:]


def _bf16_low_part(w):
    return (w - w.astype(BF16).astype(F32)).astype(BF16)


def _in_proj(x2, mod, mod_base, mod_step, seq, norm_g, w_in_bf, w_in_lo):
    n = x2.shape[0]
    tm = 256
    mod_idx = lambda i: (mod_base + mod_step * ((i * tm) // seq), 0, 0)
    return pl.pallas_call(
        _in_kernel,
        grid=(n // tm,),
        in_specs=[pl.BlockSpec((tm, D_MODEL), lambda i: (i, 0)),
                  pl.BlockSpec((1, N_MOD, D_MODEL), mod_idx),
                  pl.BlockSpec((1, D_MODEL), lambda i: (0, 0)),
                  pl.BlockSpec((D_MODEL, IN_COLS), lambda i: (0, 0)),
                  pl.BlockSpec(w_in_lo.shape, lambda i: (0, 0))],
        out_specs=[pl.BlockSpec((tm, COLS_MIX), lambda i: (i, 0)),
                   pl.BlockSpec((tm, COLS_LORA), lambda i: (i, 0)),
                   pl.BlockSpec((tm, COLS_GATE), lambda i: (i, 0))],
        out_shape=[jax.ShapeDtypeStruct((n, COLS_MIX), F32),
                   jax.ShapeDtypeStruct((n, COLS_LORA), F32),
                   jax.ShapeDtypeStruct((n, COLS_GATE), F32)],
        compiler_params=_cparams("parallel"),
        name="in_proj",
    )(x2, mod, norm_g, w_in_bf, w_in_lo)


def _rope(x, cos, sin):
    lane = lax.broadcasted_iota(jnp.int32, x.shape, 1)
    first = (lane % 32) < 16
    partner = jnp.where(first, pltpu.roll(x, LANES - 16, 1), pltpu.roll(x, 16, 1))
    return x * cos + partner * sin


def _ret_kernel(*refs, seq, rope, has_init, want_fin):
    it = iter(refs)
    q_ref, k_ref, v_ref, g_ref, rd_ref, gw_ref, gb_ref = (next(it) for _ in range(7))
    cos_ref = next(it) if rope else None
    sin_ref = next(it) if rope else None
    s0_ref = next(it) if has_init else None
    y_ref = next(it)
    fin_ref = next(it) if want_fin else None
    qs_ref, ks_ref, vec_ref, dec_ref, r_ref = (next(it) for _ in range(5))

    C = RET_CHUNK
    W = BRANCH_W
    n = seq // C
    rd = rd_ref[...]
    lg = jnp.minimum(rd, 0.0) - jnp.log(1.0 + jnp.exp(-jnp.abs(rd)))
    lgf, lgb = lg[0:1, :], lg[1:2, :]
    ii = lax.broadcasted_iota(jnp.int32, (C, C), 0)
    jj = lax.broadcasted_iota(jnp.int32, (C, C), 1)
    diff = (ii - jj).astype(F32)
    col = lax.broadcasted_iota(jnp.int32, (C, W), 0).astype(F32)
    ln = lambda h: slice(h * HEAD_DIM, (h + 1) * HEAD_DIM)

    vec_ref[0] = jnp.exp((col + 1.0) * lgf)
    vec_ref[1] = jnp.exp((C - 1.0 - col) * lgf)
    vec_ref[2] = jnp.exp((C - col) * lgb)
    vec_ref[3] = jnp.exp(col * lgb)
    chunk_f = jnp.exp(C * lgf)
    chunk_b = jnp.exp(C * lgb)
    for h in range(HEADS):
        gf = lgf[:, h * HEAD_DIM:h * HEAD_DIM + 1]
        gb = lgb[:, h * HEAD_DIM:h * HEAD_DIM + 1]
        dec_ref[h] = (jnp.where(diff >= 0, jnp.exp(jnp.maximum(diff, 0.0) * gf), 0.0)
                      + jnp.where(diff <= 0, jnp.exp(jnp.maximum(-diff, 0.0) * gb), 0.0))
        for d in range(2):
            r_ref[d, h] = s0_ref[0, d, h] if has_init else jnp.zeros((HEAD_DIM, HEAD_DIM), F32)

    def rows(c):
        return pl.ds(pl.multiple_of(c * C, C), C)

    def prep(c, carry):
        r = rows(c)
        q = q_ref[0, r, :]
        k = k_ref[0, r, :] * (HEAD_DIM ** -0.5)
        if rope:
            cos, sin = cos_ref[r, :], sin_ref[r, :]
            tiles = lambda x: [x[:, j * LANES:(j + 1) * LANES] for j in range(W // LANES)]
            q = jnp.concatenate([_rope(t, cos, sin) for t in tiles(q)], axis=1)
            k = jnp.concatenate([_rope(t, cos, sin) for t in tiles(k)], axis=1)
        qs_ref[r, :] = q
        ks_ref[r, :] = k
        y_ref[0, r, :] = jnp.zeros((C, W), F32)
        return carry

    lax.fori_loop(0, n, prep, 0)

    def step(i, carry):
        rf, rb = rows(i), rows(n - 1 - i)
        qf, kf, vf = qs_ref[rf, :], ks_ref[rf, :], v_ref[0, rf, :]
        qb, kb, vb = qs_ref[rb, :], ks_ref[rb, :], v_ref[0, rb, :]
        qfx, kfx = qf * vec_ref[0], kf * vec_ref[1]
        qbx, kbx = qb * vec_ref[2], kb * vec_ref[3]
        sf = [r_ref[0, h] for h in range(HEADS)]
        sb = [r_ref[1, h] for h in range(HEADS)]
        of, ob, nf, nb = [], [], [], []
        for h in range(HEADS):
            sc = _mm_nt(qf[:, ln(h)], kf[:, ln(h)]) * dec_ref[h]
            of.append(_mm(sc, vf[:, ln(h)]) + _mm(qfx[:, ln(h)], sf[h]))
            nf.append(sf[h] * chunk_f[:, ln(h)] + _mm_tn(kfx[:, ln(h)], vf[:, ln(h)]))
            ob.append(_mm(qbx[:, ln(h)], sb[h]))
            nb.append(sb[h] * chunk_b[:, ln(h)] + _mm_tn(kbx[:, ln(h)], vb[:, ln(h)]))
        yf, yb = y_ref[0, rf, :], y_ref[0, rb, :]
        y_ref[0, rf, :] = yf + jnp.concatenate(of, axis=1)
        y_ref[0, rb, :] = yb + jnp.concatenate(ob, axis=1)
        for h in range(HEADS):
            r_ref[0, h] = nf[h]
            r_ref[1, h] = nb[h]
        return carry

    lax.fori_loop(0, n, step, 0)
    if want_fin:
        for d in range(2):
            for h in range(HEADS):
                fin_ref[0, d, h] = r_ref[d, h]

    def post(c, carry):
        r = rows(c)
        g = g_ref[0, r, :]
        y_ref[0, r, :] = _head_norm(y_ref[0, r, :], gw_ref[...], gb_ref[...]) * (g * _sigmoid(g))
        return carry

    lax.fori_loop(0, n, post, 0)


def _retention(mix3, rd_l, gn_w, gn_b, rope_tabs, s0, want_fin):
    b, seq, _ = mix3.shape
    W = BRANCH_W
    rope = rope_tabs is not None
    has_init = s0 is not None
    col_spec = lambda j: pl.BlockSpec((1, seq, W), lambda i: (i, 0, j))
    full = lambda a: pl.BlockSpec(a.shape, lambda i: (0,) * a.ndim)
    in_specs = [col_spec(0), col_spec(1), col_spec(2), col_spec(3), full(rd_l), full(gn_w), full(gn_b)]
    args = [mix3, mix3, mix3, mix3, rd_l, gn_w, gn_b]
    if rope:
        in_specs += [full(rope_tabs[0]), full(rope_tabs[1])]
        args += list(rope_tabs)
    state_spec = pl.BlockSpec((1, 2, HEADS, HEAD_DIM, HEAD_DIM), lambda i: (i, 0, 0, 0, 0))
    if has_init:
        in_specs.append(state_spec)
        args.append(s0)
    out_specs = [pl.BlockSpec((1, seq, W), lambda i: (i, 0, 0))]
    out_shape = [jax.ShapeDtypeStruct((b, seq, W), F32)]
    if want_fin:
        out_specs.append(state_spec)
        out_shape.append(jax.ShapeDtypeStruct((b, 2, HEADS, HEAD_DIM, HEAD_DIM), F32))
    res = pl.pallas_call(
        functools.partial(_ret_kernel, seq=seq, rope=rope, has_init=has_init, want_fin=want_fin),
        grid=(b,),
        in_specs=in_specs,
        out_specs=out_specs,
        out_shape=out_shape,
        scratch_shapes=[pltpu.VMEM((seq, W), F32), pltpu.VMEM((seq, W), F32),
                        pltpu.VMEM((4, RET_CHUNK, W), F32), pltpu.VMEM((HEADS, RET_CHUNK, RET_CHUNK), F32),
                        pltpu.VMEM((2, HEADS, HEAD_DIM, HEAD_DIM), F32)],
        compiler_params=_cparams("parallel"),
        name="retention",
    )(*args)
    return (res[0], res[1]) if want_fin else (res[0], None)


def _softplus(z):
    return jnp.maximum(z, 0.0) + jnp.log(1.0 + jnp.exp(-jnp.abs(z)))


def _rwkv_kernel(*refs, seq, has_init, want_fin):
    it = iter(refs)
    (r_ref, k_ref, v_ref, lora_ref, cw_ref, w0_ref, w2_ref, a0_ref, a2_ref, g2_ref,
     kk_w_ref, ka_ref, rk_ref, gw_ref, gb_ref) = (next(it) for _ in range(15))
    s0_ref = next(it) if has_init else None
    y_ref = next(it)
    fin_ref = next(it) if want_fin else None
    r_s, v_s, kk_s, g_s, bv_s, lw_s, b_s, kd_s, st_s = (next(it) for _ in range(9))

    C = RWKV_CHUNK
    n = seq // C
    W = BRANCH_W
    row_id = lax.broadcasted_iota(jnp.int32, (C, W), 0)

    def rows(c):
        return pl.ds(pl.multiple_of(c * C, C), C)

    def conv(ref, c, w):
        x = ref[0, rows(c), :]
        prev8 = ref[0, pl.ds(pl.multiple_of(jnp.maximum(c * C - 8, 0), 8), 8), :]
        next8 = ref[0, pl.ds(pl.multiple_of(jnp.minimum(c * C + C, seq - 8), 8), 8), :]
        prev_row = jnp.where(c > 0, prev8[7:8, :], 0.0)
        next_row = jnp.where(c < n - 1, next8[0:1, :], 0.0)
        xm = jnp.where(row_id == 0, prev_row, pltpu.roll(x, 1, 0))
        xp = jnp.where(row_id == C - 1, next_row, pltpu.roll(x, C - 1, 0))
        return w[0:1] * xm + w[1:2] * x + w[2:3] * xp

    def prep(c, carry):
        rws = rows(c)
        cw = cw_ref[...]
        r = conv(r_ref, c, cw[:, 0:W])
        k = conv(k_ref, c, cw[:, W:2 * W])
        v = conv(v_ref, c, cw[:, 2 * W:3 * W])
        lo = lora_ref[0, rws, :]
        dw = lo[:, 0:LORA_W]
        da = lo[:, LORA_W:LORA_W + LORA_A]
        dg = lo[:, LORA_W + LORA_A:]
        kk = k * kk_w_ref[...]
        kk = kk * lax.rsqrt(_head_sum(kk * kk) + 1e-12)
        y_ref[0, rws, :] = jnp.zeros((C, W), F32)
        r_s[rws, :] = r
        v_s[rws, :] = v
        kk_s[rws, :] = kk
        g_s[rws, :] = _mm(_sigmoid(dg), g2_ref[...])
        bv_s[rws, :] = _head_sum(r * k * rk_ref[...]) * v
        tdw = jnp.tanh(dw)
        for d in range(2):
            w_log = -_softplus(-(w0_ref[d:d + 1, :] + _mm(tdw, w2_ref[d]))) - 0.5
            a = _sigmoid(a0_ref[d:d + 1, :] + _mm(da, a2_ref[d]))
            lw_s[d, rws, :] = -jnp.exp(w_log)
            b_s[d, rws, :] = kk * a
            kd_s[d, rws, :] = k * (1.0 + (a - 1.0) * ka_ref[...])
        return carry

    lax.fori_loop(0, n, prep, 0)

    ii = lax.broadcasted_iota(jnp.int32, (C, C), 0)
    jj = lax.broadcasted_iota(jnp.int32, (C, C), 1)
    incl = [ii >= jj, ii <= jj]
    strict = [ii > jj, ii < jj]

    for d in range(2):
        for h in range(HEADS):
            st_s[d, h] = s0_ref[0, d, h] if has_init else jnp.zeros((HEAD_DIM, HEAD_DIM), F32)

    def chunk(i, carry):
        rws, at, bt, kt, rt, bh, kh, etot, vc = ([None] * 2 for _ in range(9))
        for d in range(2):
            rws[d] = rows(i if d == 0 else n - 1 - i)
            rc, vc[d], kkc = r_s[rws[d], :], v_s[rws[d], :], kk_s[rws[d], :]
            lwc, bc, kc = lw_s[d, rws[d], :], b_s[d, rws[d], :], kd_s[d, rws[d], :]
            cum = jnp.dot(incl[d].astype(F32), lwc, precision=lax.Precision.HIGHEST,
                          preferred_element_type=F32)
            tot = cum[C - 1:C, :] if d == 0 else cum[0:1, :]
            pinv = jnp.exp(-cum)
            pend = jnp.exp(tot - cum)
            at[d] = -kkc * jnp.exp(cum - lwc)
            bt[d] = bc * pinv
            kt[d] = kc * pinv
            rt[d] = rc * jnp.exp(cum)
            bh[d] = bc * pend
            kh[d] = kc * pend
            etot[d] = jnp.exp(tot)
        chains = [(d, h) for d in range(2) for h in range(HEADS)]
        ln = lambda h: slice(h * HEAD_DIM, (h + 1) * HEAD_DIM)
        wk = [jnp.concatenate([bt[d][:, ln(h)], kt[d][:, ln(h)]], axis=0) for d, h in chains]
        ma = [_mm_nt(at[d][:, ln(h)], wk[j]) for j, (d, h) in enumerate(chains)]
        mr = [_mm_nt(rt[d][:, ln(h)], wk[j]) for j, (d, h) in enumerate(chains)]
        a_ak = [jnp.where(strict[d], ma[j][:, C:2 * C], 0.0) for j, (d, h) in enumerate(chains)]
        p = [jnp.where(strict[d], ma[j][:, 0:C], 0.0) for j, (d, h) in enumerate(chains)]
        a_rb = [jnp.where(incl[d], mr[j][:, 0:C], 0.0) for j, (d, h) in enumerate(chains)]
        a_rk = [jnp.where(incl[d], mr[j][:, C:2 * C], 0.0) for j, (d, h) in enumerate(chains)]
        vh = [vc[d][:, ln(h)] for d, h in chains]
        x = [jnp.concatenate([at[d][:, ln(h)], _mm(a_ak[j], vh[j])], axis=1)
             for j, (d, h) in enumerate(chains)]
        for step in range(6):
            x = [x[j] + _mm(p[j], x[j]) for j in range(len(chains))]
            if step < 5:
                p = [_mm(p[j], p[j]) for j in range(len(chains))]
        gh = [_mm_tn(x[j], bh[d][:, ln(h)]) for j, (d, h) in enumerate(chains)]
        vk = [_mm_tn_3pass(vh[j], kh[d][:, ln(h)]) for j, (d, h) in enumerate(chains)]
        ax = [_mm(a_rb[j], x[j]) for j in range(len(chains))]
        qt = [rt[d][:, ln(h)] + ax[j][:, 0:HEAD_DIM] for j, (d, h) in enumerate(chains)]
        y0 = [ax[j][:, HEAD_DIM:2 * HEAD_DIM] + _mm(a_rk[j], vh[j]) for j in range(len(chains))]
        ys = []
        for j, (d, h) in enumerate(chains):
            s = st_s[d, h]
            ys.append(_mm_nt(qt[j], s) + y0[j])
            st_s[d, h] = (s * etot[d][:, ln(h)] + _mm(s, gh[j][0:HEAD_DIM, :])
                          + gh[j][HEAD_DIM:2 * HEAD_DIM, :] + vk[j])
        for d in range(2):
            y = jnp.concatenate(ys[d * HEADS:(d + 1) * HEADS], axis=1)
            y_ref[0, rws[d], :] = y_ref[0, rws[d], :] + y
        return carry

    lax.fori_loop(0, n, chunk, 0)
    if want_fin:
        for d in range(2):
            for h in range(HEADS):
                fin_ref[0, d, h] = st_s[d, h]

    def post(c, carry):
        rws = rows(c)
        y = _head_norm(y_ref[0, rws, :], gw_ref[...], gb_ref[...])
        y_ref[0, rws, :] = (y + bv_s[rws, :]) * g_s[rws, :]
        return carry

    lax.fori_loop(0, n, post, 0)


def _rwkv(mix3, lora3, lw, s0, want_fin):
    b, seq, _ = mix3.shape
    has_init = s0 is not None
    W = BRANCH_W
    col_spec = lambda j: pl.BlockSpec((1, seq, W), lambda i: (i, 0, j))
    full = lambda a: pl.BlockSpec(a.shape, lambda i: (0,) * a.ndim)
    weights = [lw['conv'], lw['w0'], lw['w2'], lw['a0'], lw['a2'], lw['g2'],
               lw['k_k'], lw['k_a'], lw['r_k'], lw['gn_w'], lw['gn_b']]
    in_specs = [col_spec(4), col_spec(5), col_spec(6),
                pl.BlockSpec((1, seq, COLS_LORA), lambda i: (i, 0, 0))] + [full(a) for a in weights]
    args = [mix3, mix3, mix3, lora3] + weights
    if has_init:
        in_specs.append(pl.BlockSpec((1, 2, HEADS, HEAD_DIM, HEAD_DIM), lambda i: (i, 0, 0, 0, 0)))
        args.append(s0)
    out_specs = [pl.BlockSpec((1, seq, W), lambda i: (i, 0, 0))]
    out_shape = [jax.ShapeDtypeStruct((b, seq, W), F32)]
    if want_fin:
        out_specs.append(pl.BlockSpec((1, 2, HEADS, HEAD_DIM, HEAD_DIM), lambda i: (i, 0, 0, 0, 0)))
        out_shape.append(jax.ShapeDtypeStruct((b, 2, HEADS, HEAD_DIM, HEAD_DIM), F32))
    sw = pltpu.VMEM((seq, W), F32)
    sw2 = pltpu.VMEM((2, seq, W), F32)
    res = pl.pallas_call(
        functools.partial(_rwkv_kernel, seq=seq, has_init=has_init, want_fin=want_fin),
        grid=(b,),
        in_specs=in_specs,
        out_specs=out_specs,
        out_shape=out_shape,
        scratch_shapes=[sw, sw, sw, sw, sw, sw2, sw2, sw2, pltpu.VMEM((2, HEADS, HEAD_DIM, HEAD_DIM), F32)],
        compiler_params=_cparams("parallel"),
        name="rwkv7",
    )(*args)
    return (res[0], res[1]) if want_fin else (res[0], None)


def _merge_kernel(ya_ref, yb_ref, gate_ref, x_ref, mod_ref, wa_ref, wb_ref, wo_ref, n2_ref, x1_ref, h2_ref):
    m = mod_ref[0]
    br_a = jnp.dot(ya_ref[...].astype(BF16), wa_ref[...], preferred_element_type=F32)
    br_b = jnp.dot(yb_ref[...].astype(BF16), wb_ref[...], preferred_element_type=F32)
    gate = gate_ref[...]
    merged = _sigmoid(gate[:, 0:D_MODEL]) * br_a + _sigmoid(gate[:, D_MODEL:]) * br_b
    mix = jnp.dot(merged.astype(BF16), wo_ref[...], preferred_element_type=F32)
    x1 = x_ref[...] + m[2:3] * mix
    x1_ref[...] = x1
    h2_ref[...] = (_rms(x1, n2_ref[...]) * (1.0 + m[4:5]) + m[3:4]).astype(BF16)


def _merge(ya2, yb2, gate2, x2, mod, mod_base, mod_step, seq, wa, wb, wo, norm2_g):
    n = x2.shape[0]
    tm = 256
    mod_idx = lambda i: (mod_base + mod_step * ((i * tm) // seq), 0, 0)
    row = lambda w: pl.BlockSpec((tm, w), lambda i: (i, 0))
    full = lambda a: pl.BlockSpec(a.shape, lambda i: (0,) * a.ndim)
    return pl.pallas_call(
        _merge_kernel,
        grid=(n // tm,),
        in_specs=[row(BRANCH_W), row(BRANCH_W), row(COLS_GATE), row(D_MODEL),
                  pl.BlockSpec((1, N_MOD, D_MODEL), mod_idx),
                  full(wa), full(wb), full(wo), full(norm2_g)],
        out_specs=[row(D_MODEL), row(D_MODEL)],
        out_shape=[jax.ShapeDtypeStruct((n, D_MODEL), F32), jax.ShapeDtypeStruct((n, D_MODEL), BF16)],
        compiler_params=_cparams("parallel"),
        name="merge_out",
    )(ya2, yb2, gate2, x2, mod, wa, wb, wo, norm2_g)


def _bitonic_pairs(n):
    pairs = []
    k = 2
    while k <= n:
        j = k // 2
        while j >= 1:
            for i in range(n):
                l = i ^ j
                if l > i:
                    pairs.append((i, l) if (i & k) == 0 else (l, i))
            j //= 2
        k *= 2
    return pairs


def _top_values_of_keys(s, k):
    r, t = s.shape
    nl = r // 8
    lv = [s[8 * j:8 * j + 8, :] for j in range(nl)]
    for a, b in _bitonic_pairs(nl):
        lv[a], lv[b] = jnp.maximum(lv[a], lv[b]), jnp.minimum(lv[a], lv[b])
    sid = lax.broadcasted_iota(jnp.int32, (8, t), 0)
    out = []
    for it in range(k):
        m = jnp.max(lv[0], axis=0, keepdims=True)
        out.append(m)
        first = jnp.min(jnp.where(lv[0] == m, sid, 8), axis=0, keepdims=True)
        pop = sid == first
        for j in range(min(nl, k - 1 - it)):
            lv[j] = jnp.where(pop, lv[j + 1] if j + 1 < nl else NEG_INF, lv[j])
    return out


def _peer_score_kernel(h_ref, wq_ref, keys_ref, thr_ref, g1_ref, s2_ref, e2_ref, cand_ref):
    qt = _mm_nt(wq_ref[...], h_ref[...])
    half = PEER_DQ // 2
    k1 = PEER_TOPK + 1
    pairs = [(i, j) for i in range(k1) for j in range(k1) if (i + 1) * (j + 1) <= k1]
    assert len(pairs) <= cand_ref.shape[0]
    for h in range(PEER_HEADS):
        s = [_mm(keys_ref[2 * h + c], qt[(2 * h + c) * half:(2 * h + c + 1) * half, :]) for c in range(2)]
        tops = [_top_values_of_keys(s[c], k1) for c in range(2)]
        cand_ref[...] = jnp.full(cand_ref.shape, NEG_INF, F32)
        for r, (i, j) in enumerate(pairs):
            cand_ref[r:r + 1, :] = tops[0][i] + tops[1][j]
        best = _top_values_of_keys(cand_ref[...], k1)
        mx = best[0]
        z = jnp.exp(best[0] - mx)
        for r in range(1, PEER_TOPK):
            z = z + jnp.exp(best[r] - mx)
        theta = 0.5 * (best[PEER_TOPK - 1] + best[PEER_TOPK])
        thr_ref[h] = theta - s[0]
        g1_ref[h] = jnp.exp(s[0] - tops[0][0]) * (0.5 / z)
        s2_ref[h] = s[1]
        e2_ref[h] = jnp.exp(s[1] - tops[1][0])


def _peer_scores(h2, wq_t, keys):
    n = h2.shape[0]
    tb = 256
    full = lambda a: pl.BlockSpec(a.shape, lambda i: (0,) * a.ndim)
    return pl.pallas_call(
        _peer_score_kernel,
        grid=(n // tb,),
        in_specs=[pl.BlockSpec((tb, D_MODEL), lambda i: (i, 0)), full(wq_t), full(keys)],
        out_specs=[pl.BlockSpec((PEER_HEADS, N_KEYS, tb), lambda i: (0, 0, i))] * 4,
        out_shape=[jax.ShapeDtypeStruct((PEER_HEADS, N_KEYS, n), F32)] * 4,
        scratch_shapes=[pltpu.VMEM((64, tb), F32)],
        compiler_params=_cparams("parallel"),
        name="peer_scores",
    )(h2, wq_t, keys)


def _gelu_tanh_x2(x):
    return x + x * jnp.tanh(x * (0.7978845608028654 + 0.035677408136300125 * (x * x)))


def _peer_expert_kernel(h_ref, u_ref, vt_ref, thr_ref, g1_ref, s2_ref, e2_ref, x1_ref, mod_ref, fg_ref, y_ref,
                        acc_ref, w_ref, *, rows_per_step):
    e = pl.program_id(1)
    tb = h_ref.shape[0]

    @pl.when(e == 0)
    def _():
        acc_ref[...] = jnp.zeros_like(acc_ref)

    st = _mm_nt(u_ref[...], h_ref[...])
    rows_per_chunk = 2
    partial = None
    for al in range(rows_per_step):
        a = e * rows_per_step + al
        r0 = al * N_KEYS
        thr = [thr_ref[h, pl.ds(a, 1), :] for h in range(PEER_HEADS)]
        g1 = [g1_ref[h, pl.ds(a, 1), :] for h in range(PEER_HEADS)]
        for lt in range(tb // LANES):
            ls = slice(lt * LANES, (lt + 1) * LANES)
            wsum = None
            for h in range(PEER_HEADS):
                w = jnp.where(s2_ref[h, :, ls] >= thr[h][:, ls], e2_ref[h, :, ls] * g1[h][:, ls], 0.0)
                wsum = w if wsum is None else wsum + w
            act = _gelu_tanh_x2(st[r0:r0 + N_KEYS, ls])
            w_ref[r0:r0 + N_KEYS, ls] = (wsum * act).astype(BF16)
        if (al + 1) % rows_per_chunk == 0:
            rs = slice((al + 1 - rows_per_chunk) * N_KEYS, (al + 1) * N_KEYS)
            d = jnp.dot(vt_ref[:, rs], w_ref[rs, :], preferred_element_type=F32)
            partial = d if partial is None else partial + d
    acc_ref[...] += partial

    @pl.when(e == pl.num_programs(1) - 1)
    def _():
        m = mod_ref[0]
        x2 = x1_ref[...] + m[5:6] * acc_ref[...].T
        y_ref[...] = _rms(x2, fg_ref[...])


def _peer_experts(h2, u_bf, vt_bf, stats, x1, mod, mod_base, mod_step, seq, final_g):
    n = h2.shape[0]
    tb = 512 if (mod_step == 0 or seq % 512 == 0) else seq
    rows_per_step = 8
    ec = rows_per_step * N_KEYS
    mod_idx = lambda i, e: (mod_base + mod_step * ((i * tb) // seq), 0, 0)
    stat_spec = pl.BlockSpec((PEER_HEADS, N_KEYS, tb), lambda i, e: (0, 0, i))
    return pl.pallas_call(
        functools.partial(_peer_expert_kernel, rows_per_step=rows_per_step),
        grid=(n // tb, N_EXPERTS // ec),
        in_specs=[pl.BlockSpec((tb, D_MODEL), lambda i, e: (i, 0)),
                  pl.BlockSpec((ec, D_MODEL), lambda i, e: (e, 0)),
                  pl.BlockSpec((D_MODEL, ec), lambda i, e: (0, e)),
                  stat_spec, stat_spec, stat_spec, stat_spec,
                  pl.BlockSpec((tb, D_MODEL), lambda i, e: (i, 0)),
                  pl.BlockSpec((1, N_MOD, D_MODEL), mod_idx),
                  pl.BlockSpec((1, D_MODEL), lambda i, e: (0, 0))],
        out_specs=pl.BlockSpec((tb, D_MODEL), lambda i, e: (i, 0)),
        out_shape=jax.ShapeDtypeStruct((n, D_MODEL), F32),
        scratch_shapes=[pltpu.VMEM((D_MODEL, tb), F32), pltpu.VMEM((ec, tb), BF16)],
        compiler_params=_cparams("parallel", "arbitrary"),
        name="peer_experts",
    )(h2, u_bf, vt_bf, *stats, x1, mod, final_g)


def _rope_tables(seq):
    pos = jnp.arange(seq, dtype=jnp.int32)
    lane = jnp.arange(LANES, dtype=jnp.int32) % HEAD_DIM
    use_col = (lane // 32) == 1
    p = jnp.where(use_col[None, :], (pos % GRID_W)[:, None], (pos // GRID_W)[:, None]).astype(F32)
    inv = ROPE_BASE ** (-(lane % 16).astype(F32) / 16.0)
    ang = p * inv[None, :]
    first = (lane % 32) < 16
    return jnp.cos(ang), jnp.where(first[None, :], -jnp.sin(ang), jnp.sin(ang))


def _trunk_path(x, mod, mod_base, mod_step, rope_tabs, s_ret0, s_rwkv0, want_fin, lw, final_g):
    b, seq, _ = x.shape
    x2 = x.reshape(b * seq, D_MODEL)
    mix, lora, gate = _in_proj(x2, mod, mod_base, mod_step, seq, lw['norm1_g'], lw['w_in'], lw['w_in_lo'])
    mix3 = mix.reshape(b, seq, COLS_MIX)
    lora3 = lora.reshape(b, seq, COLS_LORA)
    ya, ret_fin = _retention(mix3, lw['ret_decay'], lw['ret_gn_w'], lw['ret_gn_b'], rope_tabs, s_ret0, want_fin)
    yb, rwkv_fin = _rwkv(mix3, lora3, lw['rwkv'], s_rwkv0, want_fin)
    x1, h2 = _merge(ya.reshape(b * seq, BRANCH_W), yb.reshape(b * seq, BRANCH_W), gate, x2, mod, mod_base,
                    mod_step, seq, lw['w_br_a'], lw['w_br_b'], lw['w_out'], lw['norm2_g'])
    stats = _peer_scores(h2, lw['peer_wq_t'], lw['peer_keys'])
    y = _peer_experts(h2, lw['peer_u'], lw['peer_vt'], stats, x1, mod, mod_base, mod_step, seq, final_g)
    return y.reshape(b, seq, D_MODEL), x1.reshape(b, seq, D_MODEL), ret_fin, rwkv_fin


def kernel(x_prompt, x_sample, state_ret, state_rwkv, c, c_ctx, ada_w, ada_b, norm1_g, w_in, ret_decay, ret_gn_w, ret_gn_b, rwkv_conv, rwkv_w0, rwkv_w2, rwkv_a0, rwkv_a2, rwkv_g2, rwkv_k_k, rwkv_k_a, rwkv_r_k, rwkv_gn_w, rwkv_gn_b, w_br_a, w_br_b, w_out, norm2_g, peer_wq, peer_keys, peer_u, peer_v, final_norm_g):
    assert w_in.shape[0] == 1, "the final norm is fused into the layer's last kernel: single trunk layer only"
    row = lambda a: a.reshape(1, -1)
    cc = jnp.concatenate([c_ctx[None, :], c], axis=0)
    cc = jnp.pad(cc, ((0, (-cc.shape[0]) % 8), (0, 0)))
    rope_tabs = _rope_tables(x_sample.shape[1])
    final_g = row(final_norm_g)

    if True:
        l = 0
        lw = {
            'norm1_g': row(norm1_g[l]), 'w_in': w_in[l].astype(BF16), 'w_in_lo': _bf16_low_part(
                w_in[l][:, STATE_COL0:COLS_MIX + COLS_LORA]),
            'ret_decay': jnp.repeat(ret_decay[l], HEAD_DIM, axis=1),
            'ret_gn_w': row(ret_gn_w[l]), 'ret_gn_b': row(ret_gn_b[l]),
            'rwkv': {'conv': rwkv_conv[l], 'w0': rwkv_w0[l], 'w2': rwkv_w2[l].astype(BF16), 'a0': rwkv_a0[l],
                     'a2': rwkv_a2[l].astype(BF16), 'g2': rwkv_g2[l].astype(BF16), 'k_k': row(rwkv_k_k[l]),
                     'k_a': row(rwkv_k_a[l]), 'r_k': row(rwkv_r_k[l]), 'gn_w': row(rwkv_gn_w[l]),
                     'gn_b': row(rwkv_gn_b[l])},
            'w_br_a': w_br_a[l].astype(BF16), 'w_br_b': w_br_b[l].astype(BF16), 'w_out': w_out[l].astype(BF16),
            'norm2_g': row(norm2_g[l]),
            'peer_wq_t': peer_wq[l].T.astype(BF16),
            'peer_keys': peer_keys[l].reshape(2 * PEER_HEADS, N_KEYS, PEER_DQ // 2).astype(BF16),
            'peer_u': peer_u[l].astype(BF16), 'peer_vt': peer_v[l].T.astype(BF16),
        }
        mod = _adaln(cc, ada_w[l], row(ada_b[l])).reshape(cc.shape[0], N_MOD, D_MODEL)
        yp, _, ret_fin, rwkv_fin = _trunk_path(x_prompt, mod, 0, 0, None, None, None, True, lw, final_g)
        ys, _, _, _ = _trunk_path(x_sample, mod, 1, 1, rope_tabs, state_ret[:, l], state_rwkv[:, l], False, lw,
                                  final_g)
    return (yp, ys, ret_fin[:, None], rwkv_fin[:, None])
```

```python
import functools

import jax
import jax.numpy as jnp
from jax import lax
from jax.experimental import pallas as pl
from jax.experimental.pallas import tpu as pltpu

F32 = jnp.float32
BF16 = jnp.bfloat16

D_MODEL = 1024
GRID_W = 64
N_MOD = 6
HEADS = 8
HEAD_DIM = 64
BRANCH_W = HEADS * HEAD_DIM
RET_CHUNK = 128
RWKV_CHUNK = 64
LORA_W = 64
LORA_A = 64
LORA_G = 128
PEER_HEADS = 8
N_KEYS = 128
N_EXPERTS = N_KEYS * N_KEYS
PEER_DQ = 256
PEER_TOPK = 16
ROPE_BASE = 10000.0
NORM_EPS = 1e-6
GN_EPS = 64e-5
COLS_MIX = 7 * BRANCH_W
COLS_LORA = LORA_W + LORA_A + LORA_G
COLS_GATE = 2 * D_MODEL
IN_COLS = COLS_MIX + COLS_LORA + COLS_GATE

V7X_VMEM_LIMIT_BYTES = 56 * 1024 * 1024
LANES = 128
NEG_INF = float("-inf")


def _cparams(*sem):
    return pltpu.CompilerParams(dimension_semantics=sem, vmem_limit_bytes=V7X_VMEM_LIMIT_BYTES)


def _mm(a, b):
    return jnp.dot(a.astype(BF16), b.astype(BF16), preferred_element_type=F32)


def _mm_nt(a, b):
    return lax.dot_general(a.astype(BF16), b.astype(BF16), (((1,), (1,)), ((), ())),
                           preferred_element_type=F32)


def _mm_tn(a, b):
    return lax.dot_general(a.astype(BF16), b.astype(BF16), (((0,), (0,)), ((), ())),
                           preferred_element_type=F32)


def _mm_tn_3pass(a, b):
    a_hi, b_hi = a.astype(BF16), b.astype(BF16)
    a_lo = (a - a_hi.astype(F32)).astype(BF16)
    b_lo = (b - b_hi.astype(F32)).astype(BF16)
    dims = (((0,), (0,)), ((), ()))
    dot = lambda x, y: lax.dot_general(x, y, dims, preferred_element_type=F32)
    return dot(a_hi, b_hi) + (dot(a_hi, b_lo) + dot(a_lo, b_hi))


def _sigmoid(x):
    return 1.0 / (1.0 + jnp.exp(-x))


def _rms(x, g):
    return x * lax.rsqrt(jnp.mean(x * x, axis=-1, keepdims=True) + NORM_EPS) * g


def _head_sum(x):
    t, w = x.shape
    lo = lax.broadcasted_iota(jnp.int32, (t, LANES), 1) < HEAD_DIM
    outs = []
    for j in range(w // LANES):
        xt = x[:, j * LANES:(j + 1) * LANES]
        s_lo = jnp.sum(jnp.where(lo, xt, 0.0), axis=-1, keepdims=True)
        s_hi = jnp.sum(jnp.where(lo, 0.0, xt), axis=-1, keepdims=True)
        outs.append(jnp.where(lo, s_lo, s_hi))
    return outs[0] if len(outs) == 1 else jnp.concatenate(outs, axis=-1)


def _head_norm(y, w, b):
    mu = _head_sum(y) * (1.0 / HEAD_DIM)
    d = y - mu
    var = _head_sum(d * d) * (1.0 / HEAD_DIM)
    return d * lax.rsqrt(var + GN_EPS) * w + b


def _mod_kernel(c_ref, w_ref, b_ref, o_ref):
    c = c_ref[...]
    o_ref[...] = _mm(c * _sigmoid(c), w_ref[...]) + b_ref[...]


def _adaln(cc, ada_w, ada_b):
    rows = cc.shape[0]
    n = ada_w.shape[1]
    tn = n // 4
    return pl.pallas_call(
        _mod_kernel,
        grid=(n // tn,),
        in_specs=[pl.BlockSpec((rows, D_MODEL), lambda j: (0, 0)),
                  pl.BlockSpec((D_MODEL, tn), lambda j: (0, j)),
                  pl.BlockSpec((1, tn), lambda j: (0, j))],
        out_specs=pl.BlockSpec((rows, tn), lambda j: (0, j)),
        out_shape=jax.ShapeDtypeStruct((rows, n), F32),
        compiler_params=_cparams("parallel"),
        name="adaln_mod",
    )(cc, ada_w, ada_b)


def _in_kernel(x_ref, mod_ref, g_ref, w_ref, mix_ref, lora_ref, gate_ref):
    m = mod_ref[0]
    h = (_rms(x_ref[...], g_ref[...]) * (1.0 + m[1:2]) + m[0:1]).astype(BF16)
    mix_ref[...] = jnp.dot(h, w_ref[:, 0:COLS_MIX], preferred_element_type=F32)
    lora_ref[...] = jnp.dot(h, w_ref[:, COLS_MIX:COLS_MIX + COLS_LORA], preferred_element_type=F32)
    gate_ref[...] = jnp.dot(h, w_ref[:, COLS_MIX + COLS_LORA:IN_COLS], preferred_element_type=F32)


def _in_proj(x2, mod, mod_base, mod_step, seq, norm_g, w_in_bf):
    n = x2.shape[0]
    tm = 256
    mod_idx = lambda i: (mod_base + mod_step * ((i * tm) // seq), 0, 0)
    return pl.pallas_call(
        _in_kernel,
        grid=(n // tm,),
        in_specs=[pl.BlockSpec((tm, D_MODEL), lambda i: (i, 0)),
                  pl.BlockSpec((1, N_MOD, D_MODEL), mod_idx),
                  pl.BlockSpec((1, D_MODEL), lambda i: (0, 0)),
                  pl.BlockSpec((D_MODEL, IN_COLS), lambda i: (0, 0))],
        out_specs=[pl.BlockSpec((tm, COLS_MIX), lambda i: (i, 0)),
                   pl.BlockSpec((tm, COLS_LORA), lambda i: (i, 0)),
                   pl.BlockSpec((tm, COLS_GATE), lambda i: (i, 0))],
        out_shape=[jax.ShapeDtypeStruct((n, COLS_MIX), F32),
                   jax.ShapeDtypeStruct((n, COLS_LORA), F32),
                   jax.ShapeDtypeStruct((n, COLS_GATE), F32)],
        compiler_params=_cparams("parallel"),
        name="in_proj",
    )(x2, mod, norm_g, w_in_bf)


def _rope(x, cos, sin):
    lane = lax.broadcasted_iota(jnp.int32, x.shape, 1)
    first = (lane % 32) < 16
    partner = jnp.where(first, pltpu.roll(x, LANES - 16, 1), pltpu.roll(x, 16, 1))
    return x * cos + partner * sin


def _ret_kernel(*refs, seq, rope, has_init, want_fin):
    it = iter(refs)
    q_ref, k_ref, v_ref, g_ref, rd_ref, gw_ref, gb_ref = (next(it) for _ in range(7))
    cos_ref = next(it) if rope else None
    sin_ref = next(it) if rope else None
    s0_ref = next(it) if has_init else None
    y_ref = next(it)
    fin_ref = next(it) if want_fin else None
    qs_ref, ks_ref, vec_ref, dec_ref, r_ref = (next(it) for _ in range(5))

    C = RET_CHUNK
    W = BRANCH_W
    n = seq // C
    rd = rd_ref[...]
    lg = jnp.minimum(rd, 0.0) - jnp.log(1.0 + jnp.exp(-jnp.abs(rd)))
    lgf, lgb = lg[0:1, :], lg[1:2, :]
    ii = lax.broadcasted_iota(jnp.int32, (C, C), 0)
    jj = lax.broadcasted_iota(jnp.int32, (C, C), 1)
    diff = (ii - jj).astype(F32)
    col = lax.broadcasted_iota(jnp.int32, (C, W), 0).astype(F32)
    ln = lambda h: slice(h * HEAD_DIM, (h + 1) * HEAD_DIM)

    vec_ref[0] = jnp.exp((col + 1.0) * lgf)
    vec_ref[1] = jnp.exp((C - 1.0 - col) * lgf)
    vec_ref[2] = jnp.exp((C - col) * lgb)
    vec_ref[3] = jnp.exp(col * lgb)
    chunk_f = jnp.exp(C * lgf)
    chunk_b = jnp.exp(C * lgb)
    for h in range(HEADS):
        gf = lgf[:, h * HEAD_DIM:h * HEAD_DIM + 1]
        gb = lgb[:, h * HEAD_DIM:h * HEAD_DIM + 1]
        dec_ref[h] = (jnp.where(diff >= 0, jnp.exp(jnp.maximum(diff, 0.0) * gf), 0.0)
                      + jnp.where(diff <= 0, jnp.exp(jnp.maximum(-diff, 0.0) * gb), 0.0))
        for d in range(2):
            r_ref[d, h] = s0_ref[0, d, h] if has_init else jnp.zeros((HEAD_DIM, HEAD_DIM), F32)

    def rows(c):
        return pl.ds(pl.multiple_of(c * C, C), C)

    def prep(c, carry):
        r = rows(c)
        q = q_ref[0, r, :]
        k = k_ref[0, r, :] * (HEAD_DIM ** -0.5)
        if rope:
            cos, sin = cos_ref[r, :], sin_ref[r, :]
            tiles = lambda x: [x[:, j * LANES:(j + 1) * LANES] for j in range(W // LANES)]
            q = jnp.concatenate([_rope(t, cos, sin) for t in tiles(q)], axis=1)
            k = jnp.concatenate([_rope(t, cos, sin) for t in tiles(k)], axis=1)
        qs_ref[r, :] = q
        ks_ref[r, :] = k
        y_ref[0, r, :] = jnp.zeros((C, W), F32)
        return carry

    lax.fori_loop(0, n, prep, 0)

    def step(i, carry):
        rf, rb = rows(i), rows(n - 1 - i)
        qf, kf, vf = qs_ref[rf, :], ks_ref[rf, :], v_ref[0, rf, :]
        qb, kb, vb = qs_ref[rb, :], ks_ref[rb, :], v_ref[0, rb, :]
        qfx, kfx = qf * vec_ref[0], kf * vec_ref[1]
        qbx, kbx = qb * vec_ref[2], kb * vec_ref[3]
        sf = [r_ref[0, h] for h in range(HEADS)]
        sb = [r_ref[1, h] for h in range(HEADS)]
        of, ob, nf, nb = [], [], [], []
        for h in range(HEADS):
            sc = _mm_nt(qf[:, ln(h)], kf[:, ln(h)]) * dec_ref[h]
            of.append(_mm(sc, vf[:, ln(h)]) + _mm(qfx[:, ln(h)], sf[h]))
            nf.append(sf[h] * chunk_f[:, ln(h)] + _mm_tn(kfx[:, ln(h)], vf[:, ln(h)]))
            ob.append(_mm(qbx[:, ln(h)], sb[h]))
            nb.append(sb[h] * chunk_b[:, ln(h)] + _mm_tn(kbx[:, ln(h)], vb[:, ln(h)]))
        yf, yb = y_ref[0, rf, :], y_ref[0, rb, :]
        y_ref[0, rf, :] = yf + jnp.concatenate(of, axis=1)
        y_ref[0, rb, :] = yb + jnp.concatenate(ob, axis=1)
        for h in range(HEADS):
            r_ref[0, h] = nf[h]
            r_ref[1, h] = nb[h]
        return carry

    lax.fori_loop(0, n, step, 0)
    if want_fin:
        for d in range(2):
            for h in range(HEADS):
                fin_ref[0, d, h] = r_ref[d, h]

    def post(c, carry):
        r = rows(c)
        g = g_ref[0, r, :]
        y_ref[0, r, :] = _head_norm(y_ref[0, r, :], gw_ref[...], gb_ref[...]) * (g * _sigmoid(g))
        return carry

    lax.fori_loop(0, n, post, 0)


def _retention(mix3, rd_l, gn_w, gn_b, rope_tabs, s0, want_fin):
    b, seq, _ = mix3.shape
    W = BRANCH_W
    rope = rope_tabs is not None
    has_init = s0 is not None
    col_spec = lambda j: pl.BlockSpec((1, seq, W), lambda i: (i, 0, j))
    full = lambda a: pl.BlockSpec(a.shape, lambda i: (0,) * a.ndim)
    in_specs = [col_spec(0), col_spec(1), col_spec(2), col_spec(3), full(rd_l), full(gn_w), full(gn_b)]
    args = [mix3, mix3, mix3, mix3, rd_l, gn_w, gn_b]
    if rope:
        in_specs += [full(rope_tabs[0]), full(rope_tabs[1])]
        args += list(rope_tabs)
    state_spec = pl.BlockSpec((1, 2, HEADS, HEAD_DIM, HEAD_DIM), lambda i: (i, 0, 0, 0, 0))
    if has_init:
        in_specs.append(state_spec)
        args.append(s0)
    out_specs = [pl.BlockSpec((1, seq, W), lambda i: (i, 0, 0))]
    out_shape = [jax.ShapeDtypeStruct((b, seq, W), F32)]
    if want_fin:
        out_specs.append(state_spec)
        out_shape.append(jax.ShapeDtypeStruct((b, 2, HEADS, HEAD_DIM, HEAD_DIM), F32))
    res = pl.pallas_call(
        functools.partial(_ret_kernel, seq=seq, rope=rope, has_init=has_init, want_fin=want_fin),
        grid=(b,),
        in_specs=in_specs,
        out_specs=out_specs,
        out_shape=out_shape,
        scratch_shapes=[pltpu.VMEM((seq, W), F32), pltpu.VMEM((seq, W), F32),
                        pltpu.VMEM((4, RET_CHUNK, W), F32), pltpu.VMEM((HEADS, RET_CHUNK, RET_CHUNK), F32),
                        pltpu.VMEM((2, HEADS, HEAD_DIM, HEAD_DIM), F32)],
        compiler_params=_cparams("parallel"),
        name="retention",
    )(*args)
    return (res[0], res[1]) if want_fin else (res[0], None)


def _softplus(z):
    return jnp.maximum(z, 0.0) + jnp.log(1.0 + jnp.exp(-jnp.abs(z)))


def _rwkv_kernel(*refs, seq, has_init, want_fin):
    it = iter(refs)
    (r_ref, k_ref, v_ref, lora_ref, cw_ref, w0_ref, w2_ref, a0_ref, a2_ref, g2_ref,
     kk_w_ref, ka_ref, rk_ref, gw_ref, gb_ref) = (next(it) for _ in range(15))
    s0_ref = next(it) if has_init else None
    y_ref = next(it)
    fin_ref = next(it) if want_fin else None
    r_s, v_s, kk_s, g_s, bv_s, lw_s, b_s, kd_s, st_s = (next(it) for _ in range(9))

    C = RWKV_CHUNK
    n = seq // C
    W = BRANCH_W
    row_id = lax.broadcasted_iota(jnp.int32, (C, W), 0)

    def rows(c):
        return pl.ds(pl.multiple_of(c * C, C), C)

    def conv(ref, c, w):
        x = ref[0, rows(c), :]
        prev8 = ref[0, pl.ds(pl.multiple_of(jnp.maximum(c * C - 8, 0), 8), 8), :]
        next8 = ref[0, pl.ds(pl.multiple_of(jnp.minimum(c * C + C, seq - 8), 8), 8), :]
        prev_row = jnp.where(c > 0, prev8[7:8, :], 0.0)
        next_row = jnp.where(c < n - 1, next8[0:1, :], 0.0)
        xm = jnp.where(row_id == 0, prev_row, pltpu.roll(x, 1, 0))
        xp = jnp.where(row_id == C - 1, next_row, pltpu.roll(x, C - 1, 0))
        return w[0:1] * xm + w[1:2] * x + w[2:3] * xp

    def prep(c, carry):
        rws = rows(c)
        cw = cw_ref[...]
        r = conv(r_ref, c, cw[:, 0:W])
        k = conv(k_ref, c, cw[:, W:2 * W])
        v = conv(v_ref, c, cw[:, 2 * W:3 * W])
        lo = lora_ref[0, rws, :]
        dw = lo[:, 0:LORA_W]
        da = lo[:, LORA_W:LORA_W + LORA_A]
        dg = lo[:, LORA_W + LORA_A:]
        kk = k * kk_w_ref[...]
        kk = kk * lax.rsqrt(_head_sum(kk * kk) + 1e-12)
        y_ref[0, rws, :] = jnp.zeros((C, W), F32)
        r_s[rws, :] = r
        v_s[rws, :] = v
        kk_s[rws, :] = kk
        g_s[rws, :] = _mm(_sigmoid(dg), g2_ref[...])
        bv_s[rws, :] = _head_sum(r * k * rk_ref[...]) * v
        tdw = jnp.tanh(dw)
        for d in range(2):
            w_log = -_softplus(-(w0_ref[d:d + 1, :] + _mm(tdw, w2_ref[d]))) - 0.5
            a = _sigmoid(a0_ref[d:d + 1, :] + _mm(da, a2_ref[d]))
            lw_s[d, rws, :] = -jnp.exp(w_log)
            b_s[d, rws, :] = kk * a
            kd_s[d, rws, :] = k * (1.0 + (a - 1.0) * ka_ref[...])
        return carry

    lax.fori_loop(0, n, prep, 0)

    ii = lax.broadcasted_iota(jnp.int32, (C, C), 0)
    jj = lax.broadcasted_iota(jnp.int32, (C, C), 1)
    incl = [ii >= jj, ii <= jj]
    strict = [ii > jj, ii < jj]

    for d in range(2):
        for h in range(HEADS):
            st_s[d, h] = s0_ref[0, d, h] if has_init else jnp.zeros((HEAD_DIM, HEAD_DIM), F32)

    def chunk(i, carry):
        rws, at, bt, kt, rt, bh, kh, etot, vc = ([None] * 2 for _ in range(9))
        for d in range(2):
            rws[d] = rows(i if d == 0 else n - 1 - i)
            rc, vc[d], kkc = r_s[rws[d], :], v_s[rws[d], :], kk_s[rws[d], :]
            lwc, bc, kc = lw_s[d, rws[d], :], b_s[d, rws[d], :], kd_s[d, rws[d], :]
            cum = jnp.dot(incl[d].astype(F32), lwc, precision=lax.Precision.HIGHEST,
                          preferred_element_type=F32)
            tot = cum[C - 1:C, :] if d == 0 else cum[0:1, :]
            pinv = jnp.exp(-cum)
            pend = jnp.exp(tot - cum)
            at[d] = -kkc * jnp.exp(cum - lwc)
            bt[d] = bc * pinv
            kt[d] = kc * pinv
            rt[d] = rc * jnp.exp(cum)
            bh[d] = bc * pend
            kh[d] = kc * pend
            etot[d] = jnp.exp(tot)
        chains = [(d, h) for d in range(2) for h in range(HEADS)]
        ln = lambda h: slice(h * HEAD_DIM, (h + 1) * HEAD_DIM)
        wk = [jnp.concatenate([bt[d][:, ln(h)], kt[d][:, ln(h)]], axis=0) for d, h in chains]
        ma = [_mm_nt(at[d][:, ln(h)], wk[j]) for j, (d, h) in enumerate(chains)]
        mr = [_mm_nt(rt[d][:, ln(h)], wk[j]) for j, (d, h) in enumerate(chains)]
        a_ak = [jnp.where(strict[d], ma[j][:, C:2 * C], 0.0) for j, (d, h) in enumerate(chains)]
        p = [jnp.where(strict[d], ma[j][:, 0:C], 0.0) for j, (d, h) in enumerate(chains)]
        a_rb = [jnp.where(incl[d], mr[j][:, 0:C], 0.0) for j, (d, h) in enumerate(chains)]
        a_rk = [jnp.where(incl[d], mr[j][:, C:2 * C], 0.0) for j, (d, h) in enumerate(chains)]
        vh = [vc[d][:, ln(h)] for d, h in chains]
        x = [jnp.concatenate([at[d][:, ln(h)], _mm(a_ak[j], vh[j])], axis=1)
             for j, (d, h) in enumerate(chains)]
        for step in range(6):
            pb = [p[j].astype(BF16) for j in range(len(chains))]
            xh = [x[j].astype(BF16) for j in range(len(chains))]
            xl = [(x[j] - xh[j].astype(F32)).astype(BF16) for j in range(len(chains))]
            px = [jnp.dot(pb[j], jnp.concatenate([xh[j], xl[j]], axis=1), preferred_element_type=F32)
                  for j in range(len(chains))]
            x = [x[j] + (px[j][:, 0:2 * HEAD_DIM] + px[j][:, 2 * HEAD_DIM:]) for j in range(len(chains))]
            if step < 5:
                p = [jnp.dot(pb[j], pb[j], preferred_element_type=F32) for j in range(len(chains))]
        gh = [_mm_tn(x[j], bh[d][:, ln(h)]) for j, (d, h) in enumerate(chains)]
        vk = [_mm_tn_3pass(vh[j], kh[d][:, ln(h)]) for j, (d, h) in enumerate(chains)]
        ax = [_mm(a_rb[j], x[j]) for j in range(len(chains))]
        qt = [rt[d][:, ln(h)] + ax[j][:, 0:HEAD_DIM] for j, (d, h) in enumerate(chains)]
        y0 = [ax[j][:, HEAD_DIM:2 * HEAD_DIM] + _mm(a_rk[j], vh[j]) for j in range(len(chains))]
        ys = []
        for j, (d, h) in enumerate(chains):
            s = st_s[d, h]
            ys.append(_mm_nt(qt[j], s) + y0[j])
            st_s[d, h] = (s * etot[d][:, ln(h)] + _mm(s, gh[j][0:HEAD_DIM, :])
                          + gh[j][HEAD_DIM:2 * HEAD_DIM, :] + vk[j])
        for d in range(2):
            y = jnp.concatenate(ys[d * HEADS:(d + 1) * HEADS], axis=1)
            y_ref[0, rws[d], :] = y_ref[0, rws[d], :] + y
        return carry

    lax.fori_loop(0, n, chunk, 0)
    if want_fin:
        for d in range(2):
            for h in range(HEADS):
                fin_ref[0, d, h] = st_s[d, h]

    def post(c, carry):
        rws = rows(c)
        y = _head_norm(y_ref[0, rws, :], gw_ref[...], gb_ref[...])
        y_ref[0, rws, :] = (y + bv_s[rws, :]) * g_s[rws, :]
        return carry

    lax.fori_loop(0, n, post, 0)


def _rwkv(mix3, lora3, lw, s0, want_fin):
    b, seq, _ = mix3.shape
    has_init = s0 is not None
    W = BRANCH_W
    col_spec = lambda j: pl.BlockSpec((1, seq, W), lambda i: (i, 0, j))
    full = lambda a: pl.BlockSpec(a.shape, lambda i: (0,) * a.ndim)
    weights = [lw['conv'], lw['w0'], lw['w2'], lw['a0'], lw['a2'], lw['g2'],
               lw['k_k'], lw['k_a'], lw['r_k'], lw['gn_w'], lw['gn_b']]
    in_specs = [col_spec(4), col_spec(5), col_spec(6),
                pl.BlockSpec((1, seq, COLS_LORA), lambda i: (i, 0, 0))] + [full(a) for a in weights]
    args = [mix3, mix3, mix3, lora3] + weights
    if has_init:
        in_specs.append(pl.BlockSpec((1, 2, HEADS, HEAD_DIM, HEAD_DIM), lambda i: (i, 0, 0, 0, 0)))
        args.append(s0)
    out_specs = [pl.BlockSpec((1, seq, W), lambda i: (i, 0, 0))]
    out_shape = [jax.ShapeDtypeStruct((b, seq, W), F32)]
    if want_fin:
        out_specs.append(pl.BlockSpec((1, 2, HEADS, HEAD_DIM, HEAD_DIM), lambda i: (i, 0, 0, 0, 0)))
        out_shape.append(jax.ShapeDtypeStruct((b, 2, HEADS, HEAD_DIM, HEAD_DIM), F32))
    sw = pltpu.VMEM((seq, W), F32)
    sw2 = pltpu.VMEM((2, seq, W), F32)
    res = pl.pallas_call(
        functools.partial(_rwkv_kernel, seq=seq, has_init=has_init, want_fin=want_fin),
        grid=(b,),
        in_specs=in_specs,
        out_specs=out_specs,
        out_shape=out_shape,
        scratch_shapes=[sw, sw, sw, sw, sw, sw2, sw2, sw2, pltpu.VMEM((2, HEADS, HEAD_DIM, HEAD_DIM), F32)],
        compiler_params=_cparams("parallel"),
        name="rwkv7",
    )(*args)
    return (res[0], res[1]) if want_fin else (res[0], None)


def _merge_kernel(ya_ref, yb_ref, gate_ref, x_ref, mod_ref, wa_ref, wb_ref, wo_ref, n2_ref, x1_ref, h2_ref):
    m = mod_ref[0]
    br_a = jnp.dot(ya_ref[...].astype(BF16), wa_ref[...], preferred_element_type=F32)
    br_b = jnp.dot(yb_ref[...].astype(BF16), wb_ref[...], preferred_element_type=F32)
    gate = gate_ref[...]
    merged = _sigmoid(gate[:, 0:D_MODEL]) * br_a + _sigmoid(gate[:, D_MODEL:]) * br_b
    mix = jnp.dot(merged.astype(BF16), wo_ref[...], preferred_element_type=F32)
    x1 = x_ref[...] + m[2:3] * mix
    x1_ref[...] = x1
    h2_ref[...] = (_rms(x1, n2_ref[...]) * (1.0 + m[4:5]) + m[3:4]).astype(BF16)


def _merge(ya2, yb2, gate2, x2, mod, mod_base, mod_step, seq, wa, wb, wo, norm2_g):
    n = x2.shape[0]
    tm = 256
    mod_idx = lambda i: (mod_base + mod_step * ((i * tm) // seq), 0, 0)
    row = lambda w: pl.BlockSpec((tm, w), lambda i: (i, 0))
    full = lambda a: pl.BlockSpec(a.shape, lambda i: (0,) * a.ndim)
    return pl.pallas_call(
        _merge_kernel,
        grid=(n // tm,),
        in_specs=[row(BRANCH_W), row(BRANCH_W), row(COLS_GATE), row(D_MODEL),
                  pl.BlockSpec((1, N_MOD, D_MODEL), mod_idx),
                  full(wa), full(wb), full(wo), full(norm2_g)],
        out_specs=[row(D_MODEL), row(D_MODEL)],
        out_shape=[jax.ShapeDtypeStruct((n, D_MODEL), F32), jax.ShapeDtypeStruct((n, D_MODEL), BF16)],
        compiler_params=_cparams("parallel"),
        name="merge_out",
    )(ya2, yb2, gate2, x2, mod, wa, wb, wo, norm2_g)


def _bitonic_pairs(n):
    pairs = []
    k = 2
    while k <= n:
        j = k // 2
        while j >= 1:
            for i in range(n):
                l = i ^ j
                if l > i:
                    pairs.append((i, l) if (i & k) == 0 else (l, i))
            j //= 2
        k *= 2
    return pairs


def _top_values_of_keys(s, k):
    r, t = s.shape
    nl = r // 8
    lv = [s[8 * j:8 * j + 8, :] for j in range(nl)]
    for a, b in _bitonic_pairs(nl):
        lv[a], lv[b] = jnp.maximum(lv[a], lv[b]), jnp.minimum(lv[a], lv[b])
    sid = lax.broadcasted_iota(jnp.int32, (8, t), 0)
    out = []
    for it in range(k):
        m = jnp.max(lv[0], axis=0, keepdims=True)
        out.append(m)
        first = jnp.min(jnp.where(lv[0] == m, sid, 8), axis=0, keepdims=True)
        pop = sid == first
        for j in range(min(nl, k - 1 - it)):
            lv[j] = jnp.where(pop, lv[j + 1] if j + 1 < nl else NEG_INF, lv[j])
    return out


def _peer_score_kernel(h_ref, wq_ref, keys_ref, thr_ref, g1_ref, s2_ref, e2_ref, cand_ref):
    qt = _mm_nt(wq_ref[...], h_ref[...])
    half = PEER_DQ // 2
    k1 = PEER_TOPK + 1
    pairs = [(i, j) for i in range(k1) for j in range(k1) if (i + 1) * (j + 1) <= k1]
    assert len(pairs) <= cand_ref.shape[0]
    for h in range(PEER_HEADS):
        s = [_mm(keys_ref[2 * h + c], qt[(2 * h + c) * half:(2 * h + c + 1) * half, :]) for c in range(2)]
        tops = [_top_values_of_keys(s[c], k1) for c in range(2)]
        cand_ref[...] = jnp.full(cand_ref.shape, NEG_INF, F32)
        for r, (i, j) in enumerate(pairs):
            cand_ref[r:r + 1, :] = tops[0][i] + tops[1][j]
        best = _top_values_of_keys(cand_ref[...], k1)
        mx = best[0]
        z = jnp.exp(best[0] - mx)
        for r in range(1, PEER_TOPK):
            z = z + jnp.exp(best[r] - mx)
        theta = 0.5 * (best[PEER_TOPK - 1] + best[PEER_TOPK])
        thr_ref[h] = theta - s[0]
        g1_ref[h] = jnp.exp(s[0] - tops[0][0]) * (0.5 / z)
        s2_ref[h] = s[1]
        e2_ref[h] = jnp.exp(s[1] - tops[1][0])


def _peer_scores(h2, wq_t, keys):
    n = h2.shape[0]
    tb = 256
    full = lambda a: pl.BlockSpec(a.shape, lambda i: (0,) * a.ndim)
    return pl.pallas_call(
        _peer_score_kernel,
        grid=(n // tb,),
        in_specs=[pl.BlockSpec((tb, D_MODEL), lambda i: (i, 0)), full(wq_t), full(keys)],
        out_specs=[pl.BlockSpec((PEER_HEADS, N_KEYS, tb), lambda i: (0, 0, i))] * 4,
        out_shape=[jax.ShapeDtypeStruct((PEER_HEADS, N_KEYS, n), F32)] * 4,
        scratch_shapes=[pltpu.VMEM((64, tb), F32)],
        compiler_params=_cparams("parallel"),
        name="peer_scores",
    )(h2, wq_t, keys)


def _gelu_tanh_x2(x):
    return x + x * jnp.tanh(x * (0.7978845608028654 + 0.035677408136300125 * (x * x)))


def _peer_expert_kernel(h_ref, u_ref, vt_ref, thr_ref, g1_ref, s2_ref, e2_ref, x1_ref, mod_ref, fg_ref, y_ref,
                        acc_ref, w_ref, *, rows_per_step):
    e = pl.program_id(1)
    tb = h_ref.shape[0]

    @pl.when(e == 0)
    def _():
        acc_ref[...] = jnp.zeros_like(acc_ref)

    st = _mm_nt(u_ref[...], h_ref[...])
    rows_per_chunk = 2
    partial = None
    for al in range(rows_per_step):
        a = e * rows_per_step + al
        r0 = al * N_KEYS
        thr = [thr_ref[h, pl.ds(a, 1), :] for h in range(PEER_HEADS)]
        g1 = [g1_ref[h, pl.ds(a, 1), :] for h in range(PEER_HEADS)]
        for lt in range(tb // LANES):
            ls = slice(lt * LANES, (lt + 1) * LANES)
            wsum = None
            for h in range(PEER_HEADS):
                w = jnp.where(s2_ref[h, :, ls] >= thr[h][:, ls], e2_ref[h, :, ls] * g1[h][:, ls], 0.0)
                wsum = w if wsum is None else wsum + w
            act = _gelu_tanh_x2(st[r0:r0 + N_KEYS, ls])
            w_ref[r0:r0 + N_KEYS, ls] = (wsum * act).astype(BF16)
        if (al + 1) % rows_per_chunk == 0:
            rs = slice((al + 1 - rows_per_chunk) * N_KEYS, (al + 1) * N_KEYS)
            d = jnp.dot(vt_ref[:, rs], w_ref[rs, :], preferred_element_type=F32)
            partial = d if partial is None else partial + d
    acc_ref[...] += partial

    @pl.when(e == pl.num_programs(1) - 1)
    def _():
        m = mod_ref[0]
        x2 = x1_ref[...] + m[5:6] * acc_ref[...].T
        y_ref[...] = _rms(x2, fg_ref[...])


def _peer_experts(h2, u_bf, vt_bf, stats, x1, mod, mod_base, mod_step, seq, final_g):
    n = h2.shape[0]
    tb = 512 if (mod_step == 0 or seq % 512 == 0) else seq
    rows_per_step = 8
    ec = rows_per_step * N_KEYS
    mod_idx = lambda i, e: (mod_base + mod_step * ((i * tb) // seq), 0, 0)
    stat_spec = pl.BlockSpec((PEER_HEADS, N_KEYS, tb), lambda i, e: (0, 0, i))
    return pl.pallas_call(
        functools.partial(_peer_expert_kernel, rows_per_step=rows_per_step),
        grid=(n // tb, N_EXPERTS // ec),
        in_specs=[pl.BlockSpec((tb, D_MODEL), lambda i, e: (i, 0)),
                  pl.BlockSpec((ec, D_MODEL), lambda i, e: (e, 0)),
                  pl.BlockSpec((D_MODEL, ec), lambda i, e: (0, e)),
                  stat_spec, stat_spec, stat_spec, stat_spec,
                  pl.BlockSpec((tb, D_MODEL), lambda i, e: (i, 0)),
                  pl.BlockSpec((1, N_MOD, D_MODEL), mod_idx),
                  pl.BlockSpec((1, D_MODEL), lambda i, e: (0, 0))],
        out_specs=pl.BlockSpec((tb, D_MODEL), lambda i, e: (i, 0)),
        out_shape=jax.ShapeDtypeStruct((n, D_MODEL), F32),
        scratch_shapes=[pltpu.VMEM((D_MODEL, tb), F32), pltpu.VMEM((ec, tb), BF16)],
        compiler_params=_cparams("parallel", "arbitrary"),
        name="peer_experts",
    )(h2, u_bf, vt_bf, *stats, x1, mod, final_g)


def _rope_tables(seq):
    pos = jnp.arange(seq, dtype=jnp.int32)
    lane = jnp.arange(LANES, dtype=jnp.int32) % HEAD_DIM
    use_col = (lane // 32) == 1
    p = jnp.where(use_col[None, :], (pos % GRID_W)[:, None], (pos // GRID_W)[:, None]).astype(F32)
    inv = ROPE_BASE ** (-(lane % 16).astype(F32) / 16.0)
    ang = p * inv[None, :]
    first = (lane % 32) < 16
    return jnp.cos(ang), jnp.where(first[None, :], -jnp.sin(ang), jnp.sin(ang))


def _trunk_path(x, mod, mod_base, mod_step, rope_tabs, s_ret0, s_rwkv0, want_fin, lw, final_g):
    b, seq, _ = x.shape
    x2 = x.reshape(b * seq, D_MODEL)
    mix, lora, gate = _in_proj(x2, mod, mod_base, mod_step, seq, lw['norm1_g'], lw['w_in'])
    mix3 = mix.reshape(b, seq, COLS_MIX)
    lora3 = lora.reshape(b, seq, COLS_LORA)
    ya, ret_fin = _retention(mix3, lw['ret_decay'], lw['ret_gn_w'], lw['ret_gn_b'], rope_tabs, s_ret0, want_fin)
    yb, rwkv_fin = _rwkv(mix3, lora3, lw['rwkv'], s_rwkv0, want_fin)
    x1, h2 = _merge(ya.reshape(b * seq, BRANCH_W), yb.reshape(b * seq, BRANCH_W), gate, x2, mod, mod_base,
                    mod_step, seq, lw['w_br_a'], lw['w_br_b'], lw['w_out'], lw['norm2_g'])
    stats = _peer_scores(h2, lw['peer_wq_t'], lw['peer_keys'])
    y = _peer_experts(h2, lw['peer_u'], lw['peer_vt'], stats, x1, mod, mod_base, mod_step, seq, final_g)
    return y.reshape(b, seq, D_MODEL), x1.reshape(b, seq, D_MODEL), ret_fin, rwkv_fin


def kernel(x_prompt, x_sample, state_ret, state_rwkv, c, c_ctx, ada_w, ada_b, norm1_g, w_in, ret_decay, ret_gn_w, ret_gn_b, rwkv_conv, rwkv_w0, rwkv_w2, rwkv_a0, rwkv_a2, rwkv_g2, rwkv_k_k, rwkv_k_a, rwkv_r_k, rwkv_gn_w, rwkv_gn_b, w_br_a, w_br_b, w_out, norm2_g, peer_wq, peer_keys, peer_u, peer_v, final_norm_g):
    assert w_in.shape[0] == 1, "the final norm is fused into the layer's last kernel: single trunk layer only"
    row = lambda a: a.reshape(1, -1)
    cc = jnp.concatenate([c_ctx[None, :], c], axis=0)
    cc = jnp.pad(cc, ((0, (-cc.shape[0]) % 8), (0, 0)))
    rope_tabs = _rope_tables(x_sample.shape[1])
    final_g = row(final_norm_g)

    if True:
        l = 0
        lw = {
            'norm1_g': row(norm1_g[l]), 'w_in': w_in[l].astype(BF16),
            'ret_decay': jnp.repeat(ret_decay[l], HEAD_DIM, axis=1),
            'ret_gn_w': row(ret_gn_w[l]), 'ret_gn_b': row(ret_gn_b[l]),
            'rwkv': {'conv': rwkv_conv[l], 'w0': rwkv_w0[l], 'w2': rwkv_w2[l].astype(BF16), 'a0': rwkv_a0[l],
                     'a2': rwkv_a2[l].astype(BF16), 'g2': rwkv_g2[l].astype(BF16), 'k_k': row(rwkv_k_k[l]),
                     'k_a': row(rwkv_k_a[l]), 'r_k': row(rwkv_r_k[l]), 'gn_w': row(rwkv_gn_w[l]),
                     'gn_b': row(rwkv_gn_b[l])},
            'w_br_a': w_br_a[l].astype(BF16), 'w_br_b': w_br_b[l].astype(BF16), 'w_out': w_out[l].astype(BF16),
            'norm2_g': row(norm2_g[l]),
            'peer_wq_t': peer_wq[l].T.astype(BF16),
            'peer_keys': peer_keys[l].reshape(2 * PEER_HEADS, N_KEYS, PEER_DQ // 2).astype(BF16),
            'peer_u': peer_u[l].astype(BF16), 'peer_vt': peer_v[l].T.astype(BF16),
        }
        mod = _adaln(cc, ada_w[l], row(ada_b[l])).reshape(cc.shape[0], N_MOD, D_MODEL)
        yp, _, ret_fin, rwkv_fin = _trunk_path(x_prompt, mod, 0, 0, None, None, None, True, lw, final_g)
        ys, _, _, _ = _trunk_path(x_sample, mod, 1, 1, rope_tabs, state_ret[:, l], state_rwkv[:, l], False, lw,
                                  final_g)
    return (yp, ys, ret_fin[:, None], rwkv_fin[:, None])
```

```python
import functools

import jax
import jax.numpy as jnp
from jax import lax
from jax.experimental import pallas as pl
from jax.experimental.pallas import tpu as pltpu

F32 = jnp.float32
BF16 = jnp.bfloat16

D_MODEL = 1024
GRID_W = 64
N_MOD = 6
HEADS = 8
HEAD_DIM = 64
BRANCH_W = HEADS * HEAD_DIM
RET_CHUNK = 128
RWKV_CHUNK = 64
LORA_W = 64
LORA_A = 64
LORA_G = 128
PEER_HEADS = 8
N_KEYS = 128
N_EXPERTS = N_KEYS * N_KEYS
PEER_DQ = 256
PEER_TOPK = 16
ROPE_BASE = 10000.0
NORM_EPS = 1e-6
GN_EPS = 64e-5
COLS_MIX = 7 * BRANCH_W
COLS_LORA = LORA_W + LORA_A + LORA_G
COLS_GATE = 2 * D_MODEL
IN_COLS = COLS_MIX + COLS_LORA + COLS_GATE

V7X_VMEM_LIMIT_BYTES = 56 * 1024 * 1024
LANES = 128
NEG_INF = float("-inf")


def _cparams(*sem):
    return pltpu.CompilerParams(dimension_semantics=sem, vmem_limit_bytes=V7X_VMEM_LIMIT_BYTES)


def _mm(a, b):
    return jnp.dot(a.astype(BF16), b.astype(BF16), preferred_element_type=F32)


def _mm_nt(a, b):
    return lax.dot_general(a.astype(BF16), b.astype(BF16), (((1,), (1,)), ((), ())),
                           preferred_element_type=F32)


def _mm_tn(a, b):
    return lax.dot_general(a.astype(BF16), b.astype(BF16), (((0,), (0,)), ((), ())),
                           preferred_element_type=F32)


def _mm_tn_3pass(a, b):
    a_hi, b_hi = a.astype(BF16), b.astype(BF16)
    a_lo = (a - a_hi.astype(F32)).astype(BF16)
    b_lo = (b - b_hi.astype(F32)).astype(BF16)
    dims = (((0,), (0,)), ((), ()))
    dot = lambda x, y: lax.dot_general(x, y, dims, preferred_element_type=F32)
    return dot(a_hi, b_hi) + (dot(a_hi, b_lo) + dot(a_lo, b_hi))


def _sigmoid(x):
    return 1.0 / (1.0 + jnp.exp(-x))


def _rms(x, g):
    return x * lax.rsqrt(jnp.mean(x * x, axis=-1, keepdims=True) + NORM_EPS) * g


def _head_sum(x):
    t, w = x.shape
    lo = lax.broadcasted_iota(jnp.int32, (t, LANES), 1) < HEAD_DIM
    outs = []
    for j in range(w // LANES):
        xt = x[:, j * LANES:(j + 1) * LANES]
        s_lo = jnp.sum(jnp.where(lo, xt, 0.0), axis=-1, keepdims=True)
        s_hi = jnp.sum(jnp.where(lo, 0.0, xt), axis=-1, keepdims=True)
        outs.append(jnp.where(lo, s_lo, s_hi))
    return outs[0] if len(outs) == 1 else jnp.concatenate(outs, axis=-1)


def _head_norm(y, w, b):
    mu = _head_sum(y) * (1.0 / HEAD_DIM)
    d = y - mu
    var = _head_sum(d * d) * (1.0 / HEAD_DIM)
    return d * lax.rsqrt(var + GN_EPS) * w + b


def _mod_kernel(c_ref, w_ref, b_ref, o_ref):
    c = c_ref[...]
    o_ref[...] = _mm(c * _sigmoid(c), w_ref[...]) + b_ref[...]


def _adaln(cc, ada_w, ada_b):
    rows = cc.shape[0]
    n = ada_w.shape[1]
    tn = n // 4
    return pl.pallas_call(
        _mod_kernel,
        grid=(n // tn,),
        in_specs=[pl.BlockSpec((rows, D_MODEL), lambda j: (0, 0)),
                  pl.BlockSpec((D_MODEL, tn), lambda j: (0, j)),
                  pl.BlockSpec((1, tn), lambda j: (0, j))],
        out_specs=pl.BlockSpec((rows, tn), lambda j: (0, j)),
        out_shape=jax.ShapeDtypeStruct((rows, n), F32),
        compiler_params=_cparams("parallel"),
        name="adaln_mod",
    )(cc, ada_w, ada_b)


def _in_kernel(x_ref, mod_ref, g_ref, w_ref, mix_ref, lora_ref, gate_ref):
    m = mod_ref[0]
    h = (_rms(x_ref[...], g_ref[...]) * (1.0 + m[1:2]) + m[0:1]).astype(BF16)
    mix_ref[...] = jnp.dot(h, w_ref[:, 0:COLS_MIX], preferred_element_type=F32)
    lora_ref[...] = jnp.dot(h, w_ref[:, COLS_MIX:COLS_MIX + COLS_LORA], preferred_element_type=F32)
    gate_ref[...] = jnp.dot(h, w_ref[:, COLS_MIX + COLS_LORA:IN_COLS], preferred_element_type=F32)


def _in_proj(x2, mod, mod_base, mod_step, seq, norm_g, w_in_bf):
    n = x2.shape[0]
    tm = 256
    mod_idx = lambda i: (mod_base + mod_step * ((i * tm) // seq), 0, 0)
    return pl.pallas_call(
        _in_kernel,
        grid=(n // tm,),
        in_specs=[pl.BlockSpec((tm, D_MODEL), lambda i: (i, 0)),
                  pl.BlockSpec((1, N_MOD, D_MODEL), mod_idx),
                  pl.BlockSpec((1, D_MODEL), lambda i: (0, 0)),
                  pl.BlockSpec((D_MODEL, IN_COLS), lambda i: (0, 0))],
        out_specs=[pl.BlockSpec((tm, COLS_MIX), lambda i: (i, 0)),
                   pl.BlockSpec((tm, COLS_LORA), lambda i: (i, 0)),
                   pl.BlockSpec((tm, COLS_GATE), lambda i: (i, 0))],
        out_shape=[jax.ShapeDtypeStruct((n, COLS_MIX), F32),
                   jax.ShapeDtypeStruct((n, COLS_LORA), F32),
                   jax.ShapeDtypeStruct((n, COLS_GATE), F32)],
        compiler_params=_cparams("parallel"),
        name="in_proj",
    )(x2, mod, norm_g, w_in_bf)


def _rope(x, cos, sin):
    lane = lax.broadcasted_iota(jnp.int32, x.shape, 1)
    first = (lane % 32) < 16
    partner = jnp.where(first, pltpu.roll(x, LANES - 16, 1), pltpu.roll(x, 16, 1))
    return x * cos + partner * sin


def _ret_kernel(*refs, seq, rope, has_init, want_fin):
    it = iter(refs)
    q_ref, k_ref, v_ref, g_ref, rd_ref, gw_ref, gb_ref = (next(it) for _ in range(7))
    cos_ref = next(it) if rope else None
    sin_ref = next(it) if rope else None
    s0_ref = next(it) if has_init else None
    y_ref = next(it)
    fin_ref = next(it) if want_fin else None
    qs_ref, ks_ref, vec_ref, dec_ref, r_ref = (next(it) for _ in range(5))

    C = RET_CHUNK
    W = BRANCH_W
    n = seq // C
    rd = rd_ref[...]
    lg = jnp.minimum(rd, 0.0) - jnp.log(1.0 + jnp.exp(-jnp.abs(rd)))
    lgf, lgb = lg[0:1, :], lg[1:2, :]
    ii = lax.broadcasted_iota(jnp.int32, (C, C), 0)
    jj = lax.broadcasted_iota(jnp.int32, (C, C), 1)
    diff = (ii - jj).astype(F32)
    col = lax.broadcasted_iota(jnp.int32, (C, W), 0).astype(F32)
    ln = lambda h: slice(h * HEAD_DIM, (h + 1) * HEAD_DIM)

    vec_ref[0] = jnp.exp((col + 1.0) * lgf)
    vec_ref[1] = jnp.exp((C - 1.0 - col) * lgf)
    vec_ref[2] = jnp.exp((C - col) * lgb)
    vec_ref[3] = jnp.exp(col * lgb)
    chunk_f = jnp.exp(C * lgf)
    chunk_b = jnp.exp(C * lgb)
    for h in range(HEADS):
        gf = lgf[:, h * HEAD_DIM:h * HEAD_DIM + 1]
        gb = lgb[:, h * HEAD_DIM:h * HEAD_DIM + 1]
        dec_ref[h] = (jnp.where(diff >= 0, jnp.exp(jnp.maximum(diff, 0.0) * gf), 0.0)
                      + jnp.where(diff <= 0, jnp.exp(jnp.maximum(-diff, 0.0) * gb), 0.0))
        for d in range(2):
            r_ref[d, h] = s0_ref[0, d, h] if has_init else jnp.zeros((HEAD_DIM, HEAD_DIM), F32)

    def rows(c):
        return pl.ds(pl.multiple_of(c * C, C), C)

    def prep(c, carry):
        r = rows(c)
        q = q_ref[0, r, :]
        k = k_ref[0, r, :] * (HEAD_DIM ** -0.5)
        if rope:
            cos, sin = cos_ref[r, :], sin_ref[r, :]
            tiles = lambda x: [x[:, j * LANES:(j + 1) * LANES] for j in range(W // LANES)]
            q = jnp.concatenate([_rope(t, cos, sin) for t in tiles(q)], axis=1)
            k = jnp.concatenate([_rope(t, cos, sin) for t in tiles(k)], axis=1)
        qs_ref[r, :] = q
        ks_ref[r, :] = k
        y_ref[0, r, :] = jnp.zeros((C, W), F32)
        return carry

    lax.fori_loop(0, n, prep, 0)

    def step(i, carry):
        rf, rb = rows(i), rows(n - 1 - i)
        qf, kf, vf = qs_ref[rf, :], ks_ref[rf, :], v_ref[0, rf, :]
        qb, kb, vb = qs_ref[rb, :], ks_ref[rb, :], v_ref[0, rb, :]
        qfx, kfx = qf * vec_ref[0], kf * vec_ref[1]
        qbx, kbx = qb * vec_ref[2], kb * vec_ref[3]
        sf = [r_ref[0, h] for h in range(HEADS)]
        sb = [r_ref[1, h] for h in range(HEADS)]
        of, ob, nf, nb = [], [], [], []
        for h in range(HEADS):
            sc = _mm_nt(qf[:, ln(h)], kf[:, ln(h)]) * dec_ref[h]
            of.append(_mm(sc, vf[:, ln(h)]) + _mm(qfx[:, ln(h)], sf[h]))
            nf.append(sf[h] * chunk_f[:, ln(h)] + _mm_tn(kfx[:, ln(h)], vf[:, ln(h)]))
            ob.append(_mm(qbx[:, ln(h)], sb[h]))
            nb.append(sb[h] * chunk_b[:, ln(h)] + _mm_tn(kbx[:, ln(h)], vb[:, ln(h)]))
        yf, yb = y_ref[0, rf, :], y_ref[0, rb, :]
        y_ref[0, rf, :] = yf + jnp.concatenate(of, axis=1)
        y_ref[0, rb, :] = yb + jnp.concatenate(ob, axis=1)
        for h in range(HEADS):
            r_ref[0, h] = nf[h]
            r_ref[1, h] = nb[h]
        return carry

    lax.fori_loop(0, n, step, 0)
    if want_fin:
        for d in range(2):
            for h in range(HEADS):
                fin_ref[0, d, h] = r_ref[d, h]

    def post(c, carry):
        r = rows(c)
        g = g_ref[0, r, :]
        y_ref[0, r, :] = _head_norm(y_ref[0, r, :], gw_ref[...], gb_ref[...]) * (g * _sigmoid(g))
        return carry

    lax.fori_loop(0, n, post, 0)


def _retention(mix3, rd_l, gn_w, gn_b, rope_tabs, s0, want_fin):
    b, seq, _ = mix3.shape
    W = BRANCH_W
    rope = rope_tabs is not None
    has_init = s0 is not None
    col_spec = lambda j: pl.BlockSpec((1, seq, W), lambda i: (i, 0, j))
    full = lambda a: pl.BlockSpec(a.shape, lambda i: (0,) * a.ndim)
    in_specs = [col_spec(0), col_spec(1), col_spec(2), col_spec(3), full(rd_l), full(gn_w), full(gn_b)]
    args = [mix3, mix3, mix3, mix3, rd_l, gn_w, gn_b]
    if rope:
        in_specs += [full(rope_tabs[0]), full(rope_tabs[1])]
        args += list(rope_tabs)
    state_spec = pl.BlockSpec((1, 2, HEADS, HEAD_DIM, HEAD_DIM), lambda i: (i, 0, 0, 0, 0))
    if has_init:
        in_specs.append(state_spec)
        args.append(s0)
    out_specs = [pl.BlockSpec((1, seq, W), lambda i: (i, 0, 0))]
    out_shape = [jax.ShapeDtypeStruct((b, seq, W), F32)]
    if want_fin:
        out_specs.append(state_spec)
        out_shape.append(jax.ShapeDtypeStruct((b, 2, HEADS, HEAD_DIM, HEAD_DIM), F32))
    res = pl.pallas_call(
        functools.partial(_ret_kernel, seq=seq, rope=rope, has_init=has_init, want_fin=want_fin),
        grid=(b,),
        in_specs=in_specs,
        out_specs=out_specs,
        out_shape=out_shape,
        scratch_shapes=[pltpu.VMEM((seq, W), F32), pltpu.VMEM((seq, W), F32),
                        pltpu.VMEM((4, RET_CHUNK, W), F32), pltpu.VMEM((HEADS, RET_CHUNK, RET_CHUNK), F32),
                        pltpu.VMEM((2, HEADS, HEAD_DIM, HEAD_DIM), F32)],
        compiler_params=_cparams("parallel"),
        name="retention",
    )(*args)
    return (res[0], res[1]) if want_fin else (res[0], None)


def _softplus(z):
    return jnp.maximum(z, 0.0) + jnp.log(1.0 + jnp.exp(-jnp.abs(z)))


def _rwkv_kernel(*refs, seq, has_init, want_fin):
    it = iter(refs)
    (r_ref, k_ref, v_ref, lora_ref, cw_ref, w0_ref, w2_ref, a0_ref, a2_ref, g2_ref,
     kk_w_ref, ka_ref, rk_ref, gw_ref, gb_ref) = (next(it) for _ in range(15))
    s0_ref = next(it) if has_init else None
    y_ref = next(it)
    fin_ref = next(it) if want_fin else None
    r_s, v_s, kk_s, g_s, bv_s, lw_s, b_s, kd_s, st_s = (next(it) for _ in range(9))

    C = RWKV_CHUNK
    n = seq // C
    W = BRANCH_W
    row_id = lax.broadcasted_iota(jnp.int32, (C, W), 0)

    def rows(c):
        return pl.ds(pl.multiple_of(c * C, C), C)

    def conv(ref, c, w):
        x = ref[0, rows(c), :]
        prev8 = ref[0, pl.ds(pl.multiple_of(jnp.maximum(c * C - 8, 0), 8), 8), :]
        next8 = ref[0, pl.ds(pl.multiple_of(jnp.minimum(c * C + C, seq - 8), 8), 8), :]
        prev_row = jnp.where(c > 0, prev8[7:8, :], 0.0)
        next_row = jnp.where(c < n - 1, next8[0:1, :], 0.0)
        xm = jnp.where(row_id == 0, prev_row, pltpu.roll(x, 1, 0))
        xp = jnp.where(row_id == C - 1, next_row, pltpu.roll(x, C - 1, 0))
        return w[0:1] * xm + w[1:2] * x + w[2:3] * xp

    def prep(c, carry):
        rws = rows(c)
        cw = cw_ref[...]
        r = conv(r_ref, c, cw[:, 0:W])
        k = conv(k_ref, c, cw[:, W:2 * W])
        v = conv(v_ref, c, cw[:, 2 * W:3 * W])
        lo = lora_ref[0, rws, :]
        dw = lo[:, 0:LORA_W]
        da = lo[:, LORA_W:LORA_W + LORA_A]
        dg = lo[:, LORA_W + LORA_A:]
        kk = k * kk_w_ref[...]
        kk = kk * lax.rsqrt(_head_sum(kk * kk) + 1e-12)
        y_ref[0, rws, :] = jnp.zeros((C, W), F32)
        r_s[rws, :] = r
        v_s[rws, :] = v
        kk_s[rws, :] = kk
        g_s[rws, :] = _mm(_sigmoid(dg), g2_ref[...])
        bv_s[rws, :] = _head_sum(r * k * rk_ref[...]) * v
        tdw = jnp.tanh(dw)
        for d in range(2):
            w_log = -_softplus(-(w0_ref[d:d + 1, :] + _mm(tdw, w2_ref[d]))) - 0.5
            a = _sigmoid(a0_ref[d:d + 1, :] + _mm(da, a2_ref[d]))
            lw_s[d, rws, :] = -jnp.exp(w_log)
            b_s[d, rws, :] = kk * a
            kd_s[d, rws, :] = k * (1.0 + (a - 1.0) * ka_ref[...])
        return carry

    lax.fori_loop(0, n, prep, 0)

    ii = lax.broadcasted_iota(jnp.int32, (C, C), 0)
    jj = lax.broadcasted_iota(jnp.int32, (C, C), 1)
    incl = [ii >= jj, ii <= jj]
    strict = [ii > jj, ii < jj]

    for d in range(2):
        for h in range(HEADS):
            st_s[d, h] = s0_ref[0, d, h] if has_init else jnp.zeros((HEAD_DIM, HEAD_DIM), F32)

    def chunk(i, carry):
        rws, at, bt, kt, rt, bh, kh, etot, vc = ([None] * 2 for _ in range(9))
        for d in range(2):
            rws[d] = rows(i if d == 0 else n - 1 - i)
            rc, vc[d], kkc = r_s[rws[d], :], v_s[rws[d], :], kk_s[rws[d], :]
            lwc, bc, kc = lw_s[d, rws[d], :], b_s[d, rws[d], :], kd_s[d, rws[d], :]
            cum = jnp.dot(incl[d].astype(F32), lwc, precision=lax.Precision.HIGHEST,
                          preferred_element_type=F32)
            tot = cum[C - 1:C, :] if d == 0 else cum[0:1, :]
            pinv = jnp.exp(-cum)
            pend = jnp.exp(tot - cum)
            at[d] = -kkc * jnp.exp(cum - lwc)
            bt[d] = bc * pinv
            kt[d] = kc * pinv
            rt[d] = rc * jnp.exp(cum)
            bh[d] = bc * pend
            kh[d] = kc * pend
            etot[d] = jnp.exp(tot)
        chains = [(d, h) for d in range(2) for h in range(HEADS)]
        ln = lambda h: slice(h * HEAD_DIM, (h + 1) * HEAD_DIM)
        wk = [jnp.concatenate([bt[d][:, ln(h)], kt[d][:, ln(h)]], axis=0) for d, h in chains]
        ma = [_mm_nt(at[d][:, ln(h)], wk[j]) for j, (d, h) in enumerate(chains)]
        mr = [_mm_nt(rt[d][:, ln(h)], wk[j]) for j, (d, h) in enumerate(chains)]
        a_ak = [jnp.where(strict[d], ma[j][:, C:2 * C], 0.0) for j, (d, h) in enumerate(chains)]
        p = [jnp.where(strict[d], ma[j][:, 0:C], 0.0) for j, (d, h) in enumerate(chains)]
        a_rb = [jnp.where(incl[d], mr[j][:, 0:C], 0.0) for j, (d, h) in enumerate(chains)]
        a_rk = [jnp.where(incl[d], mr[j][:, C:2 * C], 0.0) for j, (d, h) in enumerate(chains)]
        vh = [vc[d][:, ln(h)] for d, h in chains]
        x = [jnp.concatenate([at[d][:, ln(h)], _mm(a_ak[j], vh[j])], axis=1)
             for j, (d, h) in enumerate(chains)]
        for step in range(6):
            pb = [p[j].astype(BF16) for j in range(len(chains))]
            xh = [x[j].astype(BF16) for j in range(len(chains))]
            if step < 3:
                xl = [(x[j] - xh[j].astype(F32)).astype(BF16) for j in range(len(chains))]
                px = [jnp.dot(pb[j], jnp.concatenate([xh[j], xl[j]], axis=1), preferred_element_type=F32)
                      for j in range(len(chains))]
                x = [x[j] + (px[j][:, 0:2 * HEAD_DIM] + px[j][:, 2 * HEAD_DIM:]) for j in range(len(chains))]
            else:
                x = [x[j] + jnp.dot(pb[j], xh[j], preferred_element_type=F32) for j in range(len(chains))]
            if step < 5:
                p = [jnp.dot(pb[j], pb[j], preferred_element_type=F32) for j in range(len(chains))]
        gh = [_mm_tn(x[j], bh[d][:, ln(h)]) for j, (d, h) in enumerate(chains)]
        vk = [_mm_tn_3pass(vh[j], kh[d][:, ln(h)]) for j, (d, h) in enumerate(chains)]
        ax = [_mm(a_rb[j], x[j]) for j in range(len(chains))]
        qt = [rt[d][:, ln(h)] + ax[j][:, 0:HEAD_DIM] for j, (d, h) in enumerate(chains)]
        y0 = [ax[j][:, HEAD_DIM:2 * HEAD_DIM] + _mm(a_rk[j], vh[j]) for j in range(len(chains))]
        ys = []
        for j, (d, h) in enumerate(chains):
            s = st_s[d, h]
            ys.append(_mm_nt(qt[j], s) + y0[j])
            st_s[d, h] = (s * etot[d][:, ln(h)] + _mm(s, gh[j][0:HEAD_DIM, :])
                          + gh[j][HEAD_DIM:2 * HEAD_DIM, :] + vk[j])
        for d in range(2):
            y = jnp.concatenate(ys[d * HEADS:(d + 1) * HEADS], axis=1)
            y_ref[0, rws[d], :] = y_ref[0, rws[d], :] + y
        return carry

    lax.fori_loop(0, n, chunk, 0)
    if want_fin:
        for d in range(2):
            for h in range(HEADS):
                fin_ref[0, d, h] = st_s[d, h]

    def post(c, carry):
        rws = rows(c)
        y = _head_norm(y_ref[0, rws, :], gw_ref[...], gb_ref[...])
        y_ref[0, rws, :] = (y + bv_s[rws, :]) * g_s[rws, :]
        return carry

    lax.fori_loop(0, n, post, 0)


def _rwkv(mix3, lora3, lw, s0, want_fin):
    b, seq, _ = mix3.shape
    has_init = s0 is not None
    W = BRANCH_W
    col_spec = lambda j: pl.BlockSpec((1, seq, W), lambda i: (i, 0, j))
    full = lambda a: pl.BlockSpec(a.shape, lambda i: (0,) * a.ndim)
    weights = [lw['conv'], lw['w0'], lw['w2'], lw['a0'], lw['a2'], lw['g2'],
               lw['k_k'], lw['k_a'], lw['r_k'], lw['gn_w'], lw['gn_b']]
    in_specs = [col_spec(4), col_spec(5), col_spec(6),
                pl.BlockSpec((1, seq, COLS_LORA), lambda i: (i, 0, 0))] + [full(a) for a in weights]
    args = [mix3, mix3, mix3, lora3] + weights
    if has_init:
        in_specs.append(pl.BlockSpec((1, 2, HEADS, HEAD_DIM, HEAD_DIM), lambda i: (i, 0, 0, 0, 0)))
        args.append(s0)
    out_specs = [pl.BlockSpec((1, seq, W), lambda i: (i, 0, 0))]
    out_shape = [jax.ShapeDtypeStruct((b, seq, W), F32)]
    if want_fin:
        out_specs.append(pl.BlockSpec((1, 2, HEADS, HEAD_DIM, HEAD_DIM), lambda i: (i, 0, 0, 0, 0)))
        out_shape.append(jax.ShapeDtypeStruct((b, 2, HEADS, HEAD_DIM, HEAD_DIM), F32))
    sw = pltpu.VMEM((seq, W), F32)
    sw2 = pltpu.VMEM((2, seq, W), F32)
    res = pl.pallas_call(
        functools.partial(_rwkv_kernel, seq=seq, has_init=has_init, want_fin=want_fin),
        grid=(b,),
        in_specs=in_specs,
        out_specs=out_specs,
        out_shape=out_shape,
        scratch_shapes=[sw, sw, sw, sw, sw, sw2, sw2, sw2, pltpu.VMEM((2, HEADS, HEAD_DIM, HEAD_DIM), F32)],
        compiler_params=_cparams("parallel"),
        name="rwkv7",
    )(*args)
    return (res[0], res[1]) if want_fin else (res[0], None)


def _merge_kernel(ya_ref, yb_ref, gate_ref, x_ref, mod_ref, wa_ref, wb_ref, wo_ref, n2_ref, x1_ref, h2_ref):
    m = mod_ref[0]
    br_a = jnp.dot(ya_ref[...].astype(BF16), wa_ref[...], preferred_element_type=F32)
    br_b = jnp.dot(yb_ref[...].astype(BF16), wb_ref[...], preferred_element_type=F32)
    gate = gate_ref[...]
    merged = _sigmoid(gate[:, 0:D_MODEL]) * br_a + _sigmoid(gate[:, D_MODEL:]) * br_b
    mix = jnp.dot(merged.astype(BF16), wo_ref[...], preferred_element_type=F32)
    x1 = x_ref[...] + m[2:3] * mix
    x1_ref[...] = x1
    h2_ref[...] = (_rms(x1, n2_ref[...]) * (1.0 + m[4:5]) + m[3:4]).astype(BF16)


def _merge(ya2, yb2, gate2, x2, mod, mod_base, mod_step, seq, wa, wb, wo, norm2_g):
    n = x2.shape[0]
    tm = 256
    mod_idx = lambda i: (mod_base + mod_step * ((i * tm) // seq), 0, 0)
    row = lambda w: pl.BlockSpec((tm, w), lambda i: (i, 0))
    full = lambda a: pl.BlockSpec(a.shape, lambda i: (0,) * a.ndim)
    return pl.pallas_call(
        _merge_kernel,
        grid=(n // tm,),
        in_specs=[row(BRANCH_W), row(BRANCH_W), row(COLS_GATE), row(D_MODEL),
                  pl.BlockSpec((1, N_MOD, D_MODEL), mod_idx),
                  full(wa), full(wb), full(wo), full(norm2_g)],
        out_specs=[row(D_MODEL), row(D_MODEL)],
        out_shape=[jax.ShapeDtypeStruct((n, D_MODEL), F32), jax.ShapeDtypeStruct((n, D_MODEL), BF16)],
        compiler_params=_cparams("parallel"),
        name="merge_out",
    )(ya2, yb2, gate2, x2, mod, wa, wb, wo, norm2_g)


def _bitonic_pairs(n):
    pairs = []
    k = 2
    while k <= n:
        j = k // 2
        while j >= 1:
            for i in range(n):
                l = i ^ j
                if l > i:
                    pairs.append((i, l) if (i & k) == 0 else (l, i))
            j //= 2
        k *= 2
    return pairs


def _top_values_of_keys(s, k):
    r, t = s.shape
    nl = r // 8
    lv = [s[8 * j:8 * j + 8, :] for j in range(nl)]
    for a, b in _bitonic_pairs(nl):
        lv[a], lv[b] = jnp.maximum(lv[a], lv[b]), jnp.minimum(lv[a], lv[b])
    sid = lax.broadcasted_iota(jnp.int32, (8, t), 0)
    out = []
    for it in range(k):
        m = jnp.max(lv[0], axis=0, keepdims=True)
        out.append(m)
        first = jnp.min(jnp.where(lv[0] == m, sid, 8), axis=0, keepdims=True)
        pop = sid == first
        for j in range(min(nl, k - 1 - it)):
            lv[j] = jnp.where(pop, lv[j + 1] if j + 1 < nl else NEG_INF, lv[j])
    return out


def _peer_score_kernel(h_ref, wq_ref, keys_ref, thr_ref, g1_ref, s2_ref, e2_ref, cand_ref):
    qt = _mm_nt(wq_ref[...], h_ref[...])
    half = PEER_DQ // 2
    k1 = PEER_TOPK + 1
    pairs = [(i, j) for i in range(k1) for j in range(k1) if (i + 1) * (j + 1) <= k1]
    assert len(pairs) <= cand_ref.shape[0]
    for h in range(PEER_HEADS):
        s = [_mm(keys_ref[2 * h + c], qt[(2 * h + c) * half:(2 * h + c + 1) * half, :]) for c in range(2)]
        tops = [_top_values_of_keys(s[c], k1) for c in range(2)]
        cand_ref[...] = jnp.full(cand_ref.shape, NEG_INF, F32)
        for r, (i, j) in enumerate(pairs):
            cand_ref[r:r + 1, :] = tops[0][i] + tops[1][j]
        best = _top_values_of_keys(cand_ref[...], k1)
        mx = best[0]
        z = jnp.exp(best[0] - mx)
        for r in range(1, PEER_TOPK):
            z = z + jnp.exp(best[r] - mx)
        theta = 0.5 * (best[PEER_TOPK - 1] + best[PEER_TOPK])
        thr_ref[h] = theta - s[0]
        g1_ref[h] = jnp.exp(s[0] - tops[0][0]) * (0.5 / z)
        s2_ref[h] = s[1]
        e2_ref[h] = jnp.exp(s[1] - tops[1][0])


def _peer_scores(h2, wq_t, keys):
    n = h2.shape[0]
    tb = 256
    full = lambda a: pl.BlockSpec(a.shape, lambda i: (0,) * a.ndim)
    return pl.pallas_call(
        _peer_score_kernel,
        grid=(n // tb,),
        in_specs=[pl.BlockSpec((tb, D_MODEL), lambda i: (i, 0)), full(wq_t), full(keys)],
        out_specs=[pl.BlockSpec((PEER_HEADS, N_KEYS, tb), lambda i: (0, 0, i))] * 4,
        out_shape=[jax.ShapeDtypeStruct((PEER_HEADS, N_KEYS, n), F32)] * 4,
        scratch_shapes=[pltpu.VMEM((64, tb), F32)],
        compiler_params=_cparams("parallel"),
        name="peer_scores",
    )(h2, wq_t, keys)


def _gelu_tanh_x2(x):
    return x + x * jnp.tanh(x * (0.7978845608028654 + 0.035677408136300125 * (x * x)))


def _peer_expert_kernel(h_ref, u_ref, vt_ref, thr_ref, g1_ref, s2_ref, e2_ref, x1_ref, mod_ref, fg_ref, y_ref,
                        acc_ref, w_ref, *, rows_per_step):
    e = pl.program_id(1)
    tb = h_ref.shape[0]

    @pl.when(e == 0)
    def _():
        acc_ref[...] = jnp.zeros_like(acc_ref)

    st = _mm_nt(u_ref[...], h_ref[...])
    rows_per_chunk = 2
    partial = None
    for al in range(rows_per_step):
        a = e * rows_per_step + al
        r0 = al * N_KEYS
        thr = [thr_ref[h, pl.ds(a, 1), :] for h in range(PEER_HEADS)]
        g1 = [g1_ref[h, pl.ds(a, 1), :] for h in range(PEER_HEADS)]
        for lt in range(tb // LANES):
            ls = slice(lt * LANES, (lt + 1) * LANES)
            wsum = None
            for h in range(PEER_HEADS):
                w = jnp.where(s2_ref[h, :, ls] >= thr[h][:, ls], e2_ref[h, :, ls] * g1[h][:, ls], 0.0)
                wsum = w if wsum is None else wsum + w
            act = _gelu_tanh_x2(st[r0:r0 + N_KEYS, ls])
            w_ref[r0:r0 + N_KEYS, ls] = (wsum * act).astype(BF16)
        if (al + 1) % rows_per_chunk == 0:
            rs = slice((al + 1 - rows_per_chunk) * N_KEYS, (al + 1) * N_KEYS)
            d = jnp.dot(vt_ref[:, rs], w_ref[rs, :], preferred_element_type=F32)
            partial = d if partial is None else partial + d
    acc_ref[...] += partial

    @pl.when(e == pl.num_programs(1) - 1)
    def _():
        m = mod_ref[0]
        x2 = x1_ref[...] + m[5:6] * acc_ref[...].T
        y_ref[...] = _rms(x2, fg_ref[...])


def _peer_experts(h2, u_bf, vt_bf, stats, x1, mod, mod_base, mod_step, seq, final_g):
    n = h2.shape[0]
    tb = 512 if (mod_step == 0 or seq % 512 == 0) else seq
    rows_per_step = 8
    ec = rows_per_step * N_KEYS
    mod_idx = lambda i, e: (mod_base + mod_step * ((i * tb) // seq), 0, 0)
    stat_spec = pl.BlockSpec((PEER_HEADS, N_KEYS, tb), lambda i, e: (0, 0, i))
    return pl.pallas_call(
        functools.partial(_peer_expert_kernel, rows_per_step=rows_per_step),
        grid=(n // tb, N_EXPERTS // ec),
        in_specs=[pl.BlockSpec((tb, D_MODEL), lambda i, e: (i, 0)),
                  pl.BlockSpec((ec, D_MODEL), lambda i, e: (e, 0)),
                  pl.BlockSpec((D_MODEL, ec), lambda i, e: (0, e)),
                  stat_spec, stat_spec, stat_spec, stat_spec,
                  pl.BlockSpec((tb, D_MODEL), lambda i, e: (i, 0)),
                  pl.BlockSpec((1, N_MOD, D_MODEL), mod_idx),
                  pl.BlockSpec((1, D_MODEL), lambda i, e: (0, 0))],
        out_specs=pl.BlockSpec((tb, D_MODEL), lambda i, e: (i, 0)),
        out_shape=jax.ShapeDtypeStruct((n, D_MODEL), F32),
        scratch_shapes=[pltpu.VMEM((D_MODEL, tb), F32), pltpu.VMEM((ec, tb), BF16)],
        compiler_params=_cparams("parallel", "arbitrary"),
        name="peer_experts",
    )(h2, u_bf, vt_bf, *stats, x1, mod, final_g)


def _rope_tables(seq):
    pos = jnp.arange(seq, dtype=jnp.int32)
    lane = jnp.arange(LANES, dtype=jnp.int32) % HEAD_DIM
    use_col = (lane // 32) == 1
    p = jnp.where(use_col[None, :], (pos % GRID_W)[:, None], (pos // GRID_W)[:, None]).astype(F32)
    inv = ROPE_BASE ** (-(lane % 16).astype(F32) / 16.0)
    ang = p * inv[None, :]
    first = (lane % 32) < 16
    return jnp.cos(ang), jnp.where(first[None, :], -jnp.sin(ang), jnp.sin(ang))


def _trunk_path(x, mod, mod_base, mod_step, rope_tabs, s_ret0, s_rwkv0, want_fin, lw, final_g):
    b, seq, _ = x.shape
    x2 = x.reshape(b * seq, D_MODEL)
    mix, lora, gate = _in_proj(x2, mod, mod_base, mod_step, seq, lw['norm1_g'], lw['w_in'])
    mix3 = mix.reshape(b, seq, COLS_MIX)
    lora3 = lora.reshape(b, seq, COLS_LORA)
    ya, ret_fin = _retention(mix3, lw['ret_decay'], lw['ret_gn_w'], lw['ret_gn_b'], rope_tabs, s_ret0, want_fin)
    yb, rwkv_fin = _rwkv(mix3, lora3, lw['rwkv'], s_rwkv0, want_fin)
    x1, h2 = _merge(ya.reshape(b * seq, BRANCH_W), yb.reshape(b * seq, BRANCH_W), gate, x2, mod, mod_base,
                    mod_step, seq, lw['w_br_a'], lw['w_br_b'], lw['w_out'], lw['norm2_g'])
    stats = _peer_scores(h2, lw['peer_wq_t'], lw['peer_keys'])
    y = _peer_experts(h2, lw['peer_u'], lw['peer_vt'], stats, x1, mod, mod_base, mod_step, seq, final_g)
    return y.reshape(b, seq, D_MODEL), x1.reshape(b, seq, D_MODEL), ret_fin, rwkv_fin


def kernel(x_prompt, x_sample, state_ret, state_rwkv, c, c_ctx, ada_w, ada_b, norm1_g, w_in, ret_decay, ret_gn_w, ret_gn_b, rwkv_conv, rwkv_w0, rwkv_w2, rwkv_a0, rwkv_a2, rwkv_g2, rwkv_k_k, rwkv_k_a, rwkv_r_k, rwkv_gn_w, rwkv_gn_b, w_br_a, w_br_b, w_out, norm2_g, peer_wq, peer_keys, peer_u, peer_v, final_norm_g):
    assert w_in.shape[0] == 1, "the final norm is fused into the layer's last kernel: single trunk layer only"
    row = lambda a: a.reshape(1, -1)
    cc = jnp.concatenate([c_ctx[None, :], c], axis=0)
    cc = jnp.pad(cc, ((0, (-cc.shape[0]) % 8), (0, 0)))
    rope_tabs = _rope_tables(x_sample.shape[1])
    final_g = row(final_norm_g)

    if True:
        l = 0
        lw = {
            'norm1_g': row(norm1_g[l]), 'w_in': w_in[l].astype(BF16),
            'ret_decay': jnp.repeat(ret_decay[l], HEAD_DIM, axis=1),
            'ret_gn_w': row(ret_gn_w[l]), 'ret_gn_b': row(ret_gn_b[l]),
            'rwkv': {'conv': rwkv_conv[l], 'w0': rwkv_w0[l], 'w2': rwkv_w2[l].astype(BF16), 'a0': rwkv_a0[l],
                     'a2': rwkv_a2[l].astype(BF16), 'g2': rwkv_g2[l].astype(BF16), 'k_k': row(rwkv_k_k[l]),
                     'k_a': row(rwkv_k_a[l]), 'r_k': row(rwkv_r_k[l]), 'gn_w': row(rwkv_gn_w[l]),
                     'gn_b': row(rwkv_gn_b[l])},
            'w_br_a': w_br_a[l].astype(BF16), 'w_br_b': w_br_b[l].astype(BF16), 'w_out': w_out[l].astype(BF16),
            'norm2_g': row(norm2_g[l]),
            'peer_wq_t': peer_wq[l].astype(BF16).T,
            'peer_keys': peer_keys[l].reshape(2 * PEER_HEADS, N_KEYS, PEER_DQ // 2).astype(BF16),
            'peer_u': peer_u[l].astype(BF16), 'peer_vt': peer_v[l].astype(BF16).T,
        }
        mod = _adaln(cc, ada_w[l], row(ada_b[l])).reshape(cc.shape[0], N_MOD, D_MODEL)
        yp, _, ret_fin, rwkv_fin = _trunk_path(x_prompt, mod, 0, 0, None, None, None, True, lw, final_g)
        ys, _, _, _ = _trunk_path(x_sample, mod, 1, 1, rope_tabs, state_ret[:, l], state_rwkv[:, l], False, lw,
                                  final_g)
    return (yp, ys, ret_fin[:, None], rwkv_fin[:, None])
```

```python
import functools

import jax
import jax.numpy as jnp
from jax import lax
from jax.experimental import pallas as pl
from jax.experimental.pallas import tpu as pltpu

F32 = jnp.float32
BF16 = jnp.bfloat16

D_MODEL = 1024
GRID_W = 64
N_MOD = 6
HEADS = 8
HEAD_DIM = 64
BRANCH_W = HEADS * HEAD_DIM
RET_CHUNK = 128
RWKV_CHUNK = 64
LORA_W = 64
LORA_A = 64
LORA_G = 128
PEER_HEADS = 8
N_KEYS = 128
N_EXPERTS = N_KEYS * N_KEYS
PEER_DQ = 256
PEER_TOPK = 16
ROPE_BASE = 10000.0
NORM_EPS = 1e-6
GN_EPS = 64e-5
COLS_MIX = 7 * BRANCH_W
COLS_LORA = LORA_W + LORA_A + LORA_G
COLS_GATE = 2 * D_MODEL
IN_COLS = COLS_MIX + COLS_LORA + COLS_GATE

V7X_VMEM_LIMIT_BYTES = 56 * 1024 * 1024
LANES = 128
NEG_INF = float("-inf")


def _cparams(*sem):
    return pltpu.CompilerParams(dimension_semantics=sem, vmem_limit_bytes=V7X_VMEM_LIMIT_BYTES)


def _mm(a, b):
    return jnp.dot(a.astype(BF16), b.astype(BF16), preferred_element_type=F32)


def _mm_nt(a, b):
    return lax.dot_general(a.astype(BF16), b.astype(BF16), (((1,), (1,)), ((), ())),
                           preferred_element_type=F32)


def _mm_tn(a, b):
    return lax.dot_general(a.astype(BF16), b.astype(BF16), (((0,), (0,)), ((), ())),
                           preferred_element_type=F32)


def _mm_tn_3pass(a, b):
    a_hi, b_hi = a.astype(BF16), b.astype(BF16)
    a_lo = (a - a_hi.astype(F32)).astype(BF16)
    b_lo = (b - b_hi.astype(F32)).astype(BF16)
    dims = (((0,), (0,)), ((), ()))
    dot = lambda x, y: lax.dot_general(x, y, dims, preferred_element_type=F32)
    return dot(a_hi, b_hi) + (dot(a_hi, b_lo) + dot(a_lo, b_hi))


def _sigmoid(x):
    return 1.0 / (1.0 + jnp.exp(-x))


def _rms(x, g):
    return x * lax.rsqrt(jnp.mean(x * x, axis=-1, keepdims=True) + NORM_EPS) * g


def _head_sum(x):
    t, w = x.shape
    lo = lax.broadcasted_iota(jnp.int32, (t, LANES), 1) < HEAD_DIM
    outs = []
    for j in range(w // LANES):
        xt = x[:, j * LANES:(j + 1) * LANES]
        s_lo = jnp.sum(jnp.where(lo, xt, 0.0), axis=-1, keepdims=True)
        s_hi = jnp.sum(jnp.where(lo, 0.0, xt), axis=-1, keepdims=True)
        outs.append(jnp.where(lo, s_lo, s_hi))
    return outs[0] if len(outs) == 1 else jnp.concatenate(outs, axis=-1)


def _head_norm(y, w, b):
    mu = _head_sum(y) * (1.0 / HEAD_DIM)
    d = y - mu
    var = _head_sum(d * d) * (1.0 / HEAD_DIM)
    return d * lax.rsqrt(var + GN_EPS) * w + b


def _mod_kernel(c_ref, w_ref, b_ref, o_ref):
    c = c_ref[...]
    o_ref[...] = _mm(c * _sigmoid(c), w_ref[...]) + b_ref[...]


def _adaln(cc, ada_w, ada_b):
    rows = cc.shape[0]
    n = ada_w.shape[1]
    tn = n // 4
    return pl.pallas_call(
        _mod_kernel,
        grid=(n // tn,),
        in_specs=[pl.BlockSpec((rows, D_MODEL), lambda j: (0, 0)),
                  pl.BlockSpec((D_MODEL, tn), lambda j: (0, j)),
                  pl.BlockSpec((1, tn), lambda j: (0, j))],
        out_specs=pl.BlockSpec((rows, tn), lambda j: (0, j)),
        out_shape=jax.ShapeDtypeStruct((rows, n), F32),
        compiler_params=_cparams("parallel"),
        name="adaln_mod",
    )(cc, ada_w, ada_b)


def _in_kernel(x_ref, mod_ref, g_ref, w_ref, mix_ref, lora_ref, gate_ref):
    m = mod_ref[0]
    h = (_rms(x_ref[...], g_ref[...]) * (1.0 + m[1:2]) + m[0:1]).astype(BF16)
    mix_ref[...] = jnp.dot(h, w_ref[:, 0:COLS_MIX], preferred_element_type=F32)
    lora_ref[...] = jnp.dot(h, w_ref[:, COLS_MIX:COLS_MIX + COLS_LORA], preferred_element_type=F32)
    gate_ref[...] = jnp.dot(h, w_ref[:, COLS_MIX + COLS_LORA:IN_COLS], preferred_element_type=F32)


def _in_proj(x2, mod, mod_base, mod_step, seq, norm_g, w_in_bf):
    n = x2.shape[0]
    tm = 256
    mod_idx = lambda i: (mod_base + mod_step * ((i * tm) // seq), 0, 0)
    return pl.pallas_call(
        _in_kernel,
        grid=(n // tm,),
        in_specs=[pl.BlockSpec((tm, D_MODEL), lambda i: (i, 0)),
                  pl.BlockSpec((1, N_MOD, D_MODEL), mod_idx),
                  pl.BlockSpec((1, D_MODEL), lambda i: (0, 0)),
                  pl.BlockSpec((D_MODEL, IN_COLS), lambda i: (0, 0))],
        out_specs=[pl.BlockSpec((tm, COLS_MIX), lambda i: (i, 0)),
                   pl.BlockSpec((tm, COLS_LORA), lambda i: (i, 0)),
                   pl.BlockSpec((tm, COLS_GATE), lambda i: (i, 0))],
        out_shape=[jax.ShapeDtypeStruct((n, COLS_MIX), F32),
                   jax.ShapeDtypeStruct((n, COLS_LORA), F32),
                   jax.ShapeDtypeStruct((n, COLS_GATE), F32)],
        compiler_params=_cparams("parallel"),
        name="in_proj",
    )(x2, mod, norm_g, w_in_bf)


def _rope(x, cos, sin):
    lane = lax.broadcasted_iota(jnp.int32, x.shape, 1)
    first = (lane % 32) < 16
    partner = jnp.where(first, pltpu.roll(x, LANES - 16, 1), pltpu.roll(x, 16, 1))
    return x * cos + partner * sin


def _ret_kernel(*refs, seq, rope, has_init, want_fin):
    it = iter(refs)
    q_ref, k_ref, v_ref, g_ref, rd_ref, gw_ref, gb_ref = (next(it) for _ in range(7))
    cos_ref = next(it) if rope else None
    sin_ref = next(it) if rope else None
    s0_ref = next(it) if has_init else None
    y_ref = next(it)
    fin_ref = next(it) if want_fin else None
    qs_ref, ks_ref, vec_ref, dec_ref, r_ref = (next(it) for _ in range(5))

    C = RET_CHUNK
    W = BRANCH_W
    n = seq // C
    rd = rd_ref[...]
    lg = jnp.minimum(rd, 0.0) - jnp.log(1.0 + jnp.exp(-jnp.abs(rd)))
    lgf, lgb = lg[0:1, :], lg[1:2, :]
    ii = lax.broadcasted_iota(jnp.int32, (C, C), 0)
    jj = lax.broadcasted_iota(jnp.int32, (C, C), 1)
    diff = (ii - jj).astype(F32)
    col = lax.broadcasted_iota(jnp.int32, (C, W), 0).astype(F32)
    ln = lambda h: slice(h * HEAD_DIM, (h + 1) * HEAD_DIM)

    vec_ref[0] = jnp.exp((col + 1.0) * lgf)
    vec_ref[1] = jnp.exp((C - 1.0 - col) * lgf)
    vec_ref[2] = jnp.exp((C - col) * lgb)
    vec_ref[3] = jnp.exp(col * lgb)
    chunk_f = jnp.exp(C * lgf)
    chunk_b = jnp.exp(C * lgb)
    for h in range(HEADS):
        gf = lgf[:, h * HEAD_DIM:h * HEAD_DIM + 1]
        gb = lgb[:, h * HEAD_DIM:h * HEAD_DIM + 1]
        dec_ref[h] = (jnp.where(diff >= 0, jnp.exp(jnp.maximum(diff, 0.0) * gf), 0.0)
                      + jnp.where(diff <= 0, jnp.exp(jnp.maximum(-diff, 0.0) * gb), 0.0))
        for d in range(2):
            r_ref[d, h] = s0_ref[0, d, h] if has_init else jnp.zeros((HEAD_DIM, HEAD_DIM), F32)

    def rows(c):
        return pl.ds(pl.multiple_of(c * C, C), C)

    def prep(c, carry):
        r = rows(c)
        q = q_ref[0, r, :]
        k = k_ref[0, r, :] * (HEAD_DIM ** -0.5)
        if rope:
            cos, sin = cos_ref[r, :], sin_ref[r, :]
            tiles = lambda x: [x[:, j * LANES:(j + 1) * LANES] for j in range(W // LANES)]
            q = jnp.concatenate([_rope(t, cos, sin) for t in tiles(q)], axis=1)
            k = jnp.concatenate([_rope(t, cos, sin) for t in tiles(k)], axis=1)
        qs_ref[r, :] = q
        ks_ref[r, :] = k
        y_ref[0, r, :] = jnp.zeros((C, W), F32)
        return carry

    lax.fori_loop(0, n, prep, 0)

    def step(i, carry):
        rf, rb = rows(i), rows(n - 1 - i)
        qf, kf, vf = qs_ref[rf, :], ks_ref[rf, :], v_ref[0, rf, :]
        qb, kb, vb = qs_ref[rb, :], ks_ref[rb, :], v_ref[0, rb, :]
        qfx, kfx = qf * vec_ref[0], kf * vec_ref[1]
        qbx, kbx = qb * vec_ref[2], kb * vec_ref[3]
        sf = [r_ref[0, h] for h in range(HEADS)]
        sb = [r_ref[1, h] for h in range(HEADS)]
        of, ob, nf, nb = [], [], [], []
        for h in range(HEADS):
            sc = _mm_nt(qf[:, ln(h)], kf[:, ln(h)]) * dec_ref[h]
            of.append(_mm(sc, vf[:, ln(h)]) + _mm(qfx[:, ln(h)], sf[h]))
            nf.append(sf[h] * chunk_f[:, ln(h)] + _mm_tn(kfx[:, ln(h)], vf[:, ln(h)]))
            ob.append(_mm(qbx[:, ln(h)], sb[h]))
            nb.append(sb[h] * chunk_b[:, ln(h)] + _mm_tn(kbx[:, ln(h)], vb[:, ln(h)]))
        yf, yb = y_ref[0, rf, :], y_ref[0, rb, :]
        y_ref[0, rf, :] = yf + jnp.concatenate(of, axis=1)
        y_ref[0, rb, :] = yb + jnp.concatenate(ob, axis=1)
        for h in range(HEADS):
            r_ref[0, h] = nf[h]
            r_ref[1, h] = nb[h]
        return carry

    lax.fori_loop(0, n, step, 0)
    if want_fin:
        for d in range(2):
            for h in range(HEADS):
                fin_ref[0, d, h] = r_ref[d, h]

    def post(c, carry):
        r = rows(c)
        g = g_ref[0, r, :]
        y_ref[0, r, :] = _head_norm(y_ref[0, r, :], gw_ref[...], gb_ref[...]) * (g * _sigmoid(g))
        return carry

    lax.fori_loop(0, n, post, 0)


def _retention(mix3, rd_l, gn_w, gn_b, rope_tabs, s0, want_fin):
    b, seq, _ = mix3.shape
    W = BRANCH_W
    rope = rope_tabs is not None
    has_init = s0 is not None
    col_spec = lambda j: pl.BlockSpec((1, seq, W), lambda i: (i, 0, j))
    full = lambda a: pl.BlockSpec(a.shape, lambda i: (0,) * a.ndim)
    in_specs = [col_spec(0), col_spec(1), col_spec(2), col_spec(3), full(rd_l), full(gn_w), full(gn_b)]
    args = [mix3, mix3, mix3, mix3, rd_l, gn_w, gn_b]
    if rope:
        in_specs += [full(rope_tabs[0]), full(rope_tabs[1])]
        args += list(rope_tabs)
    state_spec = pl.BlockSpec((1, 2, HEADS, HEAD_DIM, HEAD_DIM), lambda i: (i, 0, 0, 0, 0))
    if has_init:
        in_specs.append(state_spec)
        args.append(s0)
    out_specs = [pl.BlockSpec((1, seq, W), lambda i: (i, 0, 0))]
    out_shape = [jax.ShapeDtypeStruct((b, seq, W), F32)]
    if want_fin:
        out_specs.append(state_spec)
        out_shape.append(jax.ShapeDtypeStruct((b, 2, HEADS, HEAD_DIM, HEAD_DIM), F32))
    res = pl.pallas_call(
        functools.partial(_ret_kernel, seq=seq, rope=rope, has_init=has_init, want_fin=want_fin),
        grid=(b,),
        in_specs=in_specs,
        out_specs=out_specs,
        out_shape=out_shape,
        scratch_shapes=[pltpu.VMEM((seq, W), F32), pltpu.VMEM((seq, W), F32),
                        pltpu.VMEM((4, RET_CHUNK, W), F32), pltpu.VMEM((HEADS, RET_CHUNK, RET_CHUNK), F32),
                        pltpu.VMEM((2, HEADS, HEAD_DIM, HEAD_DIM), F32)],
        compiler_params=_cparams("parallel"),
        name="retention",
    )(*args)
    return (res[0], res[1]) if want_fin else (res[0], None)


def _softplus(z):
    return jnp.maximum(z, 0.0) + jnp.log(1.0 + jnp.exp(-jnp.abs(z)))


def _rwkv_kernel(*refs, seq, nb, has_init, want_fin):
    it = iter(refs)
    (r_ref, k_ref, v_ref, lora_ref, cw_ref, w0_ref, w2_ref, a0_ref, a2_ref, g2_ref,
     kk_w_ref, ka_ref, rk_ref, gw_ref, gb_ref) = (next(it) for _ in range(15))
    s0_ref = next(it) if has_init else None
    y_ref = next(it)
    fin_ref = next(it) if want_fin else None
    r_s, v_s, kk_s, g_s, bv_s, lw_s, b_s, kd_s, st_s = (next(it) for _ in range(9))

    C = RWKV_CHUNK
    n = seq // C
    W = BRANCH_W
    row_id = lax.broadcasted_iota(jnp.int32, (C, W), 0)

    def rows(c):
        return pl.ds(pl.multiple_of(c * C, C), C)

    def conv(ref, s, c, w):
        x = ref[s, rows(c), :]
        prev8 = ref[s, pl.ds(pl.multiple_of(jnp.maximum(c * C - 8, 0), 8), 8), :]
        next8 = ref[s, pl.ds(pl.multiple_of(jnp.minimum(c * C + C, seq - 8), 8), 8), :]
        prev_row = jnp.where(c > 0, prev8[7:8, :], 0.0)
        next_row = jnp.where(c < n - 1, next8[0:1, :], 0.0)
        xm = jnp.where(row_id == 0, prev_row, pltpu.roll(x, 1, 0))
        xp = jnp.where(row_id == C - 1, next_row, pltpu.roll(x, C - 1, 0))
        return w[0:1] * xm + w[1:2] * x + w[2:3] * xp

    def prep_seq(s, c):
        rws = rows(c)
        cw = cw_ref[...]
        r = conv(r_ref, s, c, cw[:, 0:W])
        k = conv(k_ref, s, c, cw[:, W:2 * W])
        v = conv(v_ref, s, c, cw[:, 2 * W:3 * W])
        lo = lora_ref[s, rws, :]
        dw = lo[:, 0:LORA_W]
        da = lo[:, LORA_W:LORA_W + LORA_A]
        dg = lo[:, LORA_W + LORA_A:]
        kk = k * kk_w_ref[...]
        kk = kk * lax.rsqrt(_head_sum(kk * kk) + 1e-12)
        y_ref[s, rws, :] = jnp.zeros((C, W), F32)
        r_s[s, rws, :] = r
        v_s[s, rws, :] = v
        kk_s[s, rws, :] = kk
        g_s[s, rws, :] = _mm(_sigmoid(dg), g2_ref[...])
        bv_s[s, rws, :] = _head_sum(r * k * rk_ref[...]) * v
        tdw = jnp.tanh(dw)
        for d in range(2):
            w_log = -_softplus(-(w0_ref[d:d + 1, :] + _mm(tdw, w2_ref[d]))) - 0.5
            a = _sigmoid(a0_ref[d:d + 1, :] + _mm(da, a2_ref[d]))
            lw_s[s, d, rws, :] = -jnp.exp(w_log)
            b_s[s, d, rws, :] = kk * a
            kd_s[s, d, rws, :] = k * (1.0 + (a - 1.0) * ka_ref[...])

    def prep(c, carry):
        for s in range(nb):
            prep_seq(s, c)
        return carry

    lax.fori_loop(0, n, prep, 0)

    ii = lax.broadcasted_iota(jnp.int32, (C, C), 0)
    jj = lax.broadcasted_iota(jnp.int32, (C, C), 1)
    incl = [ii >= jj, ii <= jj]
    strict = [ii > jj, ii < jj]

    streams = [(s, d) for s in range(nb) for d in range(2)]
    for s, d in streams:
        for h in range(HEADS):
            st_s[s, d, h] = s0_ref[s, d, h] if has_init else jnp.zeros((HEAD_DIM, HEAD_DIM), F32)

    def chunk(i, carry):
        rws, at, bt, kt, rt, bh, kh, etot, vc = ([None] * len(streams) for _ in range(9))
        for q, (s, d) in enumerate(streams):
            rws[q] = rows(i if d == 0 else n - 1 - i)
            rc, vc[q], kkc = r_s[s, rws[q], :], v_s[s, rws[q], :], kk_s[s, rws[q], :]
            lwc, bc, kc = lw_s[s, d, rws[q], :], b_s[s, d, rws[q], :], kd_s[s, d, rws[q], :]
            cum = jnp.dot(incl[d].astype(F32), lwc, precision=lax.Precision.HIGHEST,
                          preferred_element_type=F32)
            tot = cum[C - 1:C, :] if d == 0 else cum[0:1, :]
            pinv = jnp.exp(-cum)
            pend = jnp.exp(tot - cum)
            at[q] = -kkc * jnp.exp(cum - lwc)
            bt[q] = bc * pinv
            kt[q] = kc * pinv
            rt[q] = rc * jnp.exp(cum)
            bh[q] = bc * pend
            kh[q] = kc * pend
            etot[q] = jnp.exp(tot)
        chains = [(q, h) for q in range(len(streams)) for h in range(HEADS)]
        dirn = lambda q: streams[q][1]
        ln = lambda h: slice(h * HEAD_DIM, (h + 1) * HEAD_DIM)
        wk = [jnp.concatenate([bt[q][:, ln(h)], kt[q][:, ln(h)]], axis=0) for q, h in chains]
        ma = [_mm_nt(at[q][:, ln(h)], wk[j]) for j, (q, h) in enumerate(chains)]
        mr = [_mm_nt(rt[q][:, ln(h)], wk[j]) for j, (q, h) in enumerate(chains)]
        a_ak = [jnp.where(strict[dirn(q)], ma[j][:, C:2 * C], 0.0) for j, (q, h) in enumerate(chains)]
        p = [jnp.where(strict[dirn(q)], ma[j][:, 0:C], 0.0) for j, (q, h) in enumerate(chains)]
        a_rb = [jnp.where(incl[dirn(q)], mr[j][:, 0:C], 0.0) for j, (q, h) in enumerate(chains)]
        a_rk = [jnp.where(incl[dirn(q)], mr[j][:, C:2 * C], 0.0) for j, (q, h) in enumerate(chains)]
        vh = [vc[q][:, ln(h)] for q, h in chains]
        x = [jnp.concatenate([at[q][:, ln(h)], _mm(a_ak[j], vh[j])], axis=1)
             for j, (q, h) in enumerate(chains)]
        for step in range(6):
            pb = [p[j].astype(BF16) for j in range(len(chains))]
            xh = [x[j].astype(BF16) for j in range(len(chains))]
            if step < 3:
                xl = [(x[j] - xh[j].astype(F32)).astype(BF16) for j in range(len(chains))]
                px = [jnp.dot(pb[j], jnp.concatenate([xh[j], xl[j]], axis=1), preferred_element_type=F32)
                      for j in range(len(chains))]
                x = [x[j] + (px[j][:, 0:2 * HEAD_DIM] + px[j][:, 2 * HEAD_DIM:]) for j in range(len(chains))]
            else:
                x = [x[j] + jnp.dot(pb[j], xh[j], preferred_element_type=F32) for j in range(len(chains))]
            if step < 5:
                p = [jnp.dot(pb[j], pb[j], preferred_element_type=F32) for j in range(len(chains))]
        gh = [_mm_tn(x[j], bh[q][:, ln(h)]) for j, (q, h) in enumerate(chains)]
        vk = [_mm_tn_3pass(vh[j], kh[q][:, ln(h)]) for j, (q, h) in enumerate(chains)]
        ax = [_mm(a_rb[j], x[j]) for j in range(len(chains))]
        qt = [rt[q][:, ln(h)] + ax[j][:, 0:HEAD_DIM] for j, (q, h) in enumerate(chains)]
        y0 = [ax[j][:, HEAD_DIM:2 * HEAD_DIM] + _mm(a_rk[j], vh[j]) for j in range(len(chains))]
        ys = []
        for j, (q, h) in enumerate(chains):
            sq, d = streams[q]
            s = st_s[sq, d, h]
            ys.append(_mm_nt(qt[j], s) + y0[j])
            st_s[sq, d, h] = (s * etot[q][:, ln(h)] + _mm(s, gh[j][0:HEAD_DIM, :])
                              + gh[j][HEAD_DIM:2 * HEAD_DIM, :] + vk[j])
        for q, (s, d) in enumerate(streams):
            y = jnp.concatenate(ys[q * HEADS:(q + 1) * HEADS], axis=1)
            y_ref[s, rws[q], :] = y_ref[s, rws[q], :] + y
        return carry

    lax.fori_loop(0, n, chunk, 0)
    if want_fin:
        for s, d in streams:
            for h in range(HEADS):
                fin_ref[s, d, h] = st_s[s, d, h]

    def post(c, carry):
        rws = rows(c)
        for s in range(nb):
            y = _head_norm(y_ref[s, rws, :], gw_ref[...], gb_ref[...])
            y_ref[s, rws, :] = (y + bv_s[s, rws, :]) * g_s[s, rws, :]
        return carry

    lax.fori_loop(0, n, post, 0)


def _rwkv(mix3, lora3, lw, s0, want_fin):
    b, seq, _ = mix3.shape
    has_init = s0 is not None
    W = BRANCH_W
    nb = 2 if (seq <= 256 and b % 2 == 0) else 1
    col_spec = lambda j: pl.BlockSpec((nb, seq, W), lambda i: (i, 0, j))
    full = lambda a: pl.BlockSpec(a.shape, lambda i: (0,) * a.ndim)
    weights = [lw['conv'], lw['w0'], lw['w2'], lw['a0'], lw['a2'], lw['g2'],
               lw['k_k'], lw['k_a'], lw['r_k'], lw['gn_w'], lw['gn_b']]
    in_specs = [col_spec(4), col_spec(5), col_spec(6),
                pl.BlockSpec((nb, seq, COLS_LORA), lambda i: (i, 0, 0))] + [full(a) for a in weights]
    args = [mix3, mix3, mix3, lora3] + weights
    state_spec = pl.BlockSpec((nb, 2, HEADS, HEAD_DIM, HEAD_DIM), lambda i: (i, 0, 0, 0, 0))
    if has_init:
        in_specs.append(state_spec)
        args.append(s0)
    out_specs = [pl.BlockSpec((nb, seq, W), lambda i: (i, 0, 0))]
    out_shape = [jax.ShapeDtypeStruct((b, seq, W), F32)]
    if want_fin:
        out_specs.append(state_spec)
        out_shape.append(jax.ShapeDtypeStruct((b, 2, HEADS, HEAD_DIM, HEAD_DIM), F32))
    sw = pltpu.VMEM((nb, seq, W), F32)
    sw2 = pltpu.VMEM((nb, 2, seq, W), F32)
    res = pl.pallas_call(
        functools.partial(_rwkv_kernel, seq=seq, nb=nb, has_init=has_init, want_fin=want_fin),
        grid=(b // nb,),
        in_specs=in_specs,
        out_specs=out_specs,
        out_shape=out_shape,
        scratch_shapes=[sw, sw, sw, sw, sw, sw2, sw2, sw2,
                        pltpu.VMEM((nb, 2, HEADS, HEAD_DIM, HEAD_DIM), F32)],
        compiler_params=_cparams("parallel"),
        name="rwkv7",
    )(*args)
    return (res[0], res[1]) if want_fin else (res[0], None)


def _merge_kernel(ya_ref, yb_ref, gate_ref, x_ref, mod_ref, wa_ref, wb_ref, wo_ref, n2_ref, x1_ref, h2_ref):
    m = mod_ref[0]
    br_a = jnp.dot(ya_ref[...].astype(BF16), wa_ref[...], preferred_element_type=F32)
    br_b = jnp.dot(yb_ref[...].astype(BF16), wb_ref[...], preferred_element_type=F32)
    gate = gate_ref[...]
    merged = _sigmoid(gate[:, 0:D_MODEL]) * br_a + _sigmoid(gate[:, D_MODEL:]) * br_b
    mix = jnp.dot(merged.astype(BF16), wo_ref[...], preferred_element_type=F32)
    x1 = x_ref[...] + m[2:3] * mix
    x1_ref[...] = x1
    h2_ref[...] = (_rms(x1, n2_ref[...]) * (1.0 + m[4:5]) + m[3:4]).astype(BF16)


def _merge(ya2, yb2, gate2, x2, mod, mod_base, mod_step, seq, wa, wb, wo, norm2_g):
    n = x2.shape[0]
    tm = 256
    mod_idx = lambda i: (mod_base + mod_step * ((i * tm) // seq), 0, 0)
    row = lambda w: pl.BlockSpec((tm, w), lambda i: (i, 0))
    full = lambda a: pl.BlockSpec(a.shape, lambda i: (0,) * a.ndim)
    return pl.pallas_call(
        _merge_kernel,
        grid=(n // tm,),
        in_specs=[row(BRANCH_W), row(BRANCH_W), row(COLS_GATE), row(D_MODEL),
                  pl.BlockSpec((1, N_MOD, D_MODEL), mod_idx),
                  full(wa), full(wb), full(wo), full(norm2_g)],
        out_specs=[row(D_MODEL), row(D_MODEL)],
        out_shape=[jax.ShapeDtypeStruct((n, D_MODEL), F32), jax.ShapeDtypeStruct((n, D_MODEL), BF16)],
        compiler_params=_cparams("parallel"),
        name="merge_out",
    )(ya2, yb2, gate2, x2, mod, wa, wb, wo, norm2_g)


def _bitonic_pairs(n):
    pairs = []
    k = 2
    while k <= n:
        j = k // 2
        while j >= 1:
            for i in range(n):
                l = i ^ j
                if l > i:
                    pairs.append((i, l) if (i & k) == 0 else (l, i))
            j //= 2
        k *= 2
    return pairs


def _top_values_of_keys(s, k):
    r, t = s.shape
    nl = r // 8
    lv = [s[8 * j:8 * j + 8, :] for j in range(nl)]
    for a, b in _bitonic_pairs(nl):
        lv[a], lv[b] = jnp.maximum(lv[a], lv[b]), jnp.minimum(lv[a], lv[b])
    sid = lax.broadcasted_iota(jnp.int32, (8, t), 0)
    out = []
    for it in range(k):
        m = jnp.max(lv[0], axis=0, keepdims=True)
        out.append(m)
        first = jnp.min(jnp.where(lv[0] == m, sid, 8), axis=0, keepdims=True)
        pop = sid == first
        for j in range(min(nl, k - 1 - it)):
            lv[j] = jnp.where(pop, lv[j + 1] if j + 1 < nl else NEG_INF, lv[j])
    return out


def _peer_score_kernel(h_ref, wq_ref, keys_ref, thr_ref, g1_ref, s2_ref, e2_ref, cand_ref):
    qt = _mm_nt(wq_ref[...], h_ref[...])
    half = PEER_DQ // 2
    k1 = PEER_TOPK + 1
    pairs = [(i, j) for i in range(k1) for j in range(k1) if (i + 1) * (j + 1) <= k1]
    assert len(pairs) <= cand_ref.shape[0]
    for h in range(PEER_HEADS):
        s = [_mm(keys_ref[2 * h + c], qt[(2 * h + c) * half:(2 * h + c + 1) * half, :]) for c in range(2)]
        tops = [_top_values_of_keys(s[c], k1) for c in range(2)]
        cand_ref[...] = jnp.full(cand_ref.shape, NEG_INF, F32)
        for r, (i, j) in enumerate(pairs):
            cand_ref[r:r + 1, :] = tops[0][i] + tops[1][j]
        best = _top_values_of_keys(cand_ref[...], k1)
        mx = best[0]
        z = jnp.exp(best[0] - mx)
        for r in range(1, PEER_TOPK):
            z = z + jnp.exp(best[r] - mx)
        theta = 0.5 * (best[PEER_TOPK - 1] + best[PEER_TOPK])
        thr_ref[h] = theta - s[0]
        g1_ref[h] = jnp.exp(s[0] - tops[0][0]) * (0.5 / z)
        s2_ref[h] = s[1]
        e2_ref[h] = jnp.exp(s[1] - tops[1][0])


def _peer_scores(h2, wq_t, keys):
    n = h2.shape[0]
    tb = 256
    full = lambda a: pl.BlockSpec(a.shape, lambda i: (0,) * a.ndim)
    return pl.pallas_call(
        _peer_score_kernel,
        grid=(n // tb,),
        in_specs=[pl.BlockSpec((tb, D_MODEL), lambda i: (i, 0)), full(wq_t), full(keys)],
        out_specs=[pl.BlockSpec((PEER_HEADS, N_KEYS, tb), lambda i: (0, 0, i))] * 4,
        out_shape=[jax.ShapeDtypeStruct((PEER_HEADS, N_KEYS, n), F32)] * 4,
        scratch_shapes=[pltpu.VMEM((64, tb), F32)],
        compiler_params=_cparams("parallel"),
        name="peer_scores",
    )(h2, wq_t, keys)


def _gelu_tanh_x2(x):
    return x + x * jnp.tanh(x * (0.7978845608028654 + 0.035677408136300125 * (x * x)))


def _peer_expert_kernel(h_ref, u_ref, vt_ref, thr_ref, g1_ref, s2_ref, e2_ref, x1_ref, mod_ref, fg_ref, y_ref,
                        acc_ref, w_ref, *, rows_per_step):
    e = pl.program_id(1)
    tb = h_ref.shape[0]

    @pl.when(e == 0)
    def _():
        acc_ref[...] = jnp.zeros_like(acc_ref)

    st = _mm_nt(u_ref[...], h_ref[...])
    rows_per_chunk = 2
    partial = None
    for al in range(rows_per_step):
        a = e * rows_per_step + al
        r0 = al * N_KEYS
        thr = [thr_ref[h, pl.ds(a, 1), :] for h in range(PEER_HEADS)]
        g1 = [g1_ref[h, pl.ds(a, 1), :] for h in range(PEER_HEADS)]
        for lt in range(tb // LANES):
            ls = slice(lt * LANES, (lt + 1) * LANES)
            wsum = None
            for h in range(PEER_HEADS):
                w = jnp.where(s2_ref[h, :, ls] >= thr[h][:, ls], e2_ref[h, :, ls] * g1[h][:, ls], 0.0)
                wsum = w if wsum is None else wsum + w
            act = _gelu_tanh_x2(st[r0:r0 + N_KEYS, ls])
            w_ref[r0:r0 + N_KEYS, ls] = (wsum * act).astype(BF16)
        if (al + 1) % rows_per_chunk == 0:
            rs = slice((al + 1 - rows_per_chunk) * N_KEYS, (al + 1) * N_KEYS)
            d = jnp.dot(vt_ref[:, rs], w_ref[rs, :], preferred_element_type=F32)
            partial = d if partial is None else partial + d
    acc_ref[...] += partial

    @pl.when(e == pl.num_programs(1) - 1)
    def _():
        m = mod_ref[0]
        x2 = x1_ref[...] + m[5:6] * acc_ref[...].T
        y_ref[...] = _rms(x2, fg_ref[...])


def _peer_experts(h2, u_bf, vt_bf, stats, x1, mod, mod_base, mod_step, seq, final_g):
    n = h2.shape[0]
    tb = 512 if (mod_step == 0 or seq % 512 == 0) else seq
    rows_per_step = 8
    ec = rows_per_step * N_KEYS
    mod_idx = lambda i, e: (mod_base + mod_step * ((i * tb) // seq), 0, 0)
    stat_spec = pl.BlockSpec((PEER_HEADS, N_KEYS, tb), lambda i, e: (0, 0, i))
    return pl.pallas_call(
        functools.partial(_peer_expert_kernel, rows_per_step=rows_per_step),
        grid=(n // tb, N_EXPERTS // ec),
        in_specs=[pl.BlockSpec((tb, D_MODEL), lambda i, e: (i, 0)),
                  pl.BlockSpec((ec, D_MODEL), lambda i, e: (e, 0)),
                  pl.BlockSpec((D_MODEL, ec), lambda i, e: (0, e)),
                  stat_spec, stat_spec, stat_spec, stat_spec,
                  pl.BlockSpec((tb, D_MODEL), lambda i, e: (i, 0)),
                  pl.BlockSpec((1, N_MOD, D_MODEL), mod_idx),
                  pl.BlockSpec((1, D_MODEL), lambda i, e: (0, 0))],
        out_specs=pl.BlockSpec((tb, D_MODEL), lambda i, e: (i, 0)),
        out_shape=jax.ShapeDtypeStruct((n, D_MODEL), F32),
        scratch_shapes=[pltpu.VMEM((D_MODEL, tb), F32), pltpu.VMEM((ec, tb), BF16)],
        compiler_params=_cparams("parallel", "arbitrary"),
        name="peer_experts",
    )(h2, u_bf, vt_bf, *stats, x1, mod, final_g)


def _rope_tables(seq):
    pos = jnp.arange(seq, dtype=jnp.int32)
    lane = jnp.arange(LANES, dtype=jnp.int32) % HEAD_DIM
    use_col = (lane // 32) == 1
    p = jnp.where(use_col[None, :], (pos % GRID_W)[:, None], (pos // GRID_W)[:, None]).astype(F32)
    inv = ROPE_BASE ** (-(lane % 16).astype(F32) / 16.0)
    ang = p * inv[None, :]
    first = (lane % 32) < 16
    return jnp.cos(ang), jnp.where(first[None, :], -jnp.sin(ang), jnp.sin(ang))


def _trunk_path(x, mod, mod_base, mod_step, rope_tabs, s_ret0, s_rwkv0, want_fin, lw, final_g):
    b, seq, _ = x.shape
    x2 = x.reshape(b * seq, D_MODEL)
    mix, lora, gate = _in_proj(x2, mod, mod_base, mod_step, seq, lw['norm1_g'], lw['w_in'])
    mix3 = mix.reshape(b, seq, COLS_MIX)
    lora3 = lora.reshape(b, seq, COLS_LORA)
    ya, ret_fin = _retention(mix3, lw['ret_decay'], lw['ret_gn_w'], lw['ret_gn_b'], rope_tabs, s_ret0, want_fin)
    yb, rwkv_fin = _rwkv(mix3, lora3, lw['rwkv'], s_rwkv0, want_fin)
    x1, h2 = _merge(ya.reshape(b * seq, BRANCH_W), yb.reshape(b * seq, BRANCH_W), gate, x2, mod, mod_base,
                    mod_step, seq, lw['w_br_a'], lw['w_br_b'], lw['w_out'], lw['norm2_g'])
    stats = _peer_scores(h2, lw['peer_wq_t'], lw['peer_keys'])
    y = _peer_experts(h2, lw['peer_u'], lw['peer_vt'], stats, x1, mod, mod_base, mod_step, seq, final_g)
    return y.reshape(b, seq, D_MODEL), x1.reshape(b, seq, D_MODEL), ret_fin, rwkv_fin


def kernel(x_prompt, x_sample, state_ret, state_rwkv, c, c_ctx, ada_w, ada_b, norm1_g, w_in, ret_decay, ret_gn_w, ret_gn_b, rwkv_conv, rwkv_w0, rwkv_w2, rwkv_a0, rwkv_a2, rwkv_g2, rwkv_k_k, rwkv_k_a, rwkv_r_k, rwkv_gn_w, rwkv_gn_b, w_br_a, w_br_b, w_out, norm2_g, peer_wq, peer_keys, peer_u, peer_v, final_norm_g):
    assert w_in.shape[0] == 1, "the final norm is fused into the layer's last kernel: single trunk layer only"
    row = lambda a: a.reshape(1, -1)
    cc = jnp.concatenate([c_ctx[None, :], c], axis=0)
    cc = jnp.pad(cc, ((0, (-cc.shape[0]) % 8), (0, 0)))
    rope_tabs = _rope_tables(x_sample.shape[1])
    final_g = row(final_norm_g)

    if True:
        l = 0
        lw = {
            'norm1_g': row(norm1_g[l]), 'w_in': w_in[l].astype(BF16),
            'ret_decay': jnp.repeat(ret_decay[l], HEAD_DIM, axis=1),
            'ret_gn_w': row(ret_gn_w[l]), 'ret_gn_b': row(ret_gn_b[l]),
            'rwkv': {'conv': rwkv_conv[l], 'w0': rwkv_w0[l], 'w2': rwkv_w2[l].astype(BF16), 'a0': rwkv_a0[l],
                     'a2': rwkv_a2[l].astype(BF16), 'g2': rwkv_g2[l].astype(BF16), 'k_k': row(rwkv_k_k[l]),
                     'k_a': row(rwkv_k_a[l]), 'r_k': row(rwkv_r_k[l]), 'gn_w': row(rwkv_gn_w[l]),
                     'gn_b': row(rwkv_gn_b[l])},
            'w_br_a': w_br_a[l].astype(BF16), 'w_br_b': w_br_b[l].astype(BF16), 'w_out': w_out[l].astype(BF16),
            'norm2_g': row(norm2_g[l]),
            'peer_wq_t': peer_wq[l].astype(BF16).T,
            'peer_keys': peer_keys[l].reshape(2 * PEER_HEADS, N_KEYS, PEER_DQ // 2).astype(BF16),
            'peer_u': peer_u[l].astype(BF16), 'peer_vt': peer_v[l].astype(BF16).T,
        }
        mod = _adaln(cc, ada_w[l], row(ada_b[l])).reshape(cc.shape[0], N_MOD, D_MODEL)
        yp, _, ret_fin, rwkv_fin = _trunk_path(x_prompt, mod, 0, 0, None, None, None, True, lw, final_g)
        ys, _, _, _ = _trunk_path(x_sample, mod, 1, 1, rope_tabs, state_ret[:, l], state_rwkv[:, l], False, lw,
                                  final_g)
    return (yp, ys, ret_fin[:, None], rwkv_fin[:, None])
```

```python
import functools

import jax
import jax.numpy as jnp
from jax import lax
from jax.experimental import pallas as pl
from jax.experimental.pallas import tpu as pltpu

F32 = jnp.float32
BF16 = jnp.bfloat16

D_MODEL = 1024
GRID_W = 64
N_MOD = 6
HEADS = 8
HEAD_DIM = 64
BRANCH_W = HEADS * HEAD_DIM
RET_CHUNK = 128
RWKV_CHUNK = 64
LORA_W = 64
LORA_A = 64
LORA_G = 128
PEER_HEADS = 8
N_KEYS = 128
N_EXPERTS = N_KEYS * N_KEYS
PEER_DQ = 256
PEER_TOPK = 16
ROPE_BASE = 10000.0
NORM_EPS = 1e-6
GN_EPS = 64e-5
COLS_MIX = 7 * BRANCH_W
COLS_LORA = LORA_W + LORA_A + LORA_G
COLS_GATE = 2 * D_MODEL
IN_COLS = COLS_MIX + COLS_LORA + COLS_GATE

V7X_VMEM_LIMIT_BYTES = 56 * 1024 * 1024
LANES = 128
NEG_INF = float("-inf")


def _cparams(*sem):
    return pltpu.CompilerParams(dimension_semantics=sem, vmem_limit_bytes=V7X_VMEM_LIMIT_BYTES)


def _mm(a, b):
    return jnp.dot(a.astype(BF16), b.astype(BF16), preferred_element_type=F32)


def _mm_nt(a, b):
    return lax.dot_general(a.astype(BF16), b.astype(BF16), (((1,), (1,)), ((), ())),
                           preferred_element_type=F32)


def _mm_tn(a, b):
    return lax.dot_general(a.astype(BF16), b.astype(BF16), (((0,), (0,)), ((), ())),
                           preferred_element_type=F32)


def _mm_tn_3pass(a, b):
    a_hi, b_hi = a.astype(BF16), b.astype(BF16)
    a_lo = (a - a_hi.astype(F32)).astype(BF16)
    b_lo = (b - b_hi.astype(F32)).astype(BF16)
    dims = (((0,), (0,)), ((), ()))
    dot = lambda x, y: lax.dot_general(x, y, dims, preferred_element_type=F32)
    return dot(a_hi, b_hi) + (dot(a_hi, b_lo) + dot(a_lo, b_hi))


def _sigmoid(x):
    return 1.0 / (1.0 + jnp.exp(-x))


def _rms(x, g):
    return x * lax.rsqrt(jnp.mean(x * x, axis=-1, keepdims=True) + NORM_EPS) * g


def _head_sum(x):
    t, w = x.shape
    lo = lax.broadcasted_iota(jnp.int32, (t, LANES), 1) < HEAD_DIM
    outs = []
    for j in range(w // LANES):
        xt = x[:, j * LANES:(j + 1) * LANES]
        s_lo = jnp.sum(jnp.where(lo, xt, 0.0), axis=-1, keepdims=True)
        s_hi = jnp.sum(jnp.where(lo, 0.0, xt), axis=-1, keepdims=True)
        outs.append(jnp.where(lo, s_lo, s_hi))
    return outs[0] if len(outs) == 1 else jnp.concatenate(outs, axis=-1)


def _head_norm(y, w, b):
    mu = _head_sum(y) * (1.0 / HEAD_DIM)
    d = y - mu
    var = _head_sum(d * d) * (1.0 / HEAD_DIM)
    return d * lax.rsqrt(var + GN_EPS) * w + b


def _mod_kernel(c_ref, w_ref, b_ref, o_ref):
    c = c_ref[...]
    o_ref[...] = _mm(c * _sigmoid(c), w_ref[...]) + b_ref[...]


def _adaln(cc, ada_w, ada_b):
    rows = cc.shape[0]
    n = ada_w.shape[1]
    tn = n // 4
    return pl.pallas_call(
        _mod_kernel,
        grid=(n // tn,),
        in_specs=[pl.BlockSpec((rows, D_MODEL), lambda j: (0, 0)),
                  pl.BlockSpec((D_MODEL, tn), lambda j: (0, j)),
                  pl.BlockSpec((1, tn), lambda j: (0, j))],
        out_specs=pl.BlockSpec((rows, tn), lambda j: (0, j)),
        out_shape=jax.ShapeDtypeStruct((rows, n), F32),
        compiler_params=_cparams("parallel"),
        name="adaln_mod",
    )(cc, ada_w, ada_b)


def _in_kernel(x_ref, mod_ref, g_ref, w_ref, mix_ref, lora_ref, gate_ref):
    m = mod_ref[0]
    h = (_rms(x_ref[...], g_ref[...]) * (1.0 + m[1:2]) + m[0:1]).astype(BF16)
    mix_ref[...] = jnp.dot(h, w_ref[:, 0:COLS_MIX], preferred_element_type=F32)
    lora_ref[...] = jnp.dot(h, w_ref[:, COLS_MIX:COLS_MIX + COLS_LORA], preferred_element_type=F32)
    gate_ref[...] = jnp.dot(h, w_ref[:, COLS_MIX + COLS_LORA:IN_COLS], preferred_element_type=F32)


def _in_proj(x2, mod, mod_base, mod_step, seq, norm_g, w_in_bf):
    n = x2.shape[0]
    tm = 256
    mod_idx = lambda i: (mod_base + mod_step * ((i * tm) // seq), 0, 0)
    return pl.pallas_call(
        _in_kernel,
        grid=(n // tm,),
        in_specs=[pl.BlockSpec((tm, D_MODEL), lambda i: (i, 0)),
                  pl.BlockSpec((1, N_MOD, D_MODEL), mod_idx),
                  pl.BlockSpec((1, D_MODEL), lambda i: (0, 0)),
                  pl.BlockSpec((D_MODEL, IN_COLS), lambda i: (0, 0))],
        out_specs=[pl.BlockSpec((tm, COLS_MIX), lambda i: (i, 0)),
                   pl.BlockSpec((tm, COLS_LORA), lambda i: (i, 0)),
                   pl.BlockSpec((tm, COLS_GATE), lambda i: (i, 0))],
        out_shape=[jax.ShapeDtypeStruct((n, COLS_MIX), F32),
                   jax.ShapeDtypeStruct((n, COLS_LORA), F32),
                   jax.ShapeDtypeStruct((n, COLS_GATE), F32)],
        compiler_params=_cparams("parallel"),
        name="in_proj",
    )(x2, mod, norm_g, w_in_bf)


def _rope(x, cos, sin):
    lane = lax.broadcasted_iota(jnp.int32, x.shape, 1)
    first = (lane % 32) < 16
    partner = jnp.where(first, pltpu.roll(x, LANES - 16, 1), pltpu.roll(x, 16, 1))
    return x * cos + partner * sin


def _ret_kernel(*refs, seq, rope, has_init, want_fin):
    it = iter(refs)
    q_ref, k_ref, v_ref, g_ref, rd_ref, gw_ref, gb_ref = (next(it) for _ in range(7))
    cos_ref = next(it) if rope else None
    sin_ref = next(it) if rope else None
    s0_ref = next(it) if has_init else None
    y_ref = next(it)
    fin_ref = next(it) if want_fin else None
    qs_ref, ks_ref, vec_ref, dec_ref, r_ref = (next(it) for _ in range(5))

    C = RET_CHUNK
    W = BRANCH_W
    n = seq // C
    rd = rd_ref[...]
    lg = jnp.minimum(rd, 0.0) - jnp.log(1.0 + jnp.exp(-jnp.abs(rd)))
    lgf, lgb = lg[0:1, :], lg[1:2, :]
    ii = lax.broadcasted_iota(jnp.int32, (C, C), 0)
    jj = lax.broadcasted_iota(jnp.int32, (C, C), 1)
    diff = (ii - jj).astype(F32)
    col = lax.broadcasted_iota(jnp.int32, (C, W), 0).astype(F32)
    ln = lambda h: slice(h * HEAD_DIM, (h + 1) * HEAD_DIM)

    vec_ref[0] = jnp.exp((col + 1.0) * lgf)
    vec_ref[1] = jnp.exp((C - 1.0 - col) * lgf)
    vec_ref[2] = jnp.exp((C - col) * lgb)
    vec_ref[3] = jnp.exp(col * lgb)
    chunk_f = jnp.exp(C * lgf)
    chunk_b = jnp.exp(C * lgb)
    for h in range(HEADS):
        gf = lgf[:, h * HEAD_DIM:h * HEAD_DIM + 1]
        gb = lgb[:, h * HEAD_DIM:h * HEAD_DIM + 1]
        dec_ref[h] = (jnp.where(diff >= 0, jnp.exp(jnp.maximum(diff, 0.0) * gf), 0.0)
                      + jnp.where(diff <= 0, jnp.exp(jnp.maximum(-diff, 0.0) * gb), 0.0))
        for d in range(2):
            r_ref[d, h] = s0_ref[0, d, h] if has_init else jnp.zeros((HEAD_DIM, HEAD_DIM), F32)

    def rows(c):
        return pl.ds(pl.multiple_of(c * C, C), C)

    def prep(c, carry):
        r = rows(c)
        q = q_ref[0, r, :]
        k = k_ref[0, r, :] * (HEAD_DIM ** -0.5)
        if rope:
            cos, sin = cos_ref[r, :], sin_ref[r, :]
            tiles = lambda x: [x[:, j * LANES:(j + 1) * LANES] for j in range(W // LANES)]
            q = jnp.concatenate([_rope(t, cos, sin) for t in tiles(q)], axis=1)
            k = jnp.concatenate([_rope(t, cos, sin) for t in tiles(k)], axis=1)
        qs_ref[r, :] = q
        ks_ref[r, :] = k
        y_ref[0, r, :] = jnp.zeros((C, W), F32)
        return carry

    lax.fori_loop(0, n, prep, 0)

    def step(i, carry):
        rf, rb = rows(i), rows(n - 1 - i)
        qf, kf, vf = qs_ref[rf, :], ks_ref[rf, :], v_ref[0, rf, :]
        qb, kb, vb = qs_ref[rb, :], ks_ref[rb, :], v_ref[0, rb, :]
        qfx, kfx = qf * vec_ref[0], kf * vec_ref[1]
        qbx, kbx = qb * vec_ref[2], kb * vec_ref[3]
        sf = [r_ref[0, h] for h in range(HEADS)]
        sb = [r_ref[1, h] for h in range(HEADS)]
        of, ob, nf, nb = [], [], [], []
        for h in range(HEADS):
            sc = _mm_nt(qf[:, ln(h)], kf[:, ln(h)]) * dec_ref[h]
            of.append(_mm(sc, vf[:, ln(h)]) + _mm(qfx[:, ln(h)], sf[h]))
            nf.append(sf[h] * chunk_f[:, ln(h)] + _mm_tn(kfx[:, ln(h)], vf[:, ln(h)]))
            ob.append(_mm(qbx[:, ln(h)], sb[h]))
            nb.append(sb[h] * chunk_b[:, ln(h)] + _mm_tn(kbx[:, ln(h)], vb[:, ln(h)]))
        yf, yb = y_ref[0, rf, :], y_ref[0, rb, :]
        y_ref[0, rf, :] = yf + jnp.concatenate(of, axis=1)
        y_ref[0, rb, :] = yb + jnp.concatenate(ob, axis=1)
        for h in range(HEADS):
            r_ref[0, h] = nf[h]
            r_ref[1, h] = nb[h]
        return carry

    lax.fori_loop(0, n, step, 0)
    if want_fin:
        for d in range(2):
            for h in range(HEADS):
                fin_ref[0, d, h] = r_ref[d, h]

    def post(c, carry):
        r = rows(c)
        g = g_ref[0, r, :]
        y_ref[0, r, :] = _head_norm(y_ref[0, r, :], gw_ref[...], gb_ref[...]) * (g * _sigmoid(g))
        return carry

    lax.fori_loop(0, n, post, 0)


def _retention(mix3, rd_l, gn_w, gn_b, rope_tabs, s0, want_fin):
    b, seq, _ = mix3.shape
    W = BRANCH_W
    rope = rope_tabs is not None
    has_init = s0 is not None
    col_spec = lambda j: pl.BlockSpec((1, seq, W), lambda i: (i, 0, j))
    full = lambda a: pl.BlockSpec(a.shape, lambda i: (0,) * a.ndim)
    in_specs = [col_spec(0), col_spec(1), col_spec(2), col_spec(3), full(rd_l), full(gn_w), full(gn_b)]
    args = [mix3, mix3, mix3, mix3, rd_l, gn_w, gn_b]
    if rope:
        in_specs += [full(rope_tabs[0]), full(rope_tabs[1])]
        args += list(rope_tabs)
    state_spec = pl.BlockSpec((1, 2, HEADS, HEAD_DIM, HEAD_DIM), lambda i: (i, 0, 0, 0, 0))
    if has_init:
        in_specs.append(state_spec)
        args.append(s0)
    out_specs = [pl.BlockSpec((1, seq, W), lambda i: (i, 0, 0))]
    out_shape = [jax.ShapeDtypeStruct((b, seq, W), F32)]
    if want_fin:
        out_specs.append(state_spec)
        out_shape.append(jax.ShapeDtypeStruct((b, 2, HEADS, HEAD_DIM, HEAD_DIM), F32))
    res = pl.pallas_call(
        functools.partial(_ret_kernel, seq=seq, rope=rope, has_init=has_init, want_fin=want_fin),
        grid=(b,),
        in_specs=in_specs,
        out_specs=out_specs,
        out_shape=out_shape,
        scratch_shapes=[pltpu.VMEM((seq, W), F32), pltpu.VMEM((seq, W), F32),
                        pltpu.VMEM((4, RET_CHUNK, W), F32), pltpu.VMEM((HEADS, RET_CHUNK, RET_CHUNK), F32),
                        pltpu.VMEM((2, HEADS, HEAD_DIM, HEAD_DIM), F32)],
        compiler_params=_cparams("parallel"),
        name="retention",
    )(*args)
    return (res[0], res[1]) if want_fin else (res[0], None)


def _softplus(z):
    return jnp.maximum(z, 0.0) + jnp.log(1.0 + jnp.exp(-jnp.abs(z)))


def _rwkv_kernel(*refs, seq, nb, has_init, want_fin):
    it = iter(refs)
    (r_ref, k_ref, v_ref, lora_ref, cw_ref, w0_ref, w2_ref, a0_ref, a2_ref, g2_ref,
     kk_w_ref, ka_ref, rk_ref, gw_ref, gb_ref) = (next(it) for _ in range(15))
    s0_ref = next(it) if has_init else None
    y_ref = next(it)
    fin_ref = next(it) if want_fin else None
    r_s, v_s, kk_s, g_s, bv_s, lw_s, b_s, kd_s, st_s = (next(it) for _ in range(9))

    C = RWKV_CHUNK
    n = seq // C
    W = BRANCH_W
    row_id = lax.broadcasted_iota(jnp.int32, (C, W), 0)

    def rows(c):
        return pl.ds(pl.multiple_of(c * C, C), C)

    def conv(ref, s, c, w):
        x = ref[s, rows(c), :]
        prev8 = ref[s, pl.ds(pl.multiple_of(jnp.maximum(c * C - 8, 0), 8), 8), :]
        next8 = ref[s, pl.ds(pl.multiple_of(jnp.minimum(c * C + C, seq - 8), 8), 8), :]
        prev_row = jnp.where(c > 0, prev8[7:8, :], 0.0)
        next_row = jnp.where(c < n - 1, next8[0:1, :], 0.0)
        xm = jnp.where(row_id == 0, prev_row, pltpu.roll(x, 1, 0))
        xp = jnp.where(row_id == C - 1, next_row, pltpu.roll(x, C - 1, 0))
        return w[0:1] * xm + w[1:2] * x + w[2:3] * xp

    def prep_seq(s, c):
        rws = rows(c)
        cw = cw_ref[...]
        r = conv(r_ref, s, c, cw[:, 0:W])
        k = conv(k_ref, s, c, cw[:, W:2 * W])
        v = conv(v_ref, s, c, cw[:, 2 * W:3 * W])
        lo = lora_ref[s, rws, :]
        dw = lo[:, 0:LORA_W]
        da = lo[:, LORA_W:LORA_W + LORA_A]
        dg = lo[:, LORA_W + LORA_A:]
        kk = k * kk_w_ref[...]
        kk = kk * lax.rsqrt(_head_sum(kk * kk) + 1e-12)
        y_ref[s, rws, :] = jnp.zeros((C, W), F32)
        r_s[s, rws, :] = r
        v_s[s, rws, :] = v
        kk_s[s, rws, :] = kk
        g_s[s, rws, :] = _mm(_sigmoid(dg), g2_ref[...])
        bv_s[s, rws, :] = _head_sum(r * k * rk_ref[...]) * v
        tdw = jnp.tanh(dw)
        for d in range(2):
            w_log = -_softplus(-(w0_ref[d:d + 1, :] + _mm(tdw, w2_ref[d]))) - 0.5
            a = _sigmoid(a0_ref[d:d + 1, :] + _mm(da, a2_ref[d]))
            lw_s[s, d, rws, :] = -jnp.exp(w_log)
            b_s[s, d, rws, :] = kk * a
            kd_s[s, d, rws, :] = k * (1.0 + (a - 1.0) * ka_ref[...])

    def prep(c, carry):
        for s in range(nb):
            prep_seq(s, c)
        return carry

    lax.fori_loop(0, n, prep, 0)

    ii = lax.broadcasted_iota(jnp.int32, (C, C), 0)
    jj = lax.broadcasted_iota(jnp.int32, (C, C), 1)
    incl = [ii >= jj, ii <= jj]
    strict = [ii > jj, ii < jj]

    cpi = 2 if (nb == 1 and n % 2 == 0) else 1
    streams = [(s, d, u) for s in range(nb) for d in range(2) for u in range(cpi)]
    for s in range(nb):
        for d in range(2):
            for h in range(HEADS):
                st_s[s, d, h] = s0_ref[s, d, h] if has_init else jnp.zeros((HEAD_DIM, HEAD_DIM), F32)

    def chunk(i, carry):
        rws, at, bt, kt, rt, bh, kh, etot, vc = ([None] * len(streams) for _ in range(9))
        for q, (s, d, u) in enumerate(streams):
            rws[q] = rows(i * cpi + u if d == 0 else n - 1 - (i * cpi + u))
            rc, vc[q], kkc = r_s[s, rws[q], :], v_s[s, rws[q], :], kk_s[s, rws[q], :]
            lwc, bc, kc = lw_s[s, d, rws[q], :], b_s[s, d, rws[q], :], kd_s[s, d, rws[q], :]
            cum = jnp.dot(incl[d].astype(F32), lwc, precision=lax.Precision.HIGHEST,
                          preferred_element_type=F32)
            tot = cum[C - 1:C, :] if d == 0 else cum[0:1, :]
            pinv = jnp.exp(-cum)
            pend = jnp.exp(tot - cum)
            at[q] = -kkc * jnp.exp(cum - lwc)
            bt[q] = bc * pinv
            kt[q] = kc * pinv
            rt[q] = rc * jnp.exp(cum)
            bh[q] = bc * pend
            kh[q] = kc * pend
            etot[q] = jnp.exp(tot)
        chains = [(q, h) for q in range(len(streams)) for h in range(HEADS)]
        dirn = lambda q: streams[q][1]
        ln = lambda h: slice(h * HEAD_DIM, (h + 1) * HEAD_DIM)
        wk = [jnp.concatenate([bt[q][:, ln(h)], kt[q][:, ln(h)]], axis=0) for q, h in chains]
        ma = [_mm_nt(at[q][:, ln(h)], wk[j]) for j, (q, h) in enumerate(chains)]
        mr = [_mm_nt(rt[q][:, ln(h)], wk[j]) for j, (q, h) in enumerate(chains)]
        a_ak = [jnp.where(strict[dirn(q)], ma[j][:, C:2 * C], 0.0) for j, (q, h) in enumerate(chains)]
        p = [jnp.where(strict[dirn(q)], ma[j][:, 0:C], 0.0) for j, (q, h) in enumerate(chains)]
        a_rb = [jnp.where(incl[dirn(q)], mr[j][:, 0:C], 0.0) for j, (q, h) in enumerate(chains)]
        a_rk = [jnp.where(incl[dirn(q)], mr[j][:, C:2 * C], 0.0) for j, (q, h) in enumerate(chains)]
        vh = [vc[q][:, ln(h)] for q, h in chains]
        x = [jnp.concatenate([at[q][:, ln(h)], _mm(a_ak[j], vh[j])], axis=1)
             for j, (q, h) in enumerate(chains)]
        for step in range(6):
            pb = [p[j].astype(BF16) for j in range(len(chains))]
            xh = [x[j].astype(BF16) for j in range(len(chains))]
            if step < 3:
                xl = [(x[j] - xh[j].astype(F32)).astype(BF16) for j in range(len(chains))]
                px = [jnp.dot(pb[j], jnp.concatenate([xh[j], xl[j]], axis=1), preferred_element_type=F32)
                      for j in range(len(chains))]
                x = [x[j] + (px[j][:, 0:2 * HEAD_DIM] + px[j][:, 2 * HEAD_DIM:]) for j in range(len(chains))]
            else:
                x = [x[j] + jnp.dot(pb[j], xh[j], preferred_element_type=F32) for j in range(len(chains))]
            if step < 5:
                p = [jnp.dot(pb[j], pb[j], preferred_element_type=F32) for j in range(len(chains))]
        gh = [_mm_tn(x[j], bh[q][:, ln(h)]) for j, (q, h) in enumerate(chains)]
        vk = [_mm_tn_3pass(vh[j], kh[q][:, ln(h)]) for j, (q, h) in enumerate(chains)]
        ax = [_mm(a_rb[j], x[j]) for j in range(len(chains))]
        qt = [rt[q][:, ln(h)] + ax[j][:, 0:HEAD_DIM] for j, (q, h) in enumerate(chains)]
        y0 = [ax[j][:, HEAD_DIM:2 * HEAD_DIM] + _mm(a_rk[j], vh[j]) for j in range(len(chains))]
        ys = [None] * len(chains)
        scans = [(s, d, h) for s in range(nb) for d in range(2) for h in range(HEADS)]
        st = {k: st_s[k] for k in scans}
        for u in range(cpi):
            for s, d, h in scans:
                q = streams.index((s, d, u))
                j = q * HEADS + h
                cur = st[s, d, h]
                ys[j] = _mm_nt(qt[j], cur) + y0[j]
                st[s, d, h] = (cur * etot[q][:, ln(h)] + _mm(cur, gh[j][0:HEAD_DIM, :])
                               + gh[j][HEAD_DIM:2 * HEAD_DIM, :] + vk[j])
        for k in scans:
            st_s[k] = st[k]
        for q, (s, d, u) in enumerate(streams):
            y = jnp.concatenate(ys[q * HEADS:(q + 1) * HEADS], axis=1)
            y_ref[s, rws[q], :] = y_ref[s, rws[q], :] + y
        return carry

    lax.fori_loop(0, n // cpi, chunk, 0)
    if want_fin:
        for s in range(nb):
            for d in range(2):
                for h in range(HEADS):
                    fin_ref[s, d, h] = st_s[s, d, h]

    def post(c, carry):
        rws = rows(c)
        for s in range(nb):
            y = _head_norm(y_ref[s, rws, :], gw_ref[...], gb_ref[...])
            y_ref[s, rws, :] = (y + bv_s[s, rws, :]) * g_s[s, rws, :]
        return carry

    lax.fori_loop(0, n, post, 0)


def _rwkv(mix3, lora3, lw, s0, want_fin):
    b, seq, _ = mix3.shape
    has_init = s0 is not None
    W = BRANCH_W
    nb = 2 if (seq <= 256 and b % 2 == 0) else 1
    col_spec = lambda j: pl.BlockSpec((nb, seq, W), lambda i: (i, 0, j))
    full = lambda a: pl.BlockSpec(a.shape, lambda i: (0,) * a.ndim)
    weights = [lw['conv'], lw['w0'], lw['w2'], lw['a0'], lw['a2'], lw['g2'],
               lw['k_k'], lw['k_a'], lw['r_k'], lw['gn_w'], lw['gn_b']]
    in_specs = [col_spec(4), col_spec(5), col_spec(6),
                pl.BlockSpec((nb, seq, COLS_LORA), lambda i: (i, 0, 0))] + [full(a) for a in weights]
    args = [mix3, mix3, mix3, lora3] + weights
    state_spec = pl.BlockSpec((nb, 2, HEADS, HEAD_DIM, HEAD_DIM), lambda i: (i, 0, 0, 0, 0))
    if has_init:
        in_specs.append(state_spec)
        args.append(s0)
    out_specs = [pl.BlockSpec((nb, seq, W), lambda i: (i, 0, 0))]
    out_shape = [jax.ShapeDtypeStruct((b, seq, W), F32)]
    if want_fin:
        out_specs.append(state_spec)
        out_shape.append(jax.ShapeDtypeStruct((b, 2, HEADS, HEAD_DIM, HEAD_DIM), F32))
    sw = pltpu.VMEM((nb, seq, W), F32)
    sw2 = pltpu.VMEM((nb, 2, seq, W), F32)
    res = pl.pallas_call(
        functools.partial(_rwkv_kernel, seq=seq, nb=nb, has_init=has_init, want_fin=want_fin),
        grid=(b // nb,),
        in_specs=in_specs,
        out_specs=out_specs,
        out_shape=out_shape,
        scratch_shapes=[sw, sw, sw, sw, sw, sw2, sw2, sw2,
                        pltpu.VMEM((nb, 2, HEADS, HEAD_DIM, HEAD_DIM), F32)],
        compiler_params=_cparams("parallel"),
        name="rwkv7",
    )(*args)
    return (res[0], res[1]) if want_fin else (res[0], None)


def _merge_kernel(ya_ref, yb_ref, gate_ref, x_ref, mod_ref, wa_ref, wb_ref, wo_ref, n2_ref, x1_ref, h2_ref):
    m = mod_ref[0]
    br_a = jnp.dot(ya_ref[...].astype(BF16), wa_ref[...], preferred_element_type=F32)
    br_b = jnp.dot(yb_ref[...].astype(BF16), wb_ref[...], preferred_element_type=F32)
    gate = gate_ref[...]
    merged = _sigmoid(gate[:, 0:D_MODEL]) * br_a + _sigmoid(gate[:, D_MODEL:]) * br_b
    mix = jnp.dot(merged.astype(BF16), wo_ref[...], preferred_element_type=F32)
    x1 = x_ref[...] + m[2:3] * mix
    x1_ref[...] = x1
    h2_ref[...] = (_rms(x1, n2_ref[...]) * (1.0 + m[4:5]) + m[3:4]).astype(BF16)


def _merge(ya2, yb2, gate2, x2, mod, mod_base, mod_step, seq, wa, wb, wo, norm2_g):
    n = x2.shape[0]
    tm = 256
    mod_idx = lambda i: (mod_base + mod_step * ((i * tm) // seq), 0, 0)
    row = lambda w: pl.BlockSpec((tm, w), lambda i: (i, 0))
    full = lambda a: pl.BlockSpec(a.shape, lambda i: (0,) * a.ndim)
    return pl.pallas_call(
        _merge_kernel,
        grid=(n // tm,),
        in_specs=[row(BRANCH_W), row(BRANCH_W), row(COLS_GATE), row(D_MODEL),
                  pl.BlockSpec((1, N_MOD, D_MODEL), mod_idx),
                  full(wa), full(wb), full(wo), full(norm2_g)],
        out_specs=[row(D_MODEL), row(D_MODEL)],
        out_shape=[jax.ShapeDtypeStruct((n, D_MODEL), F32), jax.ShapeDtypeStruct((n, D_MODEL), BF16)],
        compiler_params=_cparams("parallel"),
        name="merge_out",
    )(ya2, yb2, gate2, x2, mod, wa, wb, wo, norm2_g)


def _bitonic_pairs(n):
    pairs = []
    k = 2
    while k <= n:
        j = k // 2
        while j >= 1:
            for i in range(n):
                l = i ^ j
                if l > i:
                    pairs.append((i, l) if (i & k) == 0 else (l, i))
            j //= 2
        k *= 2
    return pairs


def _top_values_of_keys(s, k):
    r, t = s.shape
    nl = r // 8
    lv = [s[8 * j:8 * j + 8, :] for j in range(nl)]
    for a, b in _bitonic_pairs(nl):
        lv[a], lv[b] = jnp.maximum(lv[a], lv[b]), jnp.minimum(lv[a], lv[b])
    sid = lax.broadcasted_iota(jnp.int32, (8, t), 0)
    out = []
    for it in range(k):
        m = jnp.max(lv[0], axis=0, keepdims=True)
        out.append(m)
        first = jnp.min(jnp.where(lv[0] == m, sid, 8), axis=0, keepdims=True)
        pop = sid == first
        for j in range(min(nl, k - 1 - it)):
            lv[j] = jnp.where(pop, lv[j + 1] if j + 1 < nl else NEG_INF, lv[j])
    return out


def _peer_score_kernel(h_ref, wq_ref, keys_ref, thr_ref, g1_ref, s2_ref, e2_ref, cand_ref):
    qt = _mm_nt(wq_ref[...], h_ref[...])
    half = PEER_DQ // 2
    k1 = PEER_TOPK + 1
    pairs = [(i, j) for i in range(k1) for j in range(k1) if (i + 1) * (j + 1) <= k1]
    assert len(pairs) <= cand_ref.shape[0]
    for h in range(PEER_HEADS):
        s = [_mm(keys_ref[2 * h + c], qt[(2 * h + c) * half:(2 * h + c + 1) * half, :]) for c in range(2)]
        tops = [_top_values_of_keys(s[c], k1) for c in range(2)]
        cand_ref[...] = jnp.full(cand_ref.shape, NEG_INF, F32)
        for r, (i, j) in enumerate(pairs):
            cand_ref[r:r + 1, :] = tops[0][i] + tops[1][j]
        best = _top_values_of_keys(cand_ref[...], k1)
        mx = best[0]
        z = jnp.exp(best[0] - mx)
        for r in range(1, PEER_TOPK):
            z = z + jnp.exp(best[r] - mx)
        theta = 0.5 * (best[PEER_TOPK - 1] + best[PEER_TOPK])
        thr_ref[h] = theta - s[0]
        g1_ref[h] = jnp.exp(s[0] - tops[0][0]) * (0.5 / z)
        s2_ref[h] = s[1]
        e2_ref[h] = jnp.exp(s[1] - tops[1][0])


def _peer_scores(h2, wq_t, keys):
    n = h2.shape[0]
    tb = 256
    full = lambda a: pl.BlockSpec(a.shape, lambda i: (0,) * a.ndim)
    return pl.pallas_call(
        _peer_score_kernel,
        grid=(n // tb,),
        in_specs=[pl.BlockSpec((tb, D_MODEL), lambda i: (i, 0)), full(wq_t), full(keys)],
        out_specs=[pl.BlockSpec((PEER_HEADS, N_KEYS, tb), lambda i: (0, 0, i))] * 4,
        out_shape=[jax.ShapeDtypeStruct((PEER_HEADS, N_KEYS, n), F32)] * 4,
        scratch_shapes=[pltpu.VMEM((64, tb), F32)],
        compiler_params=_cparams("parallel"),
        name="peer_scores",
    )(h2, wq_t, keys)


def _gelu_tanh_x2(x):
    return x + x * jnp.tanh(x * (0.7978845608028654 + 0.035677408136300125 * (x * x)))


def _peer_expert_kernel(h_ref, u_ref, vt_ref, thr_ref, g1_ref, s2_ref, e2_ref, x1_ref, mod_ref, fg_ref, y_ref,
                        acc_ref, w_ref, *, rows_per_step):
    e = pl.program_id(1)
    tb = h_ref.shape[0]

    @pl.when(e == 0)
    def _():
        acc_ref[...] = jnp.zeros_like(acc_ref)

    st = _mm_nt(u_ref[...], h_ref[...])
    rows_per_chunk = 2
    partial = None
    for al in range(rows_per_step):
        a = e * rows_per_step + al
        r0 = al * N_KEYS
        thr = [thr_ref[h, pl.ds(a, 1), :] for h in range(PEER_HEADS)]
        g1 = [g1_ref[h, pl.ds(a, 1), :] for h in range(PEER_HEADS)]
        for lt in range(tb // LANES):
            ls = slice(lt * LANES, (lt + 1) * LANES)
            wsum = None
            for h in range(PEER_HEADS):
                w = jnp.where(s2_ref[h, :, ls] >= thr[h][:, ls], e2_ref[h, :, ls] * g1[h][:, ls], 0.0)
                wsum = w if wsum is None else wsum + w
            act = _gelu_tanh_x2(st[r0:r0 + N_KEYS, ls])
            w_ref[r0:r0 + N_KEYS, ls] = (wsum * act).astype(BF16)
        if (al + 1) % rows_per_chunk == 0:
            rs = slice((al + 1 - rows_per_chunk) * N_KEYS, (al + 1) * N_KEYS)
            d = jnp.dot(vt_ref[:, rs], w_ref[rs, :], preferred_element_type=F32)
            partial = d if partial is None else partial + d
    acc_ref[...] += partial

    @pl.when(e == pl.num_programs(1) - 1)
    def _():
        m = mod_ref[0]
        x2 = x1_ref[...] + m[5:6] * acc_ref[...].T
        y_ref[...] = _rms(x2, fg_ref[...])


def _peer_experts(h2, u_bf, vt_bf, stats, x1, mod, mod_base, mod_step, seq, final_g):
    n = h2.shape[0]
    tb = 512 if (mod_step == 0 or seq % 512 == 0) else seq
    rows_per_step = 8
    ec = rows_per_step * N_KEYS
    mod_idx = lambda i, e: (mod_base + mod_step * ((i * tb) // seq), 0, 0)
    stat_spec = pl.BlockSpec((PEER_HEADS, N_KEYS, tb), lambda i, e: (0, 0, i))
    return pl.pallas_call(
        functools.partial(_peer_expert_kernel, rows_per_step=rows_per_step),
        grid=(n // tb, N_EXPERTS // ec),
        in_specs=[pl.BlockSpec((tb, D_MODEL), lambda i, e: (i, 0)),
                  pl.BlockSpec((ec, D_MODEL), lambda i, e: (e, 0)),
                  pl.BlockSpec((D_MODEL, ec), lambda i, e: (0, e)),
                  stat_spec, stat_spec, stat_spec, stat_spec,
                  pl.BlockSpec((tb, D_MODEL), lambda i, e: (i, 0)),
                  pl.BlockSpec((1, N_MOD, D_MODEL), mod_idx),
                  pl.BlockSpec((1, D_MODEL), lambda i, e: (0, 0))],
        out_specs=pl.BlockSpec((tb, D_MODEL), lambda i, e: (i, 0)),
        out_shape=jax.ShapeDtypeStruct((n, D_MODEL), F32),
        scratch_shapes=[pltpu.VMEM((D_MODEL, tb), F32), pltpu.VMEM((ec, tb), BF16)],
        compiler_params=_cparams("parallel", "arbitrary"),
        name="peer_experts",
    )(h2, u_bf, vt_bf, *stats, x1, mod, final_g)


def _rope_tables(seq):
    pos = jnp.arange(seq, dtype=jnp.int32)
    lane = jnp.arange(LANES, dtype=jnp.int32) % HEAD_DIM
    use_col = (lane // 32) == 1
    p = jnp.where(use_col[None, :], (pos % GRID_W)[:, None], (pos // GRID_W)[:, None]).astype(F32)
    inv = ROPE_BASE ** (-(lane % 16).astype(F32) / 16.0)
    ang = p * inv[None, :]
    first = (lane % 32) < 16
    return jnp.cos(ang), jnp.where(first[None, :], -jnp.sin(ang), jnp.sin(ang))


def _trunk_path(x, mod, mod_base, mod_step, rope_tabs, s_ret0, s_rwkv0, want_fin, lw, final_g):
    b, seq, _ = x.shape
    x2 = x.reshape(b * seq, D_MODEL)
    mix, lora, gate = _in_proj(x2, mod, mod_base, mod_step, seq, lw['norm1_g'], lw['w_in'])
    mix3 = mix.reshape(b, seq, COLS_MIX)
    lora3 = lora.reshape(b, seq, COLS_LORA)
    ya, ret_fin = _retention(mix3, lw['ret_decay'], lw['ret_gn_w'], lw['ret_gn_b'], rope_tabs, s_ret0, want_fin)
    yb, rwkv_fin = _rwkv(mix3, lora3, lw['rwkv'], s_rwkv0, want_fin)
    x1, h2 = _merge(ya.reshape(b * seq, BRANCH_W), yb.reshape(b * seq, BRANCH_W), gate, x2, mod, mod_base,
                    mod_step, seq, lw['w_br_a'], lw['w_br_b'], lw['w_out'], lw['norm2_g'])
    stats = _peer_scores(h2, lw['peer_wq_t'], lw['peer_keys'])
    y = _peer_experts(h2, lw['peer_u'], lw['peer_vt'], stats, x1, mod, mod_base, mod_step, seq, final_g)
    return y.reshape(b, seq, D_MODEL), x1.reshape(b, seq, D_MODEL), ret_fin, rwkv_fin


def kernel(x_prompt, x_sample, state_ret, state_rwkv, c, c_ctx, ada_w, ada_b, norm1_g, w_in, ret_decay, ret_gn_w, ret_gn_b, rwkv_conv, rwkv_w0, rwkv_w2, rwkv_a0, rwkv_a2, rwkv_g2, rwkv_k_k, rwkv_k_a, rwkv_r_k, rwkv_gn_w, rwkv_gn_b, w_br_a, w_br_b, w_out, norm2_g, peer_wq, peer_keys, peer_u, peer_v, final_norm_g):
    assert w_in.shape[0] == 1, "the final norm is fused into the layer's last kernel: single trunk layer only"
    row = lambda a: a.reshape(1, -1)
    cc = jnp.concatenate([c_ctx[None, :], c], axis=0)
    cc = jnp.pad(cc, ((0, (-cc.shape[0]) % 8), (0, 0)))
    rope_tabs = _rope_tables(x_sample.shape[1])
    final_g = row(final_norm_g)

    if True:
        l = 0
        lw = {
            'norm1_g': row(norm1_g[l]), 'w_in': w_in[l].astype(BF16),
            'ret_decay': jnp.repeat(ret_decay[l], HEAD_DIM, axis=1),
            'ret_gn_w': row(ret_gn_w[l]), 'ret_gn_b': row(ret_gn_b[l]),
            'rwkv': {'conv': rwkv_conv[l], 'w0': rwkv_w0[l], 'w2': rwkv_w2[l].astype(BF16), 'a0': rwkv_a0[l],
                     'a2': rwkv_a2[l].astype(BF16), 'g2': rwkv_g2[l].astype(BF16), 'k_k': row(rwkv_k_k[l]),
                     'k_a': row(rwkv_k_a[l]), 'r_k': row(rwkv_r_k[l]), 'gn_w': row(rwkv_gn_w[l]),
                     'gn_b': row(rwkv_gn_b[l])},
            'w_br_a': w_br_a[l].astype(BF16), 'w_br_b': w_br_b[l].astype(BF16), 'w_out': w_out[l].astype(BF16),
            'norm2_g': row(norm2_g[l]),
            'peer_wq_t': peer_wq[l].astype(BF16).T,
            'peer_keys': peer_keys[l].reshape(2 * PEER_HEADS, N_KEYS, PEER_DQ // 2).astype(BF16),
            'peer_u': peer_u[l].astype(BF16), 'peer_vt': peer_v[l].astype(BF16).T,
        }
        mod = _adaln(cc, ada_w[l], row(ada_b[l])).reshape(cc.shape[0], N_MOD, D_MODEL)
        yp, _, ret_fin, rwkv_fin = _trunk_path(x_prompt, mod, 0, 0, None, None, None, True, lw, final_g)
        ys, _, _, _ = _trunk_path(x_sample, mod, 1, 1, rope_tabs, state_ret[:, l], state_rwkv[:, l], False, lw,
                                  final_g)
    return (yp, ys, ret_fin[:, None], rwkv_fin[:, None])
```

```python
import functools

import jax
import jax.numpy as jnp
from jax import lax
from jax.experimental import pallas as pl
from jax.experimental.pallas import tpu as pltpu

F32 = jnp.float32
BF16 = jnp.bfloat16

D_MODEL = 1024
GRID_W = 64
N_MOD = 6
HEADS = 8
HEAD_DIM = 64
BRANCH_W = HEADS * HEAD_DIM
RET_CHUNK = 128
RWKV_CHUNK = 64
LORA_W = 64
LORA_A = 64
LORA_G = 128
PEER_HEADS = 8
N_KEYS = 128
N_EXPERTS = N_KEYS * N_KEYS
PEER_DQ = 256
PEER_TOPK = 16
ROPE_BASE = 10000.0
NORM_EPS = 1e-6
GN_EPS = 64e-5
COLS_MIX = 7 * BRANCH_W
COLS_LORA = LORA_W + LORA_A + LORA_G
COLS_GATE = 2 * D_MODEL
IN_COLS = COLS_MIX + COLS_LORA + COLS_GATE

V7X_VMEM_LIMIT_BYTES = 56 * 1024 * 1024
LANES = 128
NEG_INF = float("-inf")


def _cparams(*sem):
    return pltpu.CompilerParams(dimension_semantics=sem, vmem_limit_bytes=V7X_VMEM_LIMIT_BYTES)


def _mm(a, b):
    return jnp.dot(a.astype(BF16), b.astype(BF16), preferred_element_type=F32)


def _mm_nt(a, b):
    return lax.dot_general(a.astype(BF16), b.astype(BF16), (((1,), (1,)), ((), ())),
                           preferred_element_type=F32)


def _mm_tn(a, b):
    return lax.dot_general(a.astype(BF16), b.astype(BF16), (((0,), (0,)), ((), ())),
                           preferred_element_type=F32)


def _mm_tn_3pass(a, b):
    a_hi, b_hi = a.astype(BF16), b.astype(BF16)
    a_lo = (a - a_hi.astype(F32)).astype(BF16)
    b_lo = (b - b_hi.astype(F32)).astype(BF16)
    dims = (((0,), (0,)), ((), ()))
    dot = lambda x, y: lax.dot_general(x, y, dims, preferred_element_type=F32)
    return dot(a_hi, b_hi) + (dot(a_hi, b_lo) + dot(a_lo, b_hi))


def _sigmoid(x):
    return 1.0 / (1.0 + jnp.exp(-x))


def _rms(x, g):
    return x * lax.rsqrt(jnp.mean(x * x, axis=-1, keepdims=True) + NORM_EPS) * g


def _head_sum(x):
    t, w = x.shape
    lo = lax.broadcasted_iota(jnp.int32, (t, LANES), 1) < HEAD_DIM
    outs = []
    for j in range(w // LANES):
        xt = x[:, j * LANES:(j + 1) * LANES]
        s_lo = jnp.sum(jnp.where(lo, xt, 0.0), axis=-1, keepdims=True)
        s_hi = jnp.sum(jnp.where(lo, 0.0, xt), axis=-1, keepdims=True)
        outs.append(jnp.where(lo, s_lo, s_hi))
    return outs[0] if len(outs) == 1 else jnp.concatenate(outs, axis=-1)


def _head_norm(y, w, b):
    mu = _head_sum(y) * (1.0 / HEAD_DIM)
    d = y - mu
    var = _head_sum(d * d) * (1.0 / HEAD_DIM)
    return d * lax.rsqrt(var + GN_EPS) * w + b


def _mod_kernel(c_ref, w_ref, b_ref, o_ref):
    c = c_ref[...]
    o_ref[...] = _mm(c * _sigmoid(c), w_ref[...]) + b_ref[...]


def _adaln(cc, ada_w, ada_b):
    rows = cc.shape[0]
    n = ada_w.shape[1]
    tn = n // 4
    return pl.pallas_call(
        _mod_kernel,
        grid=(n // tn,),
        in_specs=[pl.BlockSpec((rows, D_MODEL), lambda j: (0, 0)),
                  pl.BlockSpec((D_MODEL, tn), lambda j: (0, j)),
                  pl.BlockSpec((1, tn), lambda j: (0, j))],
        out_specs=pl.BlockSpec((rows, tn), lambda j: (0, j)),
        out_shape=jax.ShapeDtypeStruct((rows, n), F32),
        compiler_params=_cparams("parallel"),
        name="adaln_mod",
    )(cc, ada_w, ada_b)


def _in_kernel(x_ref, mod_ref, g_ref, w_ref, mix_ref, lora_ref, gate_ref):
    m = mod_ref[0]
    h = (_rms(x_ref[...], g_ref[...]) * (1.0 + m[1:2]) + m[0:1]).astype(BF16)
    mix_ref[...] = jnp.dot(h, w_ref[:, 0:COLS_MIX], preferred_element_type=F32)
    lora_ref[...] = jnp.dot(h, w_ref[:, COLS_MIX:COLS_MIX + COLS_LORA], preferred_element_type=F32)
    gate_ref[...] = jnp.dot(h, w_ref[:, COLS_MIX + COLS_LORA:IN_COLS], preferred_element_type=F32)


def _in_proj(x2, mod, mod_base, mod_step, seq, norm_g, w_in_bf):
    n = x2.shape[0]
    tm = 256
    mod_idx = lambda i: (mod_base + mod_step * ((i * tm) // seq), 0, 0)
    return pl.pallas_call(
        _in_kernel,
        grid=(n // tm,),
        in_specs=[pl.BlockSpec((tm, D_MODEL), lambda i: (i, 0)),
                  pl.BlockSpec((1, N_MOD, D_MODEL), mod_idx),
                  pl.BlockSpec((1, D_MODEL), lambda i: (0, 0)),
                  pl.BlockSpec((D_MODEL, IN_COLS), lambda i: (0, 0))],
        out_specs=[pl.BlockSpec((tm, COLS_MIX), lambda i: (i, 0)),
                   pl.BlockSpec((tm, COLS_LORA), lambda i: (i, 0)),
                   pl.BlockSpec((tm, COLS_GATE), lambda i: (i, 0))],
        out_shape=[jax.ShapeDtypeStruct((n, COLS_MIX), F32),
                   jax.ShapeDtypeStruct((n, COLS_LORA), F32),
                   jax.ShapeDtypeStruct((n, COLS_GATE), F32)],
        compiler_params=_cparams("parallel"),
        name="in_proj",
    )(x2, mod, norm_g, w_in_bf)


def _rope(x, cos, sin):
    lane = lax.broadcasted_iota(jnp.int32, x.shape, 1)
    first = (lane % 32) < 16
    partner = jnp.where(first, pltpu.roll(x, LANES - 16, 1), pltpu.roll(x, 16, 1))
    return x * cos + partner * sin


def _ret_kernel(*refs, seq, rope, has_init, want_fin):
    it = iter(refs)
    q_ref, k_ref, v_ref, g_ref, rd_ref, gw_ref, gb_ref = (next(it) for _ in range(7))
    cos_ref = next(it) if rope else None
    sin_ref = next(it) if rope else None
    s0_ref = next(it) if has_init else None
    y_ref = next(it)
    fin_ref = next(it) if want_fin else None
    qs_ref, ks_ref, vec_ref, dec_ref, r_ref = (next(it) for _ in range(5))

    C = RET_CHUNK
    W = BRANCH_W
    n = seq // C
    rd = rd_ref[...]
    lg = jnp.minimum(rd, 0.0) - jnp.log(1.0 + jnp.exp(-jnp.abs(rd)))
    lgf, lgb = lg[0:1, :], lg[1:2, :]
    ii = lax.broadcasted_iota(jnp.int32, (C, C), 0)
    jj = lax.broadcasted_iota(jnp.int32, (C, C), 1)
    diff = (ii - jj).astype(F32)
    col = lax.broadcasted_iota(jnp.int32, (C, W), 0).astype(F32)
    ln = lambda h: slice(h * HEAD_DIM, (h + 1) * HEAD_DIM)

    vec_ref[0] = jnp.exp((col + 1.0) * lgf)
    vec_ref[1] = jnp.exp((C - 1.0 - col) * lgf)
    vec_ref[2] = jnp.exp((C - col) * lgb)
    vec_ref[3] = jnp.exp(col * lgb)
    chunk_f = jnp.exp(C * lgf)
    chunk_b = jnp.exp(C * lgb)
    for h in range(HEADS):
        gf = lgf[:, h * HEAD_DIM:h * HEAD_DIM + 1]
        gb = lgb[:, h * HEAD_DIM:h * HEAD_DIM + 1]
        dec_ref[h] = (jnp.where(diff >= 0, jnp.exp(jnp.maximum(diff, 0.0) * gf), 0.0)
                      + jnp.where(diff <= 0, jnp.exp(jnp.maximum(-diff, 0.0) * gb), 0.0))
        for d in range(2):
            r_ref[d, h] = s0_ref[0, d, h] if has_init else jnp.zeros((HEAD_DIM, HEAD_DIM), F32)

    def rows(c):
        return pl.ds(pl.multiple_of(c * C, C), C)

    def prep(c, carry):
        r = rows(c)
        q = q_ref[0, r, :]
        k = k_ref[0, r, :] * (HEAD_DIM ** -0.5)
        if rope:
            cos, sin = cos_ref[r, :], sin_ref[r, :]
            tiles = lambda x: [x[:, j * LANES:(j + 1) * LANES] for j in range(W // LANES)]
            q = jnp.concatenate([_rope(t, cos, sin) for t in tiles(q)], axis=1)
            k = jnp.concatenate([_rope(t, cos, sin) for t in tiles(k)], axis=1)
        qs_ref[r, :] = q
        ks_ref[r, :] = k
        y_ref[0, r, :] = jnp.zeros((C, W), F32)
        return carry

    lax.fori_loop(0, n, prep, 0)

    cpi = 2 if n % 4 == 0 else 1

    def step(i, carry):
        rf = [rows(i * cpi + u) for u in range(cpi)]
        rb = [rows(n - 1 - (i * cpi + u)) for u in range(cpi)]
        intra, qfx, inc_f, qbx, inc_b = ([[None] * HEADS for _ in range(cpi)] for _ in range(5))
        for u in range(cpi):
            qf, kf, vf = qs_ref[rf[u], :], ks_ref[rf[u], :], v_ref[0, rf[u], :]
            qb, kb, vb = qs_ref[rb[u], :], ks_ref[rb[u], :], v_ref[0, rb[u], :]
            qfs, kfs = qf * vec_ref[0], kf * vec_ref[1]
            qbs, kbs = qb * vec_ref[2], kb * vec_ref[3]
            for h in range(HEADS):
                sc = _mm_nt(qf[:, ln(h)], kf[:, ln(h)]) * dec_ref[h]
                intra[u][h] = _mm(sc, vf[:, ln(h)])
                qfx[u][h], qbx[u][h] = qfs[:, ln(h)], qbs[:, ln(h)]
                inc_f[u][h] = _mm_tn(kfs[:, ln(h)], vf[:, ln(h)])
                inc_b[u][h] = _mm_tn(kbs[:, ln(h)], vb[:, ln(h)])
        sf = [r_ref[0, h] for h in range(HEADS)]
        sb = [r_ref[1, h] for h in range(HEADS)]
        of, ob = ([[None] * HEADS for _ in range(cpi)] for _ in range(2))
        for u in range(cpi):
            for h in range(HEADS):
                of[u][h] = intra[u][h] + _mm(qfx[u][h], sf[h])
                sf[h] = sf[h] * chunk_f[:, ln(h)] + inc_f[u][h]
                ob[u][h] = _mm(qbx[u][h], sb[h])
                sb[h] = sb[h] * chunk_b[:, ln(h)] + inc_b[u][h]
        yf = [y_ref[0, rf[u], :] for u in range(cpi)]
        yb = [y_ref[0, rb[u], :] for u in range(cpi)]
        for u in range(cpi):
            y_ref[0, rf[u], :] = yf[u] + jnp.concatenate(of[u], axis=1)
            y_ref[0, rb[u], :] = yb[u] + jnp.concatenate(ob[u], axis=1)
        for h in range(HEADS):
            r_ref[0, h] = sf[h]
            r_ref[1, h] = sb[h]
        return carry

    lax.fori_loop(0, n // cpi, step, 0)
    if want_fin:
        for d in range(2):
            for h in range(HEADS):
                fin_ref[0, d, h] = r_ref[d, h]

    def post(c, carry):
        r = rows(c)
        g = g_ref[0, r, :]
        y_ref[0, r, :] = _head_norm(y_ref[0, r, :], gw_ref[...], gb_ref[...]) * (g * _sigmoid(g))
        return carry

    lax.fori_loop(0, n, post, 0)


def _retention(mix3, rd_l, gn_w, gn_b, rope_tabs, s0, want_fin):
    b, seq, _ = mix3.shape
    W = BRANCH_W
    rope = rope_tabs is not None
    has_init = s0 is not None
    col_spec = lambda j: pl.BlockSpec((1, seq, W), lambda i: (i, 0, j))
    full = lambda a: pl.BlockSpec(a.shape, lambda i: (0,) * a.ndim)
    in_specs = [col_spec(0), col_spec(1), col_spec(2), col_spec(3), full(rd_l), full(gn_w), full(gn_b)]
    args = [mix3, mix3, mix3, mix3, rd_l, gn_w, gn_b]
    if rope:
        in_specs += [full(rope_tabs[0]), full(rope_tabs[1])]
        args += list(rope_tabs)
    state_spec = pl.BlockSpec((1, 2, HEADS, HEAD_DIM, HEAD_DIM), lambda i: (i, 0, 0, 0, 0))
    if has_init:
        in_specs.append(state_spec)
        args.append(s0)
    out_specs = [pl.BlockSpec((1, seq, W), lambda i: (i, 0, 0))]
    out_shape = [jax.ShapeDtypeStruct((b, seq, W), F32)]
    if want_fin:
        out_specs.append(state_spec)
        out_shape.append(jax.ShapeDtypeStruct((b, 2, HEADS, HEAD_DIM, HEAD_DIM), F32))
    res = pl.pallas_call(
        functools.partial(_ret_kernel, seq=seq, rope=rope, has_init=has_init, want_fin=want_fin),
        grid=(b,),
        in_specs=in_specs,
        out_specs=out_specs,
        out_shape=out_shape,
        scratch_shapes=[pltpu.VMEM((seq, W), F32), pltpu.VMEM((seq, W), F32),
                        pltpu.VMEM((4, RET_CHUNK, W), F32), pltpu.VMEM((HEADS, RET_CHUNK, RET_CHUNK), F32),
                        pltpu.VMEM((2, HEADS, HEAD_DIM, HEAD_DIM), F32)],
        compiler_params=_cparams("parallel"),
        name="retention",
    )(*args)
    return (res[0], res[1]) if want_fin else (res[0], None)


def _softplus(z):
    return jnp.maximum(z, 0.0) + jnp.log(1.0 + jnp.exp(-jnp.abs(z)))


def _rwkv_kernel(*refs, seq, nb, has_init, want_fin):
    it = iter(refs)
    (r_ref, k_ref, v_ref, lora_ref, cw_ref, w0_ref, w2_ref, a0_ref, a2_ref, g2_ref,
     kk_w_ref, ka_ref, rk_ref, gw_ref, gb_ref) = (next(it) for _ in range(15))
    s0_ref = next(it) if has_init else None
    y_ref = next(it)
    fin_ref = next(it) if want_fin else None
    r_s, v_s, kk_s, g_s, bv_s, lw_s, b_s, kd_s, st_s = (next(it) for _ in range(9))

    C = RWKV_CHUNK
    n = seq // C
    W = BRANCH_W
    row_id = lax.broadcasted_iota(jnp.int32, (C, W), 0)

    def rows(c):
        return pl.ds(pl.multiple_of(c * C, C), C)

    def conv(ref, s, c, w):
        x = ref[s, rows(c), :]
        prev8 = ref[s, pl.ds(pl.multiple_of(jnp.maximum(c * C - 8, 0), 8), 8), :]
        next8 = ref[s, pl.ds(pl.multiple_of(jnp.minimum(c * C + C, seq - 8), 8), 8), :]
        prev_row = jnp.where(c > 0, prev8[7:8, :], 0.0)
        next_row = jnp.where(c < n - 1, next8[0:1, :], 0.0)
        xm = jnp.where(row_id == 0, prev_row, pltpu.roll(x, 1, 0))
        xp = jnp.where(row_id == C - 1, next_row, pltpu.roll(x, C - 1, 0))
        return w[0:1] * xm + w[1:2] * x + w[2:3] * xp

    def prep_seq(s, c):
        rws = rows(c)
        cw = cw_ref[...]
        r = conv(r_ref, s, c, cw[:, 0:W])
        k = conv(k_ref, s, c, cw[:, W:2 * W])
        v = conv(v_ref, s, c, cw[:, 2 * W:3 * W])
        lo = lora_ref[s, rws, :]
        dw = lo[:, 0:LORA_W]
        da = lo[:, LORA_W:LORA_W + LORA_A]
        dg = lo[:, LORA_W + LORA_A:]
        kk = k * kk_w_ref[...]
        kk = kk * lax.rsqrt(_head_sum(kk * kk) + 1e-12)
        y_ref[s, rws, :] = jnp.zeros((C, W), F32)
        r_s[s, rws, :] = r
        v_s[s, rws, :] = v
        kk_s[s, rws, :] = kk
        g_s[s, rws, :] = _mm(_sigmoid(dg), g2_ref[...])
        bv_s[s, rws, :] = _head_sum(r * k * rk_ref[...]) * v
        tdw = jnp.tanh(dw)
        for d in range(2):
            w_log = -_softplus(-(w0_ref[d:d + 1, :] + _mm(tdw, w2_ref[d]))) - 0.5
            a = _sigmoid(a0_ref[d:d + 1, :] + _mm(da, a2_ref[d]))
            lw_s[s, d, rws, :] = -jnp.exp(w_log)
            b_s[s, d, rws, :] = kk * a
            kd_s[s, d, rws, :] = k * (1.0 + (a - 1.0) * ka_ref[...])

    def prep(c, carry):
        for s in range(nb):
            prep_seq(s, c)
        return carry

    lax.fori_loop(0, n, prep, 0)

    ii = lax.broadcasted_iota(jnp.int32, (C, C), 0)
    jj = lax.broadcasted_iota(jnp.int32, (C, C), 1)
    incl = [ii >= jj, ii <= jj]
    strict = [ii > jj, ii < jj]

    cpi = 2 if (nb == 1 and n % 4 == 0) else 1
    streams = [(s, d, u) for s in range(nb) for d in range(2) for u in range(cpi)]
    for s in range(nb):
        for d in range(2):
            for h in range(HEADS):
                st_s[s, d, h] = s0_ref[s, d, h] if has_init else jnp.zeros((HEAD_DIM, HEAD_DIM), F32)

    def chunk(i, carry):
        rws, at, bt, kt, rt, bh, kh, etot, vc = ([None] * len(streams) for _ in range(9))
        for q, (s, d, u) in enumerate(streams):
            rws[q] = rows(i * cpi + u if d == 0 else n - 1 - (i * cpi + u))
            rc, vc[q], kkc = r_s[s, rws[q], :], v_s[s, rws[q], :], kk_s[s, rws[q], :]
            lwc, bc, kc = lw_s[s, d, rws[q], :], b_s[s, d, rws[q], :], kd_s[s, d, rws[q], :]
            cum = jnp.dot(incl[d].astype(F32), lwc, precision=lax.Precision.HIGHEST,
                          preferred_element_type=F32)
            tot = cum[C - 1:C, :] if d == 0 else cum[0:1, :]
            pinv = jnp.exp(-cum)
            pend = jnp.exp(tot - cum)
            at[q] = -kkc * jnp.exp(cum - lwc)
            bt[q] = bc * pinv
            kt[q] = kc * pinv
            rt[q] = rc * jnp.exp(cum)
            bh[q] = bc * pend
            kh[q] = kc * pend
            etot[q] = jnp.exp(tot)
        chains = [(q, h) for q in range(len(streams)) for h in range(HEADS)]
        dirn = lambda q: streams[q][1]
        ln = lambda h: slice(h * HEAD_DIM, (h + 1) * HEAD_DIM)
        wk = [jnp.concatenate([bt[q][:, ln(h)], kt[q][:, ln(h)]], axis=0) for q, h in chains]
        ma = [_mm_nt(at[q][:, ln(h)], wk[j]) for j, (q, h) in enumerate(chains)]
        mr = [_mm_nt(rt[q][:, ln(h)], wk[j]) for j, (q, h) in enumerate(chains)]
        a_ak = [jnp.where(strict[dirn(q)], ma[j][:, C:2 * C], 0.0) for j, (q, h) in enumerate(chains)]
        p = [jnp.where(strict[dirn(q)], ma[j][:, 0:C], 0.0) for j, (q, h) in enumerate(chains)]
        a_rb = [jnp.where(incl[dirn(q)], mr[j][:, 0:C], 0.0) for j, (q, h) in enumerate(chains)]
        a_rk = [jnp.where(incl[dirn(q)], mr[j][:, C:2 * C], 0.0) for j, (q, h) in enumerate(chains)]
        vh = [vc[q][:, ln(h)] for q, h in chains]
        x = [jnp.concatenate([at[q][:, ln(h)], _mm(a_ak[j], vh[j])], axis=1)
             for j, (q, h) in enumerate(chains)]
        for step in range(6):
            pb = [p[j].astype(BF16) for j in range(len(chains))]
            xh = [x[j].astype(BF16) for j in range(len(chains))]
            if step < 3:
                xl = [(x[j] - xh[j].astype(F32)).astype(BF16) for j in range(len(chains))]
                px = [jnp.dot(pb[j], jnp.concatenate([xh[j], xl[j]], axis=1), preferred_element_type=F32)
                      for j in range(len(chains))]
                x = [x[j] + (px[j][:, 0:2 * HEAD_DIM] + px[j][:, 2 * HEAD_DIM:]) for j in range(len(chains))]
            else:
                x = [x[j] + jnp.dot(pb[j], xh[j], preferred_element_type=F32) for j in range(len(chains))]
            if step < 5:
                p = [jnp.dot(pb[j], pb[j], preferred_element_type=F32) for j in range(len(chains))]
        gh = [_mm_tn(x[j], bh[q][:, ln(h)]) for j, (q, h) in enumerate(chains)]
        vk = [_mm_tn_3pass(vh[j], kh[q][:, ln(h)]) for j, (q, h) in enumerate(chains)]
        ax = [_mm(a_rb[j], x[j]) for j in range(len(chains))]
        qt = [rt[q][:, ln(h)] + ax[j][:, 0:HEAD_DIM] for j, (q, h) in enumerate(chains)]
        y0 = [ax[j][:, HEAD_DIM:2 * HEAD_DIM] + _mm(a_rk[j], vh[j]) for j in range(len(chains))]
        ys = [None] * len(chains)
        scans = [(s, d, h) for s in range(nb) for d in range(2) for h in range(HEADS)]
        st = {k: st_s[k] for k in scans}
        for u in range(cpi):
            for s, d, h in scans:
                q = streams.index((s, d, u))
                j = q * HEADS + h
                cur = st[s, d, h]
                ys[j] = _mm_nt(qt[j], cur) + y0[j]
                st[s, d, h] = (cur * etot[q][:, ln(h)] + _mm(cur, gh[j][0:HEAD_DIM, :])
                               + gh[j][HEAD_DIM:2 * HEAD_DIM, :] + vk[j])
        for k in scans:
            st_s[k] = st[k]
        for q, (s, d, u) in enumerate(streams):
            y = jnp.concatenate(ys[q * HEADS:(q + 1) * HEADS], axis=1)
            y_ref[s, rws[q], :] = y_ref[s, rws[q], :] + y
        return carry

    lax.fori_loop(0, n // cpi, chunk, 0)
    if want_fin:
        for s in range(nb):
            for d in range(2):
                for h in range(HEADS):
                    fin_ref[s, d, h] = st_s[s, d, h]

    def post(c, carry):
        rws = rows(c)
        for s in range(nb):
            y = _head_norm(y_ref[s, rws, :], gw_ref[...], gb_ref[...])
            y_ref[s, rws, :] = (y + bv_s[s, rws, :]) * g_s[s, rws, :]
        return carry

    lax.fori_loop(0, n, post, 0)


def _rwkv(mix3, lora3, lw, s0, want_fin):
    b, seq, _ = mix3.shape
    has_init = s0 is not None
    W = BRANCH_W
    nb = 2 if (seq <= 256 and b % 2 == 0) else 1
    col_spec = lambda j: pl.BlockSpec((nb, seq, W), lambda i: (i, 0, j))
    full = lambda a: pl.BlockSpec(a.shape, lambda i: (0,) * a.ndim)
    weights = [lw['conv'], lw['w0'], lw['w2'], lw['a0'], lw['a2'], lw['g2'],
               lw['k_k'], lw['k_a'], lw['r_k'], lw['gn_w'], lw['gn_b']]
    in_specs = [col_spec(4), col_spec(5), col_spec(6),
                pl.BlockSpec((nb, seq, COLS_LORA), lambda i: (i, 0, 0))] + [full(a) for a in weights]
    args = [mix3, mix3, mix3, lora3] + weights
    state_spec = pl.BlockSpec((nb, 2, HEADS, HEAD_DIM, HEAD_DIM), lambda i: (i, 0, 0, 0, 0))
    if has_init:
        in_specs.append(state_spec)
        args.append(s0)
    out_specs = [pl.BlockSpec((nb, seq, W), lambda i: (i, 0, 0))]
    out_shape = [jax.ShapeDtypeStruct((b, seq, W), F32)]
    if want_fin:
        out_specs.append(state_spec)
        out_shape.append(jax.ShapeDtypeStruct((b, 2, HEADS, HEAD_DIM, HEAD_DIM), F32))
    sw = pltpu.VMEM((nb, seq, W), F32)
    sw2 = pltpu.VMEM((nb, 2, seq, W), F32)
    res = pl.pallas_call(
        functools.partial(_rwkv_kernel, seq=seq, nb=nb, has_init=has_init, want_fin=want_fin),
        grid=(b // nb,),
        in_specs=in_specs,
        out_specs=out_specs,
        out_shape=out_shape,
        scratch_shapes=[sw, sw, sw, sw, sw, sw2, sw2, sw2,
                        pltpu.VMEM((nb, 2, HEADS, HEAD_DIM, HEAD_DIM), F32)],
        compiler_params=_cparams("parallel"),
        name="rwkv7",
    )(*args)
    return (res[0], res[1]) if want_fin else (res[0], None)


def _merge_kernel(ya_ref, yb_ref, gate_ref, x_ref, mod_ref, wa_ref, wb_ref, wo_ref, n2_ref, x1_ref, h2_ref):
    m = mod_ref[0]
    br_a = jnp.dot(ya_ref[...].astype(BF16), wa_ref[...], preferred_element_type=F32)
    br_b = jnp.dot(yb_ref[...].astype(BF16), wb_ref[...], preferred_element_type=F32)
    gate = gate_ref[...]
    merged = _sigmoid(gate[:, 0:D_MODEL]) * br_a + _sigmoid(gate[:, D_MODEL:]) * br_b
    mix = jnp.dot(merged.astype(BF16), wo_ref[...], preferred_element_type=F32)
    x1 = x_ref[...] + m[2:3] * mix
    x1_ref[...] = x1
    h2_ref[...] = (_rms(x1, n2_ref[...]) * (1.0 + m[4:5]) + m[3:4]).astype(BF16)


def _merge(ya2, yb2, gate2, x2, mod, mod_base, mod_step, seq, wa, wb, wo, norm2_g):
    n = x2.shape[0]
    tm = 256
    mod_idx = lambda i: (mod_base + mod_step * ((i * tm) // seq), 0, 0)
    row = lambda w: pl.BlockSpec((tm, w), lambda i: (i, 0))
    full = lambda a: pl.BlockSpec(a.shape, lambda i: (0,) * a.ndim)
    return pl.pallas_call(
        _merge_kernel,
        grid=(n // tm,),
        in_specs=[row(BRANCH_W), row(BRANCH_W), row(COLS_GATE), row(D_MODEL),
                  pl.BlockSpec((1, N_MOD, D_MODEL), mod_idx),
                  full(wa), full(wb), full(wo), full(norm2_g)],
        out_specs=[row(D_MODEL), row(D_MODEL)],
        out_shape=[jax.ShapeDtypeStruct((n, D_MODEL), F32), jax.ShapeDtypeStruct((n, D_MODEL), BF16)],
        compiler_params=_cparams("parallel"),
        name="merge_out",
    )(ya2, yb2, gate2, x2, mod, wa, wb, wo, norm2_g)


def _bitonic_pairs(n):
    pairs = []
    k = 2
    while k <= n:
        j = k // 2
        while j >= 1:
            for i in range(n):
                l = i ^ j
                if l > i:
                    pairs.append((i, l) if (i & k) == 0 else (l, i))
            j //= 2
        k *= 2
    return pairs


def _top_values_of_keys(s, k):
    r, t = s.shape
    nl = r // 8
    lv = [s[8 * j:8 * j + 8, :] for j in range(nl)]
    for a, b in _bitonic_pairs(nl):
        lv[a], lv[b] = jnp.maximum(lv[a], lv[b]), jnp.minimum(lv[a], lv[b])
    sid = lax.broadcasted_iota(jnp.int32, (8, t), 0)
    out = []
    for it in range(k):
        m = jnp.max(lv[0], axis=0, keepdims=True)
        out.append(m)
        first = jnp.min(jnp.where(lv[0] == m, sid, 8), axis=0, keepdims=True)
        pop = sid == first
        for j in range(min(nl, k - 1 - it)):
            lv[j] = jnp.where(pop, lv[j + 1] if j + 1 < nl else NEG_INF, lv[j])
    return out


def _peer_score_kernel(h_ref, wq_ref, keys_ref, thr_ref, g1_ref, s2_ref, e2_ref, cand_ref):
    qt = _mm_nt(wq_ref[...], h_ref[...])
    half = PEER_DQ // 2
    k1 = PEER_TOPK + 1
    pairs = [(i, j) for i in range(k1) for j in range(k1) if (i + 1) * (j + 1) <= k1]
    assert len(pairs) <= cand_ref.shape[0]
    for h in range(PEER_HEADS):
        s = [_mm(keys_ref[2 * h + c], qt[(2 * h + c) * half:(2 * h + c + 1) * half, :]) for c in range(2)]
        tops = [_top_values_of_keys(s[c], k1) for c in range(2)]
        cand_ref[...] = jnp.full(cand_ref.shape, NEG_INF, F32)
        for r, (i, j) in enumerate(pairs):
            cand_ref[r:r + 1, :] = tops[0][i] + tops[1][j]
        best = _top_values_of_keys(cand_ref[...], k1)
        mx = best[0]
        z = jnp.exp(best[0] - mx)
        for r in range(1, PEER_TOPK):
            z = z + jnp.exp(best[r] - mx)
        theta = 0.5 * (best[PEER_TOPK - 1] + best[PEER_TOPK])
        thr_ref[h] = theta - s[0]
        g1_ref[h] = jnp.exp(s[0] - tops[0][0]) * (0.5 / z)
        s2_ref[h] = s[1]
        e2_ref[h] = jnp.exp(s[1] - tops[1][0])


def _peer_scores(h2, wq_t, keys):
    n = h2.shape[0]
    tb = 256
    full = lambda a: pl.BlockSpec(a.shape, lambda i: (0,) * a.ndim)
    return pl.pallas_call(
        _peer_score_kernel,
        grid=(n // tb,),
        in_specs=[pl.BlockSpec((tb, D_MODEL), lambda i: (i, 0)), full(wq_t), full(keys)],
        out_specs=[pl.BlockSpec((PEER_HEADS, N_KEYS, tb), lambda i: (0, 0, i))] * 4,
        out_shape=[jax.ShapeDtypeStruct((PEER_HEADS, N_KEYS, n), F32)] * 4,
        scratch_shapes=[pltpu.VMEM((64, tb), F32)],
        compiler_params=_cparams("parallel"),
        name="peer_scores",
    )(h2, wq_t, keys)


def _gelu_tanh_x2(x):
    return x + x * jnp.tanh(x * (0.7978845608028654 + 0.035677408136300125 * (x * x)))


def _peer_expert_kernel(h_ref, u_ref, vt_ref, thr_ref, g1_ref, s2_ref, e2_ref, x1_ref, mod_ref, fg_ref, y_ref,
                        acc_ref, w_ref, *, rows_per_step):
    e = pl.program_id(1)
    tb = h_ref.shape[0]

    @pl.when(e == 0)
    def _():
        acc_ref[...] = jnp.zeros_like(acc_ref)

    st = _mm_nt(u_ref[...], h_ref[...])
    rows_per_chunk = 2
    partial = None
    for al in range(rows_per_step):
        a = e * rows_per_step + al
        r0 = al * N_KEYS
        thr = [thr_ref[h, pl.ds(a, 1), :] for h in range(PEER_HEADS)]
        g1 = [g1_ref[h, pl.ds(a, 1), :] for h in range(PEER_HEADS)]
        for lt in range(tb // LANES):
            ls = slice(lt * LANES, (lt + 1) * LANES)
            wsum = None
            for h in range(PEER_HEADS):
                w = jnp.where(s2_ref[h, :, ls] >= thr[h][:, ls], e2_ref[h, :, ls] * g1[h][:, ls], 0.0)
                wsum = w if wsum is None else wsum + w
            act = _gelu_tanh_x2(st[r0:r0 + N_KEYS, ls])
            w_ref[r0:r0 + N_KEYS, ls] = (wsum * act).astype(BF16)
        if (al + 1) % rows_per_chunk == 0:
            rs = slice((al + 1 - rows_per_chunk) * N_KEYS, (al + 1) * N_KEYS)
            d = jnp.dot(vt_ref[:, rs], w_ref[rs, :], preferred_element_type=F32)
            partial = d if partial is None else partial + d
    acc_ref[...] += partial

    @pl.when(e == pl.num_programs(1) - 1)
    def _():
        m = mod_ref[0]
        x2 = x1_ref[...] + m[5:6] * acc_ref[...].T
        y_ref[...] = _rms(x2, fg_ref[...])


def _peer_experts(h2, u_bf, vt_bf, stats, x1, mod, mod_base, mod_step, seq, final_g):
    n = h2.shape[0]
    tb = 512 if (mod_step == 0 or seq % 512 == 0) else seq
    rows_per_step = 8
    ec = rows_per_step * N_KEYS
    mod_idx = lambda i, e: (mod_base + mod_step * ((i * tb) // seq), 0, 0)
    stat_spec = pl.BlockSpec((PEER_HEADS, N_KEYS, tb), lambda i, e: (0, 0, i))
    return pl.pallas_call(
        functools.partial(_peer_expert_kernel, rows_per_step=rows_per_step),
        grid=(n // tb, N_EXPERTS // ec),
        in_specs=[pl.BlockSpec((tb, D_MODEL), lambda i, e: (i, 0)),
                  pl.BlockSpec((ec, D_MODEL), lambda i, e: (e, 0)),
                  pl.BlockSpec((D_MODEL, ec), lambda i, e: (0, e)),
                  stat_spec, stat_spec, stat_spec, stat_spec,
                  pl.BlockSpec((tb, D_MODEL), lambda i, e: (i, 0)),
                  pl.BlockSpec((1, N_MOD, D_MODEL), mod_idx),
                  pl.BlockSpec((1, D_MODEL), lambda i, e: (0, 0))],
        out_specs=pl.BlockSpec((tb, D_MODEL), lambda i, e: (i, 0)),
        out_shape=jax.ShapeDtypeStruct((n, D_MODEL), F32),
        scratch_shapes=[pltpu.VMEM((D_MODEL, tb), F32), pltpu.VMEM((ec, tb), BF16)],
        compiler_params=_cparams("parallel", "arbitrary"),
        name="peer_experts",
    )(h2, u_bf, vt_bf, *stats, x1, mod, final_g)


def _rope_tables(seq):
    pos = jnp.arange(seq, dtype=jnp.int32)
    lane = jnp.arange(LANES, dtype=jnp.int32) % HEAD_DIM
    use_col = (lane // 32) == 1
    p = jnp.where(use_col[None, :], (pos % GRID_W)[:, None], (pos // GRID_W)[:, None]).astype(F32)
    inv = ROPE_BASE ** (-(lane % 16).astype(F32) / 16.0)
    ang = p * inv[None, :]
    first = (lane % 32) < 16
    return jnp.cos(ang), jnp.where(first[None, :], -jnp.sin(ang), jnp.sin(ang))


def _trunk_path(x, mod, mod_base, mod_step, rope_tabs, s_ret0, s_rwkv0, want_fin, lw, final_g):
    b, seq, _ = x.shape
    x2 = x.reshape(b * seq, D_MODEL)
    mix, lora, gate = _in_proj(x2, mod, mod_base, mod_step, seq, lw['norm1_g'], lw['w_in'])
    mix3 = mix.reshape(b, seq, COLS_MIX)
    lora3 = lora.reshape(b, seq, COLS_LORA)
    ya, ret_fin = _retention(mix3, lw['ret_decay'], lw['ret_gn_w'], lw['ret_gn_b'], rope_tabs, s_ret0, want_fin)
    yb, rwkv_fin = _rwkv(mix3, lora3, lw['rwkv'], s_rwkv0, want_fin)
    x1, h2 = _merge(ya.reshape(b * seq, BRANCH_W), yb.reshape(b * seq, BRANCH_W), gate, x2, mod, mod_base,
                    mod_step, seq, lw['w_br_a'], lw['w_br_b'], lw['w_out'], lw['norm2_g'])
    stats = _peer_scores(h2, lw['peer_wq_t'], lw['peer_keys'])
    y = _peer_experts(h2, lw['peer_u'], lw['peer_vt'], stats, x1, mod, mod_base, mod_step, seq, final_g)
    return y.reshape(b, seq, D_MODEL), x1.reshape(b, seq, D_MODEL), ret_fin, rwkv_fin


def kernel(x_prompt, x_sample, state_ret, state_rwkv, c, c_ctx, ada_w, ada_b, norm1_g, w_in, ret_decay, ret_gn_w, ret_gn_b, rwkv_conv, rwkv_w0, rwkv_w2, rwkv_a0, rwkv_a2, rwkv_g2, rwkv_k_k, rwkv_k_a, rwkv_r_k, rwkv_gn_w, rwkv_gn_b, w_br_a, w_br_b, w_out, norm2_g, peer_wq, peer_keys, peer_u, peer_v, final_norm_g):
    assert w_in.shape[0] == 1, "the final norm is fused into the layer's last kernel: single trunk layer only"
    row = lambda a: a.reshape(1, -1)
    cc = jnp.concatenate([c_ctx[None, :], c], axis=0)
    cc = jnp.pad(cc, ((0, (-cc.shape[0]) % 8), (0, 0)))
    rope_tabs = _rope_tables(x_sample.shape[1])
    final_g = row(final_norm_g)

    if True:
        l = 0
        lw = {
            'norm1_g': row(norm1_g[l]), 'w_in': w_in[l].astype(BF16),
            'ret_decay': jnp.repeat(ret_decay[l], HEAD_DIM, axis=1),
            'ret_gn_w': row(ret_gn_w[l]), 'ret_gn_b': row(ret_gn_b[l]),
            'rwkv': {'conv': rwkv_conv[l], 'w0': rwkv_w0[l], 'w2': rwkv_w2[l].astype(BF16), 'a0': rwkv_a0[l],
                     'a2': rwkv_a2[l].astype(BF16), 'g2': rwkv_g2[l].astype(BF16), 'k_k': row(rwkv_k_k[l]),
                     'k_a': row(rwkv_k_a[l]), 'r_k': row(rwkv_r_k[l]), 'gn_w': row(rwkv_gn_w[l]),
                     'gn_b': row(rwkv_gn_b[l])},
            'w_br_a': w_br_a[l].astype(BF16), 'w_br_b': w_br_b[l].astype(BF16), 'w_out': w_out[l].astype(BF16),
            'norm2_g': row(norm2_g[l]),
            'peer_wq_t': peer_wq[l].astype(BF16).T,
            'peer_keys': peer_keys[l].reshape(2 * PEER_HEADS, N_KEYS, PEER_DQ // 2).astype(BF16),
            'peer_u': peer_u[l].astype(BF16), 'peer_vt': peer_v[l].astype(BF16).T,
        }
        mod = _adaln(cc, ada_w[l], row(ada_b[l])).reshape(cc.shape[0], N_MOD, D_MODEL)
        yp, _, ret_fin, rwkv_fin = _trunk_path(x_prompt, mod, 0, 0, None, None, None, True, lw, final_g)
        ys, _, _, _ = _trunk_path(x_sample, mod, 1, 1, rope_tabs, state_ret[:, l], state_rwkv[:, l], False, lw,
                                  final_g)
    return (yp, ys, ret_fin[:, None], rwkv_fin[:, None])
```

```python
import functools

import jax
import jax.numpy as jnp
from jax import lax
from jax.experimental import pallas as pl
from jax.experimental.pallas import tpu as pltpu

F32 = jnp.float32
BF16 = jnp.bfloat16

D_MODEL = 1024
GRID_W = 64
N_MOD = 6
HEADS = 8
HEAD_DIM = 64
BRANCH_W = HEADS * HEAD_DIM
RET_CHUNK = 128
RWKV_CHUNK = 64
LORA_W = 64
LORA_A = 64
LORA_G = 128
PEER_HEADS = 8
N_KEYS = 128
N_EXPERTS = N_KEYS * N_KEYS
PEER_DQ = 256
PEER_TOPK = 16
ROPE_BASE = 10000.0
NORM_EPS = 1e-6
GN_EPS = 64e-5
COLS_MIX = 7 * BRANCH_W
COLS_LORA = LORA_W + LORA_A + LORA_G
COLS_GATE = 2 * D_MODEL
IN_COLS = COLS_MIX + COLS_LORA + COLS_GATE

V7X_VMEM_LIMIT_BYTES = 56 * 1024 * 1024
LANES = 128
NEG_INF = float("-inf")


def _cparams(*sem):
    return pltpu.CompilerParams(dimension_semantics=sem, vmem_limit_bytes=V7X_VMEM_LIMIT_BYTES)


def _mm(a, b):
    return jnp.dot(a.astype(BF16), b.astype(BF16), preferred_element_type=F32)


def _mm_nt(a, b):
    return lax.dot_general(a.astype(BF16), b.astype(BF16), (((1,), (1,)), ((), ())),
                           preferred_element_type=F32)


def _mm_tn(a, b):
    return lax.dot_general(a.astype(BF16), b.astype(BF16), (((0,), (0,)), ((), ())),
                           preferred_element_type=F32)


def _mm_tn_3pass(a, b):
    a_hi, b_hi = a.astype(BF16), b.astype(BF16)
    a_lo = (a - a_hi.astype(F32)).astype(BF16)
    b_lo = (b - b_hi.astype(F32)).astype(BF16)
    dims = (((0,), (0,)), ((), ()))
    dot = lambda x, y: lax.dot_general(x, y, dims, preferred_element_type=F32)
    return dot(a_hi, b_hi) + (dot(a_hi, b_lo) + dot(a_lo, b_hi))


def _sigmoid(x):
    return 1.0 / (1.0 + jnp.exp(-x))


def _rms(x, g):
    return x * lax.rsqrt(jnp.mean(x * x, axis=-1, keepdims=True) + NORM_EPS) * g


def _head_sum(x):
    t, w = x.shape
    lo = lax.broadcasted_iota(jnp.int32, (t, LANES), 1) < HEAD_DIM
    outs = []
    for j in range(w // LANES):
        xt = x[:, j * LANES:(j + 1) * LANES]
        s_lo = jnp.sum(jnp.where(lo, xt, 0.0), axis=-1, keepdims=True)
        s_hi = jnp.sum(jnp.where(lo, 0.0, xt), axis=-1, keepdims=True)
        outs.append(jnp.where(lo, s_lo, s_hi))
    return outs[0] if len(outs) == 1 else jnp.concatenate(outs, axis=-1)


def _head_norm(y, w, b):
    mu = _head_sum(y) * (1.0 / HEAD_DIM)
    d = y - mu
    var = _head_sum(d * d) * (1.0 / HEAD_DIM)
    return d * lax.rsqrt(var + GN_EPS) * w + b


def _mod_kernel(c_ref, w_ref, b_ref, o_ref):
    c = c_ref[...]
    o_ref[...] = _mm(c * _sigmoid(c), w_ref[...]) + b_ref[...]


def _adaln(cc, ada_w, ada_b):
    rows = cc.shape[0]
    n = ada_w.shape[1]
    tn = n // 4
    return pl.pallas_call(
        _mod_kernel,
        grid=(n // tn,),
        in_specs=[pl.BlockSpec((rows, D_MODEL), lambda j: (0, 0)),
                  pl.BlockSpec((D_MODEL, tn), lambda j: (0, j)),
                  pl.BlockSpec((1, tn), lambda j: (0, j))],
        out_specs=pl.BlockSpec((rows, tn), lambda j: (0, j)),
        out_shape=jax.ShapeDtypeStruct((rows, n), F32),
        compiler_params=_cparams("parallel"),
        name="adaln_mod",
    )(cc, ada_w, ada_b)


def _in_kernel(x_ref, mod_ref, g_ref, w_ref, mix_ref, lora_ref, gate_ref):
    m = mod_ref[0]
    h = (_rms(x_ref[...], g_ref[...]) * (1.0 + m[1:2]) + m[0:1]).astype(BF16)
    mix_ref[...] = jnp.dot(h, w_ref[:, 0:COLS_MIX], preferred_element_type=F32)
    lora_ref[...] = jnp.dot(h, w_ref[:, COLS_MIX:COLS_MIX + COLS_LORA], preferred_element_type=F32)
    gate_ref[...] = jnp.dot(h, w_ref[:, COLS_MIX + COLS_LORA:IN_COLS], preferred_element_type=F32)


def _in_proj(x2, mod, mod_base, mod_step, seq, norm_g, w_in_bf):
    n = x2.shape[0]
    tm = 256
    mod_idx = lambda i: (mod_base + mod_step * ((i * tm) // seq), 0, 0)
    return pl.pallas_call(
        _in_kernel,
        grid=(n // tm,),
        in_specs=[pl.BlockSpec((tm, D_MODEL), lambda i: (i, 0)),
                  pl.BlockSpec((1, N_MOD, D_MODEL), mod_idx),
                  pl.BlockSpec((1, D_MODEL), lambda i: (0, 0)),
                  pl.BlockSpec((D_MODEL, IN_COLS), lambda i: (0, 0))],
        out_specs=[pl.BlockSpec((tm, COLS_MIX), lambda i: (i, 0)),
                   pl.BlockSpec((tm, COLS_LORA), lambda i: (i, 0)),
                   pl.BlockSpec((tm, COLS_GATE), lambda i: (i, 0))],
        out_shape=[jax.ShapeDtypeStruct((n, COLS_MIX), F32),
                   jax.ShapeDtypeStruct((n, COLS_LORA), F32),
                   jax.ShapeDtypeStruct((n, COLS_GATE), F32)],
        compiler_params=_cparams("parallel"),
        name="in_proj",
    )(x2, mod, norm_g, w_in_bf)


def _rope(x, cos, sin):
    lane = lax.broadcasted_iota(jnp.int32, x.shape, 1)
    first = (lane % 32) < 16
    partner = jnp.where(first, pltpu.roll(x, LANES - 16, 1), pltpu.roll(x, 16, 1))
    return x * cos + partner * sin


def _ret_kernel(*refs, seq, rope, has_init, want_fin):
    it = iter(refs)
    q_ref, k_ref, v_ref, g_ref, rd_ref, gw_ref, gb_ref = (next(it) for _ in range(7))
    cos_ref = next(it) if rope else None
    sin_ref = next(it) if rope else None
    s0_ref = next(it) if has_init else None
    y_ref = next(it)
    fin_ref = next(it) if want_fin else None
    qs_ref, ks_ref, vec_ref, dec_ref, r_ref = (next(it) for _ in range(5))

    C = RET_CHUNK
    W = BRANCH_W
    n = seq // C
    rd = rd_ref[...]
    lg = jnp.minimum(rd, 0.0) - jnp.log(1.0 + jnp.exp(-jnp.abs(rd)))
    lgf, lgb = lg[0:1, :], lg[1:2, :]
    ii = lax.broadcasted_iota(jnp.int32, (C, C), 0)
    jj = lax.broadcasted_iota(jnp.int32, (C, C), 1)
    diff = (ii - jj).astype(F32)
    col = lax.broadcasted_iota(jnp.int32, (C, W), 0).astype(F32)
    ln = lambda h: slice(h * HEAD_DIM, (h + 1) * HEAD_DIM)

    vec_ref[0] = jnp.exp((col + 1.0) * lgf)
    vec_ref[1] = jnp.exp((C - 1.0 - col) * lgf)
    vec_ref[2] = jnp.exp((C - col) * lgb)
    vec_ref[3] = jnp.exp(col * lgb)
    chunk_f = jnp.exp(C * lgf)
    chunk_b = jnp.exp(C * lgb)
    for h in range(HEADS):
        gf = lgf[:, h * HEAD_DIM:h * HEAD_DIM + 1]
        gb = lgb[:, h * HEAD_DIM:h * HEAD_DIM + 1]
        dec_ref[h] = (jnp.where(diff >= 0, jnp.exp(jnp.maximum(diff, 0.0) * gf), 0.0)
                      + jnp.where(diff <= 0, jnp.exp(jnp.maximum(-diff, 0.0) * gb), 0.0))
        for d in range(2):
            r_ref[d, h] = s0_ref[0, d, h] if has_init else jnp.zeros((HEAD_DIM, HEAD_DIM), F32)

    def rows(c):
        return pl.ds(pl.multiple_of(c * C, C), C)

    def prep(c, carry):
        r = rows(c)
        q = q_ref[0, r, :]
        k = k_ref[0, r, :] * (HEAD_DIM ** -0.5)
        if rope:
            cos, sin = cos_ref[r, :], sin_ref[r, :]
            tiles = lambda x: [x[:, j * LANES:(j + 1) * LANES] for j in range(W // LANES)]
            q = jnp.concatenate([_rope(t, cos, sin) for t in tiles(q)], axis=1)
            k = jnp.concatenate([_rope(t, cos, sin) for t in tiles(k)], axis=1)
        qs_ref[r, :] = q
        ks_ref[r, :] = k
        y_ref[0, r, :] = jnp.zeros((C, W), F32)
        return carry

    lax.fori_loop(0, n, prep, 0)

    cpi = 2 if n % 4 == 0 else 1

    def step(i, carry):
        rf = [rows(i * cpi + u) for u in range(cpi)]
        rb = [rows(n - 1 - (i * cpi + u)) for u in range(cpi)]
        intra, qfx, inc_f, qbx, inc_b = ([[None] * HEADS for _ in range(cpi)] for _ in range(5))
        for u in range(cpi):
            qf, kf, vf = qs_ref[rf[u], :], ks_ref[rf[u], :], v_ref[0, rf[u], :]
            qb, kb, vb = qs_ref[rb[u], :], ks_ref[rb[u], :], v_ref[0, rb[u], :]
            qfs, kfs = qf * vec_ref[0], kf * vec_ref[1]
            qbs, kbs = qb * vec_ref[2], kb * vec_ref[3]
            for h in range(HEADS):
                sc = _mm_nt(qf[:, ln(h)], kf[:, ln(h)]) * dec_ref[h]
                intra[u][h] = _mm(sc, vf[:, ln(h)])
                qfx[u][h], qbx[u][h] = qfs[:, ln(h)], qbs[:, ln(h)]
                inc_f[u][h] = _mm_tn(kfs[:, ln(h)], vf[:, ln(h)])
                inc_b[u][h] = _mm_tn(kbs[:, ln(h)], vb[:, ln(h)])
        sf = [r_ref[0, h] for h in range(HEADS)]
        sb = [r_ref[1, h] for h in range(HEADS)]
        of, ob = ([[None] * HEADS for _ in range(cpi)] for _ in range(2))
        for u in range(cpi):
            for h in range(HEADS):
                of[u][h] = intra[u][h] + _mm(qfx[u][h], sf[h])
                sf[h] = sf[h] * chunk_f[:, ln(h)] + inc_f[u][h]
                ob[u][h] = _mm(qbx[u][h], sb[h])
                sb[h] = sb[h] * chunk_b[:, ln(h)] + inc_b[u][h]
        yf = [y_ref[0, rf[u], :] for u in range(cpi)]
        yb = [y_ref[0, rb[u], :] for u in range(cpi)]
        for u in range(cpi):
            y_ref[0, rf[u], :] = yf[u] + jnp.concatenate(of[u], axis=1)
            y_ref[0, rb[u], :] = yb[u] + jnp.concatenate(ob[u], axis=1)
        for h in range(HEADS):
            r_ref[0, h] = sf[h]
            r_ref[1, h] = sb[h]
        return carry

    lax.fori_loop(0, n // cpi, step, 0)
    if want_fin:
        for d in range(2):
            for h in range(HEADS):
                fin_ref[0, d, h] = r_ref[d, h]

    def post(c, carry):
        r = rows(c)
        g = g_ref[0, r, :]
        y_ref[0, r, :] = _head_norm(y_ref[0, r, :], gw_ref[...], gb_ref[...]) * (g * _sigmoid(g))
        return carry

    lax.fori_loop(0, n, post, 0)


def _retention(mix3, rd_l, gn_w, gn_b, rope_tabs, s0, want_fin):
    b, seq, _ = mix3.shape
    W = BRANCH_W
    rope = rope_tabs is not None
    has_init = s0 is not None
    col_spec = lambda j: pl.BlockSpec((1, seq, W), lambda i: (i, 0, j))
    full = lambda a: pl.BlockSpec(a.shape, lambda i: (0,) * a.ndim)
    in_specs = [col_spec(0), col_spec(1), col_spec(2), col_spec(3), full(rd_l), full(gn_w), full(gn_b)]
    args = [mix3, mix3, mix3, mix3, rd_l, gn_w, gn_b]
    if rope:
        in_specs += [full(rope_tabs[0]), full(rope_tabs[1])]
        args += list(rope_tabs)
    state_spec = pl.BlockSpec((1, 2, HEADS, HEAD_DIM, HEAD_DIM), lambda i: (i, 0, 0, 0, 0))
    if has_init:
        in_specs.append(state_spec)
        args.append(s0)
    out_specs = [pl.BlockSpec((1, seq, W), lambda i: (i, 0, 0))]
    out_shape = [jax.ShapeDtypeStruct((b, seq, W), F32)]
    if want_fin:
        out_specs.append(state_spec)
        out_shape.append(jax.ShapeDtypeStruct((b, 2, HEADS, HEAD_DIM, HEAD_DIM), F32))
    res = pl.pallas_call(
        functools.partial(_ret_kernel, seq=seq, rope=rope, has_init=has_init, want_fin=want_fin),
        grid=(b,),
        in_specs=in_specs,
        out_specs=out_specs,
        out_shape=out_shape,
        scratch_shapes=[pltpu.VMEM((seq, W), F32), pltpu.VMEM((seq, W), F32),
                        pltpu.VMEM((4, RET_CHUNK, W), F32), pltpu.VMEM((HEADS, RET_CHUNK, RET_CHUNK), F32),
                        pltpu.VMEM((2, HEADS, HEAD_DIM, HEAD_DIM), F32)],
        compiler_params=_cparams("parallel"),
        name="retention",
    )(*args)
    return (res[0], res[1]) if want_fin else (res[0], None)


def _softplus(z):
    return jnp.maximum(z, 0.0) + jnp.log(1.0 + jnp.exp(-jnp.abs(z)))


def _rwkv_kernel(*refs, seq, nb, has_init, want_fin):
    it = iter(refs)
    (r_ref, k_ref, v_ref, lora_ref, cw_ref, w0_ref, w2_ref, a0_ref, a2_ref, g2_ref,
     kk_w_ref, ka_ref, rk_ref, gw_ref, gb_ref) = (next(it) for _ in range(15))
    s0_ref = next(it) if has_init else None
    y_ref = next(it)
    fin_ref = next(it) if want_fin else None
    r_s, v_s, kk_s, g_s, bv_s, lw_s, b_s, kd_s, st_s = (next(it) for _ in range(9))

    C = RWKV_CHUNK
    n = seq // C
    W = BRANCH_W
    row_id = lax.broadcasted_iota(jnp.int32, (C, W), 0)

    def rows(c):
        return pl.ds(pl.multiple_of(c * C, C), C)

    def conv(ref, s, c, w):
        x = ref[s, rows(c), :]
        prev8 = ref[s, pl.ds(pl.multiple_of(jnp.maximum(c * C - 8, 0), 8), 8), :]
        next8 = ref[s, pl.ds(pl.multiple_of(jnp.minimum(c * C + C, seq - 8), 8), 8), :]
        prev_row = jnp.where(c > 0, prev8[7:8, :], 0.0)
        next_row = jnp.where(c < n - 1, next8[0:1, :], 0.0)
        xm = jnp.where(row_id == 0, prev_row, pltpu.roll(x, 1, 0))
        xp = jnp.where(row_id == C - 1, next_row, pltpu.roll(x, C - 1, 0))
        return w[0:1] * xm + w[1:2] * x + w[2:3] * xp

    def prep_seq(s, c):
        rws = rows(c)
        cw = cw_ref[...]
        r = conv(r_ref, s, c, cw[:, 0:W])
        k = conv(k_ref, s, c, cw[:, W:2 * W])
        v = conv(v_ref, s, c, cw[:, 2 * W:3 * W])
        lo = lora_ref[s, rws, :]
        dw = lo[:, 0:LORA_W]
        da = lo[:, LORA_W:LORA_W + LORA_A]
        dg = lo[:, LORA_W + LORA_A:]
        kk = k * kk_w_ref[...]
        kk = kk * lax.rsqrt(_head_sum(kk * kk) + 1e-12)
        y_ref[s, rws, :] = jnp.zeros((C, W), F32)
        r_s[s, rws, :] = r
        v_s[s, rws, :] = v
        kk_s[s, rws, :] = kk
        g_s[s, rws, :] = _mm(_sigmoid(dg), g2_ref[...])
        bv_s[s, rws, :] = _head_sum(r * k * rk_ref[...]) * v
        tdw = jnp.tanh(dw)
        for d in range(2):
            w_log = -_softplus(-(w0_ref[d:d + 1, :] + _mm(tdw, w2_ref[d]))) - 0.5
            a = _sigmoid(a0_ref[d:d + 1, :] + _mm(da, a2_ref[d]))
            lw_s[s, d, rws, :] = -jnp.exp(w_log)
            b_s[s, d, rws, :] = kk * a
            kd_s[s, d, rws, :] = k * (1.0 + (a - 1.0) * ka_ref[...])

    def prep(c, carry):
        for s in range(nb):
            prep_seq(s, c)
        return carry

    lax.fori_loop(0, n, prep, 0)

    ii = lax.broadcasted_iota(jnp.int32, (C, C), 0)
    jj = lax.broadcasted_iota(jnp.int32, (C, C), 1)
    incl = [ii >= jj, ii <= jj]
    strict = [ii > jj, ii < jj]

    cpi = 2 if (nb == 1 and n % 4 == 0) else 1
    streams = [(s, d, u) for s in range(nb) for d in range(2) for u in range(cpi)]
    for s in range(nb):
        for d in range(2):
            for h in range(HEADS):
                st_s[s, d, h] = s0_ref[s, d, h] if has_init else jnp.zeros((HEAD_DIM, HEAD_DIM), F32)

    def chunk(i, carry):
        rws, at, bt, kt, rt, bh, kh, etot, vc = ([None] * len(streams) for _ in range(9))
        for q, (s, d, u) in enumerate(streams):
            rws[q] = rows(i * cpi + u if d == 0 else n - 1 - (i * cpi + u))
            rc, vc[q], kkc = r_s[s, rws[q], :], v_s[s, rws[q], :], kk_s[s, rws[q], :]
            lwc, bc, kc = lw_s[s, d, rws[q], :], b_s[s, d, rws[q], :], kd_s[s, d, rws[q], :]
            cum = lwc
            for k in (1 << p for p in range((C - 1).bit_length())):
                if d == 0:
                    cum = cum + jnp.where(row_id >= k, pltpu.roll(cum, k, 0), 0.0)
                else:
                    cum = cum + jnp.where(row_id < C - k, pltpu.roll(cum, C - k, 0), 0.0)
            tot = cum[C - 1:C, :] if d == 0 else cum[0:1, :]
            pinv = jnp.exp(-cum)
            pend = jnp.exp(tot - cum)
            at[q] = -kkc * jnp.exp(cum - lwc)
            bt[q] = bc * pinv
            kt[q] = kc * pinv
            rt[q] = rc * jnp.exp(cum)
            bh[q] = bc * pend
            kh[q] = kc * pend
            etot[q] = jnp.exp(tot)
        chains = [(q, h) for q in range(len(streams)) for h in range(HEADS)]
        dirn = lambda q: streams[q][1]
        ln = lambda h: slice(h * HEAD_DIM, (h + 1) * HEAD_DIM)
        wk = [jnp.concatenate([bt[q][:, ln(h)], kt[q][:, ln(h)]], axis=0) for q, h in chains]
        ma = [_mm_nt(at[q][:, ln(h)], wk[j]) for j, (q, h) in enumerate(chains)]
        mr = [_mm_nt(rt[q][:, ln(h)], wk[j]) for j, (q, h) in enumerate(chains)]
        a_ak = [jnp.where(strict[dirn(q)], ma[j][:, C:2 * C], 0.0) for j, (q, h) in enumerate(chains)]
        p = [jnp.where(strict[dirn(q)], ma[j][:, 0:C], 0.0) for j, (q, h) in enumerate(chains)]
        a_rb = [jnp.where(incl[dirn(q)], mr[j][:, 0:C], 0.0) for j, (q, h) in enumerate(chains)]
        a_rk = [jnp.where(incl[dirn(q)], mr[j][:, C:2 * C], 0.0) for j, (q, h) in enumerate(chains)]
        vh = [vc[q][:, ln(h)] for q, h in chains]
        x = [jnp.concatenate([at[q][:, ln(h)], _mm(a_ak[j], vh[j])], axis=1)
             for j, (q, h) in enumerate(chains)]
        for step in range(6):
            pb = [p[j].astype(BF16) for j in range(len(chains))]
            xh = [x[j].astype(BF16) for j in range(len(chains))]
            if step < 3:
                xl = [(x[j] - xh[j].astype(F32)).astype(BF16) for j in range(len(chains))]
                px = [jnp.dot(pb[j], jnp.concatenate([xh[j], xl[j]], axis=1), preferred_element_type=F32)
                      for j in range(len(chains))]
                x = [x[j] + (px[j][:, 0:2 * HEAD_DIM] + px[j][:, 2 * HEAD_DIM:]) for j in range(len(chains))]
            else:
                x = [x[j] + jnp.dot(pb[j], xh[j], preferred_element_type=F32) for j in range(len(chains))]
            if step < 5:
                p = [jnp.dot(pb[j], pb[j], preferred_element_type=F32) for j in range(len(chains))]
        gh = [_mm_tn(x[j], bh[q][:, ln(h)]) for j, (q, h) in enumerate(chains)]
        vk = [_mm_tn_3pass(vh[j], kh[q][:, ln(h)]) for j, (q, h) in enumerate(chains)]
        ax = [_mm(a_rb[j], x[j]) for j in range(len(chains))]
        qt = [rt[q][:, ln(h)] + ax[j][:, 0:HEAD_DIM] for j, (q, h) in enumerate(chains)]
        y0 = [ax[j][:, HEAD_DIM:2 * HEAD_DIM] + _mm(a_rk[j], vh[j]) for j in range(len(chains))]
        ys = [None] * len(chains)
        scans = [(s, d, h) for s in range(nb) for d in range(2) for h in range(HEADS)]
        st = {k: st_s[k] for k in scans}
        for u in range(cpi):
            for s, d, h in scans:
                q = streams.index((s, d, u))
                j = q * HEADS + h
                cur = st[s, d, h]
                ys[j] = _mm_nt(qt[j], cur) + y0[j]
                st[s, d, h] = (cur * etot[q][:, ln(h)] + _mm(cur, gh[j][0:HEAD_DIM, :])
                               + gh[j][HEAD_DIM:2 * HEAD_DIM, :] + vk[j])
        for k in scans:
            st_s[k] = st[k]
        for q, (s, d, u) in enumerate(streams):
            y = jnp.concatenate(ys[q * HEADS:(q + 1) * HEADS], axis=1)
            y_ref[s, rws[q], :] = y_ref[s, rws[q], :] + y
        return carry

    lax.fori_loop(0, n // cpi, chunk, 0)
    if want_fin:
        for s in range(nb):
            for d in range(2):
                for h in range(HEADS):
                    fin_ref[s, d, h] = st_s[s, d, h]

    def post(c, carry):
        rws = rows(c)
        for s in range(nb):
            y = _head_norm(y_ref[s, rws, :], gw_ref[...], gb_ref[...])
            y_ref[s, rws, :] = (y + bv_s[s, rws, :]) * g_s[s, rws, :]
        return carry

    lax.fori_loop(0, n, post, 0)


def _rwkv(mix3, lora3, lw, s0, want_fin):
    b, seq, _ = mix3.shape
    has_init = s0 is not None
    W = BRANCH_W
    nb = 2 if (seq <= 256 and b % 2 == 0) else 1
    col_spec = lambda j: pl.BlockSpec((nb, seq, W), lambda i: (i, 0, j))
    full = lambda a: pl.BlockSpec(a.shape, lambda i: (0,) * a.ndim)
    weights = [lw['conv'], lw['w0'], lw['w2'], lw['a0'], lw['a2'], lw['g2'],
               lw['k_k'], lw['k_a'], lw['r_k'], lw['gn_w'], lw['gn_b']]
    in_specs = [col_spec(4), col_spec(5), col_spec(6),
                pl.BlockSpec((nb, seq, COLS_LORA), lambda i: (i, 0, 0))] + [full(a) for a in weights]
    args = [mix3, mix3, mix3, lora3] + weights
    state_spec = pl.BlockSpec((nb, 2, HEADS, HEAD_DIM, HEAD_DIM), lambda i: (i, 0, 0, 0, 0))
    if has_init:
        in_specs.append(state_spec)
        args.append(s0)
    out_specs = [pl.BlockSpec((nb, seq, W), lambda i: (i, 0, 0))]
    out_shape = [jax.ShapeDtypeStruct((b, seq, W), F32)]
    if want_fin:
        out_specs.append(state_spec)
        out_shape.append(jax.ShapeDtypeStruct((b, 2, HEADS, HEAD_DIM, HEAD_DIM), F32))
    sw = pltpu.VMEM((nb, seq, W), F32)
    sw2 = pltpu.VMEM((nb, 2, seq, W), F32)
    res = pl.pallas_call(
        functools.partial(_rwkv_kernel, seq=seq, nb=nb, has_init=has_init, want_fin=want_fin),
        grid=(b // nb,),
        in_specs=in_specs,
        out_specs=out_specs,
        out_shape=out_shape,
        scratch_shapes=[sw, sw, sw, sw, sw, sw2, sw2, sw2,
                        pltpu.VMEM((nb, 2, HEADS, HEAD_DIM, HEAD_DIM), F32)],
        compiler_params=_cparams("parallel"),
        name="rwkv7",
    )(*args)
    return (res[0], res[1]) if want_fin else (res[0], None)


def _merge_kernel(ya_ref, yb_ref, gate_ref, x_ref, mod_ref, wa_ref, wb_ref, wo_ref, n2_ref, x1_ref, h2_ref):
    m = mod_ref[0]
    br_a = jnp.dot(ya_ref[...].astype(BF16), wa_ref[...], preferred_element_type=F32)
    br_b = jnp.dot(yb_ref[...].astype(BF16), wb_ref[...], preferred_element_type=F32)
    gate = gate_ref[...]
    merged = _sigmoid(gate[:, 0:D_MODEL]) * br_a + _sigmoid(gate[:, D_MODEL:]) * br_b
    mix = jnp.dot(merged.astype(BF16), wo_ref[...], preferred_element_type=F32)
    x1 = x_ref[...] + m[2:3] * mix
    x1_ref[...] = x1
    h2_ref[...] = (_rms(x1, n2_ref[...]) * (1.0 + m[4:5]) + m[3:4]).astype(BF16)


def _merge(ya2, yb2, gate2, x2, mod, mod_base, mod_step, seq, wa, wb, wo, norm2_g):
    n = x2.shape[0]
    tm = 256
    mod_idx = lambda i: (mod_base + mod_step * ((i * tm) // seq), 0, 0)
    row = lambda w: pl.BlockSpec((tm, w), lambda i: (i, 0))
    full = lambda a: pl.BlockSpec(a.shape, lambda i: (0,) * a.ndim)
    return pl.pallas_call(
        _merge_kernel,
        grid=(n // tm,),
        in_specs=[row(BRANCH_W), row(BRANCH_W), row(COLS_GATE), row(D_MODEL),
                  pl.BlockSpec((1, N_MOD, D_MODEL), mod_idx),
                  full(wa), full(wb), full(wo), full(norm2_g)],
        out_specs=[row(D_MODEL), row(D_MODEL)],
        out_shape=[jax.ShapeDtypeStruct((n, D_MODEL), F32), jax.ShapeDtypeStruct((n, D_MODEL), BF16)],
        compiler_params=_cparams("parallel"),
        name="merge_out",
    )(ya2, yb2, gate2, x2, mod, wa, wb, wo, norm2_g)


def _bitonic_pairs(n):
    pairs = []
    k = 2
    while k <= n:
        j = k // 2
        while j >= 1:
            for i in range(n):
                l = i ^ j
                if l > i:
                    pairs.append((i, l) if (i & k) == 0 else (l, i))
            j //= 2
        k *= 2
    return pairs


def _top_values_of_keys(s, k):
    r, t = s.shape
    nl = r // 8
    lv = [s[8 * j:8 * j + 8, :] for j in range(nl)]
    for a, b in _bitonic_pairs(nl):
        lv[a], lv[b] = jnp.maximum(lv[a], lv[b]), jnp.minimum(lv[a], lv[b])
    sid = lax.broadcasted_iota(jnp.int32, (8, t), 0)
    out = []
    for it in range(k):
        m = jnp.max(lv[0], axis=0, keepdims=True)
        out.append(m)
        first = jnp.min(jnp.where(lv[0] == m, sid, 8), axis=0, keepdims=True)
        pop = sid == first
        for j in range(min(nl, k - 1 - it)):
            lv[j] = jnp.where(pop, lv[j + 1] if j + 1 < nl else NEG_INF, lv[j])
    return out


def _peer_score_kernel(h_ref, wq_ref, keys_ref, thr_ref, g1_ref, s2_ref, e2_ref, cand_ref):
    qt = _mm_nt(wq_ref[...], h_ref[...])
    half = PEER_DQ // 2
    k1 = PEER_TOPK + 1
    pairs = [(i, j) for i in range(k1) for j in range(k1) if (i + 1) * (j + 1) <= k1]
    assert len(pairs) <= cand_ref.shape[0]
    for h in range(PEER_HEADS):
        s = [_mm(keys_ref[2 * h + c], qt[(2 * h + c) * half:(2 * h + c + 1) * half, :]) for c in range(2)]
        tops = [_top_values_of_keys(s[c], k1) for c in range(2)]
        cand_ref[...] = jnp.full(cand_ref.shape, NEG_INF, F32)
        for r, (i, j) in enumerate(pairs):
            cand_ref[r:r + 1, :] = tops[0][i] + tops[1][j]
        best = _top_values_of_keys(cand_ref[...], k1)
        mx = best[0]
        z = jnp.exp(best[0] - mx)
        for r in range(1, PEER_TOPK):
            z = z + jnp.exp(best[r] - mx)
        theta = 0.5 * (best[PEER_TOPK - 1] + best[PEER_TOPK])
        thr_ref[h] = theta - s[0]
        g1_ref[h] = jnp.exp(s[0] - tops[0][0]) * (0.5 / z)
        s2_ref[h] = s[1]
        e2_ref[h] = jnp.exp(s[1] - tops[1][0])


def _peer_scores(h2, wq_t, keys):
    n = h2.shape[0]
    tb = 256
    full = lambda a: pl.BlockSpec(a.shape, lambda i: (0,) * a.ndim)
    return pl.pallas_call(
        _peer_score_kernel,
        grid=(n // tb,),
        in_specs=[pl.BlockSpec((tb, D_MODEL), lambda i: (i, 0)), full(wq_t), full(keys)],
        out_specs=[pl.BlockSpec((PEER_HEADS, N_KEYS, tb), lambda i: (0, 0, i))] * 4,
        out_shape=[jax.ShapeDtypeStruct((PEER_HEADS, N_KEYS, n), F32)] * 4,
        scratch_shapes=[pltpu.VMEM((64, tb), F32)],
        compiler_params=_cparams("parallel"),
        name="peer_scores",
    )(h2, wq_t, keys)


def _gelu_tanh_x2(x):
    return x + x * jnp.tanh(x * (0.7978845608028654 + 0.035677408136300125 * (x * x)))


def _peer_expert_kernel(h_ref, u_ref, vt_ref, thr_ref, g1_ref, s2_ref, e2_ref, x1_ref, mod_ref, fg_ref, y_ref,
                        acc_ref, w_ref, *, rows_per_step):
    e = pl.program_id(1)
    tb = h_ref.shape[0]

    @pl.when(e == 0)
    def _():
        acc_ref[...] = jnp.zeros_like(acc_ref)

    st = _mm_nt(u_ref[...], h_ref[...])
    rows_per_chunk = 2
    partial = None
    for al in range(rows_per_step):
        a = e * rows_per_step + al
        r0 = al * N_KEYS
        thr = [thr_ref[h, pl.ds(a, 1), :] for h in range(PEER_HEADS)]
        g1 = [g1_ref[h, pl.ds(a, 1), :] for h in range(PEER_HEADS)]
        for lt in range(tb // LANES):
            ls = slice(lt * LANES, (lt + 1) * LANES)
            wsum = None
            for h in range(PEER_HEADS):
                w = jnp.where(s2_ref[h, :, ls] >= thr[h][:, ls], e2_ref[h, :, ls] * g1[h][:, ls], 0.0)
                wsum = w if wsum is None else wsum + w
            act = _gelu_tanh_x2(st[r0:r0 + N_KEYS, ls])
            w_ref[r0:r0 + N_KEYS, ls] = (wsum * act).astype(BF16)
        if (al + 1) % rows_per_chunk == 0:
            rs = slice((al + 1 - rows_per_chunk) * N_KEYS, (al + 1) * N_KEYS)
            d = jnp.dot(vt_ref[:, rs], w_ref[rs, :], preferred_element_type=F32)
            partial = d if partial is None else partial + d
    acc_ref[...] += partial

    @pl.when(e == pl.num_programs(1) - 1)
    def _():
        m = mod_ref[0]
        x2 = x1_ref[...] + m[5:6] * acc_ref[...].T
        y_ref[...] = _rms(x2, fg_ref[...])


def _peer_experts(h2, u_bf, vt_bf, stats, x1, mod, mod_base, mod_step, seq, final_g):
    n = h2.shape[0]
    tb = 512 if (mod_step == 0 or seq % 512 == 0) else seq
    rows_per_step = 8
    ec = rows_per_step * N_KEYS
    mod_idx = lambda i, e: (mod_base + mod_step * ((i * tb) // seq), 0, 0)
    stat_spec = pl.BlockSpec((PEER_HEADS, N_KEYS, tb), lambda i, e: (0, 0, i))
    return pl.pallas_call(
        functools.partial(_peer_expert_kernel, rows_per_step=rows_per_step),
        grid=(n // tb, N_EXPERTS // ec),
        in_specs=[pl.BlockSpec((tb, D_MODEL), lambda i, e: (i, 0)),
                  pl.BlockSpec((ec, D_MODEL), lambda i, e: (e, 0)),
                  pl.BlockSpec((D_MODEL, ec), lambda i, e: (0, e)),
                  stat_spec, stat_spec, stat_spec, stat_spec,
                  pl.BlockSpec((tb, D_MODEL), lambda i, e: (i, 0)),
                  pl.BlockSpec((1, N_MOD, D_MODEL), mod_idx),
                  pl.BlockSpec((1, D_MODEL), lambda i, e: (0, 0))],
        out_specs=pl.BlockSpec((tb, D_MODEL), lambda i, e: (i, 0)),
        out_shape=jax.ShapeDtypeStruct((n, D_MODEL), F32),
        scratch_shapes=[pltpu.VMEM((D_MODEL, tb), F32), pltpu.VMEM((ec, tb), BF16)],
        compiler_params=_cparams("parallel", "arbitrary"),
        name="peer_experts",
    )(h2, u_bf, vt_bf, *stats, x1, mod, final_g)


def _rope_tables(seq):
    pos = jnp.arange(seq, dtype=jnp.int32)
    lane = jnp.arange(LANES, dtype=jnp.int32) % HEAD_DIM
    use_col = (lane // 32) == 1
    p = jnp.where(use_col[None, :], (pos % GRID_W)[:, None], (pos // GRID_W)[:, None]).astype(F32)
    inv = ROPE_BASE ** (-(lane % 16).astype(F32) / 16.0)
    ang = p * inv[None, :]
    first = (lane % 32) < 16
    return jnp.cos(ang), jnp.where(first[None, :], -jnp.sin(ang), jnp.sin(ang))


def _trunk_path(x, mod, mod_base, mod_step, rope_tabs, s_ret0, s_rwkv0, want_fin, lw, final_g):
    b, seq, _ = x.shape
    x2 = x.reshape(b * seq, D_MODEL)
    mix, lora, gate = _in_proj(x2, mod, mod_base, mod_step, seq, lw['norm1_g'], lw['w_in'])
    mix3 = mix.reshape(b, seq, COLS_MIX)
    lora3 = lora.reshape(b, seq, COLS_LORA)
    ya, ret_fin = _retention(mix3, lw['ret_decay'], lw['ret_gn_w'], lw['ret_gn_b'], rope_tabs, s_ret0, want_fin)
    yb, rwkv_fin = _rwkv(mix3, lora3, lw['rwkv'], s_rwkv0, want_fin)
    x1, h2 = _merge(ya.reshape(b * seq, BRANCH_W), yb.reshape(b * seq, BRANCH_W), gate, x2, mod, mod_base,
                    mod_step, seq, lw['w_br_a'], lw['w_br_b'], lw['w_out'], lw['norm2_g'])
    stats = _peer_scores(h2, lw['peer_wq_t'], lw['peer_keys'])
    y = _peer_experts(h2, lw['peer_u'], lw['peer_vt'], stats, x1, mod, mod_base, mod_step, seq, final_g)
    return y.reshape(b, seq, D_MODEL), x1.reshape(b, seq, D_MODEL), ret_fin, rwkv_fin


def kernel(x_prompt, x_sample, state_ret, state_rwkv, c, c_ctx, ada_w, ada_b, norm1_g, w_in, ret_decay, ret_gn_w, ret_gn_b, rwkv_conv, rwkv_w0, rwkv_w2, rwkv_a0, rwkv_a2, rwkv_g2, rwkv_k_k, rwkv_k_a, rwkv_r_k, rwkv_gn_w, rwkv_gn_b, w_br_a, w_br_b, w_out, norm2_g, peer_wq, peer_keys, peer_u, peer_v, final_norm_g):
    assert w_in.shape[0] == 1, "the final norm is fused into the layer's last kernel: single trunk layer only"
    row = lambda a: a.reshape(1, -1)
    cc = jnp.concatenate([c_ctx[None, :], c], axis=0)
    cc = jnp.pad(cc, ((0, (-cc.shape[0]) % 8), (0, 0)))
    rope_tabs = _rope_tables(x_sample.shape[1])
    final_g = row(final_norm_g)

    if True:
        l = 0
        lw = {
            'norm1_g': row(norm1_g[l]), 'w_in': w_in[l].astype(BF16),
            'ret_decay': jnp.repeat(ret_decay[l], HEAD_DIM, axis=1),
            'ret_gn_w': row(ret_gn_w[l]), 'ret_gn_b': row(ret_gn_b[l]),
            'rwkv': {'conv': rwkv_conv[l], 'w0': rwkv_w0[l], 'w2': rwkv_w2[l].astype(BF16), 'a0': rwkv_a0[l],
                     'a2': rwkv_a2[l].astype(BF16), 'g2': rwkv_g2[l].astype(BF16), 'k_k': row(rwkv_k_k[l]),
                     'k_a': row(rwkv_k_a[l]), 'r_k': row(rwkv_r_k[l]), 'gn_w': row(rwkv_gn_w[l]),
                     'gn_b': row(rwkv_gn_b[l])},
            'w_br_a': w_br_a[l].astype(BF16), 'w_br_b': w_br_b[l].astype(BF16), 'w_out': w_out[l].astype(BF16),
            'norm2_g': row(norm2_g[l]),
            'peer_wq_t': peer_wq[l].astype(BF16).T,
            'peer_keys': peer_keys[l].reshape(2 * PEER_HEADS, N_KEYS, PEER_DQ // 2).astype(BF16),
            'peer_u': peer_u[l].astype(BF16), 'peer_vt': peer_v[l].astype(BF16).T,
        }
        mod = _adaln(cc, ada_w[l], row(ada_b[l])).reshape(cc.shape[0], N_MOD, D_MODEL)
        yp, _, ret_fin, rwkv_fin = _trunk_path(x_prompt, mod, 0, 0, None, None, None, True, lw, final_g)
        ys, _, _, _ = _trunk_path(x_sample, mod, 1, 1, rope_tabs, state_ret[:, l], state_rwkv[:, l], False, lw,
                                  final_g)
    return (yp, ys, ret_fin[:, None], rwkv_fin[:, None])
```

```python
import functools

import jax
import jax.numpy as jnp
from jax import lax
from jax.experimental import pallas as pl
from jax.experimental.pallas import tpu as pltpu

F32 = jnp.float32
BF16 = jnp.bfloat16

D_MODEL = 1024
GRID_W = 64
N_MOD = 6
HEADS = 8
HEAD_DIM = 64
BRANCH_W = HEADS * HEAD_DIM
RET_CHUNK = 128
RWKV_CHUNK = 64
LORA_W = 64
LORA_A = 64
LORA_G = 128
PEER_HEADS = 8
N_KEYS = 128
N_EXPERTS = N_KEYS * N_KEYS
PEER_DQ = 256
PEER_TOPK = 16
ROPE_BASE = 10000.0
NORM_EPS = 1e-6
GN_EPS = 64e-5
COLS_MIX = 7 * BRANCH_W
COLS_LORA = LORA_W + LORA_A + LORA_G
COLS_GATE = 2 * D_MODEL
IN_COLS = COLS_MIX + COLS_LORA + COLS_GATE

V7X_VMEM_LIMIT_BYTES = 56 * 1024 * 1024
LANES = 128
NEG_INF = float("-inf")


def _cparams(*sem):
    return pltpu.CompilerParams(dimension_semantics=sem, vmem_limit_bytes=V7X_VMEM_LIMIT_BYTES)


def _mm(a, b):
    return jnp.dot(a.astype(BF16), b.astype(BF16), preferred_element_type=F32)


def _mm_nt(a, b):
    return lax.dot_general(a.astype(BF16), b.astype(BF16), (((1,), (1,)), ((), ())),
                           preferred_element_type=F32)


def _mm_tn(a, b):
    return lax.dot_general(a.astype(BF16), b.astype(BF16), (((0,), (0,)), ((), ())),
                           preferred_element_type=F32)


def _mm_tn_3pass(a, b):
    a_hi, b_hi = a.astype(BF16), b.astype(BF16)
    a_lo = (a - a_hi.astype(F32)).astype(BF16)
    b_lo = (b - b_hi.astype(F32)).astype(BF16)
    dims = (((0,), (0,)), ((), ()))
    dot = lambda x, y: lax.dot_general(x, y, dims, preferred_element_type=F32)
    return dot(a_hi, b_hi) + (dot(a_hi, b_lo) + dot(a_lo, b_hi))


def _sigmoid(x):
    return 1.0 / (1.0 + jnp.exp(-x))


def _rms(x, g):
    return x * lax.rsqrt(jnp.mean(x * x, axis=-1, keepdims=True) + NORM_EPS) * g


def _head_sum(x):
    t, w = x.shape
    lo = lax.broadcasted_iota(jnp.int32, (t, LANES), 1) < HEAD_DIM
    outs = []
    for j in range(w // LANES):
        xt = x[:, j * LANES:(j + 1) * LANES]
        s_lo = jnp.sum(jnp.where(lo, xt, 0.0), axis=-1, keepdims=True)
        s_hi = jnp.sum(jnp.where(lo, 0.0, xt), axis=-1, keepdims=True)
        outs.append(jnp.where(lo, s_lo, s_hi))
    return outs[0] if len(outs) == 1 else jnp.concatenate(outs, axis=-1)


def _head_norm(y, w, b):
    mu = _head_sum(y) * (1.0 / HEAD_DIM)
    d = y - mu
    var = _head_sum(d * d) * (1.0 / HEAD_DIM)
    return d * lax.rsqrt(var + GN_EPS) * w + b


def _mod_kernel(c_ref, w_ref, b_ref, o_ref):
    c = c_ref[...]
    o_ref[...] = _mm(c * _sigmoid(c), w_ref[...]) + b_ref[...]


def _adaln(cc, ada_w, ada_b):
    rows = cc.shape[0]
    n = ada_w.shape[1]
    tn = n // 4
    return pl.pallas_call(
        _mod_kernel,
        grid=(n // tn,),
        in_specs=[pl.BlockSpec((rows, D_MODEL), lambda j: (0, 0)),
                  pl.BlockSpec((D_MODEL, tn), lambda j: (0, j)),
                  pl.BlockSpec((1, tn), lambda j: (0, j))],
        out_specs=pl.BlockSpec((rows, tn), lambda j: (0, j)),
        out_shape=jax.ShapeDtypeStruct((rows, n), F32),
        compiler_params=_cparams("parallel"),
        name="adaln_mod",
    )(cc, ada_w, ada_b)


def _in_kernel(x_ref, mod_ref, g_ref, w_ref, mix_ref, lora_ref, gate_ref):
    m = mod_ref[0]
    h = (_rms(x_ref[...], g_ref[...]) * (1.0 + m[1:2]) + m[0:1]).astype(BF16)
    mix_ref[...] = jnp.dot(h, w_ref[:, 0:COLS_MIX], preferred_element_type=F32)
    lora_ref[...] = jnp.dot(h, w_ref[:, COLS_MIX:COLS_MIX + COLS_LORA], preferred_element_type=F32)
    gate_ref[...] = jnp.dot(h, w_ref[:, COLS_MIX + COLS_LORA:IN_COLS], preferred_element_type=F32)


def _in_proj(x2, mod, mod_base, mod_step, seq, norm_g, w_in_bf):
    n = x2.shape[0]
    tm = 256
    mod_idx = lambda i: (mod_base + mod_step * ((i * tm) // seq), 0, 0)
    return pl.pallas_call(
        _in_kernel,
        grid=(n // tm,),
        in_specs=[pl.BlockSpec((tm, D_MODEL), lambda i: (i, 0)),
                  pl.BlockSpec((1, N_MOD, D_MODEL), mod_idx),
                  pl.BlockSpec((1, D_MODEL), lambda i: (0, 0)),
                  pl.BlockSpec((D_MODEL, IN_COLS), lambda i: (0, 0))],
        out_specs=[pl.BlockSpec((tm, COLS_MIX), lambda i: (i, 0)),
                   pl.BlockSpec((tm, COLS_LORA), lambda i: (i, 0)),
                   pl.BlockSpec((tm, COLS_GATE), lambda i: (i, 0))],
        out_shape=[jax.ShapeDtypeStruct((n, COLS_MIX), F32),
                   jax.ShapeDtypeStruct((n, COLS_LORA), F32),
                   jax.ShapeDtypeStruct((n, COLS_GATE), F32)],
        compiler_params=_cparams("parallel"),
        name="in_proj",
    )(x2, mod, norm_g, w_in_bf)


def _rope(x, cos, sin):
    lane = lax.broadcasted_iota(jnp.int32, x.shape, 1)
    first = (lane % 32) < 16
    partner = jnp.where(first, pltpu.roll(x, LANES - 16, 1), pltpu.roll(x, 16, 1))
    return x * cos + partner * sin


def _ret_kernel(*refs, seq, rope, has_init, want_fin):
    it = iter(refs)
    q_ref, k_ref, v_ref, g_ref, rd_ref, gw_ref, gb_ref = (next(it) for _ in range(7))
    cos_ref = next(it) if rope else None
    sin_ref = next(it) if rope else None
    s0_ref = next(it) if has_init else None
    y_ref = next(it)
    fin_ref = next(it) if want_fin else None
    qs_ref, ks_ref, vec_ref, dec_ref, r_ref = (next(it) for _ in range(5))

    C = RET_CHUNK
    W = BRANCH_W
    n = seq // C
    rd = rd_ref[...]
    lg = jnp.minimum(rd, 0.0) - jnp.log(1.0 + jnp.exp(-jnp.abs(rd)))
    lgf, lgb = lg[0:1, :], lg[1:2, :]
    ii = lax.broadcasted_iota(jnp.int32, (C, C), 0)
    jj = lax.broadcasted_iota(jnp.int32, (C, C), 1)
    diff = (ii - jj).astype(F32)
    col = lax.broadcasted_iota(jnp.int32, (C, W), 0).astype(F32)
    ln = lambda h: slice(h * HEAD_DIM, (h + 1) * HEAD_DIM)

    vec_ref[0] = jnp.exp((col + 1.0) * lgf)
    vec_ref[1] = jnp.exp((C - 1.0 - col) * lgf)
    vec_ref[2] = jnp.exp((C - col) * lgb)
    vec_ref[3] = jnp.exp(col * lgb)
    chunk_f = jnp.exp(C * lgf)
    chunk_b = jnp.exp(C * lgb)
    for h in range(HEADS):
        gf = lgf[:, h * HEAD_DIM:h * HEAD_DIM + 1]
        gb = lgb[:, h * HEAD_DIM:h * HEAD_DIM + 1]
        dec_ref[h] = (jnp.where(diff >= 0, jnp.exp(jnp.maximum(diff, 0.0) * gf), 0.0)
                      + jnp.where(diff <= 0, jnp.exp(jnp.maximum(-diff, 0.0) * gb), 0.0))
        for d in range(2):
            r_ref[d, h] = s0_ref[0, d, h] if has_init else jnp.zeros((HEAD_DIM, HEAD_DIM), F32)

    def rows(c):
        return pl.ds(pl.multiple_of(c * C, C), C)

    def prep(c, carry):
        r = rows(c)
        q = q_ref[0, r, :]
        k = k_ref[0, r, :] * (HEAD_DIM ** -0.5)
        if rope:
            cos, sin = cos_ref[r, :], sin_ref[r, :]
            tiles = lambda x: [x[:, j * LANES:(j + 1) * LANES] for j in range(W // LANES)]
            q = jnp.concatenate([_rope(t, cos, sin) for t in tiles(q)], axis=1)
            k = jnp.concatenate([_rope(t, cos, sin) for t in tiles(k)], axis=1)
        qs_ref[r, :] = q
        ks_ref[r, :] = k
        y_ref[0, r, :] = jnp.zeros((C, W), F32)
        return carry

    lax.fori_loop(0, n, prep, 0)

    cpi = 2 if n % 4 == 0 else 1

    def step(i, carry):
        rf = [rows(i * cpi + u) for u in range(cpi)]
        rb = [rows(n - 1 - (i * cpi + u)) for u in range(cpi)]
        intra, qfx, inc_f, qbx, inc_b = ([[None] * HEADS for _ in range(cpi)] for _ in range(5))
        for u in range(cpi):
            qf, kf, vf = qs_ref[rf[u], :], ks_ref[rf[u], :], v_ref[0, rf[u], :]
            qb, kb, vb = qs_ref[rb[u], :], ks_ref[rb[u], :], v_ref[0, rb[u], :]
            qfs, kfs = qf * vec_ref[0], kf * vec_ref[1]
            qbs, kbs = qb * vec_ref[2], kb * vec_ref[3]
            for h in range(HEADS):
                sc = _mm_nt(qf[:, ln(h)], kf[:, ln(h)]) * dec_ref[h]
                intra[u][h] = _mm(sc, vf[:, ln(h)])
                qfx[u][h], qbx[u][h] = qfs[:, ln(h)], qbs[:, ln(h)]
                inc_f[u][h] = _mm_tn(kfs[:, ln(h)], vf[:, ln(h)])
                inc_b[u][h] = _mm_tn(kbs[:, ln(h)], vb[:, ln(h)])
        sf = [r_ref[0, h] for h in range(HEADS)]
        sb = [r_ref[1, h] for h in range(HEADS)]
        of, ob = ([[None] * HEADS for _ in range(cpi)] for _ in range(2))
        for u in range(cpi):
            for h in range(HEADS):
                of[u][h] = intra[u][h] + _mm(qfx[u][h], sf[h])
                sf[h] = sf[h] * chunk_f[:, ln(h)] + inc_f[u][h]
                ob[u][h] = _mm(qbx[u][h], sb[h])
                sb[h] = sb[h] * chunk_b[:, ln(h)] + inc_b[u][h]
        yf = [y_ref[0, rf[u], :] for u in range(cpi)]
        yb = [y_ref[0, rb[u], :] for u in range(cpi)]
        for u in range(cpi):
            y_ref[0, rf[u], :] = yf[u] + jnp.concatenate(of[u], axis=1)
            y_ref[0, rb[u], :] = yb[u] + jnp.concatenate(ob[u], axis=1)
        for h in range(HEADS):
            r_ref[0, h] = sf[h]
            r_ref[1, h] = sb[h]
        return carry

    lax.fori_loop(0, n // cpi, step, 0)
    if want_fin:
        for d in range(2):
            for h in range(HEADS):
                fin_ref[0, d, h] = r_ref[d, h]

    def post(c, carry):
        r = rows(c)
        g = g_ref[0, r, :]
        y_ref[0, r, :] = _head_norm(y_ref[0, r, :], gw_ref[...], gb_ref[...]) * (g * _sigmoid(g))
        return carry

    lax.fori_loop(0, n, post, 0)


def _retention(mix3, rd_l, gn_w, gn_b, rope_tabs, s0, want_fin):
    b, seq, _ = mix3.shape
    W = BRANCH_W
    rope = rope_tabs is not None
    has_init = s0 is not None
    col_spec = lambda j: pl.BlockSpec((1, seq, W), lambda i: (i, 0, j))
    full = lambda a: pl.BlockSpec(a.shape, lambda i: (0,) * a.ndim)
    in_specs = [col_spec(0), col_spec(1), col_spec(2), col_spec(3), full(rd_l), full(gn_w), full(gn_b)]
    args = [mix3, mix3, mix3, mix3, rd_l, gn_w, gn_b]
    if rope:
        in_specs += [full(rope_tabs[0]), full(rope_tabs[1])]
        args += list(rope_tabs)
    state_spec = pl.BlockSpec((1, 2, HEADS, HEAD_DIM, HEAD_DIM), lambda i: (i, 0, 0, 0, 0))
    if has_init:
        in_specs.append(state_spec)
        args.append(s0)
    out_specs = [pl.BlockSpec((1, seq, W), lambda i: (i, 0, 0))]
    out_shape = [jax.ShapeDtypeStruct((b, seq, W), F32)]
    if want_fin:
        out_specs.append(state_spec)
        out_shape.append(jax.ShapeDtypeStruct((b, 2, HEADS, HEAD_DIM, HEAD_DIM), F32))
    res = pl.pallas_call(
        functools.partial(_ret_kernel, seq=seq, rope=rope, has_init=has_init, want_fin=want_fin),
        grid=(b,),
        in_specs=in_specs,
        out_specs=out_specs,
        out_shape=out_shape,
        scratch_shapes=[pltpu.VMEM((seq, W), F32), pltpu.VMEM((seq, W), F32),
                        pltpu.VMEM((4, RET_CHUNK, W), F32), pltpu.VMEM((HEADS, RET_CHUNK, RET_CHUNK), F32),
                        pltpu.VMEM((2, HEADS, HEAD_DIM, HEAD_DIM), F32)],
        compiler_params=_cparams("parallel"),
        name="retention",
    )(*args)
    return (res[0], res[1]) if want_fin else (res[0], None)


EXP_M_HALF = 0.6065306597126334


def _rwkv_kernel(*refs, seq, nb, has_init, want_fin):
    it = iter(refs)
    (r_ref, k_ref, v_ref, lora_ref, cw_ref, w0_ref, w2_ref, a0_ref, a2_ref, g2_ref,
     kk_w_ref, ka_ref, rk_ref, gw_ref, gb_ref) = (next(it) for _ in range(15))
    s0_ref = next(it) if has_init else None
    y_ref = next(it)
    fin_ref = next(it) if want_fin else None
    r_s, v_s, kk_s, g_s, bv_s, lw_s, b_s, kd_s, st_s = (next(it) for _ in range(9))

    C = RWKV_CHUNK
    n = seq // C
    W = BRANCH_W
    row_id = lax.broadcasted_iota(jnp.int32, (C, W), 0)

    def rows(c):
        return pl.ds(pl.multiple_of(c * C, C), C)

    def conv(ref, s, c, w):
        x = ref[s, rows(c), :]
        prev8 = ref[s, pl.ds(pl.multiple_of(jnp.maximum(c * C - 8, 0), 8), 8), :]
        next8 = ref[s, pl.ds(pl.multiple_of(jnp.minimum(c * C + C, seq - 8), 8), 8), :]
        prev_row = jnp.where(c > 0, prev8[7:8, :], 0.0)
        next_row = jnp.where(c < n - 1, next8[0:1, :], 0.0)
        xm = jnp.where(row_id == 0, prev_row, pltpu.roll(x, 1, 0))
        xp = jnp.where(row_id == C - 1, next_row, pltpu.roll(x, C - 1, 0))
        return w[0:1] * xm + w[1:2] * x + w[2:3] * xp

    def prep_seq(s, c):
        rws = rows(c)
        cw = cw_ref[...]
        r = conv(r_ref, s, c, cw[:, 0:W])
        k = conv(k_ref, s, c, cw[:, W:2 * W])
        v = conv(v_ref, s, c, cw[:, 2 * W:3 * W])
        lo = lora_ref[s, rws, :]
        dw = lo[:, 0:LORA_W]
        da = lo[:, LORA_W:LORA_W + LORA_A]
        dg = lo[:, LORA_W + LORA_A:]
        kk = k * kk_w_ref[...]
        kk = kk * lax.rsqrt(_head_sum(kk * kk) + 1e-12)
        y_ref[s, rws, :] = jnp.zeros((C, W), F32)
        r_s[s, rws, :] = r
        v_s[s, rws, :] = v
        kk_s[s, rws, :] = kk
        g_s[s, rws, :] = _mm(_sigmoid(dg), g2_ref[...])
        bv_s[s, rws, :] = _head_sum(r * k * rk_ref[...]) * v
        tdw = jnp.tanh(dw)
        kka = k * ka_ref[...]
        for d in range(2):
            z = w0_ref[d:d + 1, :] + _mm(tdw, w2_ref[d])
            a = _sigmoid(a0_ref[d:d + 1, :] + _mm(da, a2_ref[d]))
            lw_s[s, d, rws, :] = -EXP_M_HALF * _sigmoid(z)
            b_s[s, d, rws, :] = kk * a
            kd_s[s, d, rws, :] = k + kka * (a - 1.0)

    def prep(c, carry):
        for s in range(nb):
            prep_seq(s, c)
        return carry

    lax.fori_loop(0, n, prep, 0)

    ii = lax.broadcasted_iota(jnp.int32, (C, C), 0)
    jj = lax.broadcasted_iota(jnp.int32, (C, C), 1)
    incl = [ii >= jj, ii <= jj]
    strict = [ii > jj, ii < jj]

    cpi = 2 if (nb == 1 and n % 4 == 0) else 1
    streams = [(s, d, u) for s in range(nb) for d in range(2) for u in range(cpi)]
    for s in range(nb):
        for d in range(2):
            for h in range(HEADS):
                st_s[s, d, h] = s0_ref[s, d, h] if has_init else jnp.zeros((HEAD_DIM, HEAD_DIM), F32)

    def chunk(i, carry):
        rws, at, bt, kt, rt, bh, kh, etot, vc = ([None] * len(streams) for _ in range(9))
        for q, (s, d, u) in enumerate(streams):
            rws[q] = rows(i * cpi + u if d == 0 else n - 1 - (i * cpi + u))
            rc, vc[q], kkc = r_s[s, rws[q], :], v_s[s, rws[q], :], kk_s[s, rws[q], :]
            lwc, bc, kc = lw_s[s, d, rws[q], :], b_s[s, d, rws[q], :], kd_s[s, d, rws[q], :]
            cum = lwc
            for k in (1 << p for p in range((C - 1).bit_length())):
                if d == 0:
                    cum = cum + jnp.where(row_id >= k, pltpu.roll(cum, k, 0), 0.0)
                else:
                    cum = cum + jnp.where(row_id < C - k, pltpu.roll(cum, C - k, 0), 0.0)
            tot = cum[C - 1:C, :] if d == 0 else cum[0:1, :]
            pinv = jnp.exp(-cum)
            pend = jnp.exp(tot - cum)
            at[q] = -kkc * jnp.exp(cum - lwc)
            bt[q] = bc * pinv
            kt[q] = kc * pinv
            rt[q] = rc * jnp.exp(cum)
            bh[q] = bc * pend
            kh[q] = kc * pend
            etot[q] = jnp.exp(tot)
        chains = [(q, h) for q in range(len(streams)) for h in range(HEADS)]
        dirn = lambda q: streams[q][1]
        ln = lambda h: slice(h * HEAD_DIM, (h + 1) * HEAD_DIM)
        wk = [jnp.concatenate([bt[q][:, ln(h)], kt[q][:, ln(h)]], axis=0) for q, h in chains]
        ma = [_mm_nt(at[q][:, ln(h)], wk[j]) for j, (q, h) in enumerate(chains)]
        mr = [_mm_nt(rt[q][:, ln(h)], wk[j]) for j, (q, h) in enumerate(chains)]
        a_ak = [jnp.where(strict[dirn(q)], ma[j][:, C:2 * C], 0.0) for j, (q, h) in enumerate(chains)]
        p = [jnp.where(strict[dirn(q)], ma[j][:, 0:C], 0.0) for j, (q, h) in enumerate(chains)]
        a_rb = [jnp.where(incl[dirn(q)], mr[j][:, 0:C], 0.0) for j, (q, h) in enumerate(chains)]
        a_rk = [jnp.where(incl[dirn(q)], mr[j][:, C:2 * C], 0.0) for j, (q, h) in enumerate(chains)]
        vh = [vc[q][:, ln(h)] for q, h in chains]
        x = [jnp.concatenate([at[q][:, ln(h)], _mm(a_ak[j], vh[j])], axis=1)
             for j, (q, h) in enumerate(chains)]
        for step in range(6):
            pb = [p[j].astype(BF16) for j in range(len(chains))]
            xh = [x[j].astype(BF16) for j in range(len(chains))]
            if step < 3:
                xl = [(x[j] - xh[j].astype(F32)).astype(BF16) for j in range(len(chains))]
                px = [jnp.dot(pb[j], jnp.concatenate([xh[j], xl[j]], axis=1), preferred_element_type=F32)
                      for j in range(len(chains))]
                x = [x[j] + (px[j][:, 0:2 * HEAD_DIM] + px[j][:, 2 * HEAD_DIM:]) for j in range(len(chains))]
            else:
                x = [x[j] + jnp.dot(pb[j], xh[j], preferred_element_type=F32) for j in range(len(chains))]
            if step < 5:
                p = [jnp.dot(pb[j], pb[j], preferred_element_type=F32) for j in range(len(chains))]
        gh = [_mm_tn(x[j], bh[q][:, ln(h)]) for j, (q, h) in enumerate(chains)]
        vk = [_mm_tn_3pass(vh[j], kh[q][:, ln(h)]) for j, (q, h) in enumerate(chains)]
        ax = [_mm(a_rb[j], x[j]) for j in range(len(chains))]
        qt = [rt[q][:, ln(h)] + ax[j][:, 0:HEAD_DIM] for j, (q, h) in enumerate(chains)]
        y0 = [ax[j][:, HEAD_DIM:2 * HEAD_DIM] + _mm(a_rk[j], vh[j]) for j in range(len(chains))]
        ys = [None] * len(chains)
        scans = [(s, d, h) for s in range(nb) for d in range(2) for h in range(HEADS)]
        st = {k: st_s[k] for k in scans}
        for u in range(cpi):
            for s, d, h in scans:
                q = streams.index((s, d, u))
                j = q * HEADS + h
                cur = st[s, d, h]
                ys[j] = _mm_nt(qt[j], cur) + y0[j]
                st[s, d, h] = (cur * etot[q][:, ln(h)] + _mm(cur, gh[j][0:HEAD_DIM, :])
                               + gh[j][HEAD_DIM:2 * HEAD_DIM, :] + vk[j])
        for k in scans:
            st_s[k] = st[k]
        for q, (s, d, u) in enumerate(streams):
            y = jnp.concatenate(ys[q * HEADS:(q + 1) * HEADS], axis=1)
            y_ref[s, rws[q], :] = y_ref[s, rws[q], :] + y
        return carry

    lax.fori_loop(0, n // cpi, chunk, 0)
    if want_fin:
        for s in range(nb):
            for d in range(2):
                for h in range(HEADS):
                    fin_ref[s, d, h] = st_s[s, d, h]

    def post(c, carry):
        rws = rows(c)
        for s in range(nb):
            y = _head_norm(y_ref[s, rws, :], gw_ref[...], gb_ref[...])
            y_ref[s, rws, :] = (y + bv_s[s, rws, :]) * g_s[s, rws, :]
        return carry

    lax.fori_loop(0, n, post, 0)


def _rwkv(mix3, lora3, lw, s0, want_fin):
    b, seq, _ = mix3.shape
    has_init = s0 is not None
    W = BRANCH_W
    nb = 2 if (seq <= 256 and b % 2 == 0) else 1
    col_spec = lambda j: pl.BlockSpec((nb, seq, W), lambda i: (i, 0, j))
    full = lambda a: pl.BlockSpec(a.shape, lambda i: (0,) * a.ndim)
    weights = [lw['conv'], lw['w0'], lw['w2'], lw['a0'], lw['a2'], lw['g2'],
               lw['k_k'], lw['k_a'], lw['r_k'], lw['gn_w'], lw['gn_b']]
    in_specs = [col_spec(4), col_spec(5), col_spec(6),
                pl.BlockSpec((nb, seq, COLS_LORA), lambda i: (i, 0, 0))] + [full(a) for a in weights]
    args = [mix3, mix3, mix3, lora3] + weights
    state_spec = pl.BlockSpec((nb, 2, HEADS, HEAD_DIM, HEAD_DIM), lambda i: (i, 0, 0, 0, 0))
    if has_init:
        in_specs.append(state_spec)
        args.append(s0)
    out_specs = [pl.BlockSpec((nb, seq, W), lambda i: (i, 0, 0))]
    out_shape = [jax.ShapeDtypeStruct((b, seq, W), F32)]
    if want_fin:
        out_specs.append(state_spec)
        out_shape.append(jax.ShapeDtypeStruct((b, 2, HEADS, HEAD_DIM, HEAD_DIM), F32))
    sw = pltpu.VMEM((nb, seq, W), F32)
    sw2 = pltpu.VMEM((nb, 2, seq, W), F32)
    res = pl.pallas_call(
        functools.partial(_rwkv_kernel, seq=seq, nb=nb, has_init=has_init, want_fin=want_fin),
        grid=(b // nb,),
        in_specs=in_specs,
        out_specs=out_specs,
        out_shape=out_shape,
        scratch_shapes=[sw, sw, sw, sw, sw, sw2, sw2, sw2,
                        pltpu.VMEM((nb, 2, HEADS, HEAD_DIM, HEAD_DIM), F32)],
        compiler_params=_cparams("parallel"),
        name="rwkv7",
    )(*args)
    return (res[0], res[1]) if want_fin else (res[0], None)


def _merge_kernel(ya_ref, yb_ref, gate_ref, x_ref, mod_ref, wa_ref, wb_ref, wo_ref, n2_ref, x1_ref, h2_ref):
    m = mod_ref[0]
    br_a = jnp.dot(ya_ref[...].astype(BF16), wa_ref[...], preferred_element_type=F32)
    br_b = jnp.dot(yb_ref[...].astype(BF16), wb_ref[...], preferred_element_type=F32)
    gate = gate_ref[...]
    merged = _sigmoid(gate[:, 0:D_MODEL]) * br_a + _sigmoid(gate[:, D_MODEL:]) * br_b
    mix = jnp.dot(merged.astype(BF16), wo_ref[...], preferred_element_type=F32)
    x1 = x_ref[...] + m[2:3] * mix
    x1_ref[...] = x1
    h2_ref[...] = (_rms(x1, n2_ref[...]) * (1.0 + m[4:5]) + m[3:4]).astype(BF16)


def _merge(ya2, yb2, gate2, x2, mod, mod_base, mod_step, seq, wa, wb, wo, norm2_g):
    n = x2.shape[0]
    tm = 256
    mod_idx = lambda i: (mod_base + mod_step * ((i * tm) // seq), 0, 0)
    row = lambda w: pl.BlockSpec((tm, w), lambda i: (i, 0))
    full = lambda a: pl.BlockSpec(a.shape, lambda i: (0,) * a.ndim)
    return pl.pallas_call(
        _merge_kernel,
        grid=(n // tm,),
        in_specs=[row(BRANCH_W), row(BRANCH_W), row(COLS_GATE), row(D_MODEL),
                  pl.BlockSpec((1, N_MOD, D_MODEL), mod_idx),
                  full(wa), full(wb), full(wo), full(norm2_g)],
        out_specs=[row(D_MODEL), row(D_MODEL)],
        out_shape=[jax.ShapeDtypeStruct((n, D_MODEL), F32), jax.ShapeDtypeStruct((n, D_MODEL), BF16)],
        compiler_params=_cparams("parallel"),
        name="merge_out",
    )(ya2, yb2, gate2, x2, mod, wa, wb, wo, norm2_g)


def _bitonic_pairs(n):
    pairs = []
    k = 2
    while k <= n:
        j = k // 2
        while j >= 1:
            for i in range(n):
                l = i ^ j
                if l > i:
                    pairs.append((i, l) if (i & k) == 0 else (l, i))
            j //= 2
        k *= 2
    return pairs


def _top_values_of_keys(s, k):
    r, t = s.shape
    nl = r // 8
    lv = [s[8 * j:8 * j + 8, :] for j in range(nl)]
    for a, b in _bitonic_pairs(nl):
        lv[a], lv[b] = jnp.maximum(lv[a], lv[b]), jnp.minimum(lv[a], lv[b])
    sid = lax.broadcasted_iota(jnp.int32, (8, t), 0)
    out = []
    for it in range(k):
        m = jnp.max(lv[0], axis=0, keepdims=True)
        out.append(m)
        first = jnp.min(jnp.where(lv[0] == m, sid, 8), axis=0, keepdims=True)
        pop = sid == first
        for j in range(min(nl, k - 1 - it)):
            lv[j] = jnp.where(pop, lv[j + 1] if j + 1 < nl else NEG_INF, lv[j])
    return out


def _peer_score_kernel(h_ref, wq_ref, keys_ref, thr_ref, g1_ref, s2_ref, e2_ref, cand_ref):
    qt = _mm_nt(wq_ref[...], h_ref[...])
    half = PEER_DQ // 2
    k1 = PEER_TOPK + 1
    pairs = [(i, j) for i in range(k1) for j in range(k1) if (i + 1) * (j + 1) <= k1]
    assert len(pairs) <= cand_ref.shape[0]
    for h in range(PEER_HEADS):
        s = [_mm(keys_ref[2 * h + c], qt[(2 * h + c) * half:(2 * h + c + 1) * half, :]) for c in range(2)]
        tops = [_top_values_of_keys(s[c], k1) for c in range(2)]
        cand_ref[...] = jnp.full(cand_ref.shape, NEG_INF, F32)
        for r, (i, j) in enumerate(pairs):
            cand_ref[r:r + 1, :] = tops[0][i] + tops[1][j]
        best = _top_values_of_keys(cand_ref[...], k1)
        mx = best[0]
        z = jnp.exp(best[0] - mx)
        for r in range(1, PEER_TOPK):
            z = z + jnp.exp(best[r] - mx)
        theta = 0.5 * (best[PEER_TOPK - 1] + best[PEER_TOPK])
        thr_ref[h] = theta - s[0]
        g1_ref[h] = jnp.exp(s[0] - tops[0][0]) * (0.5 / z)
        s2_ref[h] = s[1]
        e2_ref[h] = jnp.exp(s[1] - tops[1][0])


def _peer_scores(h2, wq_t, keys):
    n = h2.shape[0]
    tb = 256
    full = lambda a: pl.BlockSpec(a.shape, lambda i: (0,) * a.ndim)
    return pl.pallas_call(
        _peer_score_kernel,
        grid=(n // tb,),
        in_specs=[pl.BlockSpec((tb, D_MODEL), lambda i: (i, 0)), full(wq_t), full(keys)],
        out_specs=[pl.BlockSpec((PEER_HEADS, N_KEYS, tb), lambda i: (0, 0, i))] * 4,
        out_shape=[jax.ShapeDtypeStruct((PEER_HEADS, N_KEYS, n), F32)] * 4,
        scratch_shapes=[pltpu.VMEM((64, tb), F32)],
        compiler_params=_cparams("parallel"),
        name="peer_scores",
    )(h2, wq_t, keys)


def _gelu_tanh_x2(x):
    return x + x * jnp.tanh(x * (0.7978845608028654 + 0.035677408136300125 * (x * x)))


def _peer_expert_kernel(h_ref, u_ref, vt_ref, thr_ref, g1_ref, s2_ref, e2_ref, x1_ref, mod_ref, fg_ref, y_ref,
                        acc_ref, w_ref, *, rows_per_step):
    e = pl.program_id(1)
    tb = h_ref.shape[0]

    @pl.when(e == 0)
    def _():
        acc_ref[...] = jnp.zeros_like(acc_ref)

    st = _mm_nt(u_ref[...], h_ref[...])
    rows_per_chunk = 2
    partial = None
    for al in range(rows_per_step):
        a = e * rows_per_step + al
        r0 = al * N_KEYS
        thr = [thr_ref[h, pl.ds(a, 1), :] for h in range(PEER_HEADS)]
        g1 = [g1_ref[h, pl.ds(a, 1), :] for h in range(PEER_HEADS)]
        for lt in range(tb // LANES):
            ls = slice(lt * LANES, (lt + 1) * LANES)
            wsum = None
            for h in range(PEER_HEADS):
                w = jnp.where(s2_ref[h, :, ls] >= thr[h][:, ls], e2_ref[h, :, ls] * g1[h][:, ls], 0.0)
                wsum = w if wsum is None else wsum + w
            act = _gelu_tanh_x2(st[r0:r0 + N_KEYS, ls])
            w_ref[r0:r0 + N_KEYS, ls] = (wsum * act).astype(BF16)
        if (al + 1) % rows_per_chunk == 0:
            rs = slice((al + 1 - rows_per_chunk) * N_KEYS, (al + 1) * N_KEYS)
            d = jnp.dot(vt_ref[:, rs], w_ref[rs, :], preferred_element_type=F32)
            partial = d if partial is None else partial + d
    acc_ref[...] += partial

    @pl.when(e == pl.num_programs(1) - 1)
    def _():
        m = mod_ref[0]
        x2 = x1_ref[...] + m[5:6] * acc_ref[...].T
        y_ref[...] = _rms(x2, fg_ref[...])


def _peer_experts(h2, u_bf, vt_bf, stats, x1, mod, mod_base, mod_step, seq, final_g):
    n = h2.shape[0]
    tb = 512 if (mod_step == 0 or seq % 512 == 0) else seq
    rows_per_step = 8
    ec = rows_per_step * N_KEYS
    mod_idx = lambda i, e: (mod_base + mod_step * ((i * tb) // seq), 0, 0)
    stat_spec = pl.BlockSpec((PEER_HEADS, N_KEYS, tb), lambda i, e: (0, 0, i))
    return pl.pallas_call(
        functools.partial(_peer_expert_kernel, rows_per_step=rows_per_step),
        grid=(n // tb, N_EXPERTS // ec),
        in_specs=[pl.BlockSpec((tb, D_MODEL), lambda i, e: (i, 0)),
                  pl.BlockSpec((ec, D_MODEL), lambda i, e: (e, 0)),
                  pl.BlockSpec((D_MODEL, ec), lambda i, e: (0, e)),
                  stat_spec, stat_spec, stat_spec, stat_spec,
                  pl.BlockSpec((tb, D_MODEL), lambda i, e: (i, 0)),
                  pl.BlockSpec((1, N_MOD, D_MODEL), mod_idx),
                  pl.BlockSpec((1, D_MODEL), lambda i, e: (0, 0))],
        out_specs=pl.BlockSpec((tb, D_MODEL), lambda i, e: (i, 0)),
        out_shape=jax.ShapeDtypeStruct((n, D_MODEL), F32),
        scratch_shapes=[pltpu.VMEM((D_MODEL, tb), F32), pltpu.VMEM((ec, tb), BF16)],
        compiler_params=_cparams("parallel", "arbitrary"),
        name="peer_experts",
    )(h2, u_bf, vt_bf, *stats, x1, mod, final_g)


def _rope_tables(seq):
    pos = jnp.arange(seq, dtype=jnp.int32)
    lane = jnp.arange(LANES, dtype=jnp.int32) % HEAD_DIM
    use_col = (lane // 32) == 1
    p = jnp.where(use_col[None, :], (pos % GRID_W)[:, None], (pos // GRID_W)[:, None]).astype(F32)
    inv = ROPE_BASE ** (-(lane % 16).astype(F32) / 16.0)
    ang = p * inv[None, :]
    first = (lane % 32) < 16
    return jnp.cos(ang), jnp.where(first[None, :], -jnp.sin(ang), jnp.sin(ang))


def _trunk_path(x, mod, mod_base, mod_step, rope_tabs, s_ret0, s_rwkv0, want_fin, lw, final_g):
    b, seq, _ = x.shape
    x2 = x.reshape(b * seq, D_MODEL)
    mix, lora, gate = _in_proj(x2, mod, mod_base, mod_step, seq, lw['norm1_g'], lw['w_in'])
    mix3 = mix.reshape(b, seq, COLS_MIX)
    lora3 = lora.reshape(b, seq, COLS_LORA)
    ya, ret_fin = _retention(mix3, lw['ret_decay'], lw['ret_gn_w'], lw['ret_gn_b'], rope_tabs, s_ret0, want_fin)
    yb, rwkv_fin = _rwkv(mix3, lora3, lw['rwkv'], s_rwkv0, want_fin)
    x1, h2 = _merge(ya.reshape(b * seq, BRANCH_W), yb.reshape(b * seq, BRANCH_W), gate, x2, mod, mod_base,
                    mod_step, seq, lw['w_br_a'], lw['w_br_b'], lw['w_out'], lw['norm2_g'])
    stats = _peer_scores(h2, lw['peer_wq_t'], lw['peer_keys'])
    y = _peer_experts(h2, lw['peer_u'], lw['peer_vt'], stats, x1, mod, mod_base, mod_step, seq, final_g)
    return y.reshape(b, seq, D_MODEL), x1.reshape(b, seq, D_MODEL), ret_fin, rwkv_fin


def kernel(x_prompt, x_sample, state_ret, state_rwkv, c, c_ctx, ada_w, ada_b, norm1_g, w_in, ret_decay, ret_gn_w, ret_gn_b, rwkv_conv, rwkv_w0, rwkv_w2, rwkv_a0, rwkv_a2, rwkv_g2, rwkv_k_k, rwkv_k_a, rwkv_r_k, rwkv_gn_w, rwkv_gn_b, w_br_a, w_br_b, w_out, norm2_g, peer_wq, peer_keys, peer_u, peer_v, final_norm_g):
    assert w_in.shape[0] == 1, "the final norm is fused into the layer's last kernel: single trunk layer only"
    row = lambda a: a.reshape(1, -1)
    cc = jnp.concatenate([c_ctx[None, :], c], axis=0)
    cc = jnp.pad(cc, ((0, (-cc.shape[0]) % 8), (0, 0)))
    rope_tabs = _rope_tables(x_sample.shape[1])
    final_g = row(final_norm_g)

    if True:
        l = 0
        lw = {
            'norm1_g': row(norm1_g[l]), 'w_in': w_in[l].astype(BF16),
            'ret_decay': jnp.repeat(ret_decay[l], HEAD_DIM, axis=1),
            'ret_gn_w': row(ret_gn_w[l]), 'ret_gn_b': row(ret_gn_b[l]),
            'rwkv': {'conv': rwkv_conv[l], 'w0': rwkv_w0[l], 'w2': rwkv_w2[l].astype(BF16), 'a0': rwkv_a0[l],
                     'a2': rwkv_a2[l].astype(BF16), 'g2': rwkv_g2[l].astype(BF16), 'k_k': row(rwkv_k_k[l]),
                     'k_a': row(rwkv_k_a[l]), 'r_k': row(rwkv_r_k[l]), 'gn_w': row(rwkv_gn_w[l]),
                     'gn_b': row(rwkv_gn_b[l])},
            'w_br_a': w_br_a[l].astype(BF16), 'w_br_b': w_br_b[l].astype(BF16), 'w_out': w_out[l].astype(BF16),
            'norm2_g': row(norm2_g[l]),
            'peer_wq_t': peer_wq[l].astype(BF16).T,
            'peer_keys': peer_keys[l].reshape(2 * PEER_HEADS, N_KEYS, PEER_DQ // 2).astype(BF16),
            'peer_u': peer_u[l].astype(BF16), 'peer_vt': peer_v[l].astype(BF16).T,
        }
        mod = _adaln(cc, ada_w[l], row(ada_b[l])).reshape(cc.shape[0], N_MOD, D_MODEL)
        yp, _, ret_fin, rwkv_fin = _trunk_path(x_prompt, mod, 0, 0, None, None, None, True, lw, final_g)
        ys, _, _, _ = _trunk_path(x_sample, mod, 1, 1, rope_tabs, state_ret[:, l], state_rwkv[:, l], False, lw,
                                  final_g)
    return (yp, ys, ret_fin[:, None], rwkv_fin[:, None])
```

```python
import functools

import jax
import jax.numpy as jnp
from jax import lax
from jax.experimental import pallas as pl
from jax.experimental.pallas import tpu as pltpu

F32 = jnp.float32
BF16 = jnp.bfloat16

D_MODEL = 1024
GRID_W = 64
N_MOD = 6
HEADS = 8
HEAD_DIM = 64
BRANCH_W = HEADS * HEAD_DIM
RET_CHUNK = 128
RWKV_CHUNK = 64
LORA_W = 64
LORA_A = 64
LORA_G = 128
PEER_HEADS = 8
N_KEYS = 128
N_EXPERTS = N_KEYS * N_KEYS
PEER_DQ = 256
PEER_TOPK = 16
ROPE_BASE = 10000.0
NORM_EPS = 1e-6
GN_EPS = 64e-5
COLS_MIX = 7 * BRANCH_W
COLS_LORA = LORA_W + LORA_A + LORA_G
COLS_GATE = 2 * D_MODEL
IN_COLS = COLS_MIX + COLS_LORA + COLS_GATE

V7X_VMEM_LIMIT_BYTES = 56 * 1024 * 1024
LANES = 128
ROW_TILE = 256
PEER_SCORE_TOKENS = 256
PEER_EXPERT_TOKENS = 512
PEER_ROWS_PER_STEP = 8
NEG_INF = float("-inf")


def _cparams(*sem):
    return pltpu.CompilerParams(dimension_semantics=sem, vmem_limit_bytes=V7X_VMEM_LIMIT_BYTES)


def _mm(a, b):
    return jnp.dot(a.astype(BF16), b.astype(BF16), preferred_element_type=F32)


def _mm_nt(a, b):
    return lax.dot_general(a.astype(BF16), b.astype(BF16), (((1,), (1,)), ((), ())),
                           preferred_element_type=F32)


def _mm_tn(a, b):
    return lax.dot_general(a.astype(BF16), b.astype(BF16), (((0,), (0,)), ((), ())),
                           preferred_element_type=F32)


def _mm_tn_3pass(a, b):
    a_hi, b_hi = a.astype(BF16), b.astype(BF16)
    a_lo = (a - a_hi.astype(F32)).astype(BF16)
    b_lo = (b - b_hi.astype(F32)).astype(BF16)
    dims = (((0,), (0,)), ((), ()))
    dot = lambda x, y: lax.dot_general(x, y, dims, preferred_element_type=F32)
    return dot(a_hi, b_hi) + (dot(a_hi, b_lo) + dot(a_lo, b_hi))


def _sigmoid(x):
    return 1.0 / (1.0 + jnp.exp(-x))


def _rms(x, g):
    return x * lax.rsqrt(jnp.mean(x * x, axis=-1, keepdims=True) + NORM_EPS) * g


def _head_sum(x):
    t, w = x.shape
    lo = lax.broadcasted_iota(jnp.int32, (t, LANES), 1) < HEAD_DIM
    outs = []
    for j in range(w // LANES):
        xt = x[:, j * LANES:(j + 1) * LANES]
        s_lo = jnp.sum(jnp.where(lo, xt, 0.0), axis=-1, keepdims=True)
        s_hi = jnp.sum(jnp.where(lo, 0.0, xt), axis=-1, keepdims=True)
        outs.append(jnp.where(lo, s_lo, s_hi))
    return outs[0] if len(outs) == 1 else jnp.concatenate(outs, axis=-1)


def _head_norm(y, w, b):
    mu = _head_sum(y) * (1.0 / HEAD_DIM)
    d = y - mu
    var = _head_sum(d * d) * (1.0 / HEAD_DIM)
    return d * lax.rsqrt(var + GN_EPS) * w + b


def _mod_kernel(c_ref, w_ref, b_ref, o_ref):
    c = c_ref[...]
    o_ref[...] = _mm(c * _sigmoid(c), w_ref[...]) + b_ref[...]


def _adaln(cc, ada_w, ada_b):
    rows = cc.shape[0]
    n = ada_w.shape[1]
    tn = n // 4
    return pl.pallas_call(
        _mod_kernel,
        grid=(n // tn,),
        in_specs=[pl.BlockSpec((rows, D_MODEL), lambda j: (0, 0)),
                  pl.BlockSpec((D_MODEL, tn), lambda j: (0, j)),
                  pl.BlockSpec((1, tn), lambda j: (0, j))],
        out_specs=pl.BlockSpec((rows, tn), lambda j: (0, j)),
        out_shape=jax.ShapeDtypeStruct((rows, n), F32),
        compiler_params=_cparams("parallel"),
        name="adaln_mod",
    )(cc, ada_w, ada_b)


def _in_kernel(x_ref, mod_ref, g_ref, w_ref, mix_ref, lora_ref, gate_ref):
    m = mod_ref[0]
    h = (_rms(x_ref[...], g_ref[...]) * (1.0 + m[1:2]) + m[0:1]).astype(BF16)
    mix_ref[...] = jnp.dot(h, w_ref[:, 0:COLS_MIX], preferred_element_type=F32)
    lora_ref[...] = jnp.dot(h, w_ref[:, COLS_MIX:COLS_MIX + COLS_LORA], preferred_element_type=F32)
    gate_ref[...] = jnp.dot(h, w_ref[:, COLS_MIX + COLS_LORA:IN_COLS], preferred_element_type=F32)


def _in_proj(x2, mod, mod_base, mod_step, seq, norm_g, w_in_bf):
    n = x2.shape[0]
    tm = ROW_TILE
    mod_idx = lambda i: (mod_base + mod_step * ((i * tm) // seq), 0, 0)
    return pl.pallas_call(
        _in_kernel,
        grid=(n // tm,),
        in_specs=[pl.BlockSpec((tm, D_MODEL), lambda i: (i, 0)),
                  pl.BlockSpec((1, N_MOD, D_MODEL), mod_idx),
                  pl.BlockSpec((1, D_MODEL), lambda i: (0, 0)),
                  pl.BlockSpec((D_MODEL, IN_COLS), lambda i: (0, 0))],
        out_specs=[pl.BlockSpec((tm, COLS_MIX), lambda i: (i, 0)),
                   pl.BlockSpec((tm, COLS_LORA), lambda i: (i, 0)),
                   pl.BlockSpec((tm, COLS_GATE), lambda i: (i, 0))],
        out_shape=[jax.ShapeDtypeStruct((n, COLS_MIX), F32),
                   jax.ShapeDtypeStruct((n, COLS_LORA), F32),
                   jax.ShapeDtypeStruct((n, COLS_GATE), F32)],
        compiler_params=_cparams("parallel"),
        name="in_proj",
    )(x2, mod, norm_g, w_in_bf)


def _rope(x, cos, sin):
    lane = lax.broadcasted_iota(jnp.int32, x.shape, 1)
    first = (lane % 32) < 16
    partner = jnp.where(first, pltpu.roll(x, LANES - 16, 1), pltpu.roll(x, 16, 1))
    return x * cos + partner * sin


def _ret_kernel(*refs, seq, rope, has_init, want_fin):
    it = iter(refs)
    q_ref, k_ref, v_ref, g_ref, rd_ref, gw_ref, gb_ref = (next(it) for _ in range(7))
    cos_ref = next(it) if rope else None
    sin_ref = next(it) if rope else None
    s0_ref = next(it) if has_init else None
    y_ref = next(it)
    fin_ref = next(it) if want_fin else None
    qs_ref, ks_ref, vec_ref, dec_ref, r_ref = (next(it) for _ in range(5))

    C = RET_CHUNK
    W = BRANCH_W
    n = seq // C
    rd = rd_ref[...]
    lg = jnp.minimum(rd, 0.0) - jnp.log(1.0 + jnp.exp(-jnp.abs(rd)))
    lgf, lgb = lg[0:1, :], lg[1:2, :]
    ii = lax.broadcasted_iota(jnp.int32, (C, C), 0)
    jj = lax.broadcasted_iota(jnp.int32, (C, C), 1)
    diff = (ii - jj).astype(F32)
    col = lax.broadcasted_iota(jnp.int32, (C, W), 0).astype(F32)
    ln = lambda h: slice(h * HEAD_DIM, (h + 1) * HEAD_DIM)

    vec_ref[0] = jnp.exp((col + 1.0) * lgf)
    vec_ref[1] = jnp.exp((C - 1.0 - col) * lgf)
    vec_ref[2] = jnp.exp((C - col) * lgb)
    vec_ref[3] = jnp.exp(col * lgb)
    chunk_f = jnp.exp(C * lgf)
    chunk_b = jnp.exp(C * lgb)
    for h in range(HEADS):
        gf = lgf[:, h * HEAD_DIM:h * HEAD_DIM + 1]
        gb = lgb[:, h * HEAD_DIM:h * HEAD_DIM + 1]
        dec_ref[h] = (jnp.where(diff >= 0, jnp.exp(jnp.maximum(diff, 0.0) * gf), 0.0)
                      + jnp.where(diff <= 0, jnp.exp(jnp.maximum(-diff, 0.0) * gb), 0.0))
        for d in range(2):
            r_ref[d, h] = s0_ref[0, d, h] if has_init else jnp.zeros((HEAD_DIM, HEAD_DIM), F32)

    def rows(c):
        return pl.ds(pl.multiple_of(c * C, C), C)

    def prep(c, carry):
        r = rows(c)
        q = q_ref[0, r, :]
        k = k_ref[0, r, :] * (HEAD_DIM ** -0.5)
        if rope:
            cos, sin = cos_ref[r, :], sin_ref[r, :]
            tiles = lambda x: [x[:, j * LANES:(j + 1) * LANES] for j in range(W // LANES)]
            q = jnp.concatenate([_rope(t, cos, sin) for t in tiles(q)], axis=1)
            k = jnp.concatenate([_rope(t, cos, sin) for t in tiles(k)], axis=1)
        qs_ref[r, :] = q
        ks_ref[r, :] = k
        y_ref[0, r, :] = jnp.zeros((C, W), F32)
        return carry

    lax.fori_loop(0, n, prep, 0)

    cpi = 2 if n % 4 == 0 else 1

    def step(i, carry):
        rf = [rows(i * cpi + u) for u in range(cpi)]
        rb = [rows(n - 1 - (i * cpi + u)) for u in range(cpi)]
        intra, qfx, inc_f, qbx, inc_b = ([[None] * HEADS for _ in range(cpi)] for _ in range(5))
        for u in range(cpi):
            qf, kf, vf = qs_ref[rf[u], :], ks_ref[rf[u], :], v_ref[0, rf[u], :]
            qb, kb, vb = qs_ref[rb[u], :], ks_ref[rb[u], :], v_ref[0, rb[u], :]
            qfs, kfs = qf * vec_ref[0], kf * vec_ref[1]
            qbs, kbs = qb * vec_ref[2], kb * vec_ref[3]
            for h in range(HEADS):
                sc = _mm_nt(qf[:, ln(h)], kf[:, ln(h)]) * dec_ref[h]
                intra[u][h] = _mm(sc, vf[:, ln(h)])
                qfx[u][h], qbx[u][h] = qfs[:, ln(h)], qbs[:, ln(h)]
                inc_f[u][h] = _mm_tn(kfs[:, ln(h)], vf[:, ln(h)])
                inc_b[u][h] = _mm_tn(kbs[:, ln(h)], vb[:, ln(h)])
        sf = [r_ref[0, h] for h in range(HEADS)]
        sb = [r_ref[1, h] for h in range(HEADS)]
        of, ob = ([[None] * HEADS for _ in range(cpi)] for _ in range(2))
        for u in range(cpi):
            for h in range(HEADS):
                of[u][h] = intra[u][h] + _mm(qfx[u][h], sf[h])
                sf[h] = sf[h] * chunk_f[:, ln(h)] + inc_f[u][h]
                ob[u][h] = _mm(qbx[u][h], sb[h])
                sb[h] = sb[h] * chunk_b[:, ln(h)] + inc_b[u][h]
        yf = [y_ref[0, rf[u], :] for u in range(cpi)]
        yb = [y_ref[0, rb[u], :] for u in range(cpi)]
        for u in range(cpi):
            y_ref[0, rf[u], :] = yf[u] + jnp.concatenate(of[u], axis=1)
            y_ref[0, rb[u], :] = yb[u] + jnp.concatenate(ob[u], axis=1)
        for h in range(HEADS):
            r_ref[0, h] = sf[h]
            r_ref[1, h] = sb[h]
        return carry

    lax.fori_loop(0, n // cpi, step, 0)
    if want_fin:
        for d in range(2):
            for h in range(HEADS):
                fin_ref[0, d, h] = r_ref[d, h]

    def post(c, carry):
        r = rows(c)
        g = g_ref[0, r, :]
        y_ref[0, r, :] = _head_norm(y_ref[0, r, :], gw_ref[...], gb_ref[...]) * (g * _sigmoid(g))
        return carry

    lax.fori_loop(0, n, post, 0)


def _retention(mix3, rd_l, gn_w, gn_b, rope_tabs, s0, want_fin):
    b, seq, _ = mix3.shape
    W = BRANCH_W
    rope = rope_tabs is not None
    has_init = s0 is not None
    col_spec = lambda j: pl.BlockSpec((1, seq, W), lambda i: (i, 0, j))
    full = lambda a: pl.BlockSpec(a.shape, lambda i: (0,) * a.ndim)
    in_specs = [col_spec(0), col_spec(1), col_spec(2), col_spec(3), full(rd_l), full(gn_w), full(gn_b)]
    args = [mix3, mix3, mix3, mix3, rd_l, gn_w, gn_b]
    if rope:
        in_specs += [full(rope_tabs[0]), full(rope_tabs[1])]
        args += list(rope_tabs)
    state_spec = pl.BlockSpec((1, 2, HEADS, HEAD_DIM, HEAD_DIM), lambda i: (i, 0, 0, 0, 0))
    if has_init:
        in_specs.append(state_spec)
        args.append(s0)
    out_specs = [pl.BlockSpec((1, seq, W), lambda i: (i, 0, 0))]
    out_shape = [jax.ShapeDtypeStruct((b, seq, W), F32)]
    if want_fin:
        out_specs.append(state_spec)
        out_shape.append(jax.ShapeDtypeStruct((b, 2, HEADS, HEAD_DIM, HEAD_DIM), F32))
    res = pl.pallas_call(
        functools.partial(_ret_kernel, seq=seq, rope=rope, has_init=has_init, want_fin=want_fin),
        grid=(b,),
        in_specs=in_specs,
        out_specs=out_specs,
        out_shape=out_shape,
        scratch_shapes=[pltpu.VMEM((seq, W), F32), pltpu.VMEM((seq, W), F32),
                        pltpu.VMEM((4, RET_CHUNK, W), F32), pltpu.VMEM((HEADS, RET_CHUNK, RET_CHUNK), F32),
                        pltpu.VMEM((2, HEADS, HEAD_DIM, HEAD_DIM), F32)],
        compiler_params=_cparams("parallel"),
        name="retention",
    )(*args)
    return (res[0], res[1]) if want_fin else (res[0], None)


EXP_M_HALF = 0.6065306597126334


def _rwkv_kernel(*refs, seq, nb, has_init, want_fin):
    it = iter(refs)
    (r_ref, k_ref, v_ref, lora_ref, cw_ref, w0_ref, w2_ref, a0_ref, a2_ref, g2_ref,
     kk_w_ref, ka_ref, rk_ref, gw_ref, gb_ref) = (next(it) for _ in range(15))
    s0_ref = next(it) if has_init else None
    y_ref = next(it)
    fin_ref = next(it) if want_fin else None
    r_s, v_s, kk_s, g_s, bv_s, lw_s, b_s, kd_s, st_s = (next(it) for _ in range(9))

    C = RWKV_CHUNK
    n = seq // C
    W = BRANCH_W
    row_id = lax.broadcasted_iota(jnp.int32, (C, W), 0)

    def rows(c):
        return pl.ds(pl.multiple_of(c * C, C), C)

    def conv(ref, s, c, w):
        x = ref[s, rows(c), :]
        prev8 = ref[s, pl.ds(pl.multiple_of(jnp.maximum(c * C - 8, 0), 8), 8), :]
        next8 = ref[s, pl.ds(pl.multiple_of(jnp.minimum(c * C + C, seq - 8), 8), 8), :]
        prev_row = jnp.where(c > 0, prev8[7:8, :], 0.0)
        next_row = jnp.where(c < n - 1, next8[0:1, :], 0.0)
        xm = jnp.where(row_id == 0, prev_row, pltpu.roll(x, 1, 0))
        xp = jnp.where(row_id == C - 1, next_row, pltpu.roll(x, C - 1, 0))
        return w[0:1] * xm + w[1:2] * x + w[2:3] * xp

    def prep_seq(s, c):
        rws = rows(c)
        cw = cw_ref[...]
        r = conv(r_ref, s, c, cw[:, 0:W])
        k = conv(k_ref, s, c, cw[:, W:2 * W])
        v = conv(v_ref, s, c, cw[:, 2 * W:3 * W])
        lo = lora_ref[s, rws, :]
        dw = lo[:, 0:LORA_W]
        da = lo[:, LORA_W:LORA_W + LORA_A]
        dg = lo[:, LORA_W + LORA_A:]
        kk = k * kk_w_ref[...]
        kk = kk * lax.rsqrt(_head_sum(kk * kk) + 1e-12)
        y_ref[s, rws, :] = jnp.zeros((C, W), F32)
        r_s[s, rws, :] = r
        v_s[s, rws, :] = v
        kk_s[s, rws, :] = kk
        g_s[s, rws, :] = _mm(_sigmoid(dg), g2_ref[...])
        bv_s[s, rws, :] = _head_sum(r * k * rk_ref[...]) * v
        tdw = jnp.tanh(dw)
        kka = k * ka_ref[...]
        for d in range(2):
            z = w0_ref[d:d + 1, :] + _mm(tdw, w2_ref[d])
            a = _sigmoid(a0_ref[d:d + 1, :] + _mm(da, a2_ref[d]))
            lw_s[s, d, rws, :] = -EXP_M_HALF * _sigmoid(z)
            b_s[s, d, rws, :] = kk * a
            kd_s[s, d, rws, :] = k + kka * (a - 1.0)

    def prep(c, carry):
        for s in range(nb):
            prep_seq(s, c)
        return carry

    lax.fori_loop(0, n, prep, 0)

    ii = lax.broadcasted_iota(jnp.int32, (C, C), 0)
    jj = lax.broadcasted_iota(jnp.int32, (C, C), 1)
    incl = [ii >= jj, ii <= jj]
    strict = [ii > jj, ii < jj]

    cpi = 2 if (nb == 1 and n % 4 == 0) else 1
    streams = [(s, d, u) for s in range(nb) for d in range(2) for u in range(cpi)]
    for s in range(nb):
        for d in range(2):
            for h in range(HEADS):
                st_s[s, d, h] = s0_ref[s, d, h] if has_init else jnp.zeros((HEAD_DIM, HEAD_DIM), F32)

    def chunk(i, carry):
        rws, at, bt, kt, rt, bh, kh, etot, vc = ([None] * len(streams) for _ in range(9))
        for q, (s, d, u) in enumerate(streams):
            rws[q] = rows(i * cpi + u if d == 0 else n - 1 - (i * cpi + u))
            rc, vc[q], kkc = r_s[s, rws[q], :], v_s[s, rws[q], :], kk_s[s, rws[q], :]
            lwc, bc, kc = lw_s[s, d, rws[q], :], b_s[s, d, rws[q], :], kd_s[s, d, rws[q], :]
            cum = lwc
            for k in (1 << p for p in range((C - 1).bit_length())):
                if d == 0:
                    cum = cum + jnp.where(row_id >= k, pltpu.roll(cum, k, 0), 0.0)
                else:
                    cum = cum + jnp.where(row_id < C - k, pltpu.roll(cum, C - k, 0), 0.0)
            tot = cum[C - 1:C, :] if d == 0 else cum[0:1, :]
            pinv = jnp.exp(-cum)
            pend = jnp.exp(tot - cum)
            at[q] = -kkc * jnp.exp(cum - lwc)
            bt[q] = bc * pinv
            kt[q] = kc * pinv
            rt[q] = rc * jnp.exp(cum)
            bh[q] = bc * pend
            kh[q] = kc * pend
            etot[q] = jnp.exp(tot)
        chains = [(q, h) for q in range(len(streams)) for h in range(HEADS)]
        dirn = lambda q: streams[q][1]
        ln = lambda h: slice(h * HEAD_DIM, (h + 1) * HEAD_DIM)
        wk = [jnp.concatenate([bt[q][:, ln(h)], kt[q][:, ln(h)]], axis=0) for q, h in chains]
        ma = [_mm_nt(at[q][:, ln(h)], wk[j]) for j, (q, h) in enumerate(chains)]
        mr = [_mm_nt(rt[q][:, ln(h)], wk[j]) for j, (q, h) in enumerate(chains)]
        a_ak = [jnp.where(strict[dirn(q)], ma[j][:, C:2 * C], 0.0) for j, (q, h) in enumerate(chains)]
        p = [jnp.where(strict[dirn(q)], ma[j][:, 0:C], 0.0) for j, (q, h) in enumerate(chains)]
        a_rb = [jnp.where(incl[dirn(q)], mr[j][:, 0:C], 0.0) for j, (q, h) in enumerate(chains)]
        a_rk = [jnp.where(incl[dirn(q)], mr[j][:, C:2 * C], 0.0) for j, (q, h) in enumerate(chains)]
        vh = [vc[q][:, ln(h)] for q, h in chains]
        x = [jnp.concatenate([at[q][:, ln(h)], _mm(a_ak[j], vh[j])], axis=1)
             for j, (q, h) in enumerate(chains)]
        for step in range(6):
            pb = [p[j].astype(BF16) for j in range(len(chains))]
            xh = [x[j].astype(BF16) for j in range(len(chains))]
            if step < 3:
                xl = [(x[j] - xh[j].astype(F32)).astype(BF16) for j in range(len(chains))]
                px = [jnp.dot(pb[j], jnp.concatenate([xh[j], xl[j]], axis=1), preferred_element_type=F32)
                      for j in range(len(chains))]
                x = [x[j] + (px[j][:, 0:2 * HEAD_DIM] + px[j][:, 2 * HEAD_DIM:]) for j in range(len(chains))]
            else:
                x = [x[j] + jnp.dot(pb[j], xh[j], preferred_element_type=F32) for j in range(len(chains))]
            if step < 5:
                p = [jnp.dot(pb[j], pb[j], preferred_element_type=F32) for j in range(len(chains))]
        gh = [_mm_tn(x[j], bh[q][:, ln(h)]) for j, (q, h) in enumerate(chains)]
        vk = [_mm_tn_3pass(vh[j], kh[q][:, ln(h)]) for j, (q, h) in enumerate(chains)]
        ax = [_mm(a_rb[j], x[j]) for j in range(len(chains))]
        qt = [rt[q][:, ln(h)] + ax[j][:, 0:HEAD_DIM] for j, (q, h) in enumerate(chains)]
        y0 = [ax[j][:, HEAD_DIM:2 * HEAD_DIM] + _mm(a_rk[j], vh[j]) for j in range(len(chains))]
        ys = [None] * len(chains)
        scans = [(s, d, h) for s in range(nb) for d in range(2) for h in range(HEADS)]
        st = {k: st_s[k] for k in scans}
        for u in range(cpi):
            for s, d, h in scans:
                q = streams.index((s, d, u))
                j = q * HEADS + h
                cur = st[s, d, h]
                ys[j] = _mm_nt(qt[j], cur) + y0[j]
                st[s, d, h] = (cur * etot[q][:, ln(h)] + _mm(cur, gh[j][0:HEAD_DIM, :])
                               + gh[j][HEAD_DIM:2 * HEAD_DIM, :] + vk[j])
        for k in scans:
            st_s[k] = st[k]
        for q, (s, d, u) in enumerate(streams):
            y = jnp.concatenate(ys[q * HEADS:(q + 1) * HEADS], axis=1)
            y_ref[s, rws[q], :] = y_ref[s, rws[q], :] + y
        return carry

    lax.fori_loop(0, n // cpi, chunk, 0)
    if want_fin:
        for s in range(nb):
            for d in range(2):
                for h in range(HEADS):
                    fin_ref[s, d, h] = st_s[s, d, h]

    def post(c, carry):
        rws = rows(c)
        for s in range(nb):
            y = _head_norm(y_ref[s, rws, :], gw_ref[...], gb_ref[...])
            y_ref[s, rws, :] = (y + bv_s[s, rws, :]) * g_s[s, rws, :]
        return carry

    lax.fori_loop(0, n, post, 0)


def _rwkv(mix3, lora3, lw, s0, want_fin):
    b, seq, _ = mix3.shape
    has_init = s0 is not None
    W = BRANCH_W
    nb = 2 if (seq <= 256 and b % 2 == 0) else 1
    col_spec = lambda j: pl.BlockSpec((nb, seq, W), lambda i: (i, 0, j))
    full = lambda a: pl.BlockSpec(a.shape, lambda i: (0,) * a.ndim)
    weights = [lw['conv'], lw['w0'], lw['w2'], lw['a0'], lw['a2'], lw['g2'],
               lw['k_k'], lw['k_a'], lw['r_k'], lw['gn_w'], lw['gn_b']]
    in_specs = [col_spec(4), col_spec(5), col_spec(6),
                pl.BlockSpec((nb, seq, COLS_LORA), lambda i: (i, 0, 0))] + [full(a) for a in weights]
    args = [mix3, mix3, mix3, lora3] + weights
    state_spec = pl.BlockSpec((nb, 2, HEADS, HEAD_DIM, HEAD_DIM), lambda i: (i, 0, 0, 0, 0))
    if has_init:
        in_specs.append(state_spec)
        args.append(s0)
    out_specs = [pl.BlockSpec((nb, seq, W), lambda i: (i, 0, 0))]
    out_shape = [jax.ShapeDtypeStruct((b, seq, W), F32)]
    if want_fin:
        out_specs.append(state_spec)
        out_shape.append(jax.ShapeDtypeStruct((b, 2, HEADS, HEAD_DIM, HEAD_DIM), F32))
    sw = pltpu.VMEM((nb, seq, W), F32)
    sw2 = pltpu.VMEM((nb, 2, seq, W), F32)
    res = pl.pallas_call(
        functools.partial(_rwkv_kernel, seq=seq, nb=nb, has_init=has_init, want_fin=want_fin),
        grid=(b // nb,),
        in_specs=in_specs,
        out_specs=out_specs,
        out_shape=out_shape,
        scratch_shapes=[sw, sw, sw, sw, sw, sw2, sw2, sw2,
                        pltpu.VMEM((nb, 2, HEADS, HEAD_DIM, HEAD_DIM), F32)],
        compiler_params=_cparams("parallel"),
        name="rwkv7",
    )(*args)
    return (res[0], res[1]) if want_fin else (res[0], None)


def _merge_kernel(ya_ref, yb_ref, gate_ref, x_ref, mod_ref, wa_ref, wb_ref, wo_ref, n2_ref, x1_ref, h2_ref):
    m = mod_ref[0]
    br_a = jnp.dot(ya_ref[...].astype(BF16), wa_ref[...], preferred_element_type=F32)
    br_b = jnp.dot(yb_ref[...].astype(BF16), wb_ref[...], preferred_element_type=F32)
    gate = gate_ref[...]
    merged = _sigmoid(gate[:, 0:D_MODEL]) * br_a + _sigmoid(gate[:, D_MODEL:]) * br_b
    mix = jnp.dot(merged.astype(BF16), wo_ref[...], preferred_element_type=F32)
    x1 = x_ref[...] + m[2:3] * mix
    x1_ref[...] = x1
    h2_ref[...] = (_rms(x1, n2_ref[...]) * (1.0 + m[4:5]) + m[3:4]).astype(BF16)


def _merge(ya2, yb2, gate2, x2, mod, mod_base, mod_step, seq, wa, wb, wo, norm2_g):
    n = x2.shape[0]
    tm = ROW_TILE
    mod_idx = lambda i: (mod_base + mod_step * ((i * tm) // seq), 0, 0)
    row = lambda w: pl.BlockSpec((tm, w), lambda i: (i, 0))
    full = lambda a: pl.BlockSpec(a.shape, lambda i: (0,) * a.ndim)
    return pl.pallas_call(
        _merge_kernel,
        grid=(n // tm,),
        in_specs=[row(BRANCH_W), row(BRANCH_W), row(COLS_GATE), row(D_MODEL),
                  pl.BlockSpec((1, N_MOD, D_MODEL), mod_idx),
                  full(wa), full(wb), full(wo), full(norm2_g)],
        out_specs=[row(D_MODEL), row(D_MODEL)],
        out_shape=[jax.ShapeDtypeStruct((n, D_MODEL), F32), jax.ShapeDtypeStruct((n, D_MODEL), BF16)],
        compiler_params=_cparams("parallel"),
        name="merge_out",
    )(ya2, yb2, gate2, x2, mod, wa, wb, wo, norm2_g)


def _bitonic_pairs(n):
    pairs = []
    k = 2
    while k <= n:
        j = k // 2
        while j >= 1:
            for i in range(n):
                l = i ^ j
                if l > i:
                    pairs.append((i, l) if (i & k) == 0 else (l, i))
            j //= 2
        k *= 2
    return pairs


def _top_values_of_keys(s, k):
    r, t = s.shape
    nl = r // 8
    lv = [s[8 * j:8 * j + 8, :] for j in range(nl)]
    for a, b in _bitonic_pairs(nl):
        lv[a], lv[b] = jnp.maximum(lv[a], lv[b]), jnp.minimum(lv[a], lv[b])
    sid = lax.broadcasted_iota(jnp.int32, (8, t), 0)
    out = []
    for it in range(k):
        m = jnp.max(lv[0], axis=0, keepdims=True)
        out.append(m)
        first = jnp.min(jnp.where(lv[0] == m, sid, 8), axis=0, keepdims=True)
        pop = sid == first
        for j in range(min(nl, k - 1 - it)):
            lv[j] = jnp.where(pop, lv[j + 1] if j + 1 < nl else NEG_INF, lv[j])
    return out


def _peer_score_kernel(h_ref, wq_ref, keys_ref, thr_ref, g1_ref, s2_ref, e2_ref, cand_ref):
    qt = _mm_nt(wq_ref[...], h_ref[...])
    half = PEER_DQ // 2
    k1 = PEER_TOPK + 1
    pairs = [(i, j) for i in range(k1) for j in range(k1) if (i + 1) * (j + 1) <= k1]
    assert len(pairs) <= cand_ref.shape[0]
    for h in range(PEER_HEADS):
        s = [_mm(keys_ref[2 * h + c], qt[(2 * h + c) * half:(2 * h + c + 1) * half, :]) for c in range(2)]
        tops = [_top_values_of_keys(s[c], k1) for c in range(2)]
        cand_ref[...] = jnp.full(cand_ref.shape, NEG_INF, F32)
        for r, (i, j) in enumerate(pairs):
            cand_ref[r:r + 1, :] = tops[0][i] + tops[1][j]
        best = _top_values_of_keys(cand_ref[...], k1)
        mx = best[0]
        z = jnp.exp(best[0] - mx)
        for r in range(1, PEER_TOPK):
            z = z + jnp.exp(best[r] - mx)
        theta = 0.5 * (best[PEER_TOPK - 1] + best[PEER_TOPK])
        thr_ref[h] = theta - s[0]
        g1_ref[h] = jnp.exp(s[0] - tops[0][0]) * (0.5 / z)
        s2_ref[h] = s[1]
        e2_ref[h] = jnp.exp(s[1] - tops[1][0])


def _peer_scores(h2, wq_t, keys):
    n = h2.shape[0]
    tb = PEER_SCORE_TOKENS
    full = lambda a: pl.BlockSpec(a.shape, lambda i: (0,) * a.ndim)
    return pl.pallas_call(
        _peer_score_kernel,
        grid=(n // tb,),
        in_specs=[pl.BlockSpec((tb, D_MODEL), lambda i: (i, 0)), full(wq_t), full(keys)],
        out_specs=[pl.BlockSpec((PEER_HEADS, N_KEYS, tb), lambda i: (0, 0, i))] * 4,
        out_shape=[jax.ShapeDtypeStruct((PEER_HEADS, N_KEYS, n), F32)] * 4,
        scratch_shapes=[pltpu.VMEM((64, tb), F32)],
        compiler_params=_cparams("parallel"),
        name="peer_scores",
    )(h2, wq_t, keys)


def _gelu_tanh_x2(x):
    return x + x * jnp.tanh(x * (0.7978845608028654 + 0.035677408136300125 * (x * x)))


def _peer_expert_kernel(h_ref, u_ref, vt_ref, thr_ref, g1_ref, s2_ref, e2_ref, x1_ref, mod_ref, fg_ref, y_ref,
                        acc_ref, w_ref, *, rows_per_step):
    e = pl.program_id(1)
    tb = h_ref.shape[0]

    @pl.when(e == 0)
    def _():
        acc_ref[...] = jnp.zeros_like(acc_ref)

    st = _mm_nt(u_ref[...], h_ref[...])
    rows_per_chunk = 2
    partial = None
    for al in range(rows_per_step):
        a = e * rows_per_step + al
        r0 = al * N_KEYS
        thr = [thr_ref[h, pl.ds(a, 1), :] for h in range(PEER_HEADS)]
        g1 = [g1_ref[h, pl.ds(a, 1), :] for h in range(PEER_HEADS)]
        for lt in range(tb // LANES):
            ls = slice(lt * LANES, (lt + 1) * LANES)
            wsum = None
            for h in range(PEER_HEADS):
                w = jnp.where(s2_ref[h, :, ls] >= thr[h][:, ls], e2_ref[h, :, ls] * g1[h][:, ls], 0.0)
                wsum = w if wsum is None else wsum + w
            act = _gelu_tanh_x2(st[r0:r0 + N_KEYS, ls])
            w_ref[r0:r0 + N_KEYS, ls] = (wsum * act).astype(BF16)
        if (al + 1) % rows_per_chunk == 0:
            rs = slice((al + 1 - rows_per_chunk) * N_KEYS, (al + 1) * N_KEYS)
            d = jnp.dot(vt_ref[:, rs], w_ref[rs, :], preferred_element_type=F32)
            partial = d if partial is None else partial + d
    acc_ref[...] += partial

    @pl.when(e == pl.num_programs(1) - 1)
    def _():
        m = mod_ref[0]
        x2 = x1_ref[...] + m[5:6] * acc_ref[...].T
        y_ref[...] = _rms(x2, fg_ref[...])


def _peer_experts(h2, u_bf, vt_bf, stats, x1, mod, mod_base, mod_step, seq, final_g):
    n = h2.shape[0]
    tb = PEER_EXPERT_TOKENS if (mod_step == 0 or seq % PEER_EXPERT_TOKENS == 0) else seq
    rows_per_step = PEER_ROWS_PER_STEP
    ec = rows_per_step * N_KEYS
    mod_idx = lambda i, e: (mod_base + mod_step * ((i * tb) // seq), 0, 0)
    stat_spec = pl.BlockSpec((PEER_HEADS, N_KEYS, tb), lambda i, e: (0, 0, i))
    return pl.pallas_call(
        functools.partial(_peer_expert_kernel, rows_per_step=rows_per_step),
        grid=(n // tb, N_EXPERTS // ec),
        in_specs=[pl.BlockSpec((tb, D_MODEL), lambda i, e: (i, 0)),
                  pl.BlockSpec((ec, D_MODEL), lambda i, e: (e, 0)),
                  pl.BlockSpec((D_MODEL, ec), lambda i, e: (0, e)),
                  stat_spec, stat_spec, stat_spec, stat_spec,
                  pl.BlockSpec((tb, D_MODEL), lambda i, e: (i, 0)),
                  pl.BlockSpec((1, N_MOD, D_MODEL), mod_idx),
                  pl.BlockSpec((1, D_MODEL), lambda i, e: (0, 0))],
        out_specs=pl.BlockSpec((tb, D_MODEL), lambda i, e: (i, 0)),
        out_shape=jax.ShapeDtypeStruct((n, D_MODEL), F32),
        scratch_shapes=[pltpu.VMEM((D_MODEL, tb), F32), pltpu.VMEM((ec, tb), BF16)],
        compiler_params=_cparams("parallel", "arbitrary"),
        name="peer_experts",
    )(h2, u_bf, vt_bf, *stats, x1, mod, final_g)


def _rope_tables(seq):
    pos = jnp.arange(seq, dtype=jnp.int32)
    lane = jnp.arange(LANES, dtype=jnp.int32) % HEAD_DIM
    use_col = (lane // 32) == 1
    p = jnp.where(use_col[None, :], (pos % GRID_W)[:, None], (pos // GRID_W)[:, None]).astype(F32)
    inv = ROPE_BASE ** (-(lane % 16).astype(F32) / 16.0)
    ang = p * inv[None, :]
    first = (lane % 32) < 16
    return jnp.cos(ang), jnp.where(first[None, :], -jnp.sin(ang), jnp.sin(ang))


def _trunk_path(x, mod, mod_base, mod_step, rope_tabs, s_ret0, s_rwkv0, want_fin, lw, final_g):
    b, seq, _ = x.shape
    x2 = x.reshape(b * seq, D_MODEL)
    mix, lora, gate = _in_proj(x2, mod, mod_base, mod_step, seq, lw['norm1_g'], lw['w_in'])
    mix3 = mix.reshape(b, seq, COLS_MIX)
    lora3 = lora.reshape(b, seq, COLS_LORA)
    ya, ret_fin = _retention(mix3, lw['ret_decay'], lw['ret_gn_w'], lw['ret_gn_b'], rope_tabs, s_ret0, want_fin)
    yb, rwkv_fin = _rwkv(mix3, lora3, lw['rwkv'], s_rwkv0, want_fin)
    x1, h2 = _merge(ya.reshape(b * seq, BRANCH_W), yb.reshape(b * seq, BRANCH_W), gate, x2, mod, mod_base,
                    mod_step, seq, lw['w_br_a'], lw['w_br_b'], lw['w_out'], lw['norm2_g'])
    stats = _peer_scores(h2, lw['peer_wq_t'], lw['peer_keys'])
    y = _peer_experts(h2, lw['peer_u'], lw['peer_vt'], stats, x1, mod, mod_base, mod_step, seq, final_g)
    return y.reshape(b, seq, D_MODEL), ret_fin, rwkv_fin


def kernel(x_prompt, x_sample, state_ret, state_rwkv, c, c_ctx, ada_w, ada_b, norm1_g, w_in, ret_decay, ret_gn_w, ret_gn_b, rwkv_conv, rwkv_w0, rwkv_w2, rwkv_a0, rwkv_a2, rwkv_g2, rwkv_k_k, rwkv_k_a, rwkv_r_k, rwkv_gn_w, rwkv_gn_b, w_br_a, w_br_b, w_out, norm2_g, peer_wq, peer_keys, peer_u, peer_v, final_norm_g):
    assert w_in.shape[0] == 1, "the final norm is fused into the layer's last kernel: single trunk layer only"
    row = lambda a: a.reshape(1, -1)
    cc = jnp.concatenate([c_ctx[None, :], c], axis=0)
    cc = jnp.pad(cc, ((0, (-cc.shape[0]) % 8), (0, 0)))
    rope_tabs = _rope_tables(x_sample.shape[1])
    final_g = row(final_norm_g)

    l = 0
    lw = {
        'norm1_g': row(norm1_g[l]), 'w_in': w_in[l].astype(BF16),
        'ret_decay': jnp.repeat(ret_decay[l], HEAD_DIM, axis=1),
        'ret_gn_w': row(ret_gn_w[l]), 'ret_gn_b': row(ret_gn_b[l]),
        'rwkv': {'conv': rwkv_conv[l], 'w0': rwkv_w0[l], 'w2': rwkv_w2[l].astype(BF16), 'a0': rwkv_a0[l],
                 'a2': rwkv_a2[l].astype(BF16), 'g2': rwkv_g2[l].astype(BF16), 'k_k': row(rwkv_k_k[l]),
                 'k_a': row(rwkv_k_a[l]), 'r_k': row(rwkv_r_k[l]), 'gn_w': row(rwkv_gn_w[l]),
                 'gn_b': row(rwkv_gn_b[l])},
        'w_br_a': w_br_a[l].astype(BF16), 'w_br_b': w_br_b[l].astype(BF16), 'w_out': w_out[l].astype(BF16),
        'norm2_g': row(norm2_g[l]),
        'peer_wq_t': peer_wq[l].astype(BF16).T,
        'peer_keys': peer_keys[l].reshape(2 * PEER_HEADS, N_KEYS, PEER_DQ // 2).astype(BF16),
        'peer_u': peer_u[l].astype(BF16), 'peer_vt': peer_v[l].astype(BF16).T,
    }
    mod = _adaln(cc, ada_w[l], row(ada_b[l])).reshape(cc.shape[0], N_MOD, D_MODEL)
    yp, ret_fin, rwkv_fin = _trunk_path(x_prompt, mod, 0, 0, None, None, None, True, lw, final_g)
    ys, _, _ = _trunk_path(x_sample, mod, 1, 1, rope_tabs, state_ret[:, l], state_rwkv[:, l], False, lw, final_g)
    return (yp, ys, ret_fin[:, None], rwkv_fin[:, None])
```

```python
import functools

import jax
import jax.numpy as jnp
from jax import lax
from jax.experimental import pallas as pl
from jax.experimental.pallas import tpu as pltpu

F32 = jnp.float32
BF16 = jnp.bfloat16

D_MODEL = 1024
GRID_W = 64
N_MOD = 6
HEADS = 8
HEAD_DIM = 64
BRANCH_W = HEADS * HEAD_DIM
RET_CHUNK = 128
RWKV_CHUNK = 64
LORA_W = 64
LORA_A = 64
LORA_G = 128
PEER_HEADS = 8
N_KEYS = 128
N_EXPERTS = N_KEYS * N_KEYS
PEER_DQ = 256
PEER_TOPK = 16
ROPE_BASE = 10000.0
NORM_EPS = 1e-6
GN_EPS = 64e-5
COLS_MIX = 7 * BRANCH_W
COLS_LORA = LORA_W + LORA_A + LORA_G
COLS_GATE = 2 * D_MODEL
IN_COLS = COLS_MIX + COLS_LORA + COLS_GATE

V7X_VMEM_LIMIT_BYTES = 56 * 1024 * 1024
LANES = 128
ROW_TILE = 256
PEER_SCORE_TOKENS = 256
PEER_EXPERT_TOKENS = 512
PEER_ROWS_PER_STEP = 8
NEG_INF = float("-inf")


def _cparams(*sem):
    return pltpu.CompilerParams(dimension_semantics=sem, vmem_limit_bytes=V7X_VMEM_LIMIT_BYTES)


def _mm(a, b):
    return jnp.dot(a.astype(BF16), b.astype(BF16), preferred_element_type=F32)


def _mm_nt(a, b):
    return lax.dot_general(a.astype(BF16), b.astype(BF16), (((1,), (1,)), ((), ())),
                           preferred_element_type=F32)


def _mm_tn(a, b):
    return lax.dot_general(a.astype(BF16), b.astype(BF16), (((0,), (0,)), ((), ())),
                           preferred_element_type=F32)


def _mm_tn_3pass(a, b):
    a_hi, b_hi = a.astype(BF16), b.astype(BF16)
    a_lo = (a - a_hi.astype(F32)).astype(BF16)
    b_lo = (b - b_hi.astype(F32)).astype(BF16)
    dims = (((0,), (0,)), ((), ()))
    dot = lambda x, y: lax.dot_general(x, y, dims, preferred_element_type=F32)
    return dot(a_hi, b_hi) + (dot(a_hi, b_lo) + dot(a_lo, b_hi))


def _sigmoid(x):
    return 1.0 / (1.0 + jnp.exp(-x))


def _rms(x, g):
    return x * lax.rsqrt(jnp.mean(x * x, axis=-1, keepdims=True) + NORM_EPS) * g


def _head_sum(x):
    t, w = x.shape
    lo = lax.broadcasted_iota(jnp.int32, (t, LANES), 1) < HEAD_DIM
    outs = []
    for j in range(w // LANES):
        xt = x[:, j * LANES:(j + 1) * LANES]
        s_lo = jnp.sum(jnp.where(lo, xt, 0.0), axis=-1, keepdims=True)
        s_hi = jnp.sum(jnp.where(lo, 0.0, xt), axis=-1, keepdims=True)
        outs.append(jnp.where(lo, s_lo, s_hi))
    return outs[0] if len(outs) == 1 else jnp.concatenate(outs, axis=-1)


def _head_norm(y, w, b):
    mu = _head_sum(y) * (1.0 / HEAD_DIM)
    d = y - mu
    var = _head_sum(d * d) * (1.0 / HEAD_DIM)
    return d * lax.rsqrt(var + GN_EPS) * w + b


def _mod_kernel(c_ref, w_ref, b_ref, o_ref):
    c = c_ref[...]
    o_ref[...] = _mm(c * _sigmoid(c), w_ref[...]) + b_ref[...]


def _adaln(cc, ada_w, ada_b):
    rows = cc.shape[0]
    n = ada_w.shape[1]
    tn = n // 4
    return pl.pallas_call(
        _mod_kernel,
        grid=(n // tn,),
        in_specs=[pl.BlockSpec((rows, D_MODEL), lambda j: (0, 0)),
                  pl.BlockSpec((D_MODEL, tn), lambda j: (0, j)),
                  pl.BlockSpec((1, tn), lambda j: (0, j))],
        out_specs=pl.BlockSpec((rows, tn), lambda j: (0, j)),
        out_shape=jax.ShapeDtypeStruct((rows, n), F32),
        compiler_params=_cparams("parallel"),
        name="adaln_mod",
    )(cc, ada_w, ada_b)


def _in_kernel(x_ref, mod_ref, g_ref, w_ref, mix_ref, lora_ref, gate_ref):
    m = mod_ref[0]
    h = (_rms(x_ref[...], g_ref[...]) * (1.0 + m[1:2]) + m[0:1]).astype(BF16)
    mix_ref[...] = jnp.dot(h, w_ref[:, 0:COLS_MIX], preferred_element_type=F32)
    lora_ref[...] = jnp.dot(h, w_ref[:, COLS_MIX:COLS_MIX + COLS_LORA], preferred_element_type=F32)
    gate_ref[...] = jnp.dot(h, w_ref[:, COLS_MIX + COLS_LORA:IN_COLS], preferred_element_type=F32)


def _in_proj(x2, mod, mod_base, mod_step, seq, norm_g, w_in_bf):
    n = x2.shape[0]
    tm = ROW_TILE
    mod_idx = lambda i: (mod_base + mod_step * ((i * tm) // seq), 0, 0)
    return pl.pallas_call(
        _in_kernel,
        grid=(n // tm,),
        in_specs=[pl.BlockSpec((tm, D_MODEL), lambda i: (i, 0)),
                  pl.BlockSpec((1, N_MOD, D_MODEL), mod_idx),
                  pl.BlockSpec((1, D_MODEL), lambda i: (0, 0)),
                  pl.BlockSpec((D_MODEL, IN_COLS), lambda i: (0, 0))],
        out_specs=[pl.BlockSpec((tm, COLS_MIX), lambda i: (i, 0)),
                   pl.BlockSpec((tm, COLS_LORA), lambda i: (i, 0)),
                   pl.BlockSpec((tm, COLS_GATE), lambda i: (i, 0))],
        out_shape=[jax.ShapeDtypeStruct((n, COLS_MIX), F32),
                   jax.ShapeDtypeStruct((n, COLS_LORA), F32),
                   jax.ShapeDtypeStruct((n, COLS_GATE), F32)],
        compiler_params=_cparams("parallel"),
        name="in_proj",
    )(x2, mod, norm_g, w_in_bf)


def _rope(x, cos, sin):
    lane = lax.broadcasted_iota(jnp.int32, x.shape, 1)
    first = (lane % 32) < 16
    partner = jnp.where(first, pltpu.roll(x, LANES - 16, 1), pltpu.roll(x, 16, 1))
    return x * cos + partner * sin


def _ret_kernel(*refs, seq, rope, has_init, want_fin):
    it = iter(refs)
    q_ref, k_ref, v_ref, g_ref, rd_ref, gw_ref, gb_ref = (next(it) for _ in range(7))
    cos_ref = next(it) if rope else None
    sin_ref = next(it) if rope else None
    s0_ref = next(it) if has_init else None
    y_ref = next(it)
    fin_ref = next(it) if want_fin else None
    qs_ref, ks_ref, vec_ref, dec_ref, r_ref = (next(it) for _ in range(5))

    C = RET_CHUNK
    W = BRANCH_W
    n = seq // C
    rd = rd_ref[...]
    lg = jnp.minimum(rd, 0.0) - jnp.log(1.0 + jnp.exp(-jnp.abs(rd)))
    lgf, lgb = lg[0:1, :], lg[1:2, :]
    ii = lax.broadcasted_iota(jnp.int32, (C, C), 0)
    jj = lax.broadcasted_iota(jnp.int32, (C, C), 1)
    diff = (ii - jj).astype(F32)
    col = lax.broadcasted_iota(jnp.int32, (C, W), 0).astype(F32)
    ln = lambda h: slice(h * HEAD_DIM, (h + 1) * HEAD_DIM)

    vec_ref[0] = jnp.exp((col + 1.0) * lgf)
    vec_ref[1] = jnp.exp((C - 1.0 - col) * lgf)
    vec_ref[2] = jnp.exp((C - col) * lgb)
    vec_ref[3] = jnp.exp(col * lgb)
    chunk_f = jnp.exp(C * lgf)
    chunk_b = jnp.exp(C * lgb)
    for h in range(HEADS):
        gf = lgf[:, h * HEAD_DIM:h * HEAD_DIM + 1]
        gb = lgb[:, h * HEAD_DIM:h * HEAD_DIM + 1]
        dec_ref[h] = (jnp.where(diff >= 0, jnp.exp(jnp.maximum(diff, 0.0) * gf), 0.0)
                      + jnp.where(diff <= 0, jnp.exp(jnp.maximum(-diff, 0.0) * gb), 0.0))
        for d in range(2):
            r_ref[d, h] = s0_ref[0, d, h] if has_init else jnp.zeros((HEAD_DIM, HEAD_DIM), F32)

    def rows(c):
        return pl.ds(pl.multiple_of(c * C, C), C)

    def prep(c, carry):
        r = rows(c)
        q = q_ref[0, r, :]
        k = k_ref[0, r, :] * (HEAD_DIM ** -0.5)
        if rope:
            cos, sin = cos_ref[r, :], sin_ref[r, :]
            tiles = lambda x: [x[:, j * LANES:(j + 1) * LANES] for j in range(W // LANES)]
            q = jnp.concatenate([_rope(t, cos, sin) for t in tiles(q)], axis=1)
            k = jnp.concatenate([_rope(t, cos, sin) for t in tiles(k)], axis=1)
        qs_ref[r, :] = q
        ks_ref[r, :] = k
        y_ref[0, r, :] = jnp.zeros((C, W), F32)
        return carry

    lax.fori_loop(0, n, prep, 0)

    cpi = 2 if n % 4 == 0 else 1

    def step(i, carry):
        rf = [rows(i * cpi + u) for u in range(cpi)]
        rb = [rows(n - 1 - (i * cpi + u)) for u in range(cpi)]
        intra, qfx, inc_f, qbx, inc_b = ([[None] * HEADS for _ in range(cpi)] for _ in range(5))
        for u in range(cpi):
            qf, kf, vf = qs_ref[rf[u], :], ks_ref[rf[u], :], v_ref[0, rf[u], :]
            qb, kb, vb = qs_ref[rb[u], :], ks_ref[rb[u], :], v_ref[0, rb[u], :]
            qfs, kfs = qf * vec_ref[0], kf * vec_ref[1]
            qbs, kbs = qb * vec_ref[2], kb * vec_ref[3]
            for h in range(HEADS):
                sc = _mm_nt(qf[:, ln(h)], kf[:, ln(h)]) * dec_ref[h]
                intra[u][h] = _mm(sc, vf[:, ln(h)])
                qfx[u][h], qbx[u][h] = qfs[:, ln(h)], qbs[:, ln(h)]
                inc_f[u][h] = _mm_tn(kfs[:, ln(h)], vf[:, ln(h)])
                inc_b[u][h] = _mm_tn(kbs[:, ln(h)], vb[:, ln(h)])
        sf = [r_ref[0, h] for h in range(HEADS)]
        sb = [r_ref[1, h] for h in range(HEADS)]
        of, ob = ([[None] * HEADS for _ in range(cpi)] for _ in range(2))
        for u in range(cpi):
            for h in range(HEADS):
                of[u][h] = intra[u][h] + _mm(qfx[u][h], sf[h])
                sf[h] = sf[h] * chunk_f[:, ln(h)] + inc_f[u][h]
                ob[u][h] = _mm(qbx[u][h], sb[h])
                sb[h] = sb[h] * chunk_b[:, ln(h)] + inc_b[u][h]
        yf = [y_ref[0, rf[u], :] for u in range(cpi)]
        yb = [y_ref[0, rb[u], :] for u in range(cpi)]
        for u in range(cpi):
            y_ref[0, rf[u], :] = yf[u] + jnp.concatenate(of[u], axis=1)
            y_ref[0, rb[u], :] = yb[u] + jnp.concatenate(ob[u], axis=1)
        for h in range(HEADS):
            r_ref[0, h] = sf[h]
            r_ref[1, h] = sb[h]
        return carry

    lax.fori_loop(0, n // cpi, step, 0)
    if want_fin:
        for d in range(2):
            for h in range(HEADS):
                fin_ref[0, d, h] = r_ref[d, h]

    def post(c, carry):
        r = rows(c)
        g = g_ref[0, r, :]
        y_ref[0, r, :] = _head_norm(y_ref[0, r, :], gw_ref[...], gb_ref[...]) * (g * _sigmoid(g))
        return carry

    lax.fori_loop(0, n, post, 0)


def _retention(mix3, rd_l, gn_w, gn_b, rope_tabs, s0, want_fin):
    b, seq, _ = mix3.shape
    W = BRANCH_W
    rope = rope_tabs is not None
    has_init = s0 is not None
    col_spec = lambda j: pl.BlockSpec((1, seq, W), lambda i: (i, 0, j))
    full = lambda a: pl.BlockSpec(a.shape, lambda i: (0,) * a.ndim)
    in_specs = [col_spec(0), col_spec(1), col_spec(2), col_spec(3), full(rd_l), full(gn_w), full(gn_b)]
    args = [mix3, mix3, mix3, mix3, rd_l, gn_w, gn_b]
    if rope:
        in_specs += [full(rope_tabs[0]), full(rope_tabs[1])]
        args += list(rope_tabs)
    state_spec = pl.BlockSpec((1, 2, HEADS, HEAD_DIM, HEAD_DIM), lambda i: (i, 0, 0, 0, 0))
    if has_init:
        in_specs.append(state_spec)
        args.append(s0)
    out_specs = [pl.BlockSpec((1, seq, W), lambda i: (i, 0, 0))]
    out_shape = [jax.ShapeDtypeStruct((b, seq, W), F32)]
    if want_fin:
        out_specs.append(state_spec)
        out_shape.append(jax.ShapeDtypeStruct((b, 2, HEADS, HEAD_DIM, HEAD_DIM), F32))
    res = pl.pallas_call(
        functools.partial(_ret_kernel, seq=seq, rope=rope, has_init=has_init, want_fin=want_fin),
        grid=(b,),
        in_specs=in_specs,
        out_specs=out_specs,
        out_shape=out_shape,
        scratch_shapes=[pltpu.VMEM((seq, W), F32), pltpu.VMEM((seq, W), F32),
                        pltpu.VMEM((4, RET_CHUNK, W), F32), pltpu.VMEM((HEADS, RET_CHUNK, RET_CHUNK), F32),
                        pltpu.VMEM((2, HEADS, HEAD_DIM, HEAD_DIM), F32)],
        compiler_params=_cparams("parallel"),
        name="retention",
    )(*args)
    return (res[0], res[1]) if want_fin else (res[0], None)


EXP_M_HALF = 0.6065306597126334


def _rwkv_kernel(*refs, seq, nb, has_init, want_fin):
    it = iter(refs)
    (r_ref, k_ref, v_ref, lora_ref, cw_ref, w0_ref, w2_ref, a0_ref, a2_ref, g2_ref,
     kk_w_ref, ka_ref, rk_ref, gw_ref, gb_ref) = (next(it) for _ in range(15))
    s0_ref = next(it) if has_init else None
    y_ref = next(it)
    fin_ref = next(it) if want_fin else None
    r_s, v_s, kk_s, g_s, bv_s, lw_s, b_s, kd_s, st_s = (next(it) for _ in range(9))

    C = RWKV_CHUNK
    n = seq // C
    W = BRANCH_W
    row_id = lax.broadcasted_iota(jnp.int32, (C, W), 0)

    def rows(c):
        return pl.ds(pl.multiple_of(c * C, C), C)

    def conv(ref, s, c, w):
        x = ref[s, rows(c), :]
        prev8 = ref[s, pl.ds(pl.multiple_of(jnp.maximum(c * C - 8, 0), 8), 8), :]
        next8 = ref[s, pl.ds(pl.multiple_of(jnp.minimum(c * C + C, seq - 8), 8), 8), :]
        prev_row = jnp.where(c > 0, prev8[7:8, :], 0.0)
        next_row = jnp.where(c < n - 1, next8[0:1, :], 0.0)
        xm = jnp.where(row_id == 0, prev_row, pltpu.roll(x, 1, 0))
        xp = jnp.where(row_id == C - 1, next_row, pltpu.roll(x, C - 1, 0))
        return w[0:1] * xm + w[1:2] * x + w[2:3] * xp

    def prep_seq(s, c):
        rws = rows(c)
        cw = cw_ref[...]
        r = conv(r_ref, s, c, cw[:, 0:W])
        k = conv(k_ref, s, c, cw[:, W:2 * W])
        v = conv(v_ref, s, c, cw[:, 2 * W:3 * W])
        lo = lora_ref[s, rws, :]
        dw = lo[:, 0:LORA_W]
        da = lo[:, LORA_W:LORA_W + LORA_A]
        dg = lo[:, LORA_W + LORA_A:]
        kk = k * kk_w_ref[...]
        kk = kk * lax.rsqrt(_head_sum(kk * kk) + 1e-12)
        y_ref[s, rws, :] = jnp.zeros((C, W), F32)
        r_s[s, rws, :] = r
        v_s[s, rws, :] = v
        kk_s[s, rws, :] = kk
        g_s[s, rws, :] = _mm(_sigmoid(dg), g2_ref[...])
        bv_s[s, rws, :] = _head_sum(r * k * rk_ref[...]) * v
        tdw = jnp.tanh(dw)
        kka = k * ka_ref[...]
        for d in range(2):
            z = w0_ref[d:d + 1, :] + _mm(tdw, w2_ref[d])
            a = _sigmoid(a0_ref[d:d + 1, :] + _mm(da, a2_ref[d]))
            lw_s[s, d, rws, :] = -EXP_M_HALF * _sigmoid(z)
            b_s[s, d, rws, :] = kk * a
            kd_s[s, d, rws, :] = k + kka * (a - 1.0)

    def prep(c, carry):
        for s in range(nb):
            prep_seq(s, c)
        return carry

    lax.fori_loop(0, n, prep, 0)

    ii = lax.broadcasted_iota(jnp.int32, (C, C), 0)
    jj = lax.broadcasted_iota(jnp.int32, (C, C), 1)
    incl = [ii >= jj, ii <= jj]
    strict = [ii > jj, ii < jj]

    cpi = 2 if (nb == 1 and n % 4 == 0) else 1
    streams = [(s, d, u) for s in range(nb) for d in range(2) for u in range(cpi)]
    for s in range(nb):
        for d in range(2):
            for h in range(HEADS):
                st_s[s, d, h] = s0_ref[s, d, h] if has_init else jnp.zeros((HEAD_DIM, HEAD_DIM), F32)

    def chunk(i, carry):
        rws, at, bt, kt, rt, bh, kh, etot, vc = ([None] * len(streams) for _ in range(9))
        for q, (s, d, u) in enumerate(streams):
            rws[q] = rows(i * cpi + u if d == 0 else n - 1 - (i * cpi + u))
            rc, vc[q], kkc = r_s[s, rws[q], :], v_s[s, rws[q], :], kk_s[s, rws[q], :]
            lwc, bc, kc = lw_s[s, d, rws[q], :], b_s[s, d, rws[q], :], kd_s[s, d, rws[q], :]
            cum = lwc
            for k in (1 << p for p in range((C - 1).bit_length())):
                if d == 0:
                    cum = cum + jnp.where(row_id >= k, pltpu.roll(cum, k, 0), 0.0)
                else:
                    cum = cum + jnp.where(row_id < C - k, pltpu.roll(cum, C - k, 0), 0.0)
            tot = cum[C - 1:C, :] if d == 0 else cum[0:1, :]
            pinv = jnp.exp(-cum)
            pend = jnp.exp(tot - cum)
            at[q] = -kkc * jnp.exp(cum - lwc)
            bt[q] = bc * pinv
            kt[q] = kc * pinv
            rt[q] = rc * jnp.exp(cum)
            bh[q] = bc * pend
            kh[q] = kc * pend
            etot[q] = jnp.exp(tot)
        chains = [(q, h) for q in range(len(streams)) for h in range(HEADS)]
        dirn = lambda q: streams[q][1]
        ln = lambda h: slice(h * HEAD_DIM, (h + 1) * HEAD_DIM)
        wk = [jnp.concatenate([bt[q][:, ln(h)], kt[q][:, ln(h)]], axis=0) for q, h in chains]
        ma = [_mm_nt(at[q][:, ln(h)], wk[j]) for j, (q, h) in enumerate(chains)]
        mr = [_mm_nt(rt[q][:, ln(h)], wk[j]) for j, (q, h) in enumerate(chains)]
        a_ak = [jnp.where(strict[dirn(q)], ma[j][:, C:2 * C], 0.0) for j, (q, h) in enumerate(chains)]
        p = [jnp.where(strict[dirn(q)], ma[j][:, 0:C], 0.0) for j, (q, h) in enumerate(chains)]
        a_rb = [jnp.where(incl[dirn(q)], mr[j][:, 0:C], 0.0) for j, (q, h) in enumerate(chains)]
        a_rk = [jnp.where(incl[dirn(q)], mr[j][:, C:2 * C], 0.0) for j, (q, h) in enumerate(chains)]
        vh = [vc[q][:, ln(h)] for q, h in chains]
        x = [jnp.concatenate([at[q][:, ln(h)], _mm(a_ak[j], vh[j])], axis=1)
             for j, (q, h) in enumerate(chains)]
        for step in range(6):
            pb = [p[j].astype(BF16) for j in range(len(chains))]
            xh = [x[j].astype(BF16) for j in range(len(chains))]
            if step < 3:
                xl = [(x[j] - xh[j].astype(F32)).astype(BF16) for j in range(len(chains))]
                px = [jnp.dot(pb[j], jnp.concatenate([xh[j], xl[j]], axis=1), preferred_element_type=F32)
                      for j in range(len(chains))]
                x = [x[j] + (px[j][:, 0:2 * HEAD_DIM] + px[j][:, 2 * HEAD_DIM:]) for j in range(len(chains))]
            else:
                x = [x[j] + jnp.dot(pb[j], xh[j], preferred_element_type=F32) for j in range(len(chains))]
            if step < 5:
                p = [jnp.dot(pb[j], pb[j], preferred_element_type=F32) for j in range(len(chains))]
        gh = [_mm_tn(x[j], bh[q][:, ln(h)]) for j, (q, h) in enumerate(chains)]
        vk = [_mm_tn_3pass(vh[j], kh[q][:, ln(h)]) for j, (q, h) in enumerate(chains)]
        ax = [_mm(a_rb[j], x[j]) for j in range(len(chains))]
        qt = [rt[q][:, ln(h)] + ax[j][:, 0:HEAD_DIM] for j, (q, h) in enumerate(chains)]
        y0 = [ax[j][:, HEAD_DIM:2 * HEAD_DIM] + _mm(a_rk[j], vh[j]) for j in range(len(chains))]
        ys = [None] * len(chains)
        scans = [(s, d, h) for s in range(nb) for d in range(2) for h in range(HEADS)]
        st = {k: st_s[k] for k in scans}
        for u in range(cpi):
            for s, d, h in scans:
                q = streams.index((s, d, u))
                j = q * HEADS + h
                cur = st[s, d, h]
                ys[j] = _mm_nt(qt[j], cur) + y0[j]
                st[s, d, h] = (cur * etot[q][:, ln(h)] + _mm(cur, gh[j][0:HEAD_DIM, :])
                               + gh[j][HEAD_DIM:2 * HEAD_DIM, :] + vk[j])
        for k in scans:
            st_s[k] = st[k]
        for q, (s, d, u) in enumerate(streams):
            y = jnp.concatenate(ys[q * HEADS:(q + 1) * HEADS], axis=1)
            y_ref[s, rws[q], :] = y_ref[s, rws[q], :] + y
        return carry

    lax.fori_loop(0, n // cpi, chunk, 0)
    if want_fin:
        for s in range(nb):
            for d in range(2):
                for h in range(HEADS):
                    fin_ref[s, d, h] = st_s[s, d, h]

    def post(c, carry):
        rws = rows(c)
        for s in range(nb):
            y = _head_norm(y_ref[s, rws, :], gw_ref[...], gb_ref[...])
            y_ref[s, rws, :] = (y + bv_s[s, rws, :]) * g_s[s, rws, :]
        return carry

    lax.fori_loop(0, n, post, 0)


def _rwkv(mix3, lora3, lw, s0, want_fin):
    b, seq, _ = mix3.shape
    has_init = s0 is not None
    W = BRANCH_W
    nb = 2 if (seq <= 256 and b % 2 == 0) else 1
    col_spec = lambda j: pl.BlockSpec((nb, seq, W), lambda i: (i, 0, j))
    full = lambda a: pl.BlockSpec(a.shape, lambda i: (0,) * a.ndim)
    weights = [lw['conv'], lw['w0'], lw['w2'], lw['a0'], lw['a2'], lw['g2'],
               lw['k_k'], lw['k_a'], lw['r_k'], lw['gn_w'], lw['gn_b']]
    in_specs = [col_spec(4), col_spec(5), col_spec(6),
                pl.BlockSpec((nb, seq, COLS_LORA), lambda i: (i, 0, 0))] + [full(a) for a in weights]
    args = [mix3, mix3, mix3, lora3] + weights
    state_spec = pl.BlockSpec((nb, 2, HEADS, HEAD_DIM, HEAD_DIM), lambda i: (i, 0, 0, 0, 0))
    if has_init:
        in_specs.append(state_spec)
        args.append(s0)
    out_specs = [pl.BlockSpec((nb, seq, W), lambda i: (i, 0, 0))]
    out_shape = [jax.ShapeDtypeStruct((b, seq, W), F32)]
    if want_fin:
        out_specs.append(state_spec)
        out_shape.append(jax.ShapeDtypeStruct((b, 2, HEADS, HEAD_DIM, HEAD_DIM), F32))
    sw = pltpu.VMEM((nb, seq, W), F32)
    sw2 = pltpu.VMEM((nb, 2, seq, W), F32)
    res = pl.pallas_call(
        functools.partial(_rwkv_kernel, seq=seq, nb=nb, has_init=has_init, want_fin=want_fin),
        grid=(b // nb,),
        in_specs=in_specs,
        out_specs=out_specs,
        out_shape=out_shape,
        scratch_shapes=[sw, sw, sw, sw, sw, sw2, sw2, sw2,
                        pltpu.VMEM((nb, 2, HEADS, HEAD_DIM, HEAD_DIM), F32)],
        compiler_params=_cparams("parallel"),
        name="rwkv7",
    )(*args)
    return (res[0], res[1]) if want_fin else (res[0], None)


def _merge_kernel(ya_ref, yb_ref, gate_ref, x_ref, mod_ref, wa_ref, wb_ref, wo_ref, n2_ref, x1_ref, h2_ref):
    m = mod_ref[0]
    br_a = jnp.dot(ya_ref[...].astype(BF16), wa_ref[...], preferred_element_type=F32)
    br_b = jnp.dot(yb_ref[...].astype(BF16), wb_ref[...], preferred_element_type=F32)
    gate = gate_ref[...]
    merged = _sigmoid(gate[:, 0:D_MODEL]) * br_a + _sigmoid(gate[:, D_MODEL:]) * br_b
    mix = jnp.dot(merged.astype(BF16), wo_ref[...], preferred_element_type=F32)
    x1 = x_ref[...] + m[2:3] * mix
    x1_ref[...] = x1
    h2_ref[...] = (_rms(x1, n2_ref[...]) * (1.0 + m[4:5]) + m[3:4]).astype(BF16)


def _merge(ya2, yb2, gate2, x2, mod, mod_base, mod_step, seq, wa, wb, wo, norm2_g):
    n = x2.shape[0]
    tm = ROW_TILE
    mod_idx = lambda i: (mod_base + mod_step * ((i * tm) // seq), 0, 0)
    row = lambda w: pl.BlockSpec((tm, w), lambda i: (i, 0))
    full = lambda a: pl.BlockSpec(a.shape, lambda i: (0,) * a.ndim)
    return pl.pallas_call(
        _merge_kernel,
        grid=(n // tm,),
        in_specs=[row(BRANCH_W), row(BRANCH_W), row(COLS_GATE), row(D_MODEL),
                  pl.BlockSpec((1, N_MOD, D_MODEL), mod_idx),
                  full(wa), full(wb), full(wo), full(norm2_g)],
        out_specs=[row(D_MODEL), row(D_MODEL)],
        out_shape=[jax.ShapeDtypeStruct((n, D_MODEL), F32), jax.ShapeDtypeStruct((n, D_MODEL), BF16)],
        compiler_params=_cparams("parallel"),
        name="merge_out",
    )(ya2, yb2, gate2, x2, mod, wa, wb, wo, norm2_g)


def _bitonic_pairs(n):
    pairs = []
    k = 2
    while k <= n:
        j = k // 2
        while j >= 1:
            for i in range(n):
                l = i ^ j
                if l > i:
                    pairs.append((i, l) if (i & k) == 0 else (l, i))
            j //= 2
        k *= 2
    return pairs


def _top_values_of_keys(s, k):
    r, t = s.shape
    nl = r // 8
    lv = [s[8 * j:8 * j + 8, :] for j in range(nl)]
    for a, b in _bitonic_pairs(nl):
        lv[a], lv[b] = jnp.maximum(lv[a], lv[b]), jnp.minimum(lv[a], lv[b])
    sid = lax.broadcasted_iota(jnp.int32, (8, t), 0)
    out = []
    for it in range(k):
        m = jnp.max(lv[0], axis=0, keepdims=True)
        out.append(m)
        first = jnp.min(jnp.where(lv[0] == m, sid, 8), axis=0, keepdims=True)
        pop = sid == first
        for j in range(min(nl, k - 1 - it)):
            lv[j] = jnp.where(pop, lv[j + 1] if j + 1 < nl else NEG_INF, lv[j])
    return out


def _peer_stats(h, wq_ref, keys_ref, thr_ref, g1_ref, s2_ref, e2_ref, cand_ref, lanes):
    qt = _mm_nt(wq_ref[...], h)
    half = PEER_DQ // 2
    k1 = PEER_TOPK + 1
    pairs = [(i, j) for i in range(k1) for j in range(k1) if (i + 1) * (j + 1) <= k1]
    assert len(pairs) <= cand_ref.shape[0]
    for h in range(PEER_HEADS):
        s = [_mm(keys_ref[2 * h + c], qt[(2 * h + c) * half:(2 * h + c + 1) * half, :]) for c in range(2)]
        tops = [_top_values_of_keys(s[c], k1) for c in range(2)]
        cand_ref[...] = jnp.full(cand_ref.shape, NEG_INF, F32)
        for r, (i, j) in enumerate(pairs):
            cand_ref[r:r + 1, :] = tops[0][i] + tops[1][j]
        best = _top_values_of_keys(cand_ref[...], k1)
        mx = best[0]
        z = jnp.exp(best[0] - mx)
        for r in range(1, PEER_TOPK):
            z = z + jnp.exp(best[r] - mx)
        theta = 0.5 * (best[PEER_TOPK - 1] + best[PEER_TOPK])
        thr_ref[h, :, lanes] = theta - s[0]
        g1_ref[h, :, lanes] = jnp.exp(s[0] - tops[0][0]) * (0.5 / z)
        s2_ref[h, :, lanes] = s[1]
        e2_ref[h, :, lanes] = jnp.exp(s[1] - tops[1][0])


def _gelu_tanh_x2(x):
    return x + x * jnp.tanh(x * (0.7978845608028654 + 0.035677408136300125 * (x * x)))


def _peer_expert_kernel(h_ref, u_ref, vt_ref, wq_ref, keys_ref, x1_ref, mod_ref, fg_ref, y_ref,
                        acc_ref, w_ref, thr_ref, g1_ref, s2_ref, e2_ref, cand_ref, *, rows_per_step):
    e = pl.program_id(1)
    tb = h_ref.shape[0]
    ts = cand_ref.shape[1]

    @pl.when(e == 0)
    def _():
        acc_ref[...] = jnp.zeros_like(acc_ref)
        for c in range(tb // ts):
            tok = slice(c * ts, (c + 1) * ts)
            _peer_stats(h_ref[tok, :], wq_ref, keys_ref, thr_ref, g1_ref, s2_ref, e2_ref, cand_ref, tok)

    st = _mm_nt(u_ref[...], h_ref[...])
    rows_per_chunk = 2
    partial = None
    for al in range(rows_per_step):
        a = e * rows_per_step + al
        r0 = al * N_KEYS
        thr = [thr_ref[h, pl.ds(a, 1), :] for h in range(PEER_HEADS)]
        g1 = [g1_ref[h, pl.ds(a, 1), :] for h in range(PEER_HEADS)]
        for lt in range(tb // LANES):
            ls = slice(lt * LANES, (lt + 1) * LANES)
            wsum = None
            for h in range(PEER_HEADS):
                w = jnp.where(s2_ref[h, :, ls] >= thr[h][:, ls], e2_ref[h, :, ls] * g1[h][:, ls], 0.0)
                wsum = w if wsum is None else wsum + w
            act = _gelu_tanh_x2(st[r0:r0 + N_KEYS, ls])
            w_ref[r0:r0 + N_KEYS, ls] = (wsum * act).astype(BF16)
        if (al + 1) % rows_per_chunk == 0:
            rs = slice((al + 1 - rows_per_chunk) * N_KEYS, (al + 1) * N_KEYS)
            d = jnp.dot(vt_ref[:, rs], w_ref[rs, :], preferred_element_type=F32)
            partial = d if partial is None else partial + d
    acc_ref[...] += partial

    @pl.when(e == pl.num_programs(1) - 1)
    def _():
        m = mod_ref[0]
        x2 = x1_ref[...] + m[5:6] * acc_ref[...].T
        y_ref[...] = _rms(x2, fg_ref[...])


def _peer_ffn(h2, u_bf, vt_bf, wq_t, keys, x1, mod, mod_base, mod_step, seq, final_g):
    n = h2.shape[0]
    tb = PEER_EXPERT_TOKENS if (mod_step == 0 or seq % PEER_EXPERT_TOKENS == 0) else seq
    ts = min(PEER_SCORE_TOKENS, tb)
    rows_per_step = PEER_ROWS_PER_STEP
    ec = rows_per_step * N_KEYS
    mod_idx = lambda i, e: (mod_base + mod_step * ((i * tb) // seq), 0, 0)
    full = lambda a: pl.BlockSpec(a.shape, lambda i, e: (0,) * a.ndim)
    stat = pltpu.VMEM((PEER_HEADS, N_KEYS, tb), F32)
    return pl.pallas_call(
        functools.partial(_peer_expert_kernel, rows_per_step=rows_per_step),
        grid=(n // tb, N_EXPERTS // ec),
        in_specs=[pl.BlockSpec((tb, D_MODEL), lambda i, e: (i, 0)),
                  pl.BlockSpec((ec, D_MODEL), lambda i, e: (e, 0)),
                  pl.BlockSpec((D_MODEL, ec), lambda i, e: (0, e)),
                  full(wq_t), full(keys),
                  pl.BlockSpec((tb, D_MODEL), lambda i, e: (i, 0)),
                  pl.BlockSpec((1, N_MOD, D_MODEL), mod_idx),
                  pl.BlockSpec((1, D_MODEL), lambda i, e: (0, 0))],
        out_specs=pl.BlockSpec((tb, D_MODEL), lambda i, e: (i, 0)),
        out_shape=jax.ShapeDtypeStruct((n, D_MODEL), F32),
        scratch_shapes=[pltpu.VMEM((D_MODEL, tb), F32), pltpu.VMEM((ec, tb), BF16), stat, stat, stat, stat,
                        pltpu.VMEM((64, ts), F32)],
        compiler_params=_cparams("parallel", "arbitrary"),
        name="peer_ffn",
    )(h2, u_bf, vt_bf, wq_t, keys, x1, mod, final_g)


def _rope_tables(seq):
    pos = jnp.arange(seq, dtype=jnp.int32)
    lane = jnp.arange(LANES, dtype=jnp.int32) % HEAD_DIM
    use_col = (lane // 32) == 1
    p = jnp.where(use_col[None, :], (pos % GRID_W)[:, None], (pos // GRID_W)[:, None]).astype(F32)
    inv = ROPE_BASE ** (-(lane % 16).astype(F32) / 16.0)
    ang = p * inv[None, :]
    first = (lane % 32) < 16
    return jnp.cos(ang), jnp.where(first[None, :], -jnp.sin(ang), jnp.sin(ang))


def _trunk_path(x, mod, mod_base, mod_step, rope_tabs, s_ret0, s_rwkv0, want_fin, lw, final_g):
    b, seq, _ = x.shape
    x2 = x.reshape(b * seq, D_MODEL)
    mix, lora, gate = _in_proj(x2, mod, mod_base, mod_step, seq, lw['norm1_g'], lw['w_in'])
    mix3 = mix.reshape(b, seq, COLS_MIX)
    lora3 = lora.reshape(b, seq, COLS_LORA)
    ya, ret_fin = _retention(mix3, lw['ret_decay'], lw['ret_gn_w'], lw['ret_gn_b'], rope_tabs, s_ret0, want_fin)
    yb, rwkv_fin = _rwkv(mix3, lora3, lw['rwkv'], s_rwkv0, want_fin)
    x1, h2 = _merge(ya.reshape(b * seq, BRANCH_W), yb.reshape(b * seq, BRANCH_W), gate, x2, mod, mod_base,
                    mod_step, seq, lw['w_br_a'], lw['w_br_b'], lw['w_out'], lw['norm2_g'])
    y = _peer_ffn(h2, lw['peer_u'], lw['peer_vt'], lw['peer_wq_t'], lw['peer_keys'], x1, mod, mod_base, mod_step,
                  seq, final_g)
    return y.reshape(b, seq, D_MODEL), ret_fin, rwkv_fin


def kernel(x_prompt, x_sample, state_ret, state_rwkv, c, c_ctx, ada_w, ada_b, norm1_g, w_in, ret_decay, ret_gn_w, ret_gn_b, rwkv_conv, rwkv_w0, rwkv_w2, rwkv_a0, rwkv_a2, rwkv_g2, rwkv_k_k, rwkv_k_a, rwkv_r_k, rwkv_gn_w, rwkv_gn_b, w_br_a, w_br_b, w_out, norm2_g, peer_wq, peer_keys, peer_u, peer_v, final_norm_g):
    assert w_in.shape[0] == 1, "the final norm is fused into the layer's last kernel: single trunk layer only"
    row = lambda a: a.reshape(1, -1)
    cc = jnp.concatenate([c_ctx[None, :], c], axis=0)
    cc = jnp.pad(cc, ((0, (-cc.shape[0]) % 8), (0, 0)))
    rope_tabs = _rope_tables(x_sample.shape[1])
    final_g = row(final_norm_g)

    l = 0
    lw = {
        'norm1_g': row(norm1_g[l]), 'w_in': w_in[l].astype(BF16),
        'ret_decay': jnp.repeat(ret_decay[l], HEAD_DIM, axis=1),
        'ret_gn_w': row(ret_gn_w[l]), 'ret_gn_b': row(ret_gn_b[l]),
        'rwkv': {'conv': rwkv_conv[l], 'w0': rwkv_w0[l], 'w2': rwkv_w2[l].astype(BF16), 'a0': rwkv_a0[l],
                 'a2': rwkv_a2[l].astype(BF16), 'g2': rwkv_g2[l].astype(BF16), 'k_k': row(rwkv_k_k[l]),
                 'k_a': row(rwkv_k_a[l]), 'r_k': row(rwkv_r_k[l]), 'gn_w': row(rwkv_gn_w[l]),
                 'gn_b': row(rwkv_gn_b[l])},
        'w_br_a': w_br_a[l].astype(BF16), 'w_br_b': w_br_b[l].astype(BF16), 'w_out': w_out[l].astype(BF16),
        'norm2_g': row(norm2_g[l]),
        'peer_wq_t': peer_wq[l].astype(BF16).T,
        'peer_keys': peer_keys[l].reshape(2 * PEER_HEADS, N_KEYS, PEER_DQ // 2).astype(BF16),
        'peer_u': peer_u[l].astype(BF16), 'peer_vt': peer_v[l].astype(BF16).T,
    }
    mod = _adaln(cc, ada_w[l], row(ada_b[l])).reshape(cc.shape[0], N_MOD, D_MODEL)
    yp, ret_fin, rwkv_fin = _trunk_path(x_prompt, mod, 0, 0, None, None, None, True, lw, final_g)
    ys, _, _ = _trunk_path(x_sample, mod, 1, 1, rope_tabs, state_ret[:, l], state_rwkv[:, l], False, lw, final_g)
    return (yp, ys, ret_fin[:, None], rwkv_fin[:, None])
```

```python
import functools

import jax
import jax.numpy as jnp
from jax import lax
from jax.experimental import pallas as pl
from jax.experimental.pallas import tpu as pltpu

F32 = jnp.float32
BF16 = jnp.bfloat16

D_MODEL = 1024
GRID_W = 64
N_MOD = 6
HEADS = 8
HEAD_DIM = 64
BRANCH_W = HEADS * HEAD_DIM
RET_CHUNK = 128
RWKV_CHUNK = 64
LORA_W = 64
LORA_A = 64
LORA_G = 128
PEER_HEADS = 8
N_KEYS = 128
N_EXPERTS = N_KEYS * N_KEYS
PEER_DQ = 256
PEER_TOPK = 16
ROPE_BASE = 10000.0
NORM_EPS = 1e-6
GN_EPS = 64e-5
COLS_MIX = 7 * BRANCH_W
COLS_LORA = LORA_W + LORA_A + LORA_G
COLS_GATE = 2 * D_MODEL
IN_COLS = COLS_MIX + COLS_LORA + COLS_GATE

V7X_VMEM_LIMIT_BYTES = 56 * 1024 * 1024
LANES = 128
ROW_TILE = 256
PEER_SCORE_TOKENS = 256
PEER_EXPERT_TOKENS = 512
PEER_ROWS_PER_STEP = 8
NEG_INF = float("-inf")


def _cparams(*sem):
    return pltpu.CompilerParams(dimension_semantics=sem, vmem_limit_bytes=V7X_VMEM_LIMIT_BYTES)


def _mm(a, b):
    return jnp.dot(a.astype(BF16), b.astype(BF16), preferred_element_type=F32)


def _mm_nt(a, b):
    return lax.dot_general(a.astype(BF16), b.astype(BF16), (((1,), (1,)), ((), ())),
                           preferred_element_type=F32)


def _mm_tn(a, b):
    return lax.dot_general(a.astype(BF16), b.astype(BF16), (((0,), (0,)), ((), ())),
                           preferred_element_type=F32)


def _mm_tn_3pass(a, b):
    a_hi, b_hi = a.astype(BF16), b.astype(BF16)
    a_lo = (a - a_hi.astype(F32)).astype(BF16)
    b_lo = (b - b_hi.astype(F32)).astype(BF16)
    dims = (((0,), (0,)), ((), ()))
    dot = lambda x, y: lax.dot_general(x, y, dims, preferred_element_type=F32)
    return dot(a_hi, b_hi) + (dot(a_hi, b_lo) + dot(a_lo, b_hi))


def _sigmoid(x):
    return 1.0 / (1.0 + jnp.exp(-x))


def _rms(x, g):
    return x * lax.rsqrt(jnp.mean(x * x, axis=-1, keepdims=True) + NORM_EPS) * g


def _head_sum(x):
    t, w = x.shape
    lo = lax.broadcasted_iota(jnp.int32, (t, LANES), 1) < HEAD_DIM
    outs = []
    for j in range(w // LANES):
        xt = x[:, j * LANES:(j + 1) * LANES]
        s_lo = jnp.sum(jnp.where(lo, xt, 0.0), axis=-1, keepdims=True)
        s_hi = jnp.sum(jnp.where(lo, 0.0, xt), axis=-1, keepdims=True)
        outs.append(jnp.where(lo, s_lo, s_hi))
    return outs[0] if len(outs) == 1 else jnp.concatenate(outs, axis=-1)


def _head_norm(y, w, b):
    mu = _head_sum(y) * (1.0 / HEAD_DIM)
    d = y - mu
    var = _head_sum(d * d) * (1.0 / HEAD_DIM)
    return d * lax.rsqrt(var + GN_EPS) * w + b


def _mod_kernel(c_ref, w_ref, b_ref, o_ref):
    c = c_ref[...]
    o_ref[...] = _mm(c * _sigmoid(c), w_ref[...]) + b_ref[...]


def _adaln(cc, ada_w, ada_b):
    rows = cc.shape[0]
    n = ada_w.shape[1]
    tn = n // 4
    return pl.pallas_call(
        _mod_kernel,
        grid=(n // tn,),
        in_specs=[pl.BlockSpec((rows, D_MODEL), lambda j: (0, 0)),
                  pl.BlockSpec((D_MODEL, tn), lambda j: (0, j)),
                  pl.BlockSpec((1, tn), lambda j: (0, j))],
        out_specs=pl.BlockSpec((rows, tn), lambda j: (0, j)),
        out_shape=jax.ShapeDtypeStruct((rows, n), F32),
        compiler_params=_cparams("parallel"),
        name="adaln_mod",
    )(cc, ada_w, ada_b)


def _in_kernel(x_ref, mod_ref, g_ref, w_ref, mix_ref, lora_ref, gate_ref):
    m = mod_ref[0]
    h = (_rms(x_ref[...], g_ref[...]) * (1.0 + m[1:2]) + m[0:1]).astype(BF16)
    mix_ref[...] = jnp.dot(h, w_ref[:, 0:COLS_MIX], preferred_element_type=F32)
    lora_ref[...] = jnp.dot(h, w_ref[:, COLS_MIX:COLS_MIX + COLS_LORA], preferred_element_type=F32)
    gate_ref[...] = jnp.dot(h, w_ref[:, COLS_MIX + COLS_LORA:IN_COLS], preferred_element_type=F32).astype(BF16)


def _in_proj(x2, mod, mod_base, mod_step, seq, norm_g, w_in_bf):
    n = x2.shape[0]
    tm = ROW_TILE
    mod_idx = lambda i: (mod_base + mod_step * ((i * tm) // seq), 0, 0)
    return pl.pallas_call(
        _in_kernel,
        grid=(n // tm,),
        in_specs=[pl.BlockSpec((tm, D_MODEL), lambda i: (i, 0)),
                  pl.BlockSpec((1, N_MOD, D_MODEL), mod_idx),
                  pl.BlockSpec((1, D_MODEL), lambda i: (0, 0)),
                  pl.BlockSpec((D_MODEL, IN_COLS), lambda i: (0, 0))],
        out_specs=[pl.BlockSpec((tm, COLS_MIX), lambda i: (i, 0)),
                   pl.BlockSpec((tm, COLS_LORA), lambda i: (i, 0)),
                   pl.BlockSpec((tm, COLS_GATE), lambda i: (i, 0))],
        out_shape=[jax.ShapeDtypeStruct((n, COLS_MIX), F32),
                   jax.ShapeDtypeStruct((n, COLS_LORA), F32),
                   jax.ShapeDtypeStruct((n, COLS_GATE), BF16)],
        compiler_params=_cparams("parallel"),
        name="in_proj",
    )(x2, mod, norm_g, w_in_bf)


def _rope(x, cos, sin):
    lane = lax.broadcasted_iota(jnp.int32, x.shape, 1)
    first = (lane % 32) < 16
    partner = jnp.where(first, pltpu.roll(x, LANES - 16, 1), pltpu.roll(x, 16, 1))
    return x * cos + partner * sin


def _ret_kernel(*refs, seq, rope, has_init, want_fin):
    it = iter(refs)
    q_ref, k_ref, v_ref, g_ref, rd_ref, gw_ref, gb_ref = (next(it) for _ in range(7))
    cos_ref = next(it) if rope else None
    sin_ref = next(it) if rope else None
    s0_ref = next(it) if has_init else None
    out_ref = next(it)
    fin_ref = next(it) if want_fin else None
    qs_ref, ks_ref, vec_ref, dec_ref, r_ref, y_ref = (next(it) for _ in range(6))

    C = RET_CHUNK
    W = BRANCH_W
    n = seq // C
    rd = rd_ref[...]
    lg = jnp.minimum(rd, 0.0) - jnp.log(1.0 + jnp.exp(-jnp.abs(rd)))
    lgf, lgb = lg[0:1, :], lg[1:2, :]
    ii = lax.broadcasted_iota(jnp.int32, (C, C), 0)
    jj = lax.broadcasted_iota(jnp.int32, (C, C), 1)
    diff = (ii - jj).astype(F32)
    col = lax.broadcasted_iota(jnp.int32, (C, W), 0).astype(F32)
    ln = lambda h: slice(h * HEAD_DIM, (h + 1) * HEAD_DIM)

    vec_ref[0] = jnp.exp((col + 1.0) * lgf)
    vec_ref[1] = jnp.exp((C - 1.0 - col) * lgf)
    vec_ref[2] = jnp.exp((C - col) * lgb)
    vec_ref[3] = jnp.exp(col * lgb)
    chunk_f = jnp.exp(C * lgf)
    chunk_b = jnp.exp(C * lgb)
    for h in range(HEADS):
        gf = lgf[:, h * HEAD_DIM:h * HEAD_DIM + 1]
        gb = lgb[:, h * HEAD_DIM:h * HEAD_DIM + 1]
        dec_ref[h] = (jnp.where(diff >= 0, jnp.exp(jnp.maximum(diff, 0.0) * gf), 0.0)
                      + jnp.where(diff <= 0, jnp.exp(jnp.maximum(-diff, 0.0) * gb), 0.0))
        for d in range(2):
            r_ref[d, h] = s0_ref[0, d, h] if has_init else jnp.zeros((HEAD_DIM, HEAD_DIM), F32)

    def rows(c):
        return pl.ds(pl.multiple_of(c * C, C), C)

    def prep(c, carry):
        r = rows(c)
        q = q_ref[0, r, :]
        k = k_ref[0, r, :] * (HEAD_DIM ** -0.5)
        if rope:
            cos, sin = cos_ref[r, :], sin_ref[r, :]
            tiles = lambda x: [x[:, j * LANES:(j + 1) * LANES] for j in range(W // LANES)]
            q = jnp.concatenate([_rope(t, cos, sin) for t in tiles(q)], axis=1)
            k = jnp.concatenate([_rope(t, cos, sin) for t in tiles(k)], axis=1)
        qs_ref[r, :] = q
        ks_ref[r, :] = k
        y_ref[0, r, :] = jnp.zeros((C, W), F32)
        return carry

    lax.fori_loop(0, n, prep, 0)

    cpi = 2 if n % 4 == 0 else 1

    def step(i, carry):
        rf = [rows(i * cpi + u) for u in range(cpi)]
        rb = [rows(n - 1 - (i * cpi + u)) for u in range(cpi)]
        intra, qfx, inc_f, qbx, inc_b = ([[None] * HEADS for _ in range(cpi)] for _ in range(5))
        for u in range(cpi):
            qf, kf, vf = qs_ref[rf[u], :], ks_ref[rf[u], :], v_ref[0, rf[u], :]
            qb, kb, vb = qs_ref[rb[u], :], ks_ref[rb[u], :], v_ref[0, rb[u], :]
            qfs, kfs = qf * vec_ref[0], kf * vec_ref[1]
            qbs, kbs = qb * vec_ref[2], kb * vec_ref[3]
            for h in range(HEADS):
                sc = _mm_nt(qf[:, ln(h)], kf[:, ln(h)]) * dec_ref[h]
                intra[u][h] = _mm(sc, vf[:, ln(h)])
                qfx[u][h], qbx[u][h] = qfs[:, ln(h)], qbs[:, ln(h)]
                inc_f[u][h] = _mm_tn(kfs[:, ln(h)], vf[:, ln(h)])
                inc_b[u][h] = _mm_tn(kbs[:, ln(h)], vb[:, ln(h)])
        sf = [r_ref[0, h] for h in range(HEADS)]
        sb = [r_ref[1, h] for h in range(HEADS)]
        of, ob = ([[None] * HEADS for _ in range(cpi)] for _ in range(2))
        for u in range(cpi):
            for h in range(HEADS):
                of[u][h] = intra[u][h] + _mm(qfx[u][h], sf[h])
                sf[h] = sf[h] * chunk_f[:, ln(h)] + inc_f[u][h]
                ob[u][h] = _mm(qbx[u][h], sb[h])
                sb[h] = sb[h] * chunk_b[:, ln(h)] + inc_b[u][h]
        yf = [y_ref[0, rf[u], :] for u in range(cpi)]
        yb = [y_ref[0, rb[u], :] for u in range(cpi)]
        for u in range(cpi):
            y_ref[0, rf[u], :] = yf[u] + jnp.concatenate(of[u], axis=1)
            y_ref[0, rb[u], :] = yb[u] + jnp.concatenate(ob[u], axis=1)
        for h in range(HEADS):
            r_ref[0, h] = sf[h]
            r_ref[1, h] = sb[h]
        return carry

    lax.fori_loop(0, n // cpi, step, 0)
    if want_fin:
        for d in range(2):
            for h in range(HEADS):
                fin_ref[0, d, h] = r_ref[d, h]

    def post(c, carry):
        r = rows(c)
        g = g_ref[0, r, :]
        y = _head_norm(y_ref[0, r, :], gw_ref[...], gb_ref[...]) * (g * _sigmoid(g))
        out_ref[0, r, :] = y.astype(BF16)
        return carry

    lax.fori_loop(0, n, post, 0)


def _retention(mix3, rd_l, gn_w, gn_b, rope_tabs, s0, want_fin):
    b, seq, _ = mix3.shape
    W = BRANCH_W
    rope = rope_tabs is not None
    has_init = s0 is not None
    col_spec = lambda j: pl.BlockSpec((1, seq, W), lambda i: (i, 0, j))
    full = lambda a: pl.BlockSpec(a.shape, lambda i: (0,) * a.ndim)
    in_specs = [col_spec(0), col_spec(1), col_spec(2), col_spec(3), full(rd_l), full(gn_w), full(gn_b)]
    args = [mix3, mix3, mix3, mix3, rd_l, gn_w, gn_b]
    if rope:
        in_specs += [full(rope_tabs[0]), full(rope_tabs[1])]
        args += list(rope_tabs)
    state_spec = pl.BlockSpec((1, 2, HEADS, HEAD_DIM, HEAD_DIM), lambda i: (i, 0, 0, 0, 0))
    if has_init:
        in_specs.append(state_spec)
        args.append(s0)
    out_specs = [pl.BlockSpec((1, seq, W), lambda i: (i, 0, 0))]
    out_shape = [jax.ShapeDtypeStruct((b, seq, W), BF16)]
    if want_fin:
        out_specs.append(state_spec)
        out_shape.append(jax.ShapeDtypeStruct((b, 2, HEADS, HEAD_DIM, HEAD_DIM), F32))
    res = pl.pallas_call(
        functools.partial(_ret_kernel, seq=seq, rope=rope, has_init=has_init, want_fin=want_fin),
        grid=(b,),
        in_specs=in_specs,
        out_specs=out_specs,
        out_shape=out_shape,
        scratch_shapes=[pltpu.VMEM((seq, W), F32), pltpu.VMEM((seq, W), F32),
                        pltpu.VMEM((4, RET_CHUNK, W), F32), pltpu.VMEM((HEADS, RET_CHUNK, RET_CHUNK), F32),
                        pltpu.VMEM((2, HEADS, HEAD_DIM, HEAD_DIM), F32), pltpu.VMEM((1, seq, W), F32)],
        compiler_params=_cparams("parallel"),
        name="retention",
    )(*args)
    return (res[0], res[1]) if want_fin else (res[0], None)


EXP_M_HALF = 0.6065306597126334


def _rwkv_kernel(*refs, seq, nb, has_init, want_fin):
    it = iter(refs)
    (r_ref, k_ref, v_ref, lora_ref, cw_ref, w0_ref, w2_ref, a0_ref, a2_ref, g2_ref,
     kk_w_ref, ka_ref, rk_ref, gw_ref, gb_ref) = (next(it) for _ in range(15))
    s0_ref = next(it) if has_init else None
    out_ref = next(it)
    fin_ref = next(it) if want_fin else None
    r_s, v_s, kk_s, g_s, bv_s, lw_s, b_s, kd_s, st_s, y_ref = (next(it) for _ in range(10))

    C = RWKV_CHUNK
    n = seq // C
    W = BRANCH_W
    row_id = lax.broadcasted_iota(jnp.int32, (C, W), 0)

    def rows(c):
        return pl.ds(pl.multiple_of(c * C, C), C)

    def conv(ref, s, c, w):
        x = ref[s, rows(c), :]
        prev8 = ref[s, pl.ds(pl.multiple_of(jnp.maximum(c * C - 8, 0), 8), 8), :]
        next8 = ref[s, pl.ds(pl.multiple_of(jnp.minimum(c * C + C, seq - 8), 8), 8), :]
        prev_row = jnp.where(c > 0, prev8[7:8, :], 0.0)
        next_row = jnp.where(c < n - 1, next8[0:1, :], 0.0)
        xm = jnp.where(row_id == 0, prev_row, pltpu.roll(x, 1, 0))
        xp = jnp.where(row_id == C - 1, next_row, pltpu.roll(x, C - 1, 0))
        return w[0:1] * xm + w[1:2] * x + w[2:3] * xp

    def prep_seq(s, c):
        rws = rows(c)
        cw = cw_ref[...]
        r = conv(r_ref, s, c, cw[:, 0:W])
        k = conv(k_ref, s, c, cw[:, W:2 * W])
        v = conv(v_ref, s, c, cw[:, 2 * W:3 * W])
        lo = lora_ref[s, rws, :]
        dw = lo[:, 0:LORA_W]
        da = lo[:, LORA_W:LORA_W + LORA_A]
        dg = lo[:, LORA_W + LORA_A:]
        kk = k * kk_w_ref[...]
        kk = kk * lax.rsqrt(_head_sum(kk * kk) + 1e-12)
        y_ref[s, rws, :] = jnp.zeros((C, W), F32)
        r_s[s, rws, :] = r
        v_s[s, rws, :] = v
        kk_s[s, rws, :] = kk
        g_s[s, rws, :] = _mm(_sigmoid(dg), g2_ref[...])
        bv_s[s, rws, :] = _head_sum(r * k * rk_ref[...]) * v
        tdw = jnp.tanh(dw)
        kka = k * ka_ref[...]
        for d in range(2):
            z = w0_ref[d:d + 1, :] + _mm(tdw, w2_ref[d])
            a = _sigmoid(a0_ref[d:d + 1, :] + _mm(da, a2_ref[d]))
            lw_s[s, d, rws, :] = -EXP_M_HALF * _sigmoid(z)
            b_s[s, d, rws, :] = kk * a
            kd_s[s, d, rws, :] = k + kka * (a - 1.0)

    def prep(c, carry):
        for s in range(nb):
            prep_seq(s, c)
        return carry

    lax.fori_loop(0, n, prep, 0)

    ii = lax.broadcasted_iota(jnp.int32, (C, C), 0)
    jj = lax.broadcasted_iota(jnp.int32, (C, C), 1)
    incl = [ii >= jj, ii <= jj]
    strict = [ii > jj, ii < jj]

    cpi = 2 if (nb == 1 and n % 4 == 0) else 1
    streams = [(s, d, u) for s in range(nb) for d in range(2) for u in range(cpi)]
    for s in range(nb):
        for d in range(2):
            for h in range(HEADS):
                st_s[s, d, h] = s0_ref[s, d, h] if has_init else jnp.zeros((HEAD_DIM, HEAD_DIM), F32)

    def chunk(i, carry):
        rws, at, bt, kt, rt, bh, kh, etot, vc = ([None] * len(streams) for _ in range(9))
        for q, (s, d, u) in enumerate(streams):
            rws[q] = rows(i * cpi + u if d == 0 else n - 1 - (i * cpi + u))
            rc, vc[q], kkc = r_s[s, rws[q], :], v_s[s, rws[q], :], kk_s[s, rws[q], :]
            lwc, bc, kc = lw_s[s, d, rws[q], :], b_s[s, d, rws[q], :], kd_s[s, d, rws[q], :]
            cum = lwc
            for k in (1 << p for p in range((C - 1).bit_length())):
                if d == 0:
                    cum = cum + jnp.where(row_id >= k, pltpu.roll(cum, k, 0), 0.0)
                else:
                    cum = cum + jnp.where(row_id < C - k, pltpu.roll(cum, C - k, 0), 0.0)
            tot = cum[C - 1:C, :] if d == 0 else cum[0:1, :]
            pinv = jnp.exp(-cum)
            pend = jnp.exp(tot - cum)
            at[q] = -kkc * jnp.exp(cum - lwc)
            bt[q] = bc * pinv
            kt[q] = kc * pinv
            rt[q] = rc * jnp.exp(cum)
            bh[q] = bc * pend
            kh[q] = kc * pend
            etot[q] = jnp.exp(tot)
        chains = [(q, h) for q in range(len(streams)) for h in range(HEADS)]
        dirn = lambda q: streams[q][1]
        ln = lambda h: slice(h * HEAD_DIM, (h + 1) * HEAD_DIM)
        wk = [jnp.concatenate([bt[q][:, ln(h)], kt[q][:, ln(h)]], axis=0) for q, h in chains]
        ma = [_mm_nt(at[q][:, ln(h)], wk[j]) for j, (q, h) in enumerate(chains)]
        mr = [_mm_nt(rt[q][:, ln(h)], wk[j]) for j, (q, h) in enumerate(chains)]
        a_ak = [jnp.where(strict[dirn(q)], ma[j][:, C:2 * C], 0.0) for j, (q, h) in enumerate(chains)]
        p = [jnp.where(strict[dirn(q)], ma[j][:, 0:C], 0.0) for j, (q, h) in enumerate(chains)]
        a_rb = [jnp.where(incl[dirn(q)], mr[j][:, 0:C], 0.0) for j, (q, h) in enumerate(chains)]
        a_rk = [jnp.where(incl[dirn(q)], mr[j][:, C:2 * C], 0.0) for j, (q, h) in enumerate(chains)]
        vh = [vc[q][:, ln(h)] for q, h in chains]
        x = [jnp.concatenate([at[q][:, ln(h)], _mm(a_ak[j], vh[j])], axis=1)
             for j, (q, h) in enumerate(chains)]
        for step in range(6):
            pb = [p[j].astype(BF16) for j in range(len(chains))]
            xh = [x[j].astype(BF16) for j in range(len(chains))]
            if step < 3:
                xl = [(x[j] - xh[j].astype(F32)).astype(BF16) for j in range(len(chains))]
                px = [jnp.dot(pb[j], jnp.concatenate([xh[j], xl[j]], axis=1), preferred_element_type=F32)
                      for j in range(len(chains))]
                x = [x[j] + (px[j][:, 0:2 * HEAD_DIM] + px[j][:, 2 * HEAD_DIM:]) for j in range(len(chains))]
            else:
                x = [x[j] + jnp.dot(pb[j], xh[j], preferred_element_type=F32) for j in range(len(chains))]
            if step < 5:
                p = [jnp.dot(pb[j], pb[j], preferred_element_type=F32) for j in range(len(chains))]
        gh = [_mm_tn(x[j], bh[q][:, ln(h)]) for j, (q, h) in enumerate(chains)]
        vk = [_mm_tn_3pass(vh[j], kh[q][:, ln(h)]) for j, (q, h) in enumerate(chains)]
        ax = [_mm(a_rb[j], x[j]) for j in range(len(chains))]
        qt = [rt[q][:, ln(h)] + ax[j][:, 0:HEAD_DIM] for j, (q, h) in enumerate(chains)]
        y0 = [ax[j][:, HEAD_DIM:2 * HEAD_DIM] + _mm(a_rk[j], vh[j]) for j in range(len(chains))]
        ys = [None] * len(chains)
        scans = [(s, d, h) for s in range(nb) for d in range(2) for h in range(HEADS)]
        st = {k: st_s[k] for k in scans}
        for u in range(cpi):
            for s, d, h in scans:
                q = streams.index((s, d, u))
                j = q * HEADS + h
                cur = st[s, d, h]
                ys[j] = _mm_nt(qt[j], cur) + y0[j]
                st[s, d, h] = (cur * etot[q][:, ln(h)] + _mm(cur, gh[j][0:HEAD_DIM, :])
                               + gh[j][HEAD_DIM:2 * HEAD_DIM, :] + vk[j])
        for k in scans:
            st_s[k] = st[k]
        for q, (s, d, u) in enumerate(streams):
            y = jnp.concatenate(ys[q * HEADS:(q + 1) * HEADS], axis=1)
            y_ref[s, rws[q], :] = y_ref[s, rws[q], :] + y
        return carry

    lax.fori_loop(0, n // cpi, chunk, 0)
    if want_fin:
        for s in range(nb):
            for d in range(2):
                for h in range(HEADS):
                    fin_ref[s, d, h] = st_s[s, d, h]

    def post(c, carry):
        rws = rows(c)
        for s in range(nb):
            y = _head_norm(y_ref[s, rws, :], gw_ref[...], gb_ref[...])
            out_ref[s, rws, :] = ((y + bv_s[s, rws, :]) * g_s[s, rws, :]).astype(BF16)
        return carry

    lax.fori_loop(0, n, post, 0)


def _rwkv(mix3, lora3, lw, s0, want_fin):
    b, seq, _ = mix3.shape
    has_init = s0 is not None
    W = BRANCH_W
    nb = 2 if (seq <= 256 and b % 2 == 0) else 1
    col_spec = lambda j: pl.BlockSpec((nb, seq, W), lambda i: (i, 0, j))
    full = lambda a: pl.BlockSpec(a.shape, lambda i: (0,) * a.ndim)
    weights = [lw['conv'], lw['w0'], lw['w2'], lw['a0'], lw['a2'], lw['g2'],
               lw['k_k'], lw['k_a'], lw['r_k'], lw['gn_w'], lw['gn_b']]
    in_specs = [col_spec(4), col_spec(5), col_spec(6),
                pl.BlockSpec((nb, seq, COLS_LORA), lambda i: (i, 0, 0))] + [full(a) for a in weights]
    args = [mix3, mix3, mix3, lora3] + weights
    state_spec = pl.BlockSpec((nb, 2, HEADS, HEAD_DIM, HEAD_DIM), lambda i: (i, 0, 0, 0, 0))
    if has_init:
        in_specs.append(state_spec)
        args.append(s0)
    out_specs = [pl.BlockSpec((nb, seq, W), lambda i: (i, 0, 0))]
    out_shape = [jax.ShapeDtypeStruct((b, seq, W), BF16)]
    if want_fin:
        out_specs.append(state_spec)
        out_shape.append(jax.ShapeDtypeStruct((b, 2, HEADS, HEAD_DIM, HEAD_DIM), F32))
    sw = pltpu.VMEM((nb, seq, W), F32)
    sw2 = pltpu.VMEM((nb, 2, seq, W), F32)
    res = pl.pallas_call(
        functools.partial(_rwkv_kernel, seq=seq, nb=nb, has_init=has_init, want_fin=want_fin),
        grid=(b // nb,),
        in_specs=in_specs,
        out_specs=out_specs,
        out_shape=out_shape,
        scratch_shapes=[sw, sw, sw, sw, sw, sw2, sw2, sw2,
                        pltpu.VMEM((nb, 2, HEADS, HEAD_DIM, HEAD_DIM), F32), sw],
        compiler_params=_cparams("parallel"),
        name="rwkv7",
    )(*args)
    return (res[0], res[1]) if want_fin else (res[0], None)


def _merge_kernel(ya_ref, yb_ref, gate_ref, x_ref, mod_ref, wa_ref, wb_ref, wo_ref, n2_ref, x1_ref, h2_ref):
    m = mod_ref[0]
    br_a = jnp.dot(ya_ref[...], wa_ref[...], preferred_element_type=F32)
    br_b = jnp.dot(yb_ref[...], wb_ref[...], preferred_element_type=F32)
    gate = gate_ref[...].astype(F32)
    merged = _sigmoid(gate[:, 0:D_MODEL]) * br_a + _sigmoid(gate[:, D_MODEL:]) * br_b
    mix = jnp.dot(merged.astype(BF16), wo_ref[...], preferred_element_type=F32)
    x1 = x_ref[...] + m[2:3] * mix
    x1_ref[...] = x1
    h2_ref[...] = (_rms(x1, n2_ref[...]) * (1.0 + m[4:5]) + m[3:4]).astype(BF16)


def _merge(ya2, yb2, gate2, x2, mod, mod_base, mod_step, seq, wa, wb, wo, norm2_g):
    n = x2.shape[0]
    tm = ROW_TILE
    mod_idx = lambda i: (mod_base + mod_step * ((i * tm) // seq), 0, 0)
    row = lambda w: pl.BlockSpec((tm, w), lambda i: (i, 0))
    full = lambda a: pl.BlockSpec(a.shape, lambda i: (0,) * a.ndim)
    return pl.pallas_call(
        _merge_kernel,
        grid=(n // tm,),
        in_specs=[row(BRANCH_W), row(BRANCH_W), row(COLS_GATE), row(D_MODEL),
                  pl.BlockSpec((1, N_MOD, D_MODEL), mod_idx),
                  full(wa), full(wb), full(wo), full(norm2_g)],
        out_specs=[row(D_MODEL), row(D_MODEL)],
        out_shape=[jax.ShapeDtypeStruct((n, D_MODEL), F32), jax.ShapeDtypeStruct((n, D_MODEL), BF16)],
        compiler_params=_cparams("parallel"),
        name="merge_out",
    )(ya2, yb2, gate2, x2, mod, wa, wb, wo, norm2_g)


def _bitonic_pairs(n):
    pairs = []
    k = 2
    while k <= n:
        j = k // 2
        while j >= 1:
            for i in range(n):
                l = i ^ j
                if l > i:
                    pairs.append((i, l) if (i & k) == 0 else (l, i))
            j //= 2
        k *= 2
    return pairs


def _top_values_of_keys(s, k):
    r, t = s.shape
    nl = r // 8
    lv = [s[8 * j:8 * j + 8, :] for j in range(nl)]
    for a, b in _bitonic_pairs(nl):
        lv[a], lv[b] = jnp.maximum(lv[a], lv[b]), jnp.minimum(lv[a], lv[b])
    sid = lax.broadcasted_iota(jnp.int32, (8, t), 0)
    out = []
    for it in range(k):
        m = jnp.max(lv[0], axis=0, keepdims=True)
        out.append(m)
        first = jnp.min(jnp.where(lv[0] == m, sid, 8), axis=0, keepdims=True)
        pop = sid == first
        for j in range(min(nl, k - 1 - it)):
            lv[j] = jnp.where(pop, lv[j + 1] if j + 1 < nl else NEG_INF, lv[j])
    return out


def _peer_score_kernel(h_ref, wq_ref, keys_ref, thr_ref, g1_ref, s2_ref, e2_ref, cand_ref):
    qt = _mm_nt(wq_ref[...], h_ref[...])
    half = PEER_DQ // 2
    k1 = PEER_TOPK + 1
    pairs = [(i, j) for i in range(k1) for j in range(k1) if (i + 1) * (j + 1) <= k1]
    assert len(pairs) <= cand_ref.shape[0]
    for h in range(PEER_HEADS):
        s = [_mm(keys_ref[2 * h + c], qt[(2 * h + c) * half:(2 * h + c + 1) * half, :]) for c in range(2)]
        tops = [_top_values_of_keys(s[c], k1) for c in range(2)]
        cand_ref[...] = jnp.full(cand_ref.shape, NEG_INF, F32)
        for r, (i, j) in enumerate(pairs):
            cand_ref[r:r + 1, :] = tops[0][i] + tops[1][j]
        best = _top_values_of_keys(cand_ref[...], k1)
        mx = best[0]
        z = jnp.exp(best[0] - mx)
        for r in range(1, PEER_TOPK):
            z = z + jnp.exp(best[r] - mx)
        theta = 0.5 * (best[PEER_TOPK - 1] + best[PEER_TOPK])
        thr_ref[h] = theta - s[0]
        g1_ref[h] = jnp.exp(s[0] - tops[0][0]) * (0.5 / z)
        s2_ref[h] = s[1]
        e2_ref[h] = jnp.exp(s[1] - tops[1][0])


def _peer_scores(h2, wq_t, keys):
    n = h2.shape[0]
    tb = PEER_SCORE_TOKENS
    full = lambda a: pl.BlockSpec(a.shape, lambda i: (0,) * a.ndim)
    return pl.pallas_call(
        _peer_score_kernel,
        grid=(n // tb,),
        in_specs=[pl.BlockSpec((tb, D_MODEL), lambda i: (i, 0)), full(wq_t), full(keys)],
        out_specs=[pl.BlockSpec((PEER_HEADS, N_KEYS, tb), lambda i: (0, 0, i))] * 4,
        out_shape=[jax.ShapeDtypeStruct((PEER_HEADS, N_KEYS, n), F32)] * 4,
        scratch_shapes=[pltpu.VMEM((64, tb), F32)],
        compiler_params=_cparams("parallel"),
        name="peer_scores",
    )(h2, wq_t, keys)


def _gelu_tanh_x2(x):
    return x + x * jnp.tanh(x * (0.7978845608028654 + 0.035677408136300125 * (x * x)))


def _peer_expert_kernel(h_ref, u_ref, vt_ref, thr_ref, g1_ref, s2_ref, e2_ref, x1_ref, mod_ref, fg_ref, y_ref,
                        acc_ref, w_ref, *, rows_per_step):
    e = pl.program_id(1)
    tb = h_ref.shape[0]

    @pl.when(e == 0)
    def _():
        acc_ref[...] = jnp.zeros_like(acc_ref)

    st = _mm_nt(u_ref[...], h_ref[...])
    rows_per_chunk = 2
    partial = None
    for al in range(rows_per_step):
        a = e * rows_per_step + al
        r0 = al * N_KEYS
        thr = [thr_ref[h, pl.ds(a, 1), :] for h in range(PEER_HEADS)]
        g1 = [g1_ref[h, pl.ds(a, 1), :] for h in range(PEER_HEADS)]
        for lt in range(tb // LANES):
            ls = slice(lt * LANES, (lt + 1) * LANES)
            wsum = None
            for h in range(PEER_HEADS):
                w = jnp.where(s2_ref[h, :, ls] >= thr[h][:, ls], e2_ref[h, :, ls] * g1[h][:, ls], 0.0)
                wsum = w if wsum is None else wsum + w
            act = _gelu_tanh_x2(st[r0:r0 + N_KEYS, ls])
            w_ref[r0:r0 + N_KEYS, ls] = (wsum * act).astype(BF16)
        if (al + 1) % rows_per_chunk == 0:
            rs = slice((al + 1 - rows_per_chunk) * N_KEYS, (al + 1) * N_KEYS)
            d = jnp.dot(vt_ref[:, rs], w_ref[rs, :], preferred_element_type=F32)
            partial = d if partial is None else partial + d
    acc_ref[...] += partial

    @pl.when(e == pl.num_programs(1) - 1)
    def _():
        m = mod_ref[0]
        x2 = x1_ref[...] + m[5:6] * acc_ref[...].T
        y_ref[...] = _rms(x2, fg_ref[...])


def _peer_experts(h2, u_bf, vt_bf, stats, x1, mod, mod_base, mod_step, seq, final_g):
    n = h2.shape[0]
    tb = PEER_EXPERT_TOKENS if (mod_step == 0 or seq % PEER_EXPERT_TOKENS == 0) else seq
    rows_per_step = PEER_ROWS_PER_STEP
    ec = rows_per_step * N_KEYS
    mod_idx = lambda i, e: (mod_base + mod_step * ((i * tb) // seq), 0, 0)
    stat_spec = pl.BlockSpec((PEER_HEADS, N_KEYS, tb), lambda i, e: (0, 0, i))
    return pl.pallas_call(
        functools.partial(_peer_expert_kernel, rows_per_step=rows_per_step),
        grid=(n // tb, N_EXPERTS // ec),
        in_specs=[pl.BlockSpec((tb, D_MODEL), lambda i, e: (i, 0)),
                  pl.BlockSpec((ec, D_MODEL), lambda i, e: (e, 0)),
                  pl.BlockSpec((D_MODEL, ec), lambda i, e: (0, e)),
                  stat_spec, stat_spec, stat_spec, stat_spec,
                  pl.BlockSpec((tb, D_MODEL), lambda i, e: (i, 0)),
                  pl.BlockSpec((1, N_MOD, D_MODEL), mod_idx),
                  pl.BlockSpec((1, D_MODEL), lambda i, e: (0, 0))],
        out_specs=pl.BlockSpec((tb, D_MODEL), lambda i, e: (i, 0)),
        out_shape=jax.ShapeDtypeStruct((n, D_MODEL), F32),
        scratch_shapes=[pltpu.VMEM((D_MODEL, tb), F32), pltpu.VMEM((ec, tb), BF16)],
        compiler_params=_cparams("parallel", "arbitrary"),
        name="peer_experts",
    )(h2, u_bf, vt_bf, *stats, x1, mod, final_g)


def _rope_tables(seq):
    pos = jnp.arange(seq, dtype=jnp.int32)
    lane = jnp.arange(LANES, dtype=jnp.int32) % HEAD_DIM
    use_col = (lane // 32) == 1
    p = jnp.where(use_col[None, :], (pos % GRID_W)[:, None], (pos // GRID_W)[:, None]).astype(F32)
    inv = ROPE_BASE ** (-(lane % 16).astype(F32) / 16.0)
    ang = p * inv[None, :]
    first = (lane % 32) < 16
    return jnp.cos(ang), jnp.where(first[None, :], -jnp.sin(ang), jnp.sin(ang))


def _trunk_path(x, mod, mod_base, mod_step, rope_tabs, s_ret0, s_rwkv0, want_fin, lw, final_g):
    b, seq, _ = x.shape
    x2 = x.reshape(b * seq, D_MODEL)
    mix, lora, gate = _in_proj(x2, mod, mod_base, mod_step, seq, lw['norm1_g'], lw['w_in'])
    mix3 = mix.reshape(b, seq, COLS_MIX)
    lora3 = lora.reshape(b, seq, COLS_LORA)
    ya, ret_fin = _retention(mix3, lw['ret_decay'], lw['ret_gn_w'], lw['ret_gn_b'], rope_tabs, s_ret0, want_fin)
    yb, rwkv_fin = _rwkv(mix3, lora3, lw['rwkv'], s_rwkv0, want_fin)
    x1, h2 = _merge(ya.reshape(b * seq, BRANCH_W), yb.reshape(b * seq, BRANCH_W), gate, x2, mod, mod_base,
                    mod_step, seq, lw['w_br_a'], lw['w_br_b'], lw['w_out'], lw['norm2_g'])
    stats = _peer_scores(h2, lw['peer_wq_t'], lw['peer_keys'])
    y = _peer_experts(h2, lw['peer_u'], lw['peer_vt'], stats, x1, mod, mod_base, mod_step, seq, final_g)
    return y.reshape(b, seq, D_MODEL), ret_fin, rwkv_fin


def kernel(x_prompt, x_sample, state_ret, state_rwkv, c, c_ctx, ada_w, ada_b, norm1_g, w_in, ret_decay, ret_gn_w, ret_gn_b, rwkv_conv, rwkv_w0, rwkv_w2, rwkv_a0, rwkv_a2, rwkv_g2, rwkv_k_k, rwkv_k_a, rwkv_r_k, rwkv_gn_w, rwkv_gn_b, w_br_a, w_br_b, w_out, norm2_g, peer_wq, peer_keys, peer_u, peer_v, final_norm_g):
    assert w_in.shape[0] == 1, "the final norm is fused into the layer's last kernel: single trunk layer only"
    row = lambda a: a.reshape(1, -1)
    cc = jnp.concatenate([c_ctx[None, :], c], axis=0)
    cc = jnp.pad(cc, ((0, (-cc.shape[0]) % 8), (0, 0)))
    rope_tabs = _rope_tables(x_sample.shape[1])
    final_g = row(final_norm_g)

    l = 0
    lw = {
        'norm1_g': row(norm1_g[l]), 'w_in': w_in[l].astype(BF16),
        'ret_decay': jnp.repeat(ret_decay[l], HEAD_DIM, axis=1),
        'ret_gn_w': row(ret_gn_w[l]), 'ret_gn_b': row(ret_gn_b[l]),
        'rwkv': {'conv': rwkv_conv[l], 'w0': rwkv_w0[l], 'w2': rwkv_w2[l].astype(BF16), 'a0': rwkv_a0[l],
                 'a2': rwkv_a2[l].astype(BF16), 'g2': rwkv_g2[l].astype(BF16), 'k_k': row(rwkv_k_k[l]),
                 'k_a': row(rwkv_k_a[l]), 'r_k': row(rwkv_r_k[l]), 'gn_w': row(rwkv_gn_w[l]),
                 'gn_b': row(rwkv_gn_b[l])},
        'w_br_a': w_br_a[l].astype(BF16), 'w_br_b': w_br_b[l].astype(BF16), 'w_out': w_out[l].astype(BF16),
        'norm2_g': row(norm2_g[l]),
        'peer_wq_t': peer_wq[l].astype(BF16).T,
        'peer_keys': peer_keys[l].reshape(2 * PEER_HEADS, N_KEYS, PEER_DQ // 2).astype(BF16),
        'peer_u': peer_u[l].astype(BF16), 'peer_vt': peer_v[l].astype(BF16).T,
    }
    mod = _adaln(cc, ada_w[l], row(ada_b[l])).reshape(cc.shape[0], N_MOD, D_MODEL)
    yp, ret_fin, rwkv_fin = _trunk_path(x_prompt, mod, 0, 0, None, None, None, True, lw, final_g)
    ys, _, _ = _trunk_path(x_sample, mod, 1, 1, rope_tabs, state_ret[:, l], state_rwkv[:, l], False, lw, final_g)
    return (yp, ys, ret_fin[:, None], rwkv_fin[:, None])
```

```python
import functools

import jax
import jax.numpy as jnp
from jax import lax
from jax.experimental import pallas as pl
from jax.experimental.pallas import tpu as pltpu

F32 = jnp.float32
BF16 = jnp.bfloat16

D_MODEL = 1024
GRID_W = 64
N_MOD = 6
HEADS = 8
HEAD_DIM = 64
BRANCH_W = HEADS * HEAD_DIM
RET_CHUNK = 128
RWKV_CHUNK = 64
LORA_W = 64
LORA_A = 64
LORA_G = 128
PEER_HEADS = 8
N_KEYS = 128
N_EXPERTS = N_KEYS * N_KEYS
PEER_DQ = 256
PEER_TOPK = 16
ROPE_BASE = 10000.0
NORM_EPS = 1e-6
GN_EPS = 64e-5
COLS_MIX = 7 * BRANCH_W
COLS_LORA = LORA_W + LORA_A + LORA_G
COLS_GATE = 2 * D_MODEL
IN_COLS = COLS_MIX + COLS_LORA + COLS_GATE

V7X_VMEM_LIMIT_BYTES = 56 * 1024 * 1024
LANES = 128
ROW_TILE = 256
MERGE_ROW_TILE = 512
PEER_SCORE_TOKENS = 256
PEER_EXPERT_TOKENS = 512
PEER_ROWS_PER_STEP = 8
NEG_INF = float("-inf")


def _cparams(*sem):
    return pltpu.CompilerParams(dimension_semantics=sem, vmem_limit_bytes=V7X_VMEM_LIMIT_BYTES)


def _mm(a, b):
    return jnp.dot(a.astype(BF16), b.astype(BF16), preferred_element_type=F32)


def _mm_nt(a, b):
    return lax.dot_general(a.astype(BF16), b.astype(BF16), (((1,), (1,)), ((), ())),
                           preferred_element_type=F32)


def _mm_tn(a, b):
    return lax.dot_general(a.astype(BF16), b.astype(BF16), (((0,), (0,)), ((), ())),
                           preferred_element_type=F32)


def _mm_tn_3pass(a, b):
    a_hi, b_hi = a.astype(BF16), b.astype(BF16)
    a_lo = (a - a_hi.astype(F32)).astype(BF16)
    b_lo = (b - b_hi.astype(F32)).astype(BF16)
    dims = (((0,), (0,)), ((), ()))
    dot = lambda x, y: lax.dot_general(x, y, dims, preferred_element_type=F32)
    return dot(a_hi, b_hi) + (dot(a_hi, b_lo) + dot(a_lo, b_hi))


def _sigmoid(x):
    return 1.0 / (1.0 + jnp.exp(-x))


def _rms(x, g):
    return x * lax.rsqrt(jnp.mean(x * x, axis=-1, keepdims=True) + NORM_EPS) * g


def _head_sum(x):
    t, w = x.shape
    lo = lax.broadcasted_iota(jnp.int32, (t, LANES), 1) < HEAD_DIM
    outs = []
    for j in range(w // LANES):
        xt = x[:, j * LANES:(j + 1) * LANES]
        s_lo = jnp.sum(jnp.where(lo, xt, 0.0), axis=-1, keepdims=True)
        s_hi = jnp.sum(jnp.where(lo, 0.0, xt), axis=-1, keepdims=True)
        outs.append(jnp.where(lo, s_lo, s_hi))
    return outs[0] if len(outs) == 1 else jnp.concatenate(outs, axis=-1)


def _head_norm(y, w, b):
    mu = _head_sum(y) * (1.0 / HEAD_DIM)
    d = y - mu
    var = _head_sum(d * d) * (1.0 / HEAD_DIM)
    return d * lax.rsqrt(var + GN_EPS) * w + b


def _mod_kernel(c_ref, w_ref, b_ref, o_ref):
    c = c_ref[...]
    o_ref[...] = _mm(c * _sigmoid(c), w_ref[...]) + b_ref[...]


def _adaln(cc, ada_w, ada_b):
    rows = cc.shape[0]
    n = ada_w.shape[1]
    tn = n // 4
    return pl.pallas_call(
        _mod_kernel,
        grid=(n // tn,),
        in_specs=[pl.BlockSpec((rows, D_MODEL), lambda j: (0, 0)),
                  pl.BlockSpec((D_MODEL, tn), lambda j: (0, j)),
                  pl.BlockSpec((1, tn), lambda j: (0, j))],
        out_specs=pl.BlockSpec((rows, tn), lambda j: (0, j)),
        out_shape=jax.ShapeDtypeStruct((rows, n), F32),
        compiler_params=_cparams("parallel"),
        name="adaln_mod",
    )(cc, ada_w, ada_b)


def _in_kernel(x_ref, mod_ref, g_ref, w_ref, mix_ref, lora_ref, gate_ref):
    m = mod_ref[0]
    h = (_rms(x_ref[...], g_ref[...]) * (1.0 + m[1:2]) + m[0:1]).astype(BF16)
    mix_ref[...] = jnp.dot(h, w_ref[:, 0:COLS_MIX], preferred_element_type=F32)
    lora_ref[...] = jnp.dot(h, w_ref[:, COLS_MIX:COLS_MIX + COLS_LORA], preferred_element_type=F32)
    gate_ref[...] = jnp.dot(h, w_ref[:, COLS_MIX + COLS_LORA:IN_COLS], preferred_element_type=F32).astype(BF16)


def _in_proj(x2, mod, mod_base, mod_step, seq, norm_g, w_in_bf):
    n = x2.shape[0]
    tm = ROW_TILE
    mod_idx = lambda i: (mod_base + mod_step * ((i * tm) // seq), 0, 0)
    return pl.pallas_call(
        _in_kernel,
        grid=(n // tm,),
        in_specs=[pl.BlockSpec((tm, D_MODEL), lambda i: (i, 0)),
                  pl.BlockSpec((1, N_MOD, D_MODEL), mod_idx),
                  pl.BlockSpec((1, D_MODEL), lambda i: (0, 0)),
                  pl.BlockSpec((D_MODEL, IN_COLS), lambda i: (0, 0))],
        out_specs=[pl.BlockSpec((tm, COLS_MIX), lambda i: (i, 0)),
                   pl.BlockSpec((tm, COLS_LORA), lambda i: (i, 0)),
                   pl.BlockSpec((tm, COLS_GATE), lambda i: (i, 0))],
        out_shape=[jax.ShapeDtypeStruct((n, COLS_MIX), F32),
                   jax.ShapeDtypeStruct((n, COLS_LORA), F32),
                   jax.ShapeDtypeStruct((n, COLS_GATE), BF16)],
        compiler_params=_cparams("parallel"),
        name="in_proj",
    )(x2, mod, norm_g, w_in_bf)


def _rope(x, cos, sin):
    lane = lax.broadcasted_iota(jnp.int32, x.shape, 1)
    first = (lane % 32) < 16
    partner = jnp.where(first, pltpu.roll(x, LANES - 16, 1), pltpu.roll(x, 16, 1))
    return x * cos + partner * sin


def _ret_kernel(*refs, seq, rope, has_init, want_fin):
    it = iter(refs)
    q_ref, k_ref, v_ref, g_ref, rd_ref, gw_ref, gb_ref = (next(it) for _ in range(7))
    cos_ref = next(it) if rope else None
    sin_ref = next(it) if rope else None
    s0_ref = next(it) if has_init else None
    out_ref = next(it)
    fin_ref = next(it) if want_fin else None
    qs_ref, ks_ref, vec_ref, dec_ref, r_ref, y_ref = (next(it) for _ in range(6))

    C = RET_CHUNK
    W = BRANCH_W
    n = seq // C
    rd = rd_ref[...]
    lg = jnp.minimum(rd, 0.0) - jnp.log(1.0 + jnp.exp(-jnp.abs(rd)))
    lgf, lgb = lg[0:1, :], lg[1:2, :]
    ii = lax.broadcasted_iota(jnp.int32, (C, C), 0)
    jj = lax.broadcasted_iota(jnp.int32, (C, C), 1)
    diff = (ii - jj).astype(F32)
    col = lax.broadcasted_iota(jnp.int32, (C, W), 0).astype(F32)
    ln = lambda h: slice(h * HEAD_DIM, (h + 1) * HEAD_DIM)

    vec_ref[0] = jnp.exp((col + 1.0) * lgf)
    vec_ref[1] = jnp.exp((C - 1.0 - col) * lgf)
    vec_ref[2] = jnp.exp((C - col) * lgb)
    vec_ref[3] = jnp.exp(col * lgb)
    chunk_f = jnp.exp(C * lgf)
    chunk_b = jnp.exp(C * lgb)
    for h in range(HEADS):
        gf = lgf[:, h * HEAD_DIM:h * HEAD_DIM + 1]
        gb = lgb[:, h * HEAD_DIM:h * HEAD_DIM + 1]
        dec_ref[h] = (jnp.where(diff >= 0, jnp.exp(jnp.maximum(diff, 0.0) * gf), 0.0)
                      + jnp.where(diff <= 0, jnp.exp(jnp.maximum(-diff, 0.0) * gb), 0.0))
        for d in range(2):
            r_ref[d, h] = s0_ref[0, d, h] if has_init else jnp.zeros((HEAD_DIM, HEAD_DIM), F32)

    def rows(c):
        return pl.ds(pl.multiple_of(c * C, C), C)

    def prep(c, carry):
        r = rows(c)
        q = q_ref[0, r, :]
        k = k_ref[0, r, :] * (HEAD_DIM ** -0.5)
        if rope:
            cos, sin = cos_ref[r, :], sin_ref[r, :]
            tiles = lambda x: [x[:, j * LANES:(j + 1) * LANES] for j in range(W // LANES)]
            q = jnp.concatenate([_rope(t, cos, sin) for t in tiles(q)], axis=1)
            k = jnp.concatenate([_rope(t, cos, sin) for t in tiles(k)], axis=1)
        qs_ref[r, :] = q
        ks_ref[r, :] = k
        y_ref[0, r, :] = jnp.zeros((C, W), F32)
        return carry

    lax.fori_loop(0, n, prep, 0)

    cpi = 2 if n % 4 == 0 else 1

    def step(i, carry):
        rf = [rows(i * cpi + u) for u in range(cpi)]
        rb = [rows(n - 1 - (i * cpi + u)) for u in range(cpi)]
        intra, qfx, inc_f, qbx, inc_b = ([[None] * HEADS for _ in range(cpi)] for _ in range(5))
        for u in range(cpi):
            qf, kf, vf = qs_ref[rf[u], :], ks_ref[rf[u], :], v_ref[0, rf[u], :]
            qb, kb, vb = qs_ref[rb[u], :], ks_ref[rb[u], :], v_ref[0, rb[u], :]
            qfs, kfs = qf * vec_ref[0], kf * vec_ref[1]
            qbs, kbs = qb * vec_ref[2], kb * vec_ref[3]
            for h in range(HEADS):
                sc = _mm_nt(qf[:, ln(h)], kf[:, ln(h)]) * dec_ref[h]
                intra[u][h] = _mm(sc, vf[:, ln(h)])
                qfx[u][h], qbx[u][h] = qfs[:, ln(h)], qbs[:, ln(h)]
                inc_f[u][h] = _mm_tn(kfs[:, ln(h)], vf[:, ln(h)])
                inc_b[u][h] = _mm_tn(kbs[:, ln(h)], vb[:, ln(h)])
        sf = [r_ref[0, h] for h in range(HEADS)]
        sb = [r_ref[1, h] for h in range(HEADS)]
        of, ob = ([[None] * HEADS for _ in range(cpi)] for _ in range(2))
        for u in range(cpi):
            for h in range(HEADS):
                of[u][h] = intra[u][h] + _mm(qfx[u][h], sf[h])
                sf[h] = sf[h] * chunk_f[:, ln(h)] + inc_f[u][h]
                ob[u][h] = _mm(qbx[u][h], sb[h])
                sb[h] = sb[h] * chunk_b[:, ln(h)] + inc_b[u][h]
        yf = [y_ref[0, rf[u], :] for u in range(cpi)]
        yb = [y_ref[0, rb[u], :] for u in range(cpi)]
        for u in range(cpi):
            y_ref[0, rf[u], :] = yf[u] + jnp.concatenate(of[u], axis=1)
            y_ref[0, rb[u], :] = yb[u] + jnp.concatenate(ob[u], axis=1)
        for h in range(HEADS):
            r_ref[0, h] = sf[h]
            r_ref[1, h] = sb[h]
        return carry

    lax.fori_loop(0, n // cpi, step, 0)
    if want_fin:
        for d in range(2):
            for h in range(HEADS):
                fin_ref[0, d, h] = r_ref[d, h]

    def post(c, carry):
        r = rows(c)
        g = g_ref[0, r, :]
        y = _head_norm(y_ref[0, r, :], gw_ref[...], gb_ref[...]) * (g * _sigmoid(g))
        out_ref[0, r, :] = y.astype(BF16)
        return carry

    lax.fori_loop(0, n, post, 0)


def _retention(mix3, rd_l, gn_w, gn_b, rope_tabs, s0, want_fin):
    b, seq, _ = mix3.shape
    W = BRANCH_W
    rope = rope_tabs is not None
    has_init = s0 is not None
    col_spec = lambda j: pl.BlockSpec((1, seq, W), lambda i: (i, 0, j))
    full = lambda a: pl.BlockSpec(a.shape, lambda i: (0,) * a.ndim)
    in_specs = [col_spec(0), col_spec(1), col_spec(2), col_spec(3), full(rd_l), full(gn_w), full(gn_b)]
    args = [mix3, mix3, mix3, mix3, rd_l, gn_w, gn_b]
    if rope:
        in_specs += [full(rope_tabs[0]), full(rope_tabs[1])]
        args += list(rope_tabs)
    state_spec = pl.BlockSpec((1, 2, HEADS, HEAD_DIM, HEAD_DIM), lambda i: (i, 0, 0, 0, 0))
    if has_init:
        in_specs.append(state_spec)
        args.append(s0)
    out_specs = [pl.BlockSpec((1, seq, W), lambda i: (i, 0, 0))]
    out_shape = [jax.ShapeDtypeStruct((b, seq, W), BF16)]
    if want_fin:
        out_specs.append(state_spec)
        out_shape.append(jax.ShapeDtypeStruct((b, 2, HEADS, HEAD_DIM, HEAD_DIM), F32))
    res = pl.pallas_call(
        functools.partial(_ret_kernel, seq=seq, rope=rope, has_init=has_init, want_fin=want_fin),
        grid=(b,),
        in_specs=in_specs,
        out_specs=out_specs,
        out_shape=out_shape,
        scratch_shapes=[pltpu.VMEM((seq, W), F32), pltpu.VMEM((seq, W), F32),
                        pltpu.VMEM((4, RET_CHUNK, W), F32), pltpu.VMEM((HEADS, RET_CHUNK, RET_CHUNK), F32),
                        pltpu.VMEM((2, HEADS, HEAD_DIM, HEAD_DIM), F32), pltpu.VMEM((1, seq, W), F32)],
        compiler_params=_cparams("parallel"),
        name="retention",
    )(*args)
    return (res[0], res[1]) if want_fin else (res[0], None)


EXP_M_HALF = 0.6065306597126334


def _rwkv_kernel(*refs, seq, nb, has_init, want_fin):
    it = iter(refs)
    (r_ref, k_ref, v_ref, lora_ref, cw_ref, w0_ref, w2_ref, a0_ref, a2_ref, g2_ref,
     kk_w_ref, ka_ref, rk_ref, gw_ref, gb_ref) = (next(it) for _ in range(15))
    s0_ref = next(it) if has_init else None
    out_ref = next(it)
    fin_ref = next(it) if want_fin else None
    r_s, v_s, kk_s, g_s, bv_s, lw_s, b_s, kd_s, st_s, y_ref = (next(it) for _ in range(10))

    C = RWKV_CHUNK
    n = seq // C
    W = BRANCH_W
    row_id = lax.broadcasted_iota(jnp.int32, (C, W), 0)

    def rows(c):
        return pl.ds(pl.multiple_of(c * C, C), C)

    def conv(ref, s, c, w):
        x = ref[s, rows(c), :]
        prev8 = ref[s, pl.ds(pl.multiple_of(jnp.maximum(c * C - 8, 0), 8), 8), :]
        next8 = ref[s, pl.ds(pl.multiple_of(jnp.minimum(c * C + C, seq - 8), 8), 8), :]
        prev_row = jnp.where(c > 0, prev8[7:8, :], 0.0)
        next_row = jnp.where(c < n - 1, next8[0:1, :], 0.0)
        xm = jnp.where(row_id == 0, prev_row, pltpu.roll(x, 1, 0))
        xp = jnp.where(row_id == C - 1, next_row, pltpu.roll(x, C - 1, 0))
        return w[0:1] * xm + w[1:2] * x + w[2:3] * xp

    def prep_seq(s, c):
        rws = rows(c)
        cw = cw_ref[...]
        r = conv(r_ref, s, c, cw[:, 0:W])
        k = conv(k_ref, s, c, cw[:, W:2 * W])
        v = conv(v_ref, s, c, cw[:, 2 * W:3 * W])
        lo = lora_ref[s, rws, :]
        dw = lo[:, 0:LORA_W]
        da = lo[:, LORA_W:LORA_W + LORA_A]
        dg = lo[:, LORA_W + LORA_A:]
        kk = k * kk_w_ref[...]
        kk = kk * lax.rsqrt(_head_sum(kk * kk) + 1e-12)
        y_ref[s, rws, :] = jnp.zeros((C, W), F32)
        r_s[s, rws, :] = r
        v_s[s, rws, :] = v
        kk_s[s, rws, :] = kk
        g_s[s, rws, :] = _mm(_sigmoid(dg), g2_ref[...])
        bv_s[s, rws, :] = _head_sum(r * k * rk_ref[...]) * v
        tdw = jnp.tanh(dw)
        kka = k * ka_ref[...]
        for d in range(2):
            z = w0_ref[d:d + 1, :] + _mm(tdw, w2_ref[d])
            a = _sigmoid(a0_ref[d:d + 1, :] + _mm(da, a2_ref[d]))
            lw_s[s, d, rws, :] = -EXP_M_HALF * _sigmoid(z)
            b_s[s, d, rws, :] = kk * a
            kd_s[s, d, rws, :] = k + kka * (a - 1.0)

    def prep(c, carry):
        for s in range(nb):
            prep_seq(s, c)
        return carry

    lax.fori_loop(0, n, prep, 0)

    ii = lax.broadcasted_iota(jnp.int32, (C, C), 0)
    jj = lax.broadcasted_iota(jnp.int32, (C, C), 1)
    incl = [ii >= jj, ii <= jj]
    strict = [ii > jj, ii < jj]

    cpi = 2 if (nb == 1 and n % 4 == 0) else 1
    streams = [(s, d, u) for s in range(nb) for d in range(2) for u in range(cpi)]
    for s in range(nb):
        for d in range(2):
            for h in range(HEADS):
                st_s[s, d, h] = s0_ref[s, d, h] if has_init else jnp.zeros((HEAD_DIM, HEAD_DIM), F32)

    def chunk(i, carry):
        rws, at, bt, kt, rt, bh, kh, etot, vc = ([None] * len(streams) for _ in range(9))
        for q, (s, d, u) in enumerate(streams):
            rws[q] = rows(i * cpi + u if d == 0 else n - 1 - (i * cpi + u))
            rc, vc[q], kkc = r_s[s, rws[q], :], v_s[s, rws[q], :], kk_s[s, rws[q], :]
            lwc, bc, kc = lw_s[s, d, rws[q], :], b_s[s, d, rws[q], :], kd_s[s, d, rws[q], :]
            cum = lwc
            for k in (1 << p for p in range((C - 1).bit_length())):
                if d == 0:
                    cum = cum + jnp.where(row_id >= k, pltpu.roll(cum, k, 0), 0.0)
                else:
                    cum = cum + jnp.where(row_id < C - k, pltpu.roll(cum, C - k, 0), 0.0)
            tot = cum[C - 1:C, :] if d == 0 else cum[0:1, :]
            pinv = jnp.exp(-cum)
            pend = jnp.exp(tot - cum)
            at[q] = -kkc * jnp.exp(cum - lwc)
            bt[q] = bc * pinv
            kt[q] = kc * pinv
            rt[q] = rc * jnp.exp(cum)
            bh[q] = bc * pend
            kh[q] = kc * pend
            etot[q] = jnp.exp(tot)
        chains = [(q, h) for q in range(len(streams)) for h in range(HEADS)]
        dirn = lambda q: streams[q][1]
        ln = lambda h: slice(h * HEAD_DIM, (h + 1) * HEAD_DIM)
        wk = [jnp.concatenate([bt[q][:, ln(h)], kt[q][:, ln(h)]], axis=0) for q, h in chains]
        ma = [_mm_nt(at[q][:, ln(h)], wk[j]) for j, (q, h) in enumerate(chains)]
        mr = [_mm_nt(rt[q][:, ln(h)], wk[j]) for j, (q, h) in enumerate(chains)]
        a_ak = [jnp.where(strict[dirn(q)], ma[j][:, C:2 * C], 0.0) for j, (q, h) in enumerate(chains)]
        p = [jnp.where(strict[dirn(q)], ma[j][:, 0:C], 0.0) for j, (q, h) in enumerate(chains)]
        a_rb = [jnp.where(incl[dirn(q)], mr[j][:, 0:C], 0.0) for j, (q, h) in enumerate(chains)]
        a_rk = [jnp.where(incl[dirn(q)], mr[j][:, C:2 * C], 0.0) for j, (q, h) in enumerate(chains)]
        vh = [vc[q][:, ln(h)] for q, h in chains]
        x = [jnp.concatenate([at[q][:, ln(h)], _mm(a_ak[j], vh[j])], axis=1)
             for j, (q, h) in enumerate(chains)]
        for step in range(6):
            pb = [p[j].astype(BF16) for j in range(len(chains))]
            xh = [x[j].astype(BF16) for j in range(len(chains))]
            if step < 3:
                xl = [(x[j] - xh[j].astype(F32)).astype(BF16) for j in range(len(chains))]
                px = [jnp.dot(pb[j], jnp.concatenate([xh[j], xl[j]], axis=1), preferred_element_type=F32)
                      for j in range(len(chains))]
                x = [x[j] + (px[j][:, 0:2 * HEAD_DIM] + px[j][:, 2 * HEAD_DIM:]) for j in range(len(chains))]
            else:
                x = [x[j] + jnp.dot(pb[j], xh[j], preferred_element_type=F32) for j in range(len(chains))]
            if step < 5:
                p = [jnp.dot(pb[j], pb[j], preferred_element_type=F32) for j in range(len(chains))]
        gh = [_mm_tn(x[j], bh[q][:, ln(h)]) for j, (q, h) in enumerate(chains)]
        vk = [_mm_tn_3pass(vh[j], kh[q][:, ln(h)]) for j, (q, h) in enumerate(chains)]
        ax = [_mm(a_rb[j], x[j]) for j in range(len(chains))]
        qt = [rt[q][:, ln(h)] + ax[j][:, 0:HEAD_DIM] for j, (q, h) in enumerate(chains)]
        y0 = [ax[j][:, HEAD_DIM:2 * HEAD_DIM] + _mm(a_rk[j], vh[j]) for j in range(len(chains))]
        ys = [None] * len(chains)
        scans = [(s, d, h) for s in range(nb) for d in range(2) for h in range(HEADS)]
        st = {k: st_s[k] for k in scans}
        for u in range(cpi):
            for s, d, h in scans:
                q = streams.index((s, d, u))
                j = q * HEADS + h
                cur = st[s, d, h]
                ys[j] = _mm_nt(qt[j], cur) + y0[j]
                st[s, d, h] = (cur * etot[q][:, ln(h)] + _mm(cur, gh[j][0:HEAD_DIM, :])
                               + gh[j][HEAD_DIM:2 * HEAD_DIM, :] + vk[j])
        for k in scans:
            st_s[k] = st[k]
        for q, (s, d, u) in enumerate(streams):
            y = jnp.concatenate(ys[q * HEADS:(q + 1) * HEADS], axis=1)
            y_ref[s, rws[q], :] = y_ref[s, rws[q], :] + y
        return carry

    lax.fori_loop(0, n // cpi, chunk, 0)
    if want_fin:
        for s in range(nb):
            for d in range(2):
                for h in range(HEADS):
                    fin_ref[s, d, h] = st_s[s, d, h]

    def post(c, carry):
        rws = rows(c)
        for s in range(nb):
            y = _head_norm(y_ref[s, rws, :], gw_ref[...], gb_ref[...])
            out_ref[s, rws, :] = ((y + bv_s[s, rws, :]) * g_s[s, rws, :]).astype(BF16)
        return carry

    lax.fori_loop(0, n, post, 0)


def _rwkv(mix3, lora3, lw, s0, want_fin):
    b, seq, _ = mix3.shape
    has_init = s0 is not None
    W = BRANCH_W
    nb = 2 if (seq <= 256 and b % 2 == 0) else 1
    col_spec = lambda j: pl.BlockSpec((nb, seq, W), lambda i: (i, 0, j))
    full = lambda a: pl.BlockSpec(a.shape, lambda i: (0,) * a.ndim)
    weights = [lw['conv'], lw['w0'], lw['w2'], lw['a0'], lw['a2'], lw['g2'],
               lw['k_k'], lw['k_a'], lw['r_k'], lw['gn_w'], lw['gn_b']]
    in_specs = [col_spec(4), col_spec(5), col_spec(6),
                pl.BlockSpec((nb, seq, COLS_LORA), lambda i: (i, 0, 0))] + [full(a) for a in weights]
    args = [mix3, mix3, mix3, lora3] + weights
    state_spec = pl.BlockSpec((nb, 2, HEADS, HEAD_DIM, HEAD_DIM), lambda i: (i, 0, 0, 0, 0))
    if has_init:
        in_specs.append(state_spec)
        args.append(s0)
    out_specs = [pl.BlockSpec((nb, seq, W), lambda i: (i, 0, 0))]
    out_shape = [jax.ShapeDtypeStruct((b, seq, W), BF16)]
    if want_fin:
        out_specs.append(state_spec)
        out_shape.append(jax.ShapeDtypeStruct((b, 2, HEADS, HEAD_DIM, HEAD_DIM), F32))
    sw = pltpu.VMEM((nb, seq, W), F32)
    sw2 = pltpu.VMEM((nb, 2, seq, W), F32)
    res = pl.pallas_call(
        functools.partial(_rwkv_kernel, seq=seq, nb=nb, has_init=has_init, want_fin=want_fin),
        grid=(b // nb,),
        in_specs=in_specs,
        out_specs=out_specs,
        out_shape=out_shape,
        scratch_shapes=[sw, sw, sw, sw, sw, sw2, sw2, sw2,
                        pltpu.VMEM((nb, 2, HEADS, HEAD_DIM, HEAD_DIM), F32), sw],
        compiler_params=_cparams("parallel"),
        name="rwkv7",
    )(*args)
    return (res[0], res[1]) if want_fin else (res[0], None)


def _merge_kernel(ya_ref, yb_ref, gate_ref, x_ref, mod_ref, wa_ref, wb_ref, wo_ref, n2_ref, x1_ref, h2_ref):
    m = mod_ref[0]
    br_a = jnp.dot(ya_ref[...], wa_ref[...], preferred_element_type=F32)
    br_b = jnp.dot(yb_ref[...], wb_ref[...], preferred_element_type=F32)
    gate = gate_ref[...].astype(F32)
    merged = _sigmoid(gate[:, 0:D_MODEL]) * br_a + _sigmoid(gate[:, D_MODEL:]) * br_b
    mix = jnp.dot(merged.astype(BF16), wo_ref[...], preferred_element_type=F32)
    x1 = x_ref[...] + m[2:3] * mix
    x1_ref[...] = x1
    h2_ref[...] = (_rms(x1, n2_ref[...]) * (1.0 + m[4:5]) + m[3:4]).astype(BF16)


def _merge(ya2, yb2, gate2, x2, mod, mod_base, mod_step, seq, wa, wb, wo, norm2_g):
    n = x2.shape[0]
    tm = MERGE_ROW_TILE
    mod_idx = lambda i: (mod_base + mod_step * ((i * tm) // seq), 0, 0)
    row = lambda w: pl.BlockSpec((tm, w), lambda i: (i, 0))
    full = lambda a: pl.BlockSpec(a.shape, lambda i: (0,) * a.ndim)
    return pl.pallas_call(
        _merge_kernel,
        grid=(n // tm,),
        in_specs=[row(BRANCH_W), row(BRANCH_W), row(COLS_GATE), row(D_MODEL),
                  pl.BlockSpec((1, N_MOD, D_MODEL), mod_idx),
                  full(wa), full(wb), full(wo), full(norm2_g)],
        out_specs=[row(D_MODEL), row(D_MODEL)],
        out_shape=[jax.ShapeDtypeStruct((n, D_MODEL), F32), jax.ShapeDtypeStruct((n, D_MODEL), BF16)],
        compiler_params=_cparams("parallel"),
        name="merge_out",
    )(ya2, yb2, gate2, x2, mod, wa, wb, wo, norm2_g)


def _bitonic_pairs(n):
    pairs = []
    k = 2
    while k <= n:
        j = k // 2
        while j >= 1:
            for i in range(n):
                l = i ^ j
                if l > i:
                    pairs.append((i, l) if (i & k) == 0 else (l, i))
            j //= 2
        k *= 2
    return pairs


def _top_values_of_keys(s, k):
    r, t = s.shape
    nl = r // 8
    lv = [s[8 * j:8 * j + 8, :] for j in range(nl)]
    for a, b in _bitonic_pairs(nl):
        lv[a], lv[b] = jnp.maximum(lv[a], lv[b]), jnp.minimum(lv[a], lv[b])
    sid = lax.broadcasted_iota(jnp.int32, (8, t), 0)
    out = []
    for it in range(k):
        m = jnp.max(lv[0], axis=0, keepdims=True)
        out.append(m)
        first = jnp.min(jnp.where(lv[0] == m, sid, 8), axis=0, keepdims=True)
        pop = sid == first
        for j in range(min(nl, k - 1 - it)):
            lv[j] = jnp.where(pop, lv[j + 1] if j + 1 < nl else NEG_INF, lv[j])
    return out


def _peer_score_kernel(h_ref, wq_ref, keys_ref, thr_ref, g1_ref, s2_ref, e2_ref, cand_ref):
    qt = _mm_nt(wq_ref[...], h_ref[...])
    half = PEER_DQ // 2
    k1 = PEER_TOPK + 1
    pairs = [(i, j) for i in range(k1) for j in range(k1) if (i + 1) * (j + 1) <= k1]
    assert len(pairs) <= cand_ref.shape[0]
    for h in range(PEER_HEADS):
        s = [_mm(keys_ref[2 * h + c], qt[(2 * h + c) * half:(2 * h + c + 1) * half, :]) for c in range(2)]
        tops = [_top_values_of_keys(s[c], k1) for c in range(2)]
        cand_ref[...] = jnp.full(cand_ref.shape, NEG_INF, F32)
        for r, (i, j) in enumerate(pairs):
            cand_ref[r:r + 1, :] = tops[0][i] + tops[1][j]
        best = _top_values_of_keys(cand_ref[...], k1)
        mx = best[0]
        z = jnp.exp(best[0] - mx)
        for r in range(1, PEER_TOPK):
            z = z + jnp.exp(best[r] - mx)
        theta = 0.5 * (best[PEER_TOPK - 1] + best[PEER_TOPK])
        thr_ref[h] = theta - s[0]
        g1_ref[h] = jnp.exp(s[0] - tops[0][0]) * (0.5 / z)
        s2_ref[h] = s[1]
        e2_ref[h] = jnp.exp(s[1] - tops[1][0])


def _peer_scores(h2, wq_t, keys):
    n = h2.shape[0]
    tb = PEER_SCORE_TOKENS
    full = lambda a: pl.BlockSpec(a.shape, lambda i: (0,) * a.ndim)
    return pl.pallas_call(
        _peer_score_kernel,
        grid=(n // tb,),
        in_specs=[pl.BlockSpec((tb, D_MODEL), lambda i: (i, 0)), full(wq_t), full(keys)],
        out_specs=[pl.BlockSpec((PEER_HEADS, N_KEYS, tb), lambda i: (0, 0, i))] * 4,
        out_shape=[jax.ShapeDtypeStruct((PEER_HEADS, N_KEYS, n), F32)] * 4,
        scratch_shapes=[pltpu.VMEM((64, tb), F32)],
        compiler_params=_cparams("parallel"),
        name="peer_scores",
    )(h2, wq_t, keys)


def _gelu_tanh_x2(x):
    return x + x * jnp.tanh(x * (0.7978845608028654 + 0.035677408136300125 * (x * x)))


def _peer_expert_kernel(h_ref, u_ref, vt_ref, thr_ref, g1_ref, s2_ref, e2_ref, x1_ref, mod_ref, fg_ref, y_ref,
                        acc_ref, w_ref, *, rows_per_step):
    e = pl.program_id(1)
    tb = h_ref.shape[0]

    @pl.when(e == 0)
    def _():
        acc_ref[...] = jnp.zeros_like(acc_ref)

    st = _mm_nt(u_ref[...], h_ref[...])
    rows_per_chunk = 2
    partial = None
    for al in range(rows_per_step):
        a = e * rows_per_step + al
        r0 = al * N_KEYS
        thr = [thr_ref[h, pl.ds(a, 1), :] for h in range(PEER_HEADS)]
        g1 = [g1_ref[h, pl.ds(a, 1), :] for h in range(PEER_HEADS)]
        for lt in range(tb // LANES):
            ls = slice(lt * LANES, (lt + 1) * LANES)
            wsum = None
            for h in range(PEER_HEADS):
                w = jnp.where(s2_ref[h, :, ls] >= thr[h][:, ls], e2_ref[h, :, ls] * g1[h][:, ls], 0.0)
                wsum = w if wsum is None else wsum + w
            act = _gelu_tanh_x2(st[r0:r0 + N_KEYS, ls])
            w_ref[r0:r0 + N_KEYS, ls] = (wsum * act).astype(BF16)
        if (al + 1) % rows_per_chunk == 0:
            rs = slice((al + 1 - rows_per_chunk) * N_KEYS, (al + 1) * N_KEYS)
            d = jnp.dot(vt_ref[:, rs], w_ref[rs, :], preferred_element_type=F32)
            partial = d if partial is None else partial + d
    acc_ref[...] += partial

    @pl.when(e == pl.num_programs(1) - 1)
    def _():
        m = mod_ref[0]
        x2 = x1_ref[...] + m[5:6] * acc_ref[...].T
        y_ref[...] = _rms(x2, fg_ref[...])


def _peer_experts(h2, u_bf, vt_bf, stats, x1, mod, mod_base, mod_step, seq, final_g):
    n = h2.shape[0]
    tb = PEER_EXPERT_TOKENS if (mod_step == 0 or seq % PEER_EXPERT_TOKENS == 0) else seq
    rows_per_step = PEER_ROWS_PER_STEP
    ec = rows_per_step * N_KEYS
    mod_idx = lambda i, e: (mod_base + mod_step * ((i * tb) // seq), 0, 0)
    stat_spec = pl.BlockSpec((PEER_HEADS, N_KEYS, tb), lambda i, e: (0, 0, i))
    return pl.pallas_call(
        functools.partial(_peer_expert_kernel, rows_per_step=rows_per_step),
        grid=(n // tb, N_EXPERTS // ec),
        in_specs=[pl.BlockSpec((tb, D_MODEL), lambda i, e: (i, 0)),
                  pl.BlockSpec((ec, D_MODEL), lambda i, e: (e, 0)),
                  pl.BlockSpec((D_MODEL, ec), lambda i, e: (0, e)),
                  stat_spec, stat_spec, stat_spec, stat_spec,
                  pl.BlockSpec((tb, D_MODEL), lambda i, e: (i, 0)),
                  pl.BlockSpec((1, N_MOD, D_MODEL), mod_idx),
                  pl.BlockSpec((1, D_MODEL), lambda i, e: (0, 0))],
        out_specs=pl.BlockSpec((tb, D_MODEL), lambda i, e: (i, 0)),
        out_shape=jax.ShapeDtypeStruct((n, D_MODEL), F32),
        scratch_shapes=[pltpu.VMEM((D_MODEL, tb), F32), pltpu.VMEM((ec, tb), BF16)],
        compiler_params=_cparams("parallel", "arbitrary"),
        name="peer_experts",
    )(h2, u_bf, vt_bf, *stats, x1, mod, final_g)


def _rope_tables(seq):
    pos = jnp.arange(seq, dtype=jnp.int32)
    lane = jnp.arange(LANES, dtype=jnp.int32) % HEAD_DIM
    use_col = (lane // 32) == 1
    p = jnp.where(use_col[None, :], (pos % GRID_W)[:, None], (pos // GRID_W)[:, None]).astype(F32)
    inv = ROPE_BASE ** (-(lane % 16).astype(F32) / 16.0)
    ang = p * inv[None, :]
    first = (lane % 32) < 16
    return jnp.cos(ang), jnp.where(first[None, :], -jnp.sin(ang), jnp.sin(ang))


def _trunk_path(x, mod, mod_base, mod_step, rope_tabs, s_ret0, s_rwkv0, want_fin, lw, final_g):
    b, seq, _ = x.shape
    x2 = x.reshape(b * seq, D_MODEL)
    mix, lora, gate = _in_proj(x2, mod, mod_base, mod_step, seq, lw['norm1_g'], lw['w_in'])
    mix3 = mix.reshape(b, seq, COLS_MIX)
    lora3 = lora.reshape(b, seq, COLS_LORA)
    ya, ret_fin = _retention(mix3, lw['ret_decay'], lw['ret_gn_w'], lw['ret_gn_b'], rope_tabs, s_ret0, want_fin)
    yb, rwkv_fin = _rwkv(mix3, lora3, lw['rwkv'], s_rwkv0, want_fin)
    x1, h2 = _merge(ya.reshape(b * seq, BRANCH_W), yb.reshape(b * seq, BRANCH_W), gate, x2, mod, mod_base,
                    mod_step, seq, lw['w_br_a'], lw['w_br_b'], lw['w_out'], lw['norm2_g'])
    stats = _peer_scores(h2, lw['peer_wq_t'], lw['peer_keys'])
    y = _peer_experts(h2, lw['peer_u'], lw['peer_vt'], stats, x1, mod, mod_base, mod_step, seq, final_g)
    return y.reshape(b, seq, D_MODEL), ret_fin, rwkv_fin


def kernel(x_prompt, x_sample, state_ret, state_rwkv, c, c_ctx, ada_w, ada_b, norm1_g, w_in, ret_decay, ret_gn_w, ret_gn_b, rwkv_conv, rwkv_w0, rwkv_w2, rwkv_a0, rwkv_a2, rwkv_g2, rwkv_k_k, rwkv_k_a, rwkv_r_k, rwkv_gn_w, rwkv_gn_b, w_br_a, w_br_b, w_out, norm2_g, peer_wq, peer_keys, peer_u, peer_v, final_norm_g):
    assert w_in.shape[0] == 1, "the final norm is fused into the layer's last kernel: single trunk layer only"
    row = lambda a: a.reshape(1, -1)
    cc = jnp.concatenate([c_ctx[None, :], c], axis=0)
    cc = jnp.pad(cc, ((0, (-cc.shape[0]) % 8), (0, 0)))
    rope_tabs = _rope_tables(x_sample.shape[1])
    final_g = row(final_norm_g)

    l = 0
    lw = {
        'norm1_g': row(norm1_g[l]), 'w_in': w_in[l].astype(BF16),
        'ret_decay': jnp.repeat(ret_decay[l], HEAD_DIM, axis=1),
        'ret_gn_w': row(ret_gn_w[l]), 'ret_gn_b': row(ret_gn_b[l]),
        'rwkv': {'conv': rwkv_conv[l], 'w0': rwkv_w0[l], 'w2': rwkv_w2[l].astype(BF16), 'a0': rwkv_a0[l],
                 'a2': rwkv_a2[l].astype(BF16), 'g2': rwkv_g2[l].astype(BF16), 'k_k': row(rwkv_k_k[l]),
                 'k_a': row(rwkv_k_a[l]), 'r_k': row(rwkv_r_k[l]), 'gn_w': row(rwkv_gn_w[l]),
                 'gn_b': row(rwkv_gn_b[l])},
        'w_br_a': w_br_a[l].astype(BF16), 'w_br_b': w_br_b[l].astype(BF16), 'w_out': w_out[l].astype(BF16),
        'norm2_g': row(norm2_g[l]),
        'peer_wq_t': peer_wq[l].astype(BF16).T,
        'peer_keys': peer_keys[l].reshape(2 * PEER_HEADS, N_KEYS, PEER_DQ // 2).astype(BF16),
        'peer_u': peer_u[l].astype(BF16), 'peer_vt': peer_v[l].astype(BF16).T,
    }
    mod = _adaln(cc, ada_w[l], row(ada_b[l])).reshape(cc.shape[0], N_MOD, D_MODEL)
    yp, ret_fin, rwkv_fin = _trunk_path(x_prompt, mod, 0, 0, None, None, None, True, lw, final_g)
    ys, _, _ = _trunk_path(x_sample, mod, 1, 1, rope_tabs, state_ret[:, l], state_rwkv[:, l], False, lw, final_g)
    return (yp, ys, ret_fin[:, None], rwkv_fin[:, None])
```

```python
import functools

import jax
import jax.numpy as jnp
from jax import lax
from jax.experimental import pallas as pl
from jax.experimental.pallas import tpu as pltpu

F32 = jnp.float32
BF16 = jnp.bfloat16

D_MODEL = 1024
GRID_W = 64
N_MOD = 6
HEADS = 8
HEAD_DIM = 64
BRANCH_W = HEADS * HEAD_DIM
RET_CHUNK = 128
RWKV_CHUNK = 64
LORA_W = 64
LORA_A = 64
LORA_G = 128
PEER_HEADS = 8
N_KEYS = 128
N_EXPERTS = N_KEYS * N_KEYS
PEER_DQ = 256
PEER_TOPK = 16
ROPE_BASE = 10000.0
NORM_EPS = 1e-6
GN_EPS = 64e-5
COLS_MIX = 7 * BRANCH_W
COLS_LORA = LORA_W + LORA_A + LORA_G
COLS_GATE = 2 * D_MODEL
IN_COLS = COLS_MIX + COLS_LORA + COLS_GATE

V7X_VMEM_LIMIT_BYTES = 56 * 1024 * 1024
LANES = 128
ROW_TILE = 512
MERGE_ROW_TILE = 512
PEER_SCORE_TOKENS = 256
PEER_EXPERT_TOKENS = 512
PEER_ROWS_PER_STEP = 8
NEG_INF = float("-inf")


def _cparams(*sem):
    return pltpu.CompilerParams(dimension_semantics=sem, vmem_limit_bytes=V7X_VMEM_LIMIT_BYTES)


def _mm(a, b):
    return jnp.dot(a.astype(BF16), b.astype(BF16), preferred_element_type=F32)


def _mm_nt(a, b):
    return lax.dot_general(a.astype(BF16), b.astype(BF16), (((1,), (1,)), ((), ())),
                           preferred_element_type=F32)


def _mm_tn(a, b):
    return lax.dot_general(a.astype(BF16), b.astype(BF16), (((0,), (0,)), ((), ())),
                           preferred_element_type=F32)


def _mm_tn_3pass(a, b):
    a_hi, b_hi = a.astype(BF16), b.astype(BF16)
    a_lo = (a - a_hi.astype(F32)).astype(BF16)
    b_lo = (b - b_hi.astype(F32)).astype(BF16)
    dims = (((0,), (0,)), ((), ()))
    dot = lambda x, y: lax.dot_general(x, y, dims, preferred_element_type=F32)
    return dot(a_hi, b_hi) + (dot(a_hi, b_lo) + dot(a_lo, b_hi))


def _sigmoid(x):
    return 1.0 / (1.0 + jnp.exp(-x))


def _rms(x, g):
    return x * lax.rsqrt(jnp.mean(x * x, axis=-1, keepdims=True) + NORM_EPS) * g


def _head_sum(x):
    t, w = x.shape
    lo = lax.broadcasted_iota(jnp.int32, (t, LANES), 1) < HEAD_DIM
    outs = []
    for j in range(w // LANES):
        xt = x[:, j * LANES:(j + 1) * LANES]
        s_lo = jnp.sum(jnp.where(lo, xt, 0.0), axis=-1, keepdims=True)
        s_hi = jnp.sum(jnp.where(lo, 0.0, xt), axis=-1, keepdims=True)
        outs.append(jnp.where(lo, s_lo, s_hi))
    return outs[0] if len(outs) == 1 else jnp.concatenate(outs, axis=-1)


def _head_norm(y, w, b):
    mu = _head_sum(y) * (1.0 / HEAD_DIM)
    d = y - mu
    var = _head_sum(d * d) * (1.0 / HEAD_DIM)
    return d * lax.rsqrt(var + GN_EPS) * w + b


def _mod_kernel(c_ref, w_ref, b_ref, o_ref):
    c = c_ref[...]
    o_ref[...] = _mm(c * _sigmoid(c), w_ref[...]) + b_ref[...]


def _adaln(cc, ada_w, ada_b):
    rows = cc.shape[0]
    n = ada_w.shape[1]
    tn = n // 4
    return pl.pallas_call(
        _mod_kernel,
        grid=(n // tn,),
        in_specs=[pl.BlockSpec((rows, D_MODEL), lambda j: (0, 0)),
                  pl.BlockSpec((D_MODEL, tn), lambda j: (0, j)),
                  pl.BlockSpec((1, tn), lambda j: (0, j))],
        out_specs=pl.BlockSpec((rows, tn), lambda j: (0, j)),
        out_shape=jax.ShapeDtypeStruct((rows, n), F32),
        compiler_params=_cparams("parallel"),
        name="adaln_mod",
    )(cc, ada_w, ada_b)


def _in_kernel(x_ref, mod_ref, g_ref, w_ref, mix_ref, lora_ref, gate_ref):
    m = mod_ref[0]
    h = (_rms(x_ref[...], g_ref[...]) * (1.0 + m[1:2]) + m[0:1]).astype(BF16)
    mix_ref[...] = jnp.dot(h, w_ref[:, 0:COLS_MIX], preferred_element_type=F32)
    lora_ref[...] = jnp.dot(h, w_ref[:, COLS_MIX:COLS_MIX + COLS_LORA], preferred_element_type=F32)
    gate_ref[...] = jnp.dot(h, w_ref[:, COLS_MIX + COLS_LORA:IN_COLS], preferred_element_type=F32).astype(BF16)


def _in_proj(x2, mod, mod_base, mod_step, seq, norm_g, w_in_bf):
    n = x2.shape[0]
    tm = ROW_TILE
    mod_idx = lambda i: (mod_base + mod_step * ((i * tm) // seq), 0, 0)
    return pl.pallas_call(
        _in_kernel,
        grid=(n // tm,),
        in_specs=[pl.BlockSpec((tm, D_MODEL), lambda i: (i, 0)),
                  pl.BlockSpec((1, N_MOD, D_MODEL), mod_idx),
                  pl.BlockSpec((1, D_MODEL), lambda i: (0, 0)),
                  pl.BlockSpec((D_MODEL, IN_COLS), lambda i: (0, 0), pipeline_mode=pl.Buffered(1))],
        out_specs=[pl.BlockSpec((tm, COLS_MIX), lambda i: (i, 0)),
                   pl.BlockSpec((tm, COLS_LORA), lambda i: (i, 0)),
                   pl.BlockSpec((tm, COLS_GATE), lambda i: (i, 0))],
        out_shape=[jax.ShapeDtypeStruct((n, COLS_MIX), F32),
                   jax.ShapeDtypeStruct((n, COLS_LORA), F32),
                   jax.ShapeDtypeStruct((n, COLS_GATE), BF16)],
        compiler_params=_cparams("parallel"),
        name="in_proj",
    )(x2, mod, norm_g, w_in_bf)


def _rope(x, cos, sin):
    lane = lax.broadcasted_iota(jnp.int32, x.shape, 1)
    first = (lane % 32) < 16
    partner = jnp.where(first, pltpu.roll(x, LANES - 16, 1), pltpu.roll(x, 16, 1))
    return x * cos + partner * sin


def _ret_kernel(*refs, seq, rope, has_init, want_fin):
    it = iter(refs)
    q_ref, k_ref, v_ref, g_ref, rd_ref, gw_ref, gb_ref = (next(it) for _ in range(7))
    cos_ref = next(it) if rope else None
    sin_ref = next(it) if rope else None
    s0_ref = next(it) if has_init else None
    out_ref = next(it)
    fin_ref = next(it) if want_fin else None
    qs_ref, ks_ref, vec_ref, dec_ref, r_ref, y_ref = (next(it) for _ in range(6))

    C = RET_CHUNK
    W = BRANCH_W
    n = seq // C
    rd = rd_ref[...]
    lg = jnp.minimum(rd, 0.0) - jnp.log(1.0 + jnp.exp(-jnp.abs(rd)))
    lgf, lgb = lg[0:1, :], lg[1:2, :]
    ii = lax.broadcasted_iota(jnp.int32, (C, C), 0)
    jj = lax.broadcasted_iota(jnp.int32, (C, C), 1)
    diff = (ii - jj).astype(F32)
    col = lax.broadcasted_iota(jnp.int32, (C, W), 0).astype(F32)
    ln = lambda h: slice(h * HEAD_DIM, (h + 1) * HEAD_DIM)

    vec_ref[0] = jnp.exp((col + 1.0) * lgf)
    vec_ref[1] = jnp.exp((C - 1.0 - col) * lgf)
    vec_ref[2] = jnp.exp((C - col) * lgb)
    vec_ref[3] = jnp.exp(col * lgb)
    chunk_f = jnp.exp(C * lgf)
    chunk_b = jnp.exp(C * lgb)
    for h in range(HEADS):
        gf = lgf[:, h * HEAD_DIM:h * HEAD_DIM + 1]
        gb = lgb[:, h * HEAD_DIM:h * HEAD_DIM + 1]
        dec_ref[h] = (jnp.where(diff >= 0, jnp.exp(jnp.maximum(diff, 0.0) * gf), 0.0)
                      + jnp.where(diff <= 0, jnp.exp(jnp.maximum(-diff, 0.0) * gb), 0.0))
        for d in range(2):
            r_ref[d, h] = s0_ref[0, d, h] if has_init else jnp.zeros((HEAD_DIM, HEAD_DIM), F32)

    def rows(c):
        return pl.ds(pl.multiple_of(c * C, C), C)

    def prep(c, carry):
        r = rows(c)
        q = q_ref[0, r, :]
        k = k_ref[0, r, :] * (HEAD_DIM ** -0.5)
        if rope:
            cos, sin = cos_ref[r, :], sin_ref[r, :]
            tiles = lambda x: [x[:, j * LANES:(j + 1) * LANES] for j in range(W // LANES)]
            q = jnp.concatenate([_rope(t, cos, sin) for t in tiles(q)], axis=1)
            k = jnp.concatenate([_rope(t, cos, sin) for t in tiles(k)], axis=1)
        qs_ref[r, :] = q
        ks_ref[r, :] = k
        y_ref[0, r, :] = jnp.zeros((C, W), F32)
        return carry

    lax.fori_loop(0, n, prep, 0)

    cpi = 2 if n % 4 == 0 else 1

    def step(i, carry):
        rf = [rows(i * cpi + u) for u in range(cpi)]
        rb = [rows(n - 1 - (i * cpi + u)) for u in range(cpi)]
        intra, qfx, inc_f, qbx, inc_b = ([[None] * HEADS for _ in range(cpi)] for _ in range(5))
        for u in range(cpi):
            qf, kf, vf = qs_ref[rf[u], :], ks_ref[rf[u], :], v_ref[0, rf[u], :]
            qb, kb, vb = qs_ref[rb[u], :], ks_ref[rb[u], :], v_ref[0, rb[u], :]
            qfs, kfs = qf * vec_ref[0], kf * vec_ref[1]
            qbs, kbs = qb * vec_ref[2], kb * vec_ref[3]
            for h in range(HEADS):
                sc = _mm_nt(qf[:, ln(h)], kf[:, ln(h)]) * dec_ref[h]
                intra[u][h] = _mm(sc, vf[:, ln(h)])
                qfx[u][h], qbx[u][h] = qfs[:, ln(h)], qbs[:, ln(h)]
                inc_f[u][h] = _mm_tn(kfs[:, ln(h)], vf[:, ln(h)])
                inc_b[u][h] = _mm_tn(kbs[:, ln(h)], vb[:, ln(h)])
        sf = [r_ref[0, h] for h in range(HEADS)]
        sb = [r_ref[1, h] for h in range(HEADS)]
        of, ob = ([[None] * HEADS for _ in range(cpi)] for _ in range(2))
        for u in range(cpi):
            for h in range(HEADS):
                of[u][h] = intra[u][h] + _mm(qfx[u][h], sf[h])
                sf[h] = sf[h] * chunk_f[:, ln(h)] + inc_f[u][h]
                ob[u][h] = _mm(qbx[u][h], sb[h])
                sb[h] = sb[h] * chunk_b[:, ln(h)] + inc_b[u][h]
        yf = [y_ref[0, rf[u], :] for u in range(cpi)]
        yb = [y_ref[0, rb[u], :] for u in range(cpi)]
        for u in range(cpi):
            y_ref[0, rf[u], :] = yf[u] + jnp.concatenate(of[u], axis=1)
            y_ref[0, rb[u], :] = yb[u] + jnp.concatenate(ob[u], axis=1)
        for h in range(HEADS):
            r_ref[0, h] = sf[h]
            r_ref[1, h] = sb[h]
        return carry

    lax.fori_loop(0, n // cpi, step, 0)
    if want_fin:
        for d in range(2):
            for h in range(HEADS):
                fin_ref[0, d, h] = r_ref[d, h]

    def post(c, carry):
        r = rows(c)
        g = g_ref[0, r, :]
        y = _head_norm(y_ref[0, r, :], gw_ref[...], gb_ref[...]) * (g * _sigmoid(g))
        out_ref[0, r, :] = y.astype(BF16)
        return carry

    lax.fori_loop(0, n, post, 0)


def _retention(mix3, rd_l, gn_w, gn_b, rope_tabs, s0, want_fin):
    b, seq, _ = mix3.shape
    W = BRANCH_W
    rope = rope_tabs is not None
    has_init = s0 is not None
    col_spec = lambda j: pl.BlockSpec((1, seq, W), lambda i: (i, 0, j))
    full = lambda a: pl.BlockSpec(a.shape, lambda i: (0,) * a.ndim)
    in_specs = [col_spec(0), col_spec(1), col_spec(2), col_spec(3), full(rd_l), full(gn_w), full(gn_b)]
    args = [mix3, mix3, mix3, mix3, rd_l, gn_w, gn_b]
    if rope:
        in_specs += [full(rope_tabs[0]), full(rope_tabs[1])]
        args += list(rope_tabs)
    state_spec = pl.BlockSpec((1, 2, HEADS, HEAD_DIM, HEAD_DIM), lambda i: (i, 0, 0, 0, 0))
    if has_init:
        in_specs.append(state_spec)
        args.append(s0)
    out_specs = [pl.BlockSpec((1, seq, W), lambda i: (i, 0, 0))]
    out_shape = [jax.ShapeDtypeStruct((b, seq, W), BF16)]
    if want_fin:
        out_specs.append(state_spec)
        out_shape.append(jax.ShapeDtypeStruct((b, 2, HEADS, HEAD_DIM, HEAD_DIM), F32))
    res = pl.pallas_call(
        functools.partial(_ret_kernel, seq=seq, rope=rope, has_init=has_init, want_fin=want_fin),
        grid=(b,),
        in_specs=in_specs,
        out_specs=out_specs,
        out_shape=out_shape,
        scratch_shapes=[pltpu.VMEM((seq, W), F32), pltpu.VMEM((seq, W), F32),
                        pltpu.VMEM((4, RET_CHUNK, W), F32), pltpu.VMEM((HEADS, RET_CHUNK, RET_CHUNK), F32),
                        pltpu.VMEM((2, HEADS, HEAD_DIM, HEAD_DIM), F32), pltpu.VMEM((1, seq, W), F32)],
        compiler_params=_cparams("parallel"),
        name="retention",
    )(*args)
    return (res[0], res[1]) if want_fin else (res[0], None)


EXP_M_HALF = 0.6065306597126334


def _rwkv_kernel(*refs, seq, nb, has_init, want_fin):
    it = iter(refs)
    (r_ref, k_ref, v_ref, lora_ref, cw_ref, w0_ref, w2_ref, a0_ref, a2_ref, g2_ref,
     kk_w_ref, ka_ref, rk_ref, gw_ref, gb_ref) = (next(it) for _ in range(15))
    s0_ref = next(it) if has_init else None
    out_ref = next(it)
    fin_ref = next(it) if want_fin else None
    r_s, v_s, kk_s, g_s, bv_s, lw_s, b_s, kd_s, st_s, y_ref = (next(it) for _ in range(10))

    C = RWKV_CHUNK
    n = seq // C
    W = BRANCH_W
    row_id = lax.broadcasted_iota(jnp.int32, (C, W), 0)

    def rows(c):
        return pl.ds(pl.multiple_of(c * C, C), C)

    def conv(ref, s, c, w):
        x = ref[s, rows(c), :]
        prev8 = ref[s, pl.ds(pl.multiple_of(jnp.maximum(c * C - 8, 0), 8), 8), :]
        next8 = ref[s, pl.ds(pl.multiple_of(jnp.minimum(c * C + C, seq - 8), 8), 8), :]
        prev_row = jnp.where(c > 0, prev8[7:8, :], 0.0)
        next_row = jnp.where(c < n - 1, next8[0:1, :], 0.0)
        xm = jnp.where(row_id == 0, prev_row, pltpu.roll(x, 1, 0))
        xp = jnp.where(row_id == C - 1, next_row, pltpu.roll(x, C - 1, 0))
        return w[0:1] * xm + w[1:2] * x + w[2:3] * xp

    def prep_seq(s, c):
        rws = rows(c)
        cw = cw_ref[...]
        r = conv(r_ref, s, c, cw[:, 0:W])
        k = conv(k_ref, s, c, cw[:, W:2 * W])
        v = conv(v_ref, s, c, cw[:, 2 * W:3 * W])
        lo = lora_ref[s, rws, :]
        dw = lo[:, 0:LORA_W]
        da = lo[:, LORA_W:LORA_W + LORA_A]
        dg = lo[:, LORA_W + LORA_A:]
        kk = k * kk_w_ref[...]
        kk = kk * lax.rsqrt(_head_sum(kk * kk) + 1e-12)
        y_ref[s, rws, :] = jnp.zeros((C, W), F32)
        r_s[s, rws, :] = r
        v_s[s, rws, :] = v
        kk_s[s, rws, :] = kk
        g_s[s, rws, :] = _mm(_sigmoid(dg), g2_ref[...])
        bv_s[s, rws, :] = _head_sum(r * k * rk_ref[...]) * v
        tdw = jnp.tanh(dw)
        kka = k * ka_ref[...]
        for d in range(2):
            z = w0_ref[d:d + 1, :] + _mm(tdw, w2_ref[d])
            a = _sigmoid(a0_ref[d:d + 1, :] + _mm(da, a2_ref[d]))
            lw_s[s, d, rws, :] = -EXP_M_HALF * _sigmoid(z)
            b_s[s, d, rws, :] = kk * a
            kd_s[s, d, rws, :] = k + kka * (a - 1.0)

    def prep(c, carry):
        for s in range(nb):
            prep_seq(s, c)
        return carry

    lax.fori_loop(0, n, prep, 0)

    ii = lax.broadcasted_iota(jnp.int32, (C, C), 0)
    jj = lax.broadcasted_iota(jnp.int32, (C, C), 1)
    incl = [ii >= jj, ii <= jj]
    strict = [ii > jj, ii < jj]

    cpi = 2 if (nb == 1 and n % 4 == 0) else 1
    streams = [(s, d, u) for s in range(nb) for d in range(2) for u in range(cpi)]
    for s in range(nb):
        for d in range(2):
            for h in range(HEADS):
                st_s[s, d, h] = s0_ref[s, d, h] if has_init else jnp.zeros((HEAD_DIM, HEAD_DIM), F32)

    def chunk(i, carry):
        rws, at, bt, kt, rt, bh, kh, etot, vc = ([None] * len(streams) for _ in range(9))
        for q, (s, d, u) in enumerate(streams):
            rws[q] = rows(i * cpi + u if d == 0 else n - 1 - (i * cpi + u))
            rc, vc[q], kkc = r_s[s, rws[q], :], v_s[s, rws[q], :], kk_s[s, rws[q], :]
            lwc, bc, kc = lw_s[s, d, rws[q], :], b_s[s, d, rws[q], :], kd_s[s, d, rws[q], :]
            cum = lwc
            for k in (1 << p for p in range((C - 1).bit_length())):
                if d == 0:
                    cum = cum + jnp.where(row_id >= k, pltpu.roll(cum, k, 0), 0.0)
                else:
                    cum = cum + jnp.where(row_id < C - k, pltpu.roll(cum, C - k, 0), 0.0)
            tot = cum[C - 1:C, :] if d == 0 else cum[0:1, :]
            pinv = jnp.exp(-cum)
            pend = jnp.exp(tot - cum)
            at[q] = -kkc * jnp.exp(cum - lwc)
            bt[q] = bc * pinv
            kt[q] = kc * pinv
            rt[q] = rc * jnp.exp(cum)
            bh[q] = bc * pend
            kh[q] = kc * pend
            etot[q] = jnp.exp(tot)
        chains = [(q, h) for q in range(len(streams)) for h in range(HEADS)]
        dirn = lambda q: streams[q][1]
        ln = lambda h: slice(h * HEAD_DIM, (h + 1) * HEAD_DIM)
        wk = [jnp.concatenate([bt[q][:, ln(h)], kt[q][:, ln(h)]], axis=0) for q, h in chains]
        ma = [_mm_nt(at[q][:, ln(h)], wk[j]) for j, (q, h) in enumerate(chains)]
        mr = [_mm_nt(rt[q][:, ln(h)], wk[j]) for j, (q, h) in enumerate(chains)]
        a_ak = [jnp.where(strict[dirn(q)], ma[j][:, C:2 * C], 0.0) for j, (q, h) in enumerate(chains)]
        p = [jnp.where(strict[dirn(q)], ma[j][:, 0:C], 0.0) for j, (q, h) in enumerate(chains)]
        a_rb = [jnp.where(incl[dirn(q)], mr[j][:, 0:C], 0.0) for j, (q, h) in enumerate(chains)]
        a_rk = [jnp.where(incl[dirn(q)], mr[j][:, C:2 * C], 0.0) for j, (q, h) in enumerate(chains)]
        vh = [vc[q][:, ln(h)] for q, h in chains]
        x = [jnp.concatenate([at[q][:, ln(h)], _mm(a_ak[j], vh[j])], axis=1)
             for j, (q, h) in enumerate(chains)]
        for step in range(6):
            pb = [p[j].astype(BF16) for j in range(len(chains))]
            xh = [x[j].astype(BF16) for j in range(len(chains))]
            if step < 3:
                xl = [(x[j] - xh[j].astype(F32)).astype(BF16) for j in range(len(chains))]
                px = [jnp.dot(pb[j], jnp.concatenate([xh[j], xl[j]], axis=1), preferred_element_type=F32)
                      for j in range(len(chains))]
                x = [x[j] + (px[j][:, 0:2 * HEAD_DIM] + px[j][:, 2 * HEAD_DIM:]) for j in range(len(chains))]
            else:
                x = [x[j] + jnp.dot(pb[j], xh[j], preferred_element_type=F32) for j in range(len(chains))]
            if step < 5:
                p = [jnp.dot(pb[j], pb[j], preferred_element_type=F32) for j in range(len(chains))]
        gh = [_mm_tn(x[j], bh[q][:, ln(h)]) for j, (q, h) in enumerate(chains)]
        vk = [_mm_tn_3pass(vh[j], kh[q][:, ln(h)]) for j, (q, h) in enumerate(chains)]
        ax = [_mm(a_rb[j], x[j]) for j in range(len(chains))]
        qt = [rt[q][:, ln(h)] + ax[j][:, 0:HEAD_DIM] for j, (q, h) in enumerate(chains)]
        y0 = [ax[j][:, HEAD_DIM:2 * HEAD_DIM] + _mm(a_rk[j], vh[j]) for j in range(len(chains))]
        ys = [None] * len(chains)
        scans = [(s, d, h) for s in range(nb) for d in range(2) for h in range(HEADS)]
        st = {k: st_s[k] for k in scans}
        for u in range(cpi):
            for s, d, h in scans:
                q = streams.index((s, d, u))
                j = q * HEADS + h
                cur = st[s, d, h]
                ys[j] = _mm_nt(qt[j], cur) + y0[j]
                st[s, d, h] = (cur * etot[q][:, ln(h)] + _mm(cur, gh[j][0:HEAD_DIM, :])
                               + gh[j][HEAD_DIM:2 * HEAD_DIM, :] + vk[j])
        for k in scans:
            st_s[k] = st[k]
        for q, (s, d, u) in enumerate(streams):
            y = jnp.concatenate(ys[q * HEADS:(q + 1) * HEADS], axis=1)
            y_ref[s, rws[q], :] = y_ref[s, rws[q], :] + y
        return carry

    lax.fori_loop(0, n // cpi, chunk, 0)
    if want_fin:
        for s in range(nb):
            for d in range(2):
                for h in range(HEADS):
                    fin_ref[s, d, h] = st_s[s, d, h]

    def post(c, carry):
        rws = rows(c)
        for s in range(nb):
            y = _head_norm(y_ref[s, rws, :], gw_ref[...], gb_ref[...])
            out_ref[s, rws, :] = ((y + bv_s[s, rws, :]) * g_s[s, rws, :]).astype(BF16)
        return carry

    lax.fori_loop(0, n, post, 0)


def _rwkv(mix3, lora3, lw, s0, want_fin):
    b, seq, _ = mix3.shape
    has_init = s0 is not None
    W = BRANCH_W
    nb = 2 if (seq <= 256 and b % 2 == 0) else 1
    col_spec = lambda j: pl.BlockSpec((nb, seq, W), lambda i: (i, 0, j))
    full = lambda a: pl.BlockSpec(a.shape, lambda i: (0,) * a.ndim)
    weights = [lw['conv'], lw['w0'], lw['w2'], lw['a0'], lw['a2'], lw['g2'],
               lw['k_k'], lw['k_a'], lw['r_k'], lw['gn_w'], lw['gn_b']]
    in_specs = [col_spec(4), col_spec(5), col_spec(6),
                pl.BlockSpec((nb, seq, COLS_LORA), lambda i: (i, 0, 0))] + [full(a) for a in weights]
    args = [mix3, mix3, mix3, lora3] + weights
    state_spec = pl.BlockSpec((nb, 2, HEADS, HEAD_DIM, HEAD_DIM), lambda i: (i, 0, 0, 0, 0))
    if has_init:
        in_specs.append(state_spec)
        args.append(s0)
    out_specs = [pl.BlockSpec((nb, seq, W), lambda i: (i, 0, 0))]
    out_shape = [jax.ShapeDtypeStruct((b, seq, W), BF16)]
    if want_fin:
        out_specs.append(state_spec)
        out_shape.append(jax.ShapeDtypeStruct((b, 2, HEADS, HEAD_DIM, HEAD_DIM), F32))
    sw = pltpu.VMEM((nb, seq, W), F32)
    sw2 = pltpu.VMEM((nb, 2, seq, W), F32)
    res = pl.pallas_call(
        functools.partial(_rwkv_kernel, seq=seq, nb=nb, has_init=has_init, want_fin=want_fin),
        grid=(b // nb,),
        in_specs=in_specs,
        out_specs=out_specs,
        out_shape=out_shape,
        scratch_shapes=[sw, sw, sw, sw, sw, sw2, sw2, sw2,
                        pltpu.VMEM((nb, 2, HEADS, HEAD_DIM, HEAD_DIM), F32), sw],
        compiler_params=_cparams("parallel"),
        name="rwkv7",
    )(*args)
    return (res[0], res[1]) if want_fin else (res[0], None)


def _merge_kernel(ya_ref, yb_ref, gate_ref, x_ref, mod_ref, wa_ref, wb_ref, wo_ref, n2_ref, x1_ref, h2_ref):
    m = mod_ref[0]
    br_a = jnp.dot(ya_ref[...], wa_ref[...], preferred_element_type=F32)
    br_b = jnp.dot(yb_ref[...], wb_ref[...], preferred_element_type=F32)
    gate = gate_ref[...].astype(F32)
    merged = _sigmoid(gate[:, 0:D_MODEL]) * br_a + _sigmoid(gate[:, D_MODEL:]) * br_b
    mix = jnp.dot(merged.astype(BF16), wo_ref[...], preferred_element_type=F32)
    x1 = x_ref[...] + m[2:3] * mix
    x1_ref[...] = x1
    h2_ref[...] = (_rms(x1, n2_ref[...]) * (1.0 + m[4:5]) + m[3:4]).astype(BF16)


def _merge(ya2, yb2, gate2, x2, mod, mod_base, mod_step, seq, wa, wb, wo, norm2_g):
    n = x2.shape[0]
    tm = MERGE_ROW_TILE
    mod_idx = lambda i: (mod_base + mod_step * ((i * tm) // seq), 0, 0)
    row = lambda w: pl.BlockSpec((tm, w), lambda i: (i, 0))
    full = lambda a: pl.BlockSpec(a.shape, lambda i: (0,) * a.ndim)
    return pl.pallas_call(
        _merge_kernel,
        grid=(n // tm,),
        in_specs=[row(BRANCH_W), row(BRANCH_W), row(COLS_GATE), row(D_MODEL),
                  pl.BlockSpec((1, N_MOD, D_MODEL), mod_idx),
                  full(wa), full(wb), full(wo), full(norm2_g)],
        out_specs=[row(D_MODEL), row(D_MODEL)],
        out_shape=[jax.ShapeDtypeStruct((n, D_MODEL), F32), jax.ShapeDtypeStruct((n, D_MODEL), BF16)],
        compiler_params=_cparams("parallel"),
        name="merge_out",
    )(ya2, yb2, gate2, x2, mod, wa, wb, wo, norm2_g)


def _bitonic_pairs(n):
    pairs = []
    k = 2
    while k <= n:
        j = k // 2
        while j >= 1:
            for i in range(n):
                l = i ^ j
                if l > i:
                    pairs.append((i, l) if (i & k) == 0 else (l, i))
            j //= 2
        k *= 2
    return pairs


def _top_values_of_keys(s, k):
    r, t = s.shape
    nl = r // 8
    lv = [s[8 * j:8 * j + 8, :] for j in range(nl)]
    for a, b in _bitonic_pairs(nl):
        lv[a], lv[b] = jnp.maximum(lv[a], lv[b]), jnp.minimum(lv[a], lv[b])
    sid = lax.broadcasted_iota(jnp.int32, (8, t), 0)
    out = []
    for it in range(k):
        m = jnp.max(lv[0], axis=0, keepdims=True)
        out.append(m)
        first = jnp.min(jnp.where(lv[0] == m, sid, 8), axis=0, keepdims=True)
        pop = sid == first
        for j in range(min(nl, k - 1 - it)):
            lv[j] = jnp.where(pop, lv[j + 1] if j + 1 < nl else NEG_INF, lv[j])
    return out


def _peer_score_kernel(h_ref, wq_ref, keys_ref, thr_ref, g1_ref, s2_ref, e2_ref, cand_ref):
    qt = _mm_nt(wq_ref[...], h_ref[...])
    half = PEER_DQ // 2
    k1 = PEER_TOPK + 1
    pairs = [(i, j) for i in range(k1) for j in range(k1) if (i + 1) * (j + 1) <= k1]
    assert len(pairs) <= cand_ref.shape[0]
    for h in range(PEER_HEADS):
        s = [_mm(keys_ref[2 * h + c], qt[(2 * h + c) * half:(2 * h + c + 1) * half, :]) for c in range(2)]
        tops = [_top_values_of_keys(s[c], k1) for c in range(2)]
        cand_ref[...] = jnp.full(cand_ref.shape, NEG_INF, F32)
        for r, (i, j) in enumerate(pairs):
            cand_ref[r:r + 1, :] = tops[0][i] + tops[1][j]
        best = _top_values_of_keys(cand_ref[...], k1)
        mx = best[0]
        z = jnp.exp(best[0] - mx)
        for r in range(1, PEER_TOPK):
            z = z + jnp.exp(best[r] - mx)
        theta = 0.5 * (best[PEER_TOPK - 1] + best[PEER_TOPK])
        thr_ref[h] = theta - s[0]
        g1_ref[h] = jnp.exp(s[0] - tops[0][0]) * (0.5 / z)
        s2_ref[h] = s[1]
        e2_ref[h] = jnp.exp(s[1] - tops[1][0])


def _peer_scores(h2, wq_t, keys):
    n = h2.shape[0]
    tb = PEER_SCORE_TOKENS
    full = lambda a: pl.BlockSpec(a.shape, lambda i: (0,) * a.ndim)
    return pl.pallas_call(
        _peer_score_kernel,
        grid=(n // tb,),
        in_specs=[pl.BlockSpec((tb, D_MODEL), lambda i: (i, 0)), full(wq_t), full(keys)],
        out_specs=[pl.BlockSpec((PEER_HEADS, N_KEYS, tb), lambda i: (0, 0, i))] * 4,
        out_shape=[jax.ShapeDtypeStruct((PEER_HEADS, N_KEYS, n), F32)] * 4,
        scratch_shapes=[pltpu.VMEM((64, tb), F32)],
        compiler_params=_cparams("parallel"),
        name="peer_scores",
    )(h2, wq_t, keys)


def _gelu_tanh_x2(x):
    return x + x * jnp.tanh(x * (0.7978845608028654 + 0.035677408136300125 * (x * x)))


def _peer_expert_kernel(h_ref, u_ref, vt_ref, thr_ref, g1_ref, s2_ref, e2_ref, x1_ref, mod_ref, fg_ref, y_ref,
                        acc_ref, w_ref, *, rows_per_step):
    e = pl.program_id(1)
    tb = h_ref.shape[0]

    @pl.when(e == 0)
    def _():
        acc_ref[...] = jnp.zeros_like(acc_ref)

    st = _mm_nt(u_ref[...], h_ref[...])
    rows_per_chunk = 2
    partial = None
    for al in range(rows_per_step):
        a = e * rows_per_step + al
        r0 = al * N_KEYS
        thr = [thr_ref[h, pl.ds(a, 1), :] for h in range(PEER_HEADS)]
        g1 = [g1_ref[h, pl.ds(a, 1), :] for h in range(PEER_HEADS)]
        for lt in range(tb // LANES):
            ls = slice(lt * LANES, (lt + 1) * LANES)
            wsum = None
            for h in range(PEER_HEADS):
                w = jnp.where(s2_ref[h, :, ls] >= thr[h][:, ls], e2_ref[h, :, ls] * g1[h][:, ls], 0.0)
                wsum = w if wsum is None else wsum + w
            act = _gelu_tanh_x2(st[r0:r0 + N_KEYS, ls])
            w_ref[r0:r0 + N_KEYS, ls] = (wsum * act).astype(BF16)
        if (al + 1) % rows_per_chunk == 0:
            rs = slice((al + 1 - rows_per_chunk) * N_KEYS, (al + 1) * N_KEYS)
            d = jnp.dot(vt_ref[:, rs], w_ref[rs, :], preferred_element_type=F32)
            partial = d if partial is None else partial + d
    acc_ref[...] += partial

    @pl.when(e == pl.num_programs(1) - 1)
    def _():
        m = mod_ref[0]
        x2 = x1_ref[...] + m[5:6] * acc_ref[...].T
        y_ref[...] = _rms(x2, fg_ref[...])


def _peer_experts(h2, u_bf, vt_bf, stats, x1, mod, mod_base, mod_step, seq, final_g):
    n = h2.shape[0]
    tb = PEER_EXPERT_TOKENS if (mod_step == 0 or seq % PEER_EXPERT_TOKENS == 0) else seq
    rows_per_step = PEER_ROWS_PER_STEP
    ec = rows_per_step * N_KEYS
    mod_idx = lambda i, e: (mod_base + mod_step * ((i * tb) // seq), 0, 0)
    stat_spec = pl.BlockSpec((PEER_HEADS, N_KEYS, tb), lambda i, e: (0, 0, i))
    return pl.pallas_call(
        functools.partial(_peer_expert_kernel, rows_per_step=rows_per_step),
        grid=(n // tb, N_EXPERTS // ec),
        in_specs=[pl.BlockSpec((tb, D_MODEL), lambda i, e: (i, 0)),
                  pl.BlockSpec((ec, D_MODEL), lambda i, e: (e, 0)),
                  pl.BlockSpec((D_MODEL, ec), lambda i, e: (0, e)),
                  stat_spec, stat_spec, stat_spec, stat_spec,
                  pl.BlockSpec((tb, D_MODEL), lambda i, e: (i, 0)),
                  pl.BlockSpec((1, N_MOD, D_MODEL), mod_idx),
                  pl.BlockSpec((1, D_MODEL), lambda i, e: (0, 0))],
        out_specs=pl.BlockSpec((tb, D_MODEL), lambda i, e: (i, 0)),
        out_shape=jax.ShapeDtypeStruct((n, D_MODEL), F32),
        scratch_shapes=[pltpu.VMEM((D_MODEL, tb), F32), pltpu.VMEM((ec, tb), BF16)],
        compiler_params=_cparams("parallel", "arbitrary"),
        name="peer_experts",
    )(h2, u_bf, vt_bf, *stats, x1, mod, final_g)


def _rope_tables(seq):
    pos = jnp.arange(seq, dtype=jnp.int32)
    lane = jnp.arange(LANES, dtype=jnp.int32) % HEAD_DIM
    use_col = (lane // 32) == 1
    p = jnp.where(use_col[None, :], (pos % GRID_W)[:, None], (pos // GRID_W)[:, None]).astype(F32)
    inv = ROPE_BASE ** (-(lane % 16).astype(F32) / 16.0)
    ang = p * inv[None, :]
    first = (lane % 32) < 16
    return jnp.cos(ang), jnp.where(first[None, :], -jnp.sin(ang), jnp.sin(ang))


def _trunk_path(x, mod, mod_base, mod_step, rope_tabs, s_ret0, s_rwkv0, want_fin, lw, final_g):
    b, seq, _ = x.shape
    x2 = x.reshape(b * seq, D_MODEL)
    mix, lora, gate = _in_proj(x2, mod, mod_base, mod_step, seq, lw['norm1_g'], lw['w_in'])
    mix3 = mix.reshape(b, seq, COLS_MIX)
    lora3 = lora.reshape(b, seq, COLS_LORA)
    ya, ret_fin = _retention(mix3, lw['ret_decay'], lw['ret_gn_w'], lw['ret_gn_b'], rope_tabs, s_ret0, want_fin)
    yb, rwkv_fin = _rwkv(mix3, lora3, lw['rwkv'], s_rwkv0, want_fin)
    x1, h2 = _merge(ya.reshape(b * seq, BRANCH_W), yb.reshape(b * seq, BRANCH_W), gate, x2, mod, mod_base,
                    mod_step, seq, lw['w_br_a'], lw['w_br_b'], lw['w_out'], lw['norm2_g'])
    stats = _peer_scores(h2, lw['peer_wq_t'], lw['peer_keys'])
    y = _peer_experts(h2, lw['peer_u'], lw['peer_vt'], stats, x1, mod, mod_base, mod_step, seq, final_g)
    return y.reshape(b, seq, D_MODEL), ret_fin, rwkv_fin


def kernel(x_prompt, x_sample, state_ret, state_rwkv, c, c_ctx, ada_w, ada_b, norm1_g, w_in, ret_decay, ret_gn_w, ret_gn_b, rwkv_conv, rwkv_w0, rwkv_w2, rwkv_a0, rwkv_a2, rwkv_g2, rwkv_k_k, rwkv_k_a, rwkv_r_k, rwkv_gn_w, rwkv_gn_b, w_br_a, w_br_b, w_out, norm2_g, peer_wq, peer_keys, peer_u, peer_v, final_norm_g):
    assert w_in.shape[0] == 1, "the final norm is fused into the layer's last kernel: single trunk layer only"
    row = lambda a: a.reshape(1, -1)
    cc = jnp.concatenate([c_ctx[None, :], c], axis=0)
    cc = jnp.pad(cc, ((0, (-cc.shape[0]) % 8), (0, 0)))
    rope_tabs = _rope_tables(x_sample.shape[1])
    final_g = row(final_norm_g)

    l = 0
    lw = {
        'norm1_g': row(norm1_g[l]), 'w_in': w_in[l].astype(BF16),
        'ret_decay': jnp.repeat(ret_decay[l], HEAD_DIM, axis=1),
        'ret_gn_w': row(ret_gn_w[l]), 'ret_gn_b': row(ret_gn_b[l]),
        'rwkv': {'conv': rwkv_conv[l], 'w0': rwkv_w0[l], 'w2': rwkv_w2[l].astype(BF16), 'a0': rwkv_a0[l],
                 'a2': rwkv_a2[l].astype(BF16), 'g2': rwkv_g2[l].astype(BF16), 'k_k': row(rwkv_k_k[l]),
                 'k_a': row(rwkv_k_a[l]), 'r_k': row(rwkv_r_k[l]), 'gn_w': row(rwkv_gn_w[l]),
                 'gn_b': row(rwkv_gn_b[l])},
        'w_br_a': w_br_a[l].astype(BF16), 'w_br_b': w_br_b[l].astype(BF16), 'w_out': w_out[l].astype(BF16),
        'norm2_g': row(norm2_g[l]),
        'peer_wq_t': peer_wq[l].astype(BF16).T,
        'peer_keys': peer_keys[l].reshape(2 * PEER_HEADS, N_KEYS, PEER_DQ // 2).astype(BF16),
        'peer_u': peer_u[l].astype(BF16), 'peer_vt': peer_v[l].astype(BF16).T,
    }
    mod = _adaln(cc, ada_w[l], row(ada_b[l])).reshape(cc.shape[0], N_MOD, D_MODEL)
    yp, ret_fin, rwkv_fin = _trunk_path(x_prompt, mod, 0, 0, None, None, None, True, lw, final_g)
    ys, _, _ = _trunk_path(x_sample, mod, 1, 1, rope_tabs, state_ret[:, l], state_rwkv[:, l], False, lw, final_g)
    return (yp, ys, ret_fin[:, None], rwkv_fin[:, None])
```

```python
import functools

import jax
import jax.numpy as jnp
from jax import lax
from jax.experimental import pallas as pl
from jax.experimental.pallas import tpu as pltpu

F32 = jnp.float32
BF16 = jnp.bfloat16

D_MODEL = 1024
GRID_W = 64
N_MOD = 6
HEADS = 8
HEAD_DIM = 64
BRANCH_W = HEADS * HEAD_DIM
RET_CHUNK = 128
RWKV_CHUNK = 64
LORA_W = 64
LORA_A = 64
LORA_G = 128
PEER_HEADS = 8
N_KEYS = 128
N_EXPERTS = N_KEYS * N_KEYS
PEER_DQ = 256
PEER_TOPK = 16
ROPE_BASE = 10000.0
NORM_EPS = 1e-6
GN_EPS = 64e-5
COLS_MIX = 7 * BRANCH_W
COLS_LORA = LORA_W + LORA_A + LORA_G
COLS_GATE = 2 * D_MODEL
IN_COLS = COLS_MIX + COLS_LORA + COLS_GATE

V7X_VMEM_LIMIT_BYTES = 56 * 1024 * 1024
LANES = 128
ROW_TILE = 512
MERGE_ROW_TILE = 512
PEER_SCORE_TOKENS = 512
PEER_EXPERT_TOKENS = 512
PEER_ROWS_PER_STEP = 8
NEG_INF = float("-inf")


def _cparams(*sem):
    return pltpu.CompilerParams(dimension_semantics=sem, vmem_limit_bytes=V7X_VMEM_LIMIT_BYTES)


def _mm(a, b):
    return jnp.dot(a.astype(BF16), b.astype(BF16), preferred_element_type=F32)


def _mm_nt(a, b):
    return lax.dot_general(a.astype(BF16), b.astype(BF16), (((1,), (1,)), ((), ())),
                           preferred_element_type=F32)


def _mm_tn(a, b):
    return lax.dot_general(a.astype(BF16), b.astype(BF16), (((0,), (0,)), ((), ())),
                           preferred_element_type=F32)


def _mm_tn_3pass(a, b):
    a_hi, b_hi = a.astype(BF16), b.astype(BF16)
    a_lo = (a - a_hi.astype(F32)).astype(BF16)
    b_lo = (b - b_hi.astype(F32)).astype(BF16)
    dims = (((0,), (0,)), ((), ()))
    dot = lambda x, y: lax.dot_general(x, y, dims, preferred_element_type=F32)
    return dot(a_hi, b_hi) + (dot(a_hi, b_lo) + dot(a_lo, b_hi))


def _sigmoid(x):
    return 1.0 / (1.0 + jnp.exp(-x))


def _rms(x, g):
    return x * lax.rsqrt(jnp.mean(x * x, axis=-1, keepdims=True) + NORM_EPS) * g


def _head_sum(x):
    t, w = x.shape
    lo = lax.broadcasted_iota(jnp.int32, (t, LANES), 1) < HEAD_DIM
    outs = []
    for j in range(w // LANES):
        xt = x[:, j * LANES:(j + 1) * LANES]
        s_lo = jnp.sum(jnp.where(lo, xt, 0.0), axis=-1, keepdims=True)
        s_hi = jnp.sum(jnp.where(lo, 0.0, xt), axis=-1, keepdims=True)
        outs.append(jnp.where(lo, s_lo, s_hi))
    return outs[0] if len(outs) == 1 else jnp.concatenate(outs, axis=-1)


def _head_norm(y, w, b):
    mu = _head_sum(y) * (1.0 / HEAD_DIM)
    d = y - mu
    var = _head_sum(d * d) * (1.0 / HEAD_DIM)
    return d * lax.rsqrt(var + GN_EPS) * w + b


def _mod_kernel(c_ref, w_ref, b_ref, o_ref):
    c = c_ref[...]
    o_ref[...] = _mm(c * _sigmoid(c), w_ref[...]) + b_ref[...]


def _adaln(cc, ada_w, ada_b):
    rows = cc.shape[0]
    n = ada_w.shape[1]
    tn = n // 4
    return pl.pallas_call(
        _mod_kernel,
        grid=(n // tn,),
        in_specs=[pl.BlockSpec((rows, D_MODEL), lambda j: (0, 0)),
                  pl.BlockSpec((D_MODEL, tn), lambda j: (0, j)),
                  pl.BlockSpec((1, tn), lambda j: (0, j))],
        out_specs=pl.BlockSpec((rows, tn), lambda j: (0, j)),
        out_shape=jax.ShapeDtypeStruct((rows, n), F32),
        compiler_params=_cparams("parallel"),
        name="adaln_mod",
    )(cc, ada_w, ada_b)


def _in_kernel(x_ref, mod_ref, g_ref, w_ref, mix_ref, lora_ref, gate_ref):
    m = mod_ref[0]
    h = (_rms(x_ref[...], g_ref[...]) * (1.0 + m[1:2]) + m[0:1]).astype(BF16)
    mix_ref[...] = jnp.dot(h, w_ref[:, 0:COLS_MIX], preferred_element_type=F32)
    lora_ref[...] = jnp.dot(h, w_ref[:, COLS_MIX:COLS_MIX + COLS_LORA], preferred_element_type=F32)
    gate_ref[...] = jnp.dot(h, w_ref[:, COLS_MIX + COLS_LORA:IN_COLS], preferred_element_type=F32).astype(BF16)


def _in_proj(x2, mod, mod_base, mod_step, seq, norm_g, w_in_bf):
    n = x2.shape[0]
    tm = ROW_TILE
    mod_idx = lambda i: (mod_base + mod_step * ((i * tm) // seq), 0, 0)
    return pl.pallas_call(
        _in_kernel,
        grid=(n // tm,),
        in_specs=[pl.BlockSpec((tm, D_MODEL), lambda i: (i, 0)),
                  pl.BlockSpec((1, N_MOD, D_MODEL), mod_idx),
                  pl.BlockSpec((1, D_MODEL), lambda i: (0, 0)),
                  pl.BlockSpec((D_MODEL, IN_COLS), lambda i: (0, 0), pipeline_mode=pl.Buffered(1))],
        out_specs=[pl.BlockSpec((tm, COLS_MIX), lambda i: (i, 0)),
                   pl.BlockSpec((tm, COLS_LORA), lambda i: (i, 0)),
                   pl.BlockSpec((tm, COLS_GATE), lambda i: (i, 0))],
        out_shape=[jax.ShapeDtypeStruct((n, COLS_MIX), F32),
                   jax.ShapeDtypeStruct((n, COLS_LORA), F32),
                   jax.ShapeDtypeStruct((n, COLS_GATE), BF16)],
        compiler_params=_cparams("parallel"),
        name="in_proj",
    )(x2, mod, norm_g, w_in_bf)


def _rope(x, cos, sin):
    lane = lax.broadcasted_iota(jnp.int32, x.shape, 1)
    first = (lane % 32) < 16
    partner = jnp.where(first, pltpu.roll(x, LANES - 16, 1), pltpu.roll(x, 16, 1))
    return x * cos + partner * sin


def _ret_kernel(*refs, seq, rope, has_init, want_fin):
    it = iter(refs)
    q_ref, k_ref, v_ref, g_ref, rd_ref, gw_ref, gb_ref = (next(it) for _ in range(7))
    cos_ref = next(it) if rope else None
    sin_ref = next(it) if rope else None
    s0_ref = next(it) if has_init else None
    out_ref = next(it)
    fin_ref = next(it) if want_fin else None
    qs_ref, ks_ref, vec_ref, dec_ref, r_ref, y_ref = (next(it) for _ in range(6))

    C = RET_CHUNK
    W = BRANCH_W
    n = seq // C
    rd = rd_ref[...]
    lg = jnp.minimum(rd, 0.0) - jnp.log(1.0 + jnp.exp(-jnp.abs(rd)))
    lgf, lgb = lg[0:1, :], lg[1:2, :]
    ii = lax.broadcasted_iota(jnp.int32, (C, C), 0)
    jj = lax.broadcasted_iota(jnp.int32, (C, C), 1)
    diff = (ii - jj).astype(F32)
    col = lax.broadcasted_iota(jnp.int32, (C, W), 0).astype(F32)
    ln = lambda h: slice(h * HEAD_DIM, (h + 1) * HEAD_DIM)

    vec_ref[0] = jnp.exp((col + 1.0) * lgf)
    vec_ref[1] = jnp.exp((C - 1.0 - col) * lgf)
    vec_ref[2] = jnp.exp((C - col) * lgb)
    vec_ref[3] = jnp.exp(col * lgb)
    chunk_f = jnp.exp(C * lgf)
    chunk_b = jnp.exp(C * lgb)
    for h in range(HEADS):
        gf = lgf[:, h * HEAD_DIM:h * HEAD_DIM + 1]
        gb = lgb[:, h * HEAD_DIM:h * HEAD_DIM + 1]
        dec_ref[h] = (jnp.where(diff >= 0, jnp.exp(jnp.maximum(diff, 0.0) * gf), 0.0)
                      + jnp.where(diff <= 0, jnp.exp(jnp.maximum(-diff, 0.0) * gb), 0.0))
        for d in range(2):
            r_ref[d, h] = s0_ref[0, d, h] if has_init else jnp.zeros((HEAD_DIM, HEAD_DIM), F32)

    def rows(c):
        return pl.ds(pl.multiple_of(c * C, C), C)

    def prep(c, carry):
        r = rows(c)
        q = q_ref[0, r, :]
        k = k_ref[0, r, :] * (HEAD_DIM ** -0.5)
        if rope:
            cos, sin = cos_ref[r, :], sin_ref[r, :]
            tiles = lambda x: [x[:, j * LANES:(j + 1) * LANES] for j in range(W // LANES)]
            q = jnp.concatenate([_rope(t, cos, sin) for t in tiles(q)], axis=1)
            k = jnp.concatenate([_rope(t, cos, sin) for t in tiles(k)], axis=1)
        qs_ref[r, :] = q
        ks_ref[r, :] = k
        y_ref[0, r, :] = jnp.zeros((C, W), F32)
        return carry

    lax.fori_loop(0, n, prep, 0)

    cpi = 2 if n % 4 == 0 else 1

    def step(i, carry):
        rf = [rows(i * cpi + u) for u in range(cpi)]
        rb = [rows(n - 1 - (i * cpi + u)) for u in range(cpi)]
        intra, qfx, inc_f, qbx, inc_b = ([[None] * HEADS for _ in range(cpi)] for _ in range(5))
        for u in range(cpi):
            qf, kf, vf = qs_ref[rf[u], :], ks_ref[rf[u], :], v_ref[0, rf[u], :]
            qb, kb, vb = qs_ref[rb[u], :], ks_ref[rb[u], :], v_ref[0, rb[u], :]
            qfs, kfs = qf * vec_ref[0], kf * vec_ref[1]
            qbs, kbs = qb * vec_ref[2], kb * vec_ref[3]
            for h in range(HEADS):
                sc = _mm_nt(qf[:, ln(h)], kf[:, ln(h)]) * dec_ref[h]
                intra[u][h] = _mm(sc, vf[:, ln(h)])
                qfx[u][h], qbx[u][h] = qfs[:, ln(h)], qbs[:, ln(h)]
                inc_f[u][h] = _mm_tn(kfs[:, ln(h)], vf[:, ln(h)])
                inc_b[u][h] = _mm_tn(kbs[:, ln(h)], vb[:, ln(h)])
        sf = [r_ref[0, h] for h in range(HEADS)]
        sb = [r_ref[1, h] for h in range(HEADS)]
        of, ob = ([[None] * HEADS for _ in range(cpi)] for _ in range(2))
        for u in range(cpi):
            for h in range(HEADS):
                of[u][h] = intra[u][h] + _mm(qfx[u][h], sf[h])
                sf[h] = sf[h] * chunk_f[:, ln(h)] + inc_f[u][h]
                ob[u][h] = _mm(qbx[u][h], sb[h])
                sb[h] = sb[h] * chunk_b[:, ln(h)] + inc_b[u][h]
        yf = [y_ref[0, rf[u], :] for u in range(cpi)]
        yb = [y_ref[0, rb[u], :] for u in range(cpi)]
        for u in range(cpi):
            y_ref[0, rf[u], :] = yf[u] + jnp.concatenate(of[u], axis=1)
            y_ref[0, rb[u], :] = yb[u] + jnp.concatenate(ob[u], axis=1)
        for h in range(HEADS):
            r_ref[0, h] = sf[h]
            r_ref[1, h] = sb[h]
        return carry

    lax.fori_loop(0, n // cpi, step, 0)
    if want_fin:
        for d in range(2):
            for h in range(HEADS):
                fin_ref[0, d, h] = r_ref[d, h]

    def post(c, carry):
        r = rows(c)
        g = g_ref[0, r, :]
        y = _head_norm(y_ref[0, r, :], gw_ref[...], gb_ref[...]) * (g * _sigmoid(g))
        out_ref[0, r, :] = y.astype(BF16)
        return carry

    lax.fori_loop(0, n, post, 0)


def _retention(mix3, rd_l, gn_w, gn_b, rope_tabs, s0, want_fin):
    b, seq, _ = mix3.shape
    W = BRANCH_W
    rope = rope_tabs is not None
    has_init = s0 is not None
    col_spec = lambda j: pl.BlockSpec((1, seq, W), lambda i: (i, 0, j))
    full = lambda a: pl.BlockSpec(a.shape, lambda i: (0,) * a.ndim)
    in_specs = [col_spec(0), col_spec(1), col_spec(2), col_spec(3), full(rd_l), full(gn_w), full(gn_b)]
    args = [mix3, mix3, mix3, mix3, rd_l, gn_w, gn_b]
    if rope:
        in_specs += [full(rope_tabs[0]), full(rope_tabs[1])]
        args += list(rope_tabs)
    state_spec = pl.BlockSpec((1, 2, HEADS, HEAD_DIM, HEAD_DIM), lambda i: (i, 0, 0, 0, 0))
    if has_init:
        in_specs.append(state_spec)
        args.append(s0)
    out_specs = [pl.BlockSpec((1, seq, W), lambda i: (i, 0, 0))]
    out_shape = [jax.ShapeDtypeStruct((b, seq, W), BF16)]
    if want_fin:
        out_specs.append(state_spec)
        out_shape.append(jax.ShapeDtypeStruct((b, 2, HEADS, HEAD_DIM, HEAD_DIM), F32))
    res = pl.pallas_call(
        functools.partial(_ret_kernel, seq=seq, rope=rope, has_init=has_init, want_fin=want_fin),
        grid=(b,),
        in_specs=in_specs,
        out_specs=out_specs,
        out_shape=out_shape,
        scratch_shapes=[pltpu.VMEM((seq, W), F32), pltpu.VMEM((seq, W), F32),
                        pltpu.VMEM((4, RET_CHUNK, W), F32), pltpu.VMEM((HEADS, RET_CHUNK, RET_CHUNK), F32),
                        pltpu.VMEM((2, HEADS, HEAD_DIM, HEAD_DIM), F32), pltpu.VMEM((1, seq, W), F32)],
        compiler_params=_cparams("parallel"),
        name="retention",
    )(*args)
    return (res[0], res[1]) if want_fin else (res[0], None)


EXP_M_HALF = 0.6065306597126334


def _rwkv_kernel(*refs, seq, nb, has_init, want_fin):
    it = iter(refs)
    (r_ref, k_ref, v_ref, lora_ref, cw_ref, w0_ref, w2_ref, a0_ref, a2_ref, g2_ref,
     kk_w_ref, ka_ref, rk_ref, gw_ref, gb_ref) = (next(it) for _ in range(15))
    s0_ref = next(it) if has_init else None
    out_ref = next(it)
    fin_ref = next(it) if want_fin else None
    r_s, v_s, kk_s, g_s, bv_s, lw_s, b_s, kd_s, st_s, y_ref = (next(it) for _ in range(10))

    C = RWKV_CHUNK
    n = seq // C
    W = BRANCH_W
    row_id = lax.broadcasted_iota(jnp.int32, (C, W), 0)

    def rows(c):
        return pl.ds(pl.multiple_of(c * C, C), C)

    def conv(ref, s, c, w):
        x = ref[s, rows(c), :]
        prev8 = ref[s, pl.ds(pl.multiple_of(jnp.maximum(c * C - 8, 0), 8), 8), :]
        next8 = ref[s, pl.ds(pl.multiple_of(jnp.minimum(c * C + C, seq - 8), 8), 8), :]
        prev_row = jnp.where(c > 0, prev8[7:8, :], 0.0)
        next_row = jnp.where(c < n - 1, next8[0:1, :], 0.0)
        xm = jnp.where(row_id == 0, prev_row, pltpu.roll(x, 1, 0))
        xp = jnp.where(row_id == C - 1, next_row, pltpu.roll(x, C - 1, 0))
        return w[0:1] * xm + w[1:2] * x + w[2:3] * xp

    def prep_seq(s, c):
        rws = rows(c)
        cw = cw_ref[...]
        r = conv(r_ref, s, c, cw[:, 0:W])
        k = conv(k_ref, s, c, cw[:, W:2 * W])
        v = conv(v_ref, s, c, cw[:, 2 * W:3 * W])
        lo = lora_ref[s, rws, :]
        dw = lo[:, 0:LORA_W]
        da = lo[:, LORA_W:LORA_W + LORA_A]
        dg = lo[:, LORA_W + LORA_A:]
        kk = k * kk_w_ref[...]
        kk = kk * lax.rsqrt(_head_sum(kk * kk) + 1e-12)
        y_ref[s, rws, :] = jnp.zeros((C, W), F32)
        r_s[s, rws, :] = r
        v_s[s, rws, :] = v
        kk_s[s, rws, :] = kk
        g_s[s, rws, :] = _mm(_sigmoid(dg), g2_ref[...])
        bv_s[s, rws, :] = _head_sum(r * k * rk_ref[...]) * v
        tdw = jnp.tanh(dw)
        kka = k * ka_ref[...]
        for d in range(2):
            z = w0_ref[d:d + 1, :] + _mm(tdw, w2_ref[d])
            a = _sigmoid(a0_ref[d:d + 1, :] + _mm(da, a2_ref[d]))
            lw_s[s, d, rws, :] = -EXP_M_HALF * _sigmoid(z)
            b_s[s, d, rws, :] = kk * a
            kd_s[s, d, rws, :] = k + kka * (a - 1.0)

    def prep(c, carry):
        for s in range(nb):
            prep_seq(s, c)
        return carry

    lax.fori_loop(0, n, prep, 0)

    ii = lax.broadcasted_iota(jnp.int32, (C, C), 0)
    jj = lax.broadcasted_iota(jnp.int32, (C, C), 1)
    incl = [ii >= jj, ii <= jj]
    strict = [ii > jj, ii < jj]

    cpi = 2 if (nb == 1 and n % 4 == 0) else 1
    streams = [(s, d, u) for s in range(nb) for d in range(2) for u in range(cpi)]
    for s in range(nb):
        for d in range(2):
            for h in range(HEADS):
                st_s[s, d, h] = s0_ref[s, d, h] if has_init else jnp.zeros((HEAD_DIM, HEAD_DIM), F32)

    def chunk(i, carry):
        rws, at, bt, kt, rt, bh, kh, etot, vc = ([None] * len(streams) for _ in range(9))
        for q, (s, d, u) in enumerate(streams):
            rws[q] = rows(i * cpi + u if d == 0 else n - 1 - (i * cpi + u))
            rc, vc[q], kkc = r_s[s, rws[q], :], v_s[s, rws[q], :], kk_s[s, rws[q], :]
            lwc, bc, kc = lw_s[s, d, rws[q], :], b_s[s, d, rws[q], :], kd_s[s, d, rws[q], :]
            cum = lwc
            for k in (1 << p for p in range((C - 1).bit_length())):
                if d == 0:
                    cum = cum + jnp.where(row_id >= k, pltpu.roll(cum, k, 0), 0.0)
                else:
                    cum = cum + jnp.where(row_id < C - k, pltpu.roll(cum, C - k, 0), 0.0)
            tot = cum[C - 1:C, :] if d == 0 else cum[0:1, :]
            pinv = jnp.exp(-cum)
            pend = jnp.exp(tot - cum)
            at[q] = -kkc * jnp.exp(cum - lwc)
            bt[q] = bc * pinv
            kt[q] = kc * pinv
            rt[q] = rc * jnp.exp(cum)
            bh[q] = bc * pend
            kh[q] = kc * pend
            etot[q] = jnp.exp(tot)
        chains = [(q, h) for q in range(len(streams)) for h in range(HEADS)]
        dirn = lambda q: streams[q][1]
        ln = lambda h: slice(h * HEAD_DIM, (h + 1) * HEAD_DIM)
        wk = [jnp.concatenate([bt[q][:, ln(h)], kt[q][:, ln(h)]], axis=0) for q, h in chains]
        ma = [_mm_nt(at[q][:, ln(h)], wk[j]) for j, (q, h) in enumerate(chains)]
        mr = [_mm_nt(rt[q][:, ln(h)], wk[j]) for j, (q, h) in enumerate(chains)]
        a_ak = [jnp.where(strict[dirn(q)], ma[j][:, C:2 * C], 0.0) for j, (q, h) in enumerate(chains)]
        p = [jnp.where(strict[dirn(q)], ma[j][:, 0:C], 0.0) for j, (q, h) in enumerate(chains)]
        a_rb = [jnp.where(incl[dirn(q)], mr[j][:, 0:C], 0.0) for j, (q, h) in enumerate(chains)]
        a_rk = [jnp.where(incl[dirn(q)], mr[j][:, C:2 * C], 0.0) for j, (q, h) in enumerate(chains)]
        vh = [vc[q][:, ln(h)] for q, h in chains]
        x = [jnp.concatenate([at[q][:, ln(h)], _mm(a_ak[j], vh[j])], axis=1)
             for j, (q, h) in enumerate(chains)]
        for step in range(6):
            pb = [p[j].astype(BF16) for j in range(len(chains))]
            xh = [x[j].astype(BF16) for j in range(len(chains))]
            if step < 3:
                xl = [(x[j] - xh[j].astype(F32)).astype(BF16) for j in range(len(chains))]
                px = [jnp.dot(pb[j], jnp.concatenate([xh[j], xl[j]], axis=1), preferred_element_type=F32)
                      for j in range(len(chains))]
                x = [x[j] + (px[j][:, 0:2 * HEAD_DIM] + px[j][:, 2 * HEAD_DIM:]) for j in range(len(chains))]
            else:
                x = [x[j] + jnp.dot(pb[j], xh[j], preferred_element_type=F32) for j in range(len(chains))]
            if step < 5:
                p = [jnp.dot(pb[j], pb[j], preferred_element_type=F32) for j in range(len(chains))]
        gh = [_mm_tn(x[j], bh[q][:, ln(h)]) for j, (q, h) in enumerate(chains)]
        vk = [_mm_tn_3pass(vh[j], kh[q][:, ln(h)]) for j, (q, h) in enumerate(chains)]
        ax = [_mm(a_rb[j], x[j]) for j in range(len(chains))]
        qt = [rt[q][:, ln(h)] + ax[j][:, 0:HEAD_DIM] for j, (q, h) in enumerate(chains)]
        y0 = [ax[j][:, HEAD_DIM:2 * HEAD_DIM] + _mm(a_rk[j], vh[j]) for j in range(len(chains))]
        ys = [None] * len(chains)
        scans = [(s, d, h) for s in range(nb) for d in range(2) for h in range(HEADS)]
        st = {k: st_s[k] for k in scans}
        for u in range(cpi):
            for s, d, h in scans:
                q = streams.index((s, d, u))
                j = q * HEADS + h
                cur = st[s, d, h]
                ys[j] = _mm_nt(qt[j], cur) + y0[j]
                st[s, d, h] = (cur * etot[q][:, ln(h)] + _mm(cur, gh[j][0:HEAD_DIM, :])
                               + gh[j][HEAD_DIM:2 * HEAD_DIM, :] + vk[j])
        for k in scans:
            st_s[k] = st[k]
        for q, (s, d, u) in enumerate(streams):
            y = jnp.concatenate(ys[q * HEADS:(q + 1) * HEADS], axis=1)
            y_ref[s, rws[q], :] = y_ref[s, rws[q], :] + y
        return carry

    lax.fori_loop(0, n // cpi, chunk, 0)
    if want_fin:
        for s in range(nb):
            for d in range(2):
                for h in range(HEADS):
                    fin_ref[s, d, h] = st_s[s, d, h]

    def post(c, carry):
        rws = rows(c)
        for s in range(nb):
            y = _head_norm(y_ref[s, rws, :], gw_ref[...], gb_ref[...])
            out_ref[s, rws, :] = ((y + bv_s[s, rws, :]) * g_s[s, rws, :]).astype(BF16)
        return carry

    lax.fori_loop(0, n, post, 0)


def _rwkv(mix3, lora3, lw, s0, want_fin):
    b, seq, _ = mix3.shape
    has_init = s0 is not None
    W = BRANCH_W
    nb = 2 if (seq <= 256 and b % 2 == 0) else 1
    col_spec = lambda j: pl.BlockSpec((nb, seq, W), lambda i: (i, 0, j))
    full = lambda a: pl.BlockSpec(a.shape, lambda i: (0,) * a.ndim)
    weights = [lw['conv'], lw['w0'], lw['w2'], lw['a0'], lw['a2'], lw['g2'],
               lw['k_k'], lw['k_a'], lw['r_k'], lw['gn_w'], lw['gn_b']]
    in_specs = [col_spec(4), col_spec(5), col_spec(6),
                pl.BlockSpec((nb, seq, COLS_LORA), lambda i: (i, 0, 0))] + [full(a) for a in weights]
    args = [mix3, mix3, mix3, lora3] + weights
    state_spec = pl.BlockSpec((nb, 2, HEADS, HEAD_DIM, HEAD_DIM), lambda i: (i, 0, 0, 0, 0))
    if has_init:
        in_specs.append(state_spec)
        args.append(s0)
    out_specs = [pl.BlockSpec((nb, seq, W), lambda i: (i, 0, 0))]
    out_shape = [jax.ShapeDtypeStruct((b, seq, W), BF16)]
    if want_fin:
        out_specs.append(state_spec)
        out_shape.append(jax.ShapeDtypeStruct((b, 2, HEADS, HEAD_DIM, HEAD_DIM), F32))
    sw = pltpu.VMEM((nb, seq, W), F32)
    sw2 = pltpu.VMEM((nb, 2, seq, W), F32)
    res = pl.pallas_call(
        functools.partial(_rwkv_kernel, seq=seq, nb=nb, has_init=has_init, want_fin=want_fin),
        grid=(b // nb,),
        in_specs=in_specs,
        out_specs=out_specs,
        out_shape=out_shape,
        scratch_shapes=[sw, sw, sw, sw, sw, sw2, sw2, sw2,
                        pltpu.VMEM((nb, 2, HEADS, HEAD_DIM, HEAD_DIM), F32), sw],
        compiler_params=_cparams("parallel"),
        name="rwkv7",
    )(*args)
    return (res[0], res[1]) if want_fin else (res[0], None)


def _merge_kernel(ya_ref, yb_ref, gate_ref, x_ref, mod_ref, wa_ref, wb_ref, wo_ref, n2_ref, x1_ref, h2_ref):
    m = mod_ref[0]
    br_a = jnp.dot(ya_ref[...], wa_ref[...], preferred_element_type=F32)
    br_b = jnp.dot(yb_ref[...], wb_ref[...], preferred_element_type=F32)
    gate = gate_ref[...].astype(F32)
    merged = _sigmoid(gate[:, 0:D_MODEL]) * br_a + _sigmoid(gate[:, D_MODEL:]) * br_b
    mix = jnp.dot(merged.astype(BF16), wo_ref[...], preferred_element_type=F32)
    x1 = x_ref[...] + m[2:3] * mix
    x1_ref[...] = x1
    h2_ref[...] = (_rms(x1, n2_ref[...]) * (1.0 + m[4:5]) + m[3:4]).astype(BF16)


def _merge(ya2, yb2, gate2, x2, mod, mod_base, mod_step, seq, wa, wb, wo, norm2_g):
    n = x2.shape[0]
    tm = MERGE_ROW_TILE
    mod_idx = lambda i: (mod_base + mod_step * ((i * tm) // seq), 0, 0)
    row = lambda w: pl.BlockSpec((tm, w), lambda i: (i, 0))
    full = lambda a: pl.BlockSpec(a.shape, lambda i: (0,) * a.ndim)
    return pl.pallas_call(
        _merge_kernel,
        grid=(n // tm,),
        in_specs=[row(BRANCH_W), row(BRANCH_W), row(COLS_GATE), row(D_MODEL),
                  pl.BlockSpec((1, N_MOD, D_MODEL), mod_idx),
                  full(wa), full(wb), full(wo), full(norm2_g)],
        out_specs=[row(D_MODEL), row(D_MODEL)],
        out_shape=[jax.ShapeDtypeStruct((n, D_MODEL), F32), jax.ShapeDtypeStruct((n, D_MODEL), BF16)],
        compiler_params=_cparams("parallel"),
        name="merge_out",
    )(ya2, yb2, gate2, x2, mod, wa, wb, wo, norm2_g)


def _bitonic_pairs(n):
    pairs = []
    k = 2
    while k <= n:
        j = k // 2
        while j >= 1:
            for i in range(n):
                l = i ^ j
                if l > i:
                    pairs.append((i, l) if (i & k) == 0 else (l, i))
            j //= 2
        k *= 2
    return pairs


def _top_values_of_keys(s, k):
    r, t = s.shape
    nl = r // 8
    lv = [s[8 * j:8 * j + 8, :] for j in range(nl)]
    for a, b in _bitonic_pairs(nl):
        lv[a], lv[b] = jnp.maximum(lv[a], lv[b]), jnp.minimum(lv[a], lv[b])
    sid = lax.broadcasted_iota(jnp.int32, (8, t), 0)
    out = []
    for it in range(k):
        m = jnp.max(lv[0], axis=0, keepdims=True)
        out.append(m)
        first = jnp.min(jnp.where(lv[0] == m, sid, 8), axis=0, keepdims=True)
        pop = sid == first
        for j in range(min(nl, k - 1 - it)):
            lv[j] = jnp.where(pop, lv[j + 1] if j + 1 < nl else NEG_INF, lv[j])
    return out


def _peer_score_kernel(h_ref, wq_ref, keys_ref, thr_ref, g1_ref, s2_ref, e2_ref, cand_ref):
    qt = _mm_nt(wq_ref[...], h_ref[...])
    half = PEER_DQ // 2
    k1 = PEER_TOPK + 1
    pairs = [(i, j) for i in range(k1) for j in range(k1) if (i + 1) * (j + 1) <= k1]
    assert len(pairs) <= cand_ref.shape[0]
    for h in range(PEER_HEADS):
        s = [_mm(keys_ref[2 * h + c], qt[(2 * h + c) * half:(2 * h + c + 1) * half, :]) for c in range(2)]
        tops = [_top_values_of_keys(s[c], k1) for c in range(2)]
        cand_ref[...] = jnp.full(cand_ref.shape, NEG_INF, F32)
        for r, (i, j) in enumerate(pairs):
            cand_ref[r:r + 1, :] = tops[0][i] + tops[1][j]
        best = _top_values_of_keys(cand_ref[...], k1)
        mx = best[0]
        z = jnp.exp(best[0] - mx)
        for r in range(1, PEER_TOPK):
            z = z + jnp.exp(best[r] - mx)
        theta = 0.5 * (best[PEER_TOPK - 1] + best[PEER_TOPK])
        thr_ref[h] = theta - s[0]
        g1_ref[h] = jnp.exp(s[0] - tops[0][0]) * (0.5 / z)
        s2_ref[h] = s[1]
        e2_ref[h] = jnp.exp(s[1] - tops[1][0])


def _peer_scores(h2, wq_t, keys):
    n = h2.shape[0]
    tb = PEER_SCORE_TOKENS
    full = lambda a: pl.BlockSpec(a.shape, lambda i: (0,) * a.ndim)
    return pl.pallas_call(
        _peer_score_kernel,
        grid=(n // tb,),
        in_specs=[pl.BlockSpec((tb, D_MODEL), lambda i: (i, 0)), full(wq_t), full(keys)],
        out_specs=[pl.BlockSpec((PEER_HEADS, N_KEYS, tb), lambda i: (0, 0, i))] * 4,
        out_shape=[jax.ShapeDtypeStruct((PEER_HEADS, N_KEYS, n), F32)] * 4,
        scratch_shapes=[pltpu.VMEM((64, tb), F32)],
        compiler_params=_cparams("parallel"),
        name="peer_scores",
    )(h2, wq_t, keys)


def _gelu_tanh_x2(x):
    return x + x * jnp.tanh(x * (0.7978845608028654 + 0.035677408136300125 * (x * x)))


def _peer_expert_kernel(h_ref, u_ref, vt_ref, thr_ref, g1_ref, s2_ref, e2_ref, x1_ref, mod_ref, fg_ref, y_ref,
                        acc_ref, w_ref, *, rows_per_step):
    e = pl.program_id(1)
    tb = h_ref.shape[0]

    @pl.when(e == 0)
    def _():
        acc_ref[...] = jnp.zeros_like(acc_ref)

    st = _mm_nt(u_ref[...], h_ref[...])
    rows_per_chunk = 2
    partial = None
    for al in range(rows_per_step):
        a = e * rows_per_step + al
        r0 = al * N_KEYS
        thr = [thr_ref[h, pl.ds(a, 1), :] for h in range(PEER_HEADS)]
        g1 = [g1_ref[h, pl.ds(a, 1), :] for h in range(PEER_HEADS)]
        for lt in range(tb // LANES):
            ls = slice(lt * LANES, (lt + 1) * LANES)
            wsum = None
            for h in range(PEER_HEADS):
                w = jnp.where(s2_ref[h, :, ls] >= thr[h][:, ls], e2_ref[h, :, ls] * g1[h][:, ls], 0.0)
                wsum = w if wsum is None else wsum + w
            act = _gelu_tanh_x2(st[r0:r0 + N_KEYS, ls])
            w_ref[r0:r0 + N_KEYS, ls] = (wsum * act).astype(BF16)
        if (al + 1) % rows_per_chunk == 0:
            rs = slice((al + 1 - rows_per_chunk) * N_KEYS, (al + 1) * N_KEYS)
            d = jnp.dot(vt_ref[:, rs], w_ref[rs, :], preferred_element_type=F32)
            partial = d if partial is None else partial + d
    acc_ref[...] += partial

    @pl.when(e == pl.num_programs(1) - 1)
    def _():
        m = mod_ref[0]
        x2 = x1_ref[...] + m[5:6] * acc_ref[...].T
        y_ref[...] = _rms(x2, fg_ref[...])


def _peer_experts(h2, u_bf, vt_bf, stats, x1, mod, mod_base, mod_step, seq, final_g):
    n = h2.shape[0]
    tb = PEER_EXPERT_TOKENS if (mod_step == 0 or seq % PEER_EXPERT_TOKENS == 0) else seq
    rows_per_step = PEER_ROWS_PER_STEP
    ec = rows_per_step * N_KEYS
    mod_idx = lambda i, e: (mod_base + mod_step * ((i * tb) // seq), 0, 0)
    stat_spec = pl.BlockSpec((PEER_HEADS, N_KEYS, tb), lambda i, e: (0, 0, i))
    return pl.pallas_call(
        functools.partial(_peer_expert_kernel, rows_per_step=rows_per_step),
        grid=(n // tb, N_EXPERTS // ec),
        in_specs=[pl.BlockSpec((tb, D_MODEL), lambda i, e: (i, 0)),
                  pl.BlockSpec((ec, D_MODEL), lambda i, e: (e, 0)),
                  pl.BlockSpec((D_MODEL, ec), lambda i, e: (0, e)),
                  stat_spec, stat_spec, stat_spec, stat_spec,
                  pl.BlockSpec((tb, D_MODEL), lambda i, e: (i, 0)),
                  pl.BlockSpec((1, N_MOD, D_MODEL), mod_idx),
                  pl.BlockSpec((1, D_MODEL), lambda i, e: (0, 0))],
        out_specs=pl.BlockSpec((tb, D_MODEL), lambda i, e: (i, 0)),
        out_shape=jax.ShapeDtypeStruct((n, D_MODEL), F32),
        scratch_shapes=[pltpu.VMEM((D_MODEL, tb), F32), pltpu.VMEM((ec, tb), BF16)],
        compiler_params=_cparams("parallel", "arbitrary"),
        name="peer_experts",
    )(h2, u_bf, vt_bf, *stats, x1, mod, final_g)


def _rope_tables(seq):
    pos = jnp.arange(seq, dtype=jnp.int32)
    lane = jnp.arange(LANES, dtype=jnp.int32) % HEAD_DIM
    use_col = (lane // 32) == 1
    p = jnp.where(use_col[None, :], (pos % GRID_W)[:, None], (pos // GRID_W)[:, None]).astype(F32)
    inv = ROPE_BASE ** (-(lane % 16).astype(F32) / 16.0)
    ang = p * inv[None, :]
    first = (lane % 32) < 16
    return jnp.cos(ang), jnp.where(first[None, :], -jnp.sin(ang), jnp.sin(ang))


def _trunk_path(x, mod, mod_base, mod_step, rope_tabs, s_ret0, s_rwkv0, want_fin, lw, final_g):
    b, seq, _ = x.shape
    x2 = x.reshape(b * seq, D_MODEL)
    mix, lora, gate = _in_proj(x2, mod, mod_base, mod_step, seq, lw['norm1_g'], lw['w_in'])
    mix3 = mix.reshape(b, seq, COLS_MIX)
    lora3 = lora.reshape(b, seq, COLS_LORA)
    ya, ret_fin = _retention(mix3, lw['ret_decay'], lw['ret_gn_w'], lw['ret_gn_b'], rope_tabs, s_ret0, want_fin)
    yb, rwkv_fin = _rwkv(mix3, lora3, lw['rwkv'], s_rwkv0, want_fin)
    x1, h2 = _merge(ya.reshape(b * seq, BRANCH_W), yb.reshape(b * seq, BRANCH_W), gate, x2, mod, mod_base,
                    mod_step, seq, lw['w_br_a'], lw['w_br_b'], lw['w_out'], lw['norm2_g'])
    stats = _peer_scores(h2, lw['peer_wq_t'], lw['peer_keys'])
    y = _peer_experts(h2, lw['peer_u'], lw['peer_vt'], stats, x1, mod, mod_base, mod_step, seq, final_g)
    return y.reshape(b, seq, D_MODEL), ret_fin, rwkv_fin


def kernel(x_prompt, x_sample, state_ret, state_rwkv, c, c_ctx, ada_w, ada_b, norm1_g, w_in, ret_decay, ret_gn_w, ret_gn_b, rwkv_conv, rwkv_w0, rwkv_w2, rwkv_a0, rwkv_a2, rwkv_g2, rwkv_k_k, rwkv_k_a, rwkv_r_k, rwkv_gn_w, rwkv_gn_b, w_br_a, w_br_b, w_out, norm2_g, peer_wq, peer_keys, peer_u, peer_v, final_norm_g):
    assert w_in.shape[0] == 1, "the final norm is fused into the layer's last kernel: single trunk layer only"
    row = lambda a: a.reshape(1, -1)
    cc = jnp.concatenate([c_ctx[None, :], c], axis=0)
    cc = jnp.pad(cc, ((0, (-cc.shape[0]) % 8), (0, 0)))
    rope_tabs = _rope_tables(x_sample.shape[1])
    final_g = row(final_norm_g)

    l = 0
    lw = {
        'norm1_g': row(norm1_g[l]), 'w_in': w_in[l].astype(BF16),
        'ret_decay': jnp.repeat(ret_decay[l], HEAD_DIM, axis=1),
        'ret_gn_w': row(ret_gn_w[l]), 'ret_gn_b': row(ret_gn_b[l]),
        'rwkv': {'conv': rwkv_conv[l], 'w0': rwkv_w0[l], 'w2': rwkv_w2[l].astype(BF16), 'a0': rwkv_a0[l],
                 'a2': rwkv_a2[l].astype(BF16), 'g2': rwkv_g2[l].astype(BF16), 'k_k': row(rwkv_k_k[l]),
                 'k_a': row(rwkv_k_a[l]), 'r_k': row(rwkv_r_k[l]), 'gn_w': row(rwkv_gn_w[l]),
                 'gn_b': row(rwkv_gn_b[l])},
        'w_br_a': w_br_a[l].astype(BF16), 'w_br_b': w_br_b[l].astype(BF16), 'w_out': w_out[l].astype(BF16),
        'norm2_g': row(norm2_g[l]),
        'peer_wq_t': peer_wq[l].astype(BF16).T,
        'peer_keys': peer_keys[l].reshape(2 * PEER_HEADS, N_KEYS, PEER_DQ // 2).astype(BF16),
        'peer_u': peer_u[l].astype(BF16), 'peer_vt': peer_v[l].astype(BF16).T,
    }
    mod = _adaln(cc, ada_w[l], row(ada_b[l])).reshape(cc.shape[0], N_MOD, D_MODEL)
    yp, ret_fin, rwkv_fin = _trunk_path(x_prompt, mod, 0, 0, None, None, None, True, lw, final_g)
    ys, _, _ = _trunk_path(x_sample, mod, 1, 1, rope_tabs, state_ret[:, l], state_rwkv[:, l], False, lw, final_g)
    return (yp, ys, ret_fin[:, None], rwkv_fin[:, None])
```
